```python
import math
import jax, jax.numpy as jnp
from jax import lax
import numpy as np

D_MODEL = 2048
BATCH = 8
SEQ = 8192
DEPTH = 2

MIX_WIDTH = D_MODEL
S5_WIDTH = MIX_WIDTH // 4
SGU_WIDTH = MIX_WIDTH // 2
POOL_WIDTH = MIX_WIDTH - S5_WIDTH - SGU_WIDTH
S5_GROUP_CH = 16
S5_GROUPS = S5_WIDTH // S5_GROUP_CH
S5_STATE = 64
DT_MIN = 0.001
DT_MAX = 0.1
CHUNK = 128
SGU_HEAD_DIM = 128
SGU_HEADS = SGU_WIDTH // SGU_HEAD_DIM
POOL_WINDOWS = (2, 4, 8, 16)
POOL_GROUPS = len(POOL_WINDOWS)
POOL_GROUP_CH = POOL_WIDTH // POOL_GROUPS
SPLIT_SIZES = (S5_WIDTH, SGU_WIDTH, SGU_WIDTH, POOL_WIDTH, S5_WIDTH, SGU_WIDTH, POOL_WIDTH)
IN_COLS = sum(SPLIT_SIZES)
SPLIT_POINTS = tuple(int(s) for s in np.cumsum(SPLIT_SIZES)[:-1])
RMS_EPS = 1e-6
LN_EPS = 1e-5

kernel_name = 'hymba_style_s5_gmlp_pool_hybrid'


def rms_norm(x, g):
    xf = x.astype(jnp.float32)
    y = xf * lax.rsqrt(jnp.mean(xf * xf, axis=-1, keepdims=True) + RMS_EPS)
    return (y * g.astype(jnp.float32)).astype(x.dtype)


def s5_mixer(xa, lam_re, lam_im, b_re, b_im, c_re, c_im, d_skip, log_dt, w_glu, b_glu):
    bsz, seq, _ = xa.shape
    f32 = jnp.float32
    xg = xa.astype(f32).reshape(bsz, seq, S5_GROUPS, S5_GROUP_CH)
    lam = lax.complex(lam_re.astype(f32), lam_im.astype(f32))
    dt = jnp.exp(log_dt.astype(f32))[:, None]
    lam_bar = jnp.exp(lam * dt)
    b = lax.complex(b_re.astype(f32), b_im.astype(f32))
    b_bar = ((lam_bar - 1.0) / lam)[..., None] * b
    c = lax.complex(c_re.astype(f32), c_im.astype(f32))
    bu = jnp.einsum('blgh,gph->blgp', xg.astype(jnp.complex64), b_bar)
    a = jnp.broadcast_to(lam_bar, (1, seq) + lam_bar.shape)

    def combine(left, right):
        a_l, b_l = left
        a_r, b_r = right
        return a_r * a_l, a_r * b_l + b_r

    _, states = lax.associative_scan(combine, (a, bu), axis=1)
    y = jnp.einsum('blgp,ghp->blgh', states, c).real + d_skip.astype(f32) * xg
    y = jax.nn.gelu(y.reshape(bsz, seq, S5_WIDTH)).astype(xa.dtype)
    return y * jax.nn.sigmoid(y @ w_glu + b_glu)


def sgu_mixer(u, v, ln_g, ln_b, w_s, b_s):
    bsz, seq, _ = v.shape
    u = jax.nn.gelu(u)
    vf = jax.nn.gelu(v).astype(jnp.float32)
    mu = jnp.mean(vf, axis=-1, keepdims=True)
    var = jnp.mean(jnp.square(vf - mu), axis=-1, keepdims=True)
    vn = ((vf - mu) * lax.rsqrt(var + LN_EPS) * ln_g.astype(jnp.float32)
          + ln_b.astype(jnp.float32)).astype(v.dtype)
    vn = vn.reshape(bsz, seq // CHUNK, CHUNK, SGU_HEADS, SGU_HEAD_DIM)
    causal = jnp.tril(jnp.ones((CHUNK, CHUNK), dtype=bool))
    ws = jnp.where(causal[None], w_s, jnp.zeros_like(w_s))
    s = jnp.einsum('hts,bcshd->bcthd', ws, vn) + jnp.transpose(b_s)[:, :, None]
    return u * s.reshape(bsz, seq, SGU_WIDTH)


def pool_mixer(xc, w_pool, pool_scale):
    bsz, seq, _ = xc.shape
    xg = xc.astype(jnp.float32).reshape(bsz, seq, POOL_GROUPS, POOL_GROUP_CH)
    cs = jnp.cumsum(xg, axis=1)
    pos = jnp.arange(1, seq + 1, dtype=jnp.float32)[None, :, None]
    outs = []
    for g, w in enumerate(POOL_WINDOWS):
        c = cs[:, :, g]
        lagged = jnp.pad(c[:, :seq - w], ((0, 0), (w, 0), (0, 0)))
        mean = (c - lagged) / jnp.minimum(pos, float(w))
        outs.append(mean - xg[:, :, g])
    p = jnp.stack(outs, axis=2).astype(xc.dtype)
    y = jnp.einsum('blgc,gcd->blgd', p, w_pool).reshape(bsz, seq, POOL_WIDTH)
    return y * pool_scale


def _fwd_setup_inputs(seed: int = 0) -> dict:
    key = jax.random.key(seed)
    ks = jax.random.split(key, 24)
    f32 = jnp.float32
    nrm = lambda k, shape: jax.random.normal(k, shape, dtype=f32)
    L, D = DEPTH, D_MODEL
    G, P, H = S5_GROUPS, S5_STATE, S5_GROUP_CH
    x = nrm(ks[0], (BATCH, SEQ, D))
    norm_g = 1.0 + 0.02 * nrm(ks[1], (L, D))
    w_in = nrm(ks[2], (L, D, IN_COLS)) * D ** -0.5
    lam_re = -0.5 + 0.01 * nrm(ks[3], (L, G, P))
    lam_im = math.pi * jnp.arange(P, dtype=f32)[None, None, :] + 0.01 * nrm(ks[4], (L, G, P))
    b_re = nrm(ks[5], (L, G, P, H)) * (2.0 * H) ** -0.5
    b_im = nrm(ks[6], (L, G, P, H)) * (2.0 * H) ** -0.5
    c_re = nrm(ks[7], (L, G, H, P)) * P ** -0.5
    c_im = nrm(ks[8], (L, G, H, P)) * P ** -0.5
    d_skip = nrm(ks[9], (L, G, H))
    log_dt = jax.random.uniform(ks[10], (L, G), dtype=f32,
                                minval=math.log(DT_MIN), maxval=math.log(DT_MAX))
    w_glu = nrm(ks[11], (L, S5_WIDTH, S5_WIDTH)) * S5_WIDTH ** -0.5
    b_glu = 0.01 * nrm(ks[12], (L, S5_WIDTH))
    ln_g = 1.0 + 0.02 * nrm(ks[13], (L, SGU_WIDTH))
    ln_b = 0.02 * nrm(ks[14], (L, SGU_WIDTH))
    w_s = nrm(ks[15], (L, SGU_HEADS, CHUNK, CHUNK)) * CHUNK ** -0.5
    b_s = 1.0 + 0.02 * nrm(ks[16], (L, SGU_HEADS, CHUNK))
    w_pool = nrm(ks[17], (L, POOL_GROUPS, POOL_GROUP_CH, POOL_GROUP_CH)) * POOL_GROUP_CH ** -0.5
    pool_scale = 1.0 + 0.1 * nrm(ks[18], (L, POOL_WIDTH))
    w_out = nrm(ks[19], (L, MIX_WIDTH, D)) * MIX_WIDTH ** -0.5
    final_g = 1.0 + 0.02 * nrm(ks[20], (D,))
    return {'x': x, 'norm_g': norm_g, 'w_in': w_in, 'lam_re': lam_re, 'lam_im': lam_im,
            'b_re': b_re, 'b_im': b_im, 'c_re': c_re, 'c_im': c_im, 'd_skip': d_skip,
            'log_dt': log_dt, 'w_glu': w_glu, 'b_glu': b_glu, 'ln_g': ln_g, 'ln_b': ln_b,
            'w_s': w_s, 'b_s': b_s, 'w_pool': w_pool, 'pool_scale': pool_scale,
            'w_out': w_out, 'final_g': final_g}


def _fwd_reference(x, norm_g, w_in, lam_re, lam_im, b_re, b_im, c_re, c_im, d_skip, log_dt,
              w_glu, b_glu, ln_g, ln_b, w_s, b_s, w_pool, pool_scale, w_out, final_g):
    for i in range(DEPTH):
        h = rms_norm(x, norm_g[i])
        z = h @ w_in[i]
        xa, u, v, xc, ga, gb, gc = jnp.split(z, SPLIT_POINTS, axis=-1)
        ya = s5_mixer(xa, lam_re[i], lam_im[i], b_re[i], b_im[i], c_re[i], c_im[i],
                      d_skip[i], log_dt[i], w_glu[i], b_glu[i]) * jax.nn.silu(ga)
        yb = sgu_mixer(u, v, ln_g[i], ln_b[i], w_s[i], b_s[i]) * jax.nn.silu(gb)
        yc = pool_mixer(xc, w_pool[i], pool_scale[i]) * jax.nn.silu(gc)
        y = jnp.concatenate([ya.astype(x.dtype), yb.astype(x.dtype), yc.astype(x.dtype)], axis=-1)
        x = x + y @ w_out[i]
    return rms_norm(x, final_g)


import jax as _jax
import jax.numpy as _jnp

TWIN_FORMAT = 'train_step'
FWD_PARAMS = ['x', 'norm_g', 'w_in', 'lam_re', 'lam_im', 'b_re', 'b_im', 'c_re', 'c_im', 'd_skip', 'log_dt', 'w_glu', 'b_glu', 'ln_g', 'ln_b', 'w_s', 'b_s', 'w_pool', 'pool_scale', 'w_out', 'final_g']
TWIN_WEIGHTS = ['norm_g', 'w_in', 'lam_re', 'lam_im', 'b_re', 'b_im', 'c_re', 'c_im', 'd_skip', 'log_dt', 'w_glu', 'b_glu', 'ln_g', 'ln_b', 'w_s', 'b_s', 'w_pool', 'pool_scale', 'w_out', 'final_g']
TWIN_DIFF_INPUT = 'x'
TWIN_INPUTS = ['x', 'norm_g', 'w_in', 'lam_re', 'lam_im', 'b_re', 'b_im', 'c_re', 'c_im', 'd_skip', 'log_dt', 'w_glu', 'b_glu', 'ln_g', 'ln_b', 'w_s', 'b_s', 'w_pool', 'pool_scale', 'w_out', 'final_g', 'loss_target', 'm_norm_g', 'm_w_in', 'm_lam_re', 'm_lam_im', 'm_b_re', 'm_b_im', 'm_c_re', 'm_c_im', 'm_d_skip', 'm_log_dt', 'm_w_glu', 'm_b_glu', 'm_ln_g', 'm_ln_b', 'm_w_s', 'm_b_s', 'm_w_pool', 'm_pool_scale', 'm_w_out', 'm_final_g', 'v_norm_g', 'v_w_in', 'v_lam_re', 'v_lam_im', 'v_b_re', 'v_b_im', 'v_c_re', 'v_c_im', 'v_d_skip', 'v_log_dt', 'v_w_glu', 'v_b_glu', 'v_ln_g', 'v_ln_b', 'v_w_s', 'v_b_s', 'v_w_pool', 'v_pool_scale', 'v_w_out', 'v_final_g']
TWIN_OUTPUTS = ['loss', 'grad_x', 'grad_norm_g', 'grad_w_in', 'grad_lam_re', 'grad_lam_im', 'grad_b_re', 'grad_b_im', 'grad_c_re', 'grad_c_im', 'grad_d_skip', 'grad_log_dt', 'grad_w_glu', 'grad_b_glu', 'grad_ln_g', 'grad_ln_b', 'grad_w_s', 'grad_b_s', 'grad_w_pool', 'grad_pool_scale', 'grad_w_out', 'grad_final_g', 'delta_norm_g', 'delta_w_in', 'delta_lam_re', 'delta_lam_im', 'delta_b_re', 'delta_b_im', 'delta_c_re', 'delta_c_im', 'delta_d_skip', 'delta_log_dt', 'delta_w_glu', 'delta_b_glu', 'delta_ln_g', 'delta_ln_b', 'delta_w_s', 'delta_b_s', 'delta_w_pool', 'delta_pool_scale', 'delta_w_out', 'delta_final_g', 'new_m_norm_g', 'new_m_w_in', 'new_m_lam_re', 'new_m_lam_im', 'new_m_b_re', 'new_m_b_im', 'new_m_c_re', 'new_m_c_im', 'new_m_d_skip', 'new_m_log_dt', 'new_m_w_glu', 'new_m_b_glu', 'new_m_ln_g', 'new_m_ln_b', 'new_m_w_s', 'new_m_b_s', 'new_m_w_pool', 'new_m_pool_scale', 'new_m_w_out', 'new_m_final_g', 'new_v_norm_g', 'new_v_w_in', 'new_v_lam_re', 'new_v_lam_im', 'new_v_b_re', 'new_v_b_im', 'new_v_c_re', 'new_v_c_im', 'new_v_d_skip', 'new_v_log_dt', 'new_v_w_glu', 'new_v_b_glu', 'new_v_ln_g', 'new_v_ln_b', 'new_v_w_s', 'new_v_b_s', 'new_v_w_pool', 'new_v_pool_scale', 'new_v_w_out', 'new_v_final_g']
TWIN_LEAF_KINDS = {'loss': 'loss', 'grad_x': 'grad_x', 'grad_norm_g': 'grad_w', 'grad_w_in': 'grad_w', 'grad_lam_re': 'grad_w', 'grad_lam_im': 'grad_w', 'grad_b_re': 'grad_w', 'grad_b_im': 'grad_w', 'grad_c_re': 'grad_w', 'grad_c_im': 'grad_w', 'grad_d_skip': 'grad_w', 'grad_log_dt': 'grad_w', 'grad_w_glu': 'grad_w', 'grad_b_glu': 'grad_w', 'grad_ln_g': 'grad_w', 'grad_ln_b': 'grad_w', 'grad_w_s': 'grad_w', 'grad_b_s': 'grad_w', 'grad_w_pool': 'grad_w', 'grad_pool_scale': 'grad_w', 'grad_w_out': 'grad_w', 'grad_final_g': 'grad_w', 'delta_norm_g': 'delta_w', 'delta_w_in': 'delta_w', 'delta_lam_re': 'delta_w', 'delta_lam_im': 'delta_w', 'delta_b_re': 'delta_w', 'delta_b_im': 'delta_w', 'delta_c_re': 'delta_w', 'delta_c_im': 'delta_w', 'delta_d_skip': 'delta_w', 'delta_log_dt': 'delta_w', 'delta_w_glu': 'delta_w', 'delta_b_glu': 'delta_w', 'delta_ln_g': 'delta_w', 'delta_ln_b': 'delta_w', 'delta_w_s': 'delta_w', 'delta_b_s': 'delta_w', 'delta_w_pool': 'delta_w', 'delta_pool_scale': 'delta_w', 'delta_w_out': 'delta_w', 'delta_final_g': 'delta_w', 'new_m_norm_g': 'new_m', 'new_m_w_in': 'new_m', 'new_m_lam_re': 'new_m', 'new_m_lam_im': 'new_m', 'new_m_b_re': 'new_m', 'new_m_b_im': 'new_m', 'new_m_c_re': 'new_m', 'new_m_c_im': 'new_m', 'new_m_d_skip': 'new_m', 'new_m_log_dt': 'new_m', 'new_m_w_glu': 'new_m', 'new_m_b_glu': 'new_m', 'new_m_ln_g': 'new_m', 'new_m_ln_b': 'new_m', 'new_m_w_s': 'new_m', 'new_m_b_s': 'new_m', 'new_m_w_pool': 'new_m', 'new_m_pool_scale': 'new_m', 'new_m_w_out': 'new_m', 'new_m_final_g': 'new_m', 'new_v_norm_g': 'new_v', 'new_v_w_in': 'new_v', 'new_v_lam_re': 'new_v', 'new_v_lam_im': 'new_v', 'new_v_b_re': 'new_v', 'new_v_b_im': 'new_v', 'new_v_c_re': 'new_v', 'new_v_c_im': 'new_v', 'new_v_d_skip': 'new_v', 'new_v_log_dt': 'new_v', 'new_v_w_glu': 'new_v', 'new_v_b_glu': 'new_v', 'new_v_ln_g': 'new_v', 'new_v_ln_b': 'new_v', 'new_v_w_s': 'new_v', 'new_v_b_s': 'new_v', 'new_v_w_pool': 'new_v', 'new_v_pool_scale': 'new_v', 'new_v_w_out': 'new_v', 'new_v_final_g': 'new_v'}


def _forward(args):
    return _fwd_reference(*[args[k] for k in FWD_PARAMS])


def _output_shape():
    def fwd():
        inp = _fwd_setup_inputs(0)
        return _fwd_reference(*[inp[k] for k in FWD_PARAMS])
    out = _jax.eval_shape(fwd)
    return out.shape, out.dtype

N_MICROBATCH = 1
ADAM_LR = 0.001
ADAM_B1 = 0.9
ADAM_B2 = 0.999
ADAM_EPS = 1e-08
ADAM_WD = 0.01
ADAM_STEP = 10
PER_EXAMPLE_BATCH_AXIS = {'x': 0, 'loss_target': 0}
SHARED_INPUTS = []
_WEIGHT_DTYPES = {'norm_g': _jnp.float32, 'w_in': _jnp.float32, 'lam_re': _jnp.float32, 'lam_im': _jnp.float32, 'b_re': _jnp.float32, 'b_im': _jnp.float32, 'c_re': _jnp.float32, 'c_im': _jnp.float32, 'd_skip': _jnp.float32, 'log_dt': _jnp.float32, 'w_glu': _jnp.float32, 'b_glu': _jnp.float32, 'ln_g': _jnp.float32, 'ln_b': _jnp.float32, 'w_s': _jnp.float32, 'b_s': _jnp.float32, 'w_pool': _jnp.float32, 'pool_scale': _jnp.float32, 'w_out': _jnp.float32, 'final_g': _jnp.float32}
MOMENT_SCALE = {'norm_g': 7.837555e-02, 'w_in': 4.966553e-02, 'lam_re': 1.950973e-03, 'lam_im': 1.961753e-03, 'b_re': 1.177395e-03, 'b_im': 1.197166e-03, 'c_re': 1.672849e-03, 'c_im': 1.687138e-03, 'd_skip': 2.846054e-02, 'log_dt': 8.270896e-01, 'w_glu': 7.217579e-03, 'b_glu': 1.116787e-02, 'ln_g': 3.094527e-02, 'ln_b': 3.212510e-02, 'w_s': 3.185233e-02, 'b_s': 4.514229e-02, 'w_pool': 6.155381e-02, 'pool_scale': 6.091467e-02, 'w_out': 5.099370e-02, 'final_g': 3.200711e+01}


def _to_microbatches(a, axis):
    t = _jnp.moveaxis(a, axis, 0)
    t = t.reshape((N_MICROBATCH, t.shape[0] // N_MICROBATCH) + t.shape[1:])
    return _jnp.moveaxis(t, 1, axis + 1)


def setup_inputs(seed: int = 0) -> dict:
    inp = _fwd_setup_inputs(seed)
    key = _jax.random.fold_in(_jax.random.key(seed), 7919)
    shape, _ = _output_shape()
    out = dict(inp)
    out["loss_target"] = _jax.random.normal(_jax.random.fold_in(key, 0), shape, _jnp.float32)
    for i, name in enumerate(TWIN_WEIGHTS):
        w = inp[name].astype(_jnp.float32)
        if MOMENT_SCALE is None:
            s = _jnp.sqrt(_jnp.mean(_jnp.square(w)) + 1e-30)
        else:
            s = MOMENT_SCALE[name]
        km, kv = _jax.random.split(_jax.random.fold_in(key, i + 1))
        out[name] = w
        out["m_" + name] = s * _jax.random.normal(km, w.shape, _jnp.float32)
        out["v_" + name] = (s * s) * _jax.random.uniform(kv, w.shape, _jnp.float32, 0.5, 1.5)
    if N_MICROBATCH > 1:
        for name, axis in PER_EXAMPLE_BATCH_AXIS.items():
            out[name] = _to_microbatches(out[name], axis)
    return {'x': out['x'], 'norm_g': out['norm_g'], 'w_in': out['w_in'], 'lam_re': out['lam_re'], 'lam_im': out['lam_im'], 'b_re': out['b_re'], 'b_im': out['b_im'], 'c_re': out['c_re'], 'c_im': out['c_im'], 'd_skip': out['d_skip'], 'log_dt': out['log_dt'], 'w_glu': out['w_glu'], 'b_glu': out['b_glu'], 'ln_g': out['ln_g'], 'ln_b': out['ln_b'], 'w_s': out['w_s'], 'b_s': out['b_s'], 'w_pool': out['w_pool'], 'pool_scale': out['pool_scale'], 'w_out': out['w_out'], 'final_g': out['final_g'], 'loss_target': out['loss_target'], 'm_norm_g': out['m_norm_g'], 'm_w_in': out['m_w_in'], 'm_lam_re': out['m_lam_re'], 'm_lam_im': out['m_lam_im'], 'm_b_re': out['m_b_re'], 'm_b_im': out['m_b_im'], 'm_c_re': out['m_c_re'], 'm_c_im': out['m_c_im'], 'm_d_skip': out['m_d_skip'], 'm_log_dt': out['m_log_dt'], 'm_w_glu': out['m_w_glu'], 'm_b_glu': out['m_b_glu'], 'm_ln_g': out['m_ln_g'], 'm_ln_b': out['m_ln_b'], 'm_w_s': out['m_w_s'], 'm_b_s': out['m_b_s'], 'm_w_pool': out['m_w_pool'], 'm_pool_scale': out['m_pool_scale'], 'm_w_out': out['m_w_out'], 'm_final_g': out['m_final_g'], 'v_norm_g': out['v_norm_g'], 'v_w_in': out['v_w_in'], 'v_lam_re': out['v_lam_re'], 'v_lam_im': out['v_lam_im'], 'v_b_re': out['v_b_re'], 'v_b_im': out['v_b_im'], 'v_c_re': out['v_c_re'], 'v_c_im': out['v_c_im'], 'v_d_skip': out['v_d_skip'], 'v_log_dt': out['v_log_dt'], 'v_w_glu': out['v_w_glu'], 'v_b_glu': out['v_b_glu'], 'v_ln_g': out['v_ln_g'], 'v_ln_b': out['v_ln_b'], 'v_w_s': out['v_w_s'], 'v_b_s': out['v_b_s'], 'v_w_pool': out['v_w_pool'], 'v_pool_scale': out['v_pool_scale'], 'v_w_out': out['v_w_out'], 'v_final_g': out['v_final_g']}


def _loss(weights, diff, rest, loss_target):
    with _jax.named_scope("forward"):
        args = {**rest, TWIN_DIFF_INPUT: diff, **{k: w.astype(_WEIGHT_DTYPES[k]) for k, w in weights.items()}}
        y = _forward(args)
    with _jax.named_scope("loss_head"):
        err = _jnp.square(y.astype(_jnp.float32) - loss_target)
        return 0.5 * _jnp.sum(_jnp.mean(err, axis=-1)) if err.ndim else 0.5 * err


def _adamw(w, g, m, v):
    m = ADAM_B1 * m + (1.0 - ADAM_B1) * g
    v = ADAM_B2 * v + (1.0 - ADAM_B2) * _jnp.square(g)
    m_hat = m / (1.0 - ADAM_B1 ** ADAM_STEP)
    v_hat = v / (1.0 - ADAM_B2 ** ADAM_STEP)
    delta = -ADAM_LR * (m_hat / (_jnp.sqrt(v_hat) + ADAM_EPS) + ADAM_WD * w)
    return delta, m, v


def reference(x, norm_g, w_in, lam_re, lam_im, b_re, b_im, c_re, c_im, d_skip, log_dt, w_glu, b_glu, ln_g, ln_b, w_s, b_s, w_pool, pool_scale, w_out, final_g, loss_target, m_norm_g, m_w_in, m_lam_re, m_lam_im, m_b_re, m_b_im, m_c_re, m_c_im, m_d_skip, m_log_dt, m_w_glu, m_b_glu, m_ln_g, m_ln_b, m_w_s, m_b_s, m_w_pool, m_pool_scale, m_w_out, m_final_g, v_norm_g, v_w_in, v_lam_re, v_lam_im, v_b_re, v_b_im, v_c_re, v_c_im, v_d_skip, v_log_dt, v_w_glu, v_b_glu, v_ln_g, v_ln_b, v_w_s, v_b_s, v_w_pool, v_pool_scale, v_w_out, v_final_g):
    given = dict(x=x, norm_g=norm_g, w_in=w_in, lam_re=lam_re, lam_im=lam_im, b_re=b_re, b_im=b_im, c_re=c_re, c_im=c_im, d_skip=d_skip, log_dt=log_dt, w_glu=w_glu, b_glu=b_glu, ln_g=ln_g, ln_b=ln_b, w_s=w_s, b_s=b_s, w_pool=w_pool, pool_scale=pool_scale, w_out=w_out, final_g=final_g, loss_target=loss_target, m_norm_g=m_norm_g, m_w_in=m_w_in, m_lam_re=m_lam_re, m_lam_im=m_lam_im, m_b_re=m_b_re, m_b_im=m_b_im, m_c_re=m_c_re, m_c_im=m_c_im, m_d_skip=m_d_skip, m_log_dt=m_log_dt, m_w_glu=m_w_glu, m_b_glu=m_b_glu, m_ln_g=m_ln_g, m_ln_b=m_ln_b, m_w_s=m_w_s, m_b_s=m_b_s, m_w_pool=m_w_pool, m_pool_scale=m_pool_scale, m_w_out=m_w_out, m_final_g=m_final_g, v_norm_g=v_norm_g, v_w_in=v_w_in, v_lam_re=v_lam_re, v_lam_im=v_lam_im, v_b_re=v_b_re, v_b_im=v_b_im, v_c_re=v_c_re, v_c_im=v_c_im, v_d_skip=v_d_skip, v_log_dt=v_log_dt, v_w_glu=v_w_glu, v_b_glu=v_b_glu, v_ln_g=v_ln_g, v_ln_b=v_ln_b, v_w_s=v_w_s, v_b_s=v_b_s, v_w_pool=v_w_pool, v_pool_scale=v_pool_scale, v_w_out=v_w_out, v_final_g=v_final_g)
    weights = {n: given[n] for n in TWIN_WEIGHTS}
    shared = {n: given[n] for n in SHARED_INPUTS}
    per_example = {n: given[n] for n in ['x']}
    grad_fn = _jax.value_and_grad(_loss, argnums=(0, 1))

    def one_microbatch(ex, loss_target):
        ex = dict(ex)
        diff = ex.pop(TWIN_DIFF_INPUT)
        return grad_fn(weights, diff, {**shared, **ex}, loss_target)

    if N_MICROBATCH == 1:
        loss, (grad_w, grad_x) = one_microbatch(per_example, given["loss_target"])
    else:
        def body(carry, xs):
            loss_sum, grad_sum = carry
            l_k, (gw_k, gx_k) = one_microbatch(xs[0], xs[1])
            with _jax.named_scope("update"):
                return (loss_sum + l_k, _jax.tree.map(_jnp.add, grad_sum, gw_k)), gx_k

        init = (_jnp.zeros((), _jnp.float32), _jax.tree.map(_jnp.zeros_like, weights))
        (loss, grad_w), grad_x = _jax.lax.scan(body, init, (per_example, given["loss_target"]))
    with _jax.named_scope("update"):
        delta_w, new_m, new_v = {}, {}, {}
        for n in TWIN_WEIGHTS:
            delta_w[n], new_m[n], new_v[n] = _adamw(weights[n], grad_w[n], given["m_" + n], given["v_" + n])
    return (loss, grad_x, *[grad_w[n] for n in TWIN_WEIGHTS], *[delta_w[n] for n in TWIN_WEIGHTS],
            *[new_m[n] for n in TWIN_WEIGHTS], *[new_v[n] for n in TWIN_WEIGHTS])
```

```python
import functools
import math

import jax
import jax.numpy as jnp
from jax import lax
from jax.experimental import pallas as pl
from jax.experimental.pallas import tpu as pltpu

F32 = jnp.float32
BF16 = jnp.bfloat16

D_MODEL = 2048
DEPTH = 2
S5_W, SGU_W, POOL_W = 512, 1024, 512
S5_GROUPS, S5_STATE, S5_CH = 32, 64, 16
N_STATE = S5_GROUPS * S5_STATE
CHUNK = 128
SGU_HEADS = 8
POOL_WINDOWS = (2, 4, 8, 16)
IN_COLS = 5120
RMS_EPS = 1e-6
LN_EPS = 1e-5
ADAM_LR, ADAM_B1, ADAM_B2, ADAM_EPS, ADAM_WD, ADAM_STEP = 0.001, 0.9, 0.999, 1e-08, 0.01, 10

CB = 512
N_CB = IN_COLS // CB
CB_XA, CB_U, CB_V, CB_XC, CB_GA, CB_GB, CB_GC = 0, 1, 3, 5, 6, 7, 9

N_DEV = 8
SH_IN = IN_COLS // N_DEV
SH_OUT = D_MODEL // N_DEV
SH_GLU = S5_W // N_DEV

VMEM_LIMIT = 52 * 1024 * 1024
HALO = 16
LANE_CH = 512

TN = (((0,), (0,)), ((), ()))
NT = (((1,), (1,)), ((), ()))


def _params(*sem):
    return pltpu.CompilerParams(dimension_semantics=sem if sem else None, vmem_limit_bytes=VMEM_LIMIT)


def _dot(a, b, dims=None):
    if dims is None:
        return jnp.dot(a, b, preferred_element_type=F32)
    return lax.dot_general(a, b, dims, preferred_element_type=F32)


_GELU_C = math.sqrt(2.0 / math.pi)


def _gelu(x):
    return 0.5 * x * (1.0 + jnp.tanh(_GELU_C * (x + 0.044715 * x * x * x)))


def _gelu_grad(x):
    t = jnp.tanh(_GELU_C * (x + 0.044715 * x * x * x))
    return 0.5 * (1.0 + t) + 0.5 * x * (1.0 - t * t) * _GELU_C * (1.0 + 3.0 * 0.044715 * x * x)


def _silu_and_grad(x):
    s = jax.nn.sigmoid(x)
    return x * s, s * (1.0 + x * (1.0 - s))


def _rowsum(x):
    return jnp.sum(x, axis=0, keepdims=True)


def _rms_inproj(x, g, w, tm=512):
    t = x.shape[0]

    def body(x_ref, g_ref, w_ref, z_ref, h_ref):
        @pl.when(pl.program_id(1) == 0)
        def _():
            xv = x_ref[...]
            r = lax.rsqrt(jnp.mean(xv * xv, axis=-1, keepdims=True) + RMS_EPS)
            h_ref[...] = (xv * r * g_ref[...]).astype(BF16)

        z_ref[...] = _dot(h_ref[...], w_ref[...])

    return pl.pallas_call(
        body,
        name="rms_inproj",
        grid=(t // tm, N_CB),
        in_specs=[
            pl.BlockSpec((tm, D_MODEL), lambda m, n: (m, 0)),
            pl.BlockSpec((1, D_MODEL), lambda m, n: (0, 0)),
            pl.BlockSpec((D_MODEL, CB), lambda m, n: (0, n)),
        ],
        out_specs=[
            pl.BlockSpec((tm, CB), lambda m, n: (m, n)),
            pl.BlockSpec((tm, D_MODEL), lambda m, n: (m, 0)),
        ],
        out_shape=[jax.ShapeDtypeStruct((t, IN_COLS), F32), jax.ShapeDtypeStruct((t, D_MODEL), BF16)],
        compiler_params=_params("arbitrary", "arbitrary"),
    )(x, g, w)


def _outproj(x, ya, yb, yc, w, tm=512, tn=1024):
    t = x.shape[0]

    def body(x_ref, ya_ref, yb_ref, yc_ref, w_ref, o_ref):
        acc = _dot(ya_ref[...], w_ref[0:S5_W, :])
        acc += _dot(yb_ref[...], w_ref[S5_W:S5_W + SGU_W, :])
        acc += _dot(yc_ref[...], w_ref[S5_W + SGU_W:, :])
        o_ref[...] = x_ref[...] + acc

    return pl.pallas_call(
        body,
        name="outproj",
        grid=(t // tm, D_MODEL // tn),
        in_specs=[
            pl.BlockSpec((tm, tn), lambda m, n: (m, n)),
            pl.BlockSpec((tm, S5_W), lambda m, n: (m, 0)),
            pl.BlockSpec((tm, SGU_W), lambda m, n: (m, 0)),
            pl.BlockSpec((tm, POOL_W), lambda m, n: (m, 0)),
            pl.BlockSpec((D_MODEL, tn), lambda m, n: (0, n)),
        ],
        out_specs=pl.BlockSpec((tm, tn), lambda m, n: (m, n)),
        out_shape=jax.ShapeDtypeStruct((t, D_MODEL), F32),
        compiler_params=_params("arbitrary", "arbitrary"),
    )(x, ya, yb, yc, w)


def _outproj_bwd(dx, w, tm=512):
    t = dx.shape[0]

    def body(dx_ref, w_ref, dya_ref, dyb_ref, dyc_ref):
        dy = _dot(dx_ref[...].astype(BF16), w_ref[...], NT)
        dya_ref[...] = dy[:, 0:S5_W]
        dyb_ref[...] = dy[:, S5_W:S5_W + SGU_W]
        dyc_ref[...] = dy[:, S5_W + SGU_W:]

    return pl.pallas_call(
        body,
        name="outproj_bwd",
        grid=(t // tm,),
        in_specs=[
            pl.BlockSpec((tm, D_MODEL), lambda m: (m, 0)),
            pl.BlockSpec((D_MODEL, D_MODEL), lambda m: (0, 0)),
        ],
        out_specs=[
            pl.BlockSpec((tm, S5_W), lambda m: (m, 0)),
            pl.BlockSpec((tm, SGU_W), lambda m: (m, 0)),
            pl.BlockSpec((tm, POOL_W), lambda m: (m, 0)),
        ],
        out_shape=[
            jax.ShapeDtypeStruct((t, S5_W), F32),
            jax.ShapeDtypeStruct((t, SGU_W), F32),
            jax.ShapeDtypeStruct((t, POOL_W), F32),
        ],
        compiler_params=_params("arbitrary"),
    )(dx, w)


def _dz_piece_maps():
    s5_map = lambda j: jnp.where(j >= CB_GA, 1, 0)
    sgu_map = lambda j: jnp.clip(jnp.where(j <= 4, j - 1, j - 3), 0, 5)
    pool_map = lambda j: jnp.where(j >= CB_GC, 1, 0)
    return s5_map, sgu_map, pool_map


def _pick_piece(j):
    is_s5 = jnp.logical_or(j == CB_XA, j == CB_GA)
    is_pool = jnp.logical_or(j == CB_XC, j == CB_GC)
    return is_s5, is_pool, jnp.logical_not(jnp.logical_or(is_s5, is_pool))


def _inproj_bwd(dz_s5, dz_sgu, dz_pool, w, x, g, dxo, tm=512):
    t = x.shape[0]
    s5_map, sgu_map, pool_map = _dz_piece_maps()

    def body(s5_ref, sgu_ref, pool_ref, w_ref, x_ref, g_ref, dxo_ref, dx_ref, dg_ref, acc):
        m, j = pl.program_id(0), pl.program_id(1)

        @pl.when(jnp.logical_and(m == 0, j == 0))
        def _():
            dg_ref[...] = jnp.zeros_like(dg_ref)

        @pl.when(j == 0)
        def _():
            acc[...] = jnp.zeros_like(acc)

        is_s5, is_pool, is_sgu = _pick_piece(j)

        @pl.when(is_s5)
        def _():
            acc[...] += _dot(s5_ref[...], w_ref[...], NT)

        @pl.when(is_sgu)
        def _():
            acc[...] += _dot(sgu_ref[...], w_ref[...], NT)

        @pl.when(is_pool)
        def _():
            acc[...] += _dot(pool_ref[...], w_ref[...], NT)

        @pl.when(j == N_CB - 1)
        def _():
            xv = x_ref[...]
            r = lax.rsqrt(jnp.mean(xv * xv, axis=-1, keepdims=True) + RMS_EPS)
            n = xv * r
            dh = acc[...]
            dg_ref[...] += _rowsum(dh * n)
            dn = dh * g_ref[...]
            dx_ref[...] = dxo_ref[...] + r * (dn - n * jnp.mean(dn * n, axis=-1, keepdims=True))

    return pl.pallas_call(
        body,
        name="inproj_bwd",
        grid=(t // tm, N_CB),
        in_specs=[
            pl.BlockSpec((tm, CB), lambda m, j: (m, s5_map(j))),
            pl.BlockSpec((tm, CB), lambda m, j: (m, sgu_map(j))),
            pl.BlockSpec((tm, CB), lambda m, j: (m, pool_map(j))),
            pl.BlockSpec((D_MODEL, CB), lambda m, j: (0, j)),
            pl.BlockSpec((tm, D_MODEL), lambda m, j: (m, 0)),
            pl.BlockSpec((1, D_MODEL), lambda m, j: (0, 0)),
            pl.BlockSpec((tm, D_MODEL), lambda m, j: (m, 0)),
        ],
        out_specs=[
            pl.BlockSpec((tm, D_MODEL), lambda m, j: (m, 0)),
            pl.BlockSpec((1, D_MODEL), lambda m, j: (0, 0)),
        ],
        out_shape=[jax.ShapeDtypeStruct((t, D_MODEL), F32), jax.ShapeDtypeStruct((1, D_MODEL), F32)],
        scratch_shapes=[pltpu.VMEM((tm, D_MODEL), F32)],
        compiler_params=_params("arbitrary", "arbitrary"),
    )(dz_s5, dz_sgu, dz_pool, w, x, g, dxo)


def _wgrad_in(h, dz_s5, dz_sgu, dz_pool, tm=512):
    t = h.shape[0]
    s5_map, sgu_map, pool_map = _dz_piece_maps()

    def body(h_ref, s5_ref, sgu_ref, pool_ref, o_ref, acc):
        j, m = pl.program_id(0), pl.program_id(1)

        @pl.when(m == 0)
        def _():
            acc[...] = jnp.zeros_like(acc)

        is_s5, is_pool, is_sgu = _pick_piece(j)

        @pl.when(is_s5)
        def _():
            acc[...] += _dot(h_ref[...], s5_ref[...], TN)

        @pl.when(is_sgu)
        def _():
            acc[...] += _dot(h_ref[...], sgu_ref[...], TN)

        @pl.when(is_pool)
        def _():
            acc[...] += _dot(h_ref[...], pool_ref[...], TN)

        @pl.when(m == pl.num_programs(1) - 1)
        def _():
            o_ref[...] = acc[...].astype(BF16)

    return pl.pallas_call(
        body,
        name="wgrad_in",
        grid=(N_CB, t // tm),
        in_specs=[
            pl.BlockSpec((tm, D_MODEL), lambda j, m: (m, 0)),
            pl.BlockSpec((tm, CB), lambda j, m: (m, s5_map(j))),
            pl.BlockSpec((tm, CB), lambda j, m: (m, sgu_map(j))),
            pl.BlockSpec((tm, CB), lambda j, m: (m, pool_map(j))),
        ],
        out_specs=pl.BlockSpec((D_MODEL, CB), lambda j, m: (0, j)),
        out_shape=jax.ShapeDtypeStruct((D_MODEL, IN_COLS), BF16),
        scratch_shapes=[pltpu.VMEM((D_MODEL, CB), F32)],
        compiler_params=_params("arbitrary", "arbitrary"),
    )(h, dz_s5, dz_sgu, dz_pool)


def _wgrad_out(ya, yb, yc, dx, tm=512, tn=512):
    t = dx.shape[0]

    def body(ya_ref, yb_ref, yc_ref, dx_ref, o_ref, acc):
        m = pl.program_id(1)

        @pl.when(m == 0)
        def _():
            acc[...] = jnp.zeros_like(acc)

        dxb = dx_ref[...].astype(BF16)
        acc[0:S5_W, :] += _dot(ya_ref[...], dxb, TN)
        acc[S5_W:S5_W + SGU_W, :] += _dot(yb_ref[...], dxb, TN)
        acc[S5_W + SGU_W:, :] += _dot(yc_ref[...], dxb, TN)

        @pl.when(m == pl.num_programs(1) - 1)
        def _():
            o_ref[...] = acc[...].astype(BF16)

    return pl.pallas_call(
        body,
        name="wgrad_out",
        grid=(D_MODEL // tn, t // tm),
        in_specs=[
            pl.BlockSpec((tm, S5_W), lambda n, m: (m, 0)),
            pl.BlockSpec((tm, SGU_W), lambda n, m: (m, 0)),
            pl.BlockSpec((tm, POOL_W), lambda n, m: (m, 0)),
            pl.BlockSpec((tm, tn), lambda n, m: (m, n)),
        ],
        out_specs=pl.BlockSpec((D_MODEL, tn), lambda n, m: (0, n)),
        out_shape=jax.ShapeDtypeStruct((D_MODEL, D_MODEL), BF16),
        scratch_shapes=[pltpu.VMEM((D_MODEL, tn), F32)],
        compiler_params=_params("arbitrary", "arbitrary"),
    )(ya, yb, yc, dx)


def _final_loss(x, g, target, tm=512):
    t = x.shape[0]

    def body(x_ref, g_ref, t_ref, dx_ref, loss_ref, dg_ref):
        @pl.when(pl.program_id(0) == 0)
        def _():
            loss_ref[...] = jnp.zeros_like(loss_ref)
            dg_ref[...] = jnp.zeros_like(dg_ref)

        xv = x_ref[...]
        gv = g_ref[...]
        r = lax.rsqrt(jnp.mean(xv * xv, axis=-1, keepdims=True) + RMS_EPS)
        n = xv * r
        err = n * gv - t_ref[...]
        loss_ref[...] += 0.5 * jnp.sum(jnp.mean(err * err, axis=-1, keepdims=True))
        dy = err * (1.0 / D_MODEL)
        dg_ref[...] += _rowsum(dy * n)
        dn = dy * gv
        dx_ref[...] = r * (dn - n * jnp.mean(dn * n, axis=-1, keepdims=True))

    return pl.pallas_call(
        body,
        name="final_loss",
        grid=(t // tm,),
        in_specs=[
            pl.BlockSpec((tm, D_MODEL), lambda m: (m, 0)),
            pl.BlockSpec((1, D_MODEL), lambda m: (0, 0)),
            pl.BlockSpec((tm, D_MODEL), lambda m: (m, 0)),
        ],
        out_specs=[
            pl.BlockSpec((tm, D_MODEL), lambda m: (m, 0)),
            pl.BlockSpec((8, 128), lambda m: (0, 0)),
            pl.BlockSpec((1, D_MODEL), lambda m: (0, 0)),
        ],
        out_shape=[
            jax.ShapeDtypeStruct((t, D_MODEL), F32),
            jax.ShapeDtypeStruct((8, 128), F32),
            jax.ShapeDtypeStruct((1, D_MODEL), F32),
        ],
        compiler_params=_params("arbitrary"),
    )(x, g, target)


N_Q = 2 * N_STATE // LANE_CH


def _s5_scan(st, carry, a_ref, tb, reverse):
    for c0 in range(0, N_STATE, LANE_CH):
        re = pl.ds(c0, LANE_CH)
        im = pl.ds(N_STATE + c0, LANE_CH)
        a_re = a_ref[0:1, re]
        a_im = a_ref[1:2, re]

        def step(k, c, re=re, im=im, a_re=a_re, a_im=a_im):
            s_re, s_im = c
            t = (tb - 1 - k) if reverse else k
            row = pl.ds(t, 1)
            if reverse:
                n_re = st[row, re] + (a_re * s_re + a_im * s_im)
                n_im = st[row, im] + (a_re * s_im - a_im * s_re)
            else:
                n_re = st[row, re] + (a_re * s_re - a_im * s_im)
                n_im = st[row, im] + (a_re * s_im + a_im * s_re)
            st[row, re] = n_re
            st[row, im] = n_im
            return n_re, n_im

        s_re, s_im = lax.fori_loop(0, tb, step, (carry[0:1, re], carry[0:1, im]))
        carry[0:1, re] = s_re
        carry[0:1, im] = s_im


def _s5_fwd(z, bc, cc, a, dvec, wglu, bglu, tb=256):
    t = z.shape[0]

    def body(xa_ref, ga_ref, bc_ref, cc_ref, a_ref, d_ref, wglu_ref, bglu_ref, ya_ref, s_ref, ys_ref, st, carry):
        @pl.when(pl.program_id(0) == 0)
        def _():
            carry[...] = jnp.zeros_like(carry)

        xa = xa_ref[...]
        xab = xa.astype(BF16)
        for q in range(N_Q):
            cq = pl.ds(LANE_CH * q, LANE_CH)
            st[:, cq] = _dot(xab[:, 128 * (q % 4):128 * (q % 4) + 128], bc_ref[:, cq])
        _s5_scan(st, carry, a_ref, tb, reverse=False)
        sb = st[...].astype(BF16)
        s_ref[...] = sb
        cols = []
        for j in range(4):
            lo, hi = LANE_CH * j, N_STATE + LANE_CH * j
            cols.append(_dot(sb[:, lo:lo + LANE_CH], cc_ref[lo:lo + LANE_CH, :])
                        + _dot(sb[:, hi:hi + LANE_CH], cc_ref[hi:hi + LANE_CH, :]))
        ys = jnp.concatenate(cols, axis=1) + d_ref[...] * xa
        ys_ref[...] = ys
        ya1 = _gelu(ys)
        pre = _dot(ya1.astype(BF16), wglu_ref[...]) + bglu_ref[...]
        silu_ga, _ = _silu_and_grad(ga_ref[...])
        ya_ref[...] = (ya1 * jax.nn.sigmoid(pre) * silu_ga).astype(BF16)

    const = lambda shape: pl.BlockSpec(shape, lambda i: (0,) * len(shape))
    return pl.pallas_call(
        body,
        name="s5_fwd",
        grid=(t // tb,),
        in_specs=[
            pl.BlockSpec((tb, CB), lambda i: (i, CB_XA)),
            pl.BlockSpec((tb, CB), lambda i: (i, CB_GA)),
            const((128, 2 * N_STATE)),
            const((2 * N_STATE, 128)),
            const((2, N_STATE)),
            const((1, S5_W)),
            const((S5_W, S5_W)),
            const((1, S5_W)),
        ],
        out_specs=[
            pl.BlockSpec((tb, S5_W), lambda i: (i, 0)),
            pl.BlockSpec((tb, 2 * N_STATE), lambda i: (i, 0)),
            pl.BlockSpec((tb, S5_W), lambda i: (i, 0)),
        ],
        out_shape=[
            jax.ShapeDtypeStruct((t, S5_W), BF16),
            jax.ShapeDtypeStruct((t, 2 * N_STATE), BF16),
            jax.ShapeDtypeStruct((t, S5_W), F32),
        ],
        scratch_shapes=[pltpu.VMEM((tb, 2 * N_STATE), F32), pltpu.VMEM((8, 2 * N_STATE), F32)],
        compiler_params=_params("arbitrary"),
    )(z, z, bc, cc, a, dvec, wglu, bglu)


def _s5_bwd(dya, ys, z, s, bc, cc, a, dvec, wglu, bglu, tb=256):
    t = z.shape[0]
    nb = t // tb
    rev = lambda i: nb - 1 - i

    def body(dya_ref, ys_ref, xa_ref, ga_ref, s_ref, sp_ref, bc_ref, cc_ref, a_ref, d_ref, wglu_ref, bglu_ref,
             dz_ref, dbc_ref, dcct_ref, da_ref, dd_ref, dwglu_ref, dbglu_ref, g, carry):
        i = pl.program_id(0)

        @pl.when(i == 0)
        def _():
            carry[...] = jnp.zeros_like(carry)
            for r in (dbc_ref, dcct_ref, da_ref, dd_ref, dwglu_ref, dbglu_ref):
                r[...] = jnp.zeros_like(r)

        ys = ys_ref[...]
        xa = xa_ref[...]
        ga = ga_ref[...]
        dya = dya_ref[...]
        ya1 = _gelu(ys)
        ya1b = ya1.astype(BF16)
        sg = jax.nn.sigmoid(_dot(ya1b, wglu_ref[...]) + bglu_ref[...])
        silu_ga, silu_ga_grad = _silu_and_grad(ga)
        dz_ref[:, S5_W:] = (dya * (ya1 * sg) * silu_ga_grad).astype(BF16)
        dya2 = dya * silu_ga
        dpre = dya2 * ya1 * sg * (1.0 - sg)
        dbglu_ref[...] += _rowsum(dpre)
        dpreb = dpre.astype(BF16)
        dwglu_ref[...] += _dot(ya1b, dpreb, TN)
        dys = (dya2 * sg + _dot(dpreb, wglu_ref[...], NT)) * _gelu_grad(ys)
        dd_ref[...] += _rowsum(dys * xa)
        dysb = dys.astype(BF16)
        xab = xa.astype(BF16)

        for q in range(N_Q):
            cq = pl.ds(LANE_CH * q, LANE_CH)
            x0 = 128 * (q % 4)
            dcct_ref[:, cq] += _dot(dysb[:, x0:x0 + 128], s_ref[:, cq], TN)
            g[:, cq] = _dot(dysb[:, x0:x0 + 128], cc_ref[cq, :], NT)
        _s5_scan(g, carry, a_ref, tb, reverse=True)

        first = (lax.broadcasted_iota(jnp.int32, (tb, LANE_CH), 0) == 0)
        have_prev = i < nb - 1
        dxa_cols = []
        for j in range(4):
            re = pl.ds(LANE_CH * j, LANE_CH)
            im = pl.ds(N_STATE + LANE_CH * j, LANE_CH)
            x0 = 128 * j
            g_re, g_im = g[:, re], g[:, im]
            p_re = jnp.where(have_prev, sp_ref[:, re].astype(F32)[HALO - 1:HALO, :], 0.0)
            p_im = jnp.where(have_prev, sp_ref[:, im].astype(F32)[HALO - 1:HALO, :], 0.0)
            sp_re = jnp.where(first, p_re, pltpu.roll(s_ref[:, re].astype(F32), 1, 0))
            sp_im = jnp.where(first, p_im, pltpu.roll(s_ref[:, im].astype(F32), 1, 0))
            da_ref[0:1, re] += _rowsum(sp_re * g_re + sp_im * g_im)
            da_ref[1:2, re] += _rowsum(sp_re * g_im - sp_im * g_re)
            gb_re, gb_im = g_re.astype(BF16), g_im.astype(BF16)
            dbc_ref[:, re] += _dot(xab[:, x0:x0 + 128], gb_re, TN)
            dbc_ref[:, im] += _dot(xab[:, x0:x0 + 128], gb_im, TN)
            dxa_cols.append(_dot(gb_re, bc_ref[:, re], NT) + _dot(gb_im, bc_ref[:, im], NT))
        dz_ref[:, 0:S5_W] = (dys * d_ref[...] + jnp.concatenate(dxa_cols, axis=1)).astype(BF16)

    const = lambda shape: pl.BlockSpec(shape, lambda i: (0,) * len(shape))
    per_halo = tb // HALO
    return pl.pallas_call(
        body,
        name="s5_bwd",
        grid=(nb,),
        in_specs=[
            pl.BlockSpec((tb, S5_W), lambda i: (rev(i), 0)),
            pl.BlockSpec((tb, S5_W), lambda i: (rev(i), 0)),
            pl.BlockSpec((tb, CB), lambda i: (rev(i), CB_XA)),
            pl.BlockSpec((tb, CB), lambda i: (rev(i), CB_GA)),
            pl.BlockSpec((tb, 2 * N_STATE), lambda i: (rev(i), 0)),
            pl.BlockSpec((HALO, 2 * N_STATE), lambda i: (jnp.maximum(rev(i) * per_halo - 1, 0), 0)),
            const((128, 2 * N_STATE)),
            const((2 * N_STATE, 128)),
            const((2, N_STATE)),
            const((1, S5_W)),
            const((S5_W, S5_W)),
            const((1, S5_W)),
        ],
        out_specs=[
            pl.BlockSpec((tb, 2 * CB), lambda i: (rev(i), 0)),
            const((128, 2 * N_STATE)),
            const((128, 2 * N_STATE)),
            const((2, N_STATE)),
            const((1, S5_W)),
            const((S5_W, S5_W)),
            const((1, S5_W)),
        ],
        out_shape=[
            jax.ShapeDtypeStruct((t, 2 * CB), BF16),
            jax.ShapeDtypeStruct((128, 2 * N_STATE), F32),
            jax.ShapeDtypeStruct((128, 2 * N_STATE), F32),
            jax.ShapeDtypeStruct((2, N_STATE), F32),
            jax.ShapeDtypeStruct((1, S5_W), F32),
            jax.ShapeDtypeStruct((S5_W, S5_W), F32),
            jax.ShapeDtypeStruct((1, S5_W), F32),
        ],
        scratch_shapes=[pltpu.VMEM((tb, 2 * N_STATE), F32), pltpu.VMEM((8, 2 * N_STATE), F32)],
        compiler_params=_params("arbitrary"),
    )(dya, ys, z, z, s, s, bc, cc, a, dvec, wglu, bglu)


def _sgu_norm(v0, v1, lng_ref, lnb_ref):
    g0, g1 = _gelu(v0), _gelu(v1)
    mu = (jnp.sum(g0, axis=-1, keepdims=True) + jnp.sum(g1, axis=-1, keepdims=True)) * (1.0 / SGU_W)
    c0, c1 = g0 - mu, g1 - mu
    var = (jnp.sum(c0 * c0, axis=-1, keepdims=True) + jnp.sum(c1 * c1, axis=-1, keepdims=True)) * (1.0 / SGU_W)
    rstd = lax.rsqrt(var + LN_EPS)
    vh0, vh1 = c0 * rstd, c1 * rstd
    vn0 = vh0 * lng_ref[:, 0:CB] + lnb_ref[:, 0:CB]
    vn1 = vh1 * lng_ref[:, CB:] + lnb_ref[:, CB:]
    return (vh0, vh1), (vn0, vn1), rstd


def _sgu_fwd(z, lng, lnb, ws, bsx, tb=256):
    t = z.shape[0]

    def body(u0_ref, u1_ref, v0_ref, v1_ref, gb0_ref, gb1_ref, lng_ref, lnb_ref, ws_ref, bsx_ref, yb_ref):
        _, (vn0, vn1), _ = _sgu_norm(v0_ref[...], v1_ref[...], lng_ref, lnb_ref)
        for half, (vn, u_ref, gb_ref) in enumerate(((vn0, u0_ref, gb0_ref), (vn1, u1_ref, gb1_ref))):
            vnb = vn.astype(BF16)
            silu_gb, _ = _silu_and_grad(gb_ref[...])
            gate = _gelu(u_ref[...]) * silu_gb
            for hh in range(4):
                h = 4 * half + hh
                for c in range(tb // CHUNK):
                    rows, cols = slice(CHUNK * c, CHUNK * (c + 1)), slice(128 * hh, 128 * (hh + 1))
                    sp = _dot(ws_ref[h], vnb[rows, cols]) + bsx_ref[h]
                    yb_ref[rows, CB * half + 128 * hh:CB * half + 128 * (hh + 1)] = (gate[rows, cols] * sp).astype(BF16)

    zb = lambda j: pl.BlockSpec((tb, CB), lambda i, j=j: (i, j))
    const = lambda shape: pl.BlockSpec(shape, lambda i: (0,) * len(shape))
    return pl.pallas_call(
        body,
        name="sgu_fwd",
        grid=(t // tb,),
        in_specs=[zb(CB_U), zb(CB_U + 1), zb(CB_V), zb(CB_V + 1), zb(CB_GB), zb(CB_GB + 1),
                  const((1, SGU_W)), const((1, SGU_W)), const((SGU_HEADS, CHUNK, CHUNK)), const((SGU_HEADS, CHUNK, 128))],
        out_specs=pl.BlockSpec((tb, SGU_W), lambda i: (i, 0)),
        out_shape=jax.ShapeDtypeStruct((t, SGU_W), BF16),
        compiler_params=_params("arbitrary"),
    )(z, z, z, z, z, z, lng, lnb, ws, bsx)


def _sgu_bwd(dyb, z, lng, lnb, ws, wst, bsx, tb=256):
    t = z.shape[0]

    def body(dyb_ref, u0_ref, u1_ref, v0_ref, v1_ref, gb0_ref, gb1_ref, lng_ref, lnb_ref, ws_ref, wst_ref, bsx_ref,
             dz_ref, dlng_ref, dlnb_ref, dws_ref, dbs_ref, dvn):
        @pl.when(pl.program_id(0) == 0)
        def _():
            for r in (dlng_ref, dlnb_ref, dws_ref, dbs_ref):
                r[...] = jnp.zeros_like(r)

        v0, v1 = v0_ref[...], v1_ref[...]
        (vh0, vh1), (vn0, vn1), rstd = _sgu_norm(v0, v1, lng_ref, lnb_ref)
        causal = (lax.broadcasted_iota(jnp.int32, (CHUNK, CHUNK), 0) >= lax.broadcasted_iota(jnp.int32, (CHUNK, CHUNK), 1))
        for half, (vn, u_ref, gb_ref) in enumerate(((vn0, u0_ref, gb0_ref), (vn1, u1_ref, gb1_ref))):
            vnb = vn.astype(BF16)
            u = u_ref[...]
            ug = _gelu(u)
            silu_gb, silu_gb_grad = _silu_and_grad(gb_ref[...])
            dyb = dyb_ref[:, CB * half:CB * (half + 1)]
            dyb0 = dyb * silu_gb
            ds = dyb0 * ug
            sp_cols = []
            for hh in range(4):
                h = 4 * half + hh
                cols = slice(128 * hh, 128 * (hh + 1))
                sp_rows = []
                for c in range(tb // CHUNK):
                    rows = slice(CHUNK * c, CHUNK * (c + 1))
                    vt = vnb[rows, cols]
                    sp_rows.append(_dot(ws_ref[h], vt) + bsx_ref[h])
                    dst = ds[rows, cols]
                    dstb = dst.astype(BF16)
                    dbs_ref[h] += dst
                    dws_ref[h] += jnp.where(causal, _dot(dstb, vt, NT), 0.0)
                    dvn[rows, CB * half + 128 * hh:CB * half + 128 * (hh + 1)] = _dot(wst_ref[h], dstb)
                sp_cols.append(jnp.concatenate(sp_rows, axis=0))
            sp = jnp.concatenate(sp_cols, axis=1)
            dz_ref[:, CB * half:CB * (half + 1)] = (dyb0 * sp * _gelu_grad(u)).astype(BF16)
            dz_ref[:, 2 * SGU_W + CB * half:2 * SGU_W + CB * (half + 1)] = (dyb * (ug * sp) * silu_gb_grad).astype(BF16)

        dvn0, dvn1 = dvn[:, 0:CB], dvn[:, CB:]
        dlng_ref[:, 0:CB] += _rowsum(dvn0 * vh0)
        dlng_ref[:, CB:] += _rowsum(dvn1 * vh1)
        dlnb_ref[:, 0:CB] += _rowsum(dvn0)
        dlnb_ref[:, CB:] += _rowsum(dvn1)
        dh0, dh1 = dvn0 * lng_ref[:, 0:CB], dvn1 * lng_ref[:, CB:]
        m1 = (jnp.sum(dh0, axis=-1, keepdims=True) + jnp.sum(dh1, axis=-1, keepdims=True)) * (1.0 / SGU_W)
        m2 = (jnp.sum(dh0 * vh0, axis=-1, keepdims=True) + jnp.sum(dh1 * vh1, axis=-1, keepdims=True)) * (1.0 / SGU_W)
        dz_ref[:, SGU_W:SGU_W + CB] = (rstd * (dh0 - m1 - vh0 * m2) * _gelu_grad(v0)).astype(BF16)
        dz_ref[:, SGU_W + CB:2 * SGU_W] = (rstd * (dh1 - m1 - vh1 * m2) * _gelu_grad(v1)).astype(BF16)

    zb = lambda j: pl.BlockSpec((tb, CB), lambda i, j=j: (i, j))
    const = lambda shape: pl.BlockSpec(shape, lambda i: (0,) * len(shape))
    hmat = (SGU_HEADS, CHUNK, CHUNK)
    return pl.pallas_call(
        body,
        name="sgu_bwd",
        grid=(t // tb,),
        in_specs=[pl.BlockSpec((tb, SGU_W), lambda i: (i, 0)),
                  zb(CB_U), zb(CB_U + 1), zb(CB_V), zb(CB_V + 1), zb(CB_GB), zb(CB_GB + 1),
                  const((1, SGU_W)), const((1, SGU_W)), const(hmat), const(hmat), const(hmat)],
        out_specs=[pl.BlockSpec((tb, 3 * SGU_W), lambda i: (i, 0)),
                   const((1, SGU_W)), const((1, SGU_W)), const(hmat), const(hmat)],
        out_shape=[jax.ShapeDtypeStruct((t, 3 * SGU_W), BF16),
                   jax.ShapeDtypeStruct((1, SGU_W), F32), jax.ShapeDtypeStruct((1, SGU_W), F32),
                   jax.ShapeDtypeStruct(hmat, F32), jax.ShapeDtypeStruct(hmat, F32)],
        scratch_shapes=[pltpu.VMEM((tb, SGU_W), F32)],
        compiler_params=_params("arbitrary"),
    )(dyb, z, z, z, z, z, z, lng, lnb, ws, wst, bsx)


def _window_sums(ext, lookahead):
    n = ext.shape[0]
    out = []
    for gi, w in enumerate(POOL_WINDOWS):
        acc = ext[:, 128 * gi:128 * (gi + 1)]
        k = 1
        while k < w:
            acc = acc + pltpu.roll(acc, (n - k) if lookahead else k, 0)
            k *= 2
        out.append(acc)
    return jnp.concatenate(out, axis=1)


def _pool_counts(row0, tb):
    pos = (row0 + 1 + lax.broadcasted_iota(jnp.int32, (tb, POOL_W), 0)).astype(F32)
    lane = lax.broadcasted_iota(jnp.int32, (tb, POOL_W), 1)
    win = jnp.where(lane < 128, 2.0, jnp.where(lane < 256, 4.0, jnp.where(lane < 384, 8.0, 16.0)))
    return jnp.minimum(pos, win)


def _pool_fwd(z, wpool, scale, tb=256):
    t = z.shape[0]

    def body(xc_ref, gc_ref, wp_ref, sc_ref, yc_ref, halo):
        i = pl.program_id(0)

        @pl.when(i == 0)
        def _():
            halo[...] = jnp.zeros_like(halo)

        xc = xc_ref[...]
        sums = _window_sums(jnp.concatenate([halo[...], xc], axis=0), lookahead=False)[HALO:, :]
        halo[...] = xc[tb - HALO:, :]
        pb = (sums / _pool_counts(i * tb, tb) - xc).astype(BF16)
        q = jnp.concatenate([_dot(pb[:, 128 * gi:128 * (gi + 1)], wp_ref[gi]) for gi in range(4)], axis=1)
        silu_gc, _ = _silu_and_grad(gc_ref[...])
        yc_ref[...] = (q * sc_ref[...] * silu_gc).astype(BF16)

    const = lambda shape: pl.BlockSpec(shape, lambda i: (0,) * len(shape))
    return pl.pallas_call(
        body,
        name="pool_fwd",
        grid=(t // tb,),
        in_specs=[pl.BlockSpec((tb, CB), lambda i: (i, CB_XC)), pl.BlockSpec((tb, CB), lambda i: (i, CB_GC)),
                  const((4, 128, 128)), const((1, POOL_W))],
        out_specs=pl.BlockSpec((tb, POOL_W), lambda i: (i, 0)),
        out_shape=jax.ShapeDtypeStruct((t, POOL_W), BF16),
        scratch_shapes=[pltpu.VMEM((HALO, POOL_W), F32)],
        compiler_params=_params("arbitrary"),
    )(z, z, wpool, scale)


def _pool_bwd(dyc, z, wpool, scale, tb=256):
    t = z.shape[0]
    nb = t // tb
    rev = lambda i: nb - 1 - i
    per_halo = tb // HALO

    def body(dyc_ref, xc_ref, xp_ref, gc_ref, wp_ref, sc_ref, dz_ref, dwp_ref, dsc_ref, ehalo):
        i = pl.program_id(0)

        @pl.when(i == 0)
        def _():
            ehalo[...] = jnp.zeros_like(ehalo)
            dwp_ref[...] = jnp.zeros_like(dwp_ref)
            dsc_ref[...] = jnp.zeros_like(dsc_ref)

        xc = xc_ref[...]
        prev = jnp.where(i < nb - 1, xp_ref[...], 0.0)
        sums = _window_sums(jnp.concatenate([prev, xc], axis=0), lookahead=False)[HALO:, :]
        cnt = _pool_counts(rev(i) * tb, tb)
        pb = (sums / cnt - xc).astype(BF16)
        q = jnp.concatenate([_dot(pb[:, 128 * gi:128 * (gi + 1)], wp_ref[gi]) for gi in range(4)], axis=1)
        silu_gc, silu_gc_grad = _silu_and_grad(gc_ref[...])
        dyc = dyc_ref[...]
        dz_ref[:, POOL_W:] = (dyc * (q * sc_ref[...]) * silu_gc_grad).astype(BF16)
        dyc0 = dyc * silu_gc
        dsc_ref[...] += _rowsum(dyc0 * q)
        dqb = (dyc0 * sc_ref[...]).astype(BF16)
        dp_cols = []
        for gi in range(4):
            cols = slice(128 * gi, 128 * (gi + 1))
            dwp_ref[gi] += _dot(pb[:, cols], dqb[:, cols], TN)
            dp_cols.append(_dot(dqb[:, cols], wp_ref[gi], NT))
        dp = jnp.concatenate(dp_cols, axis=1)
        e = dp / cnt
        fut = _window_sums(jnp.concatenate([e, ehalo[...]], axis=0), lookahead=True)[:tb, :]
        ehalo[...] = e[:HALO, :]
        dz_ref[:, 0:POOL_W] = (fut - dp).astype(BF16)

    const = lambda shape: pl.BlockSpec(shape, lambda i: (0,) * len(shape))
    return pl.pallas_call(
        body,
        name="pool_bwd",
        grid=(nb,),
        in_specs=[pl.BlockSpec((tb, POOL_W), lambda i: (rev(i), 0)),
                  pl.BlockSpec((tb, CB), lambda i: (rev(i), CB_XC)),
                  pl.BlockSpec((HALO, CB), lambda i: (jnp.maximum(rev(i) * per_halo - 1, 0), CB_XC)),
                  pl.BlockSpec((tb, CB), lambda i: (rev(i), CB_GC)),
                  const((4, 128, 128)), const((1, POOL_W))],
        out_specs=[pl.BlockSpec((tb, 2 * POOL_W), lambda i: (rev(i), 0)), const((4, 128, 128)), const((1, POOL_W))],
        out_shape=[jax.ShapeDtypeStruct((t, 2 * POOL_W), BF16),
                   jax.ShapeDtypeStruct((4, 128, 128), F32), jax.ShapeDtypeStruct((1, POOL_W), F32)],
        scratch_shapes=[pltpu.VMEM((HALO, POOL_W), F32)],
        compiler_params=_params("arbitrary"),
    )(dyc, z, z, z, wpool, scale)


def _adamw(w, m, v, parts, tr, name):
    r, c = w.shape

    def body(w_ref, m_ref, v_ref, p_ref, g_ref, d_ref, nm_ref, nv_ref):
        g = p_ref[0].astype(F32)
        for k in range(1, N_DEV):
            g = g + p_ref[k].astype(F32)
        nm = ADAM_B1 * m_ref[...] + (1.0 - ADAM_B1) * g
        nv = ADAM_B2 * v_ref[...] + (1.0 - ADAM_B2) * (g * g)
        m_hat = nm / (1.0 - ADAM_B1 ** ADAM_STEP)
        v_hat = nv / (1.0 - ADAM_B2 ** ADAM_STEP)
        g_ref[...] = g
        nm_ref[...] = nm
        nv_ref[...] = nv
        d_ref[...] = -ADAM_LR * (m_hat / (jnp.sqrt(v_hat) + ADAM_EPS) + ADAM_WD * w_ref[...])

    blk = pl.BlockSpec((tr, c), lambda i: (i, 0))
    return pl.pallas_call(
        body,
        name=name,
        grid=(r // tr,),
        in_specs=[blk, blk, blk, pl.BlockSpec((N_DEV, tr, c), lambda i: (0, i, 0))],
        out_specs=[blk, blk, blk, blk],
        out_shape=[jax.ShapeDtypeStruct((r, c), F32)] * 4,
        compiler_params=_params("arbitrary"),
    )(w, m, v, parts)


MESH = pl.DeviceIdType.MESH
ANY = pl.BlockSpec(memory_space=pl.ANY)


def _dev_index(dev):
    return 4 * dev[0] + 2 * dev[1] + dev[2]


def _allgather_weights(wi, wo, wg):
    n_t = 3

    def body(wi_ref, wo_ref, wg_ref, gi_ref, go_ref, gg_ref, send_sems, recv_sems, local_sems):
        x, y, c = lax.axis_index("x"), lax.axis_index("y"), lax.axis_index("c")
        me, sibling = (x, y, c), (x, y, 1 - c)
        chips = [(1 - x, y), (x, 1 - y), (1 - x, 1 - y)]
        shards = (wi_ref, wo_ref, wg_ref)

        def slot(ti, dev):
            idx = _dev_index(dev)
            if ti == 0:
                return gi_ref.at[:, :, pl.ds(pl.multiple_of(idx * SH_IN, 128), SH_IN)]
            if ti == 1:
                return go_ref.at[:, pl.ds(pl.multiple_of(idx * SH_OUT, SH_OUT), SH_OUT), :]
            return gg_ref.at[:, pl.ds(pl.multiple_of(idx * SH_GLU, SH_GLU), SH_GLU), :]

        def copy(k, ti, block, to, own=False):
            return pltpu.make_async_remote_copy(
                src_ref=shards[ti] if own else slot(ti, block), dst_ref=slot(ti, block),
                send_sem=send_sems.at[n_t * k + ti], recv_sem=recv_sems.at[n_t * k + ti],
                device_id=to, device_id_type=MESH)

        mine = [pltpu.make_async_copy(shards[ti], slot(ti, me), local_sems.at[ti]) for ti in range(n_t)]
        for cp in mine:
            cp.start()
        first = [copy(0, ti, me, sibling, own=True) for ti in range(n_t)]
        first += [copy(1 + j, ti, me, (*chip, c), own=True) for j, chip in enumerate(chips) for ti in range(n_t)]
        for cp in first:
            cp.start()
        passed = []
        for j, chip in enumerate(chips):
            for ti in range(n_t):
                copy(1 + j, ti, (*chip, c), me).wait_recv()
            onward = [copy(4 + j, ti, (*chip, c), sibling) for ti in range(n_t)]
            for cp in onward:
                cp.start()
            passed += onward
        for ti in range(n_t):
            copy(0, ti, sibling, me).wait_recv()
        for j, chip in enumerate(chips):
            for ti in range(n_t):
                copy(4 + j, ti, (*chip, 1 - c), me).wait_recv()
        for cp in first + passed:
            cp.wait_send()
        for cp in mine:
            cp.wait()

    return pl.pallas_call(
        body,
        name="allgather_weights",
        in_specs=[ANY, ANY, ANY],
        out_specs=[ANY, ANY, ANY],
        out_shape=[
            jax.ShapeDtypeStruct((DEPTH, D_MODEL, IN_COLS), BF16),
            jax.ShapeDtypeStruct((DEPTH, D_MODEL, D_MODEL), BF16),
            jax.ShapeDtypeStruct((DEPTH, S5_W, S5_W), BF16),
        ],
        scratch_shapes=[pltpu.SemaphoreType.DMA((7 * n_t,)), pltpu.SemaphoreType.DMA((7 * n_t,)),
                        pltpu.SemaphoreType.DMA((n_t,))],
    )(wi, wo, wg)


def _exchange_grads(dwi, dwo, dwg, small):
    n_t = 3 * DEPTH + 1
    rows_small = small.shape[0]

    def body(dwi0, dwi1, dwo0, dwo1, dwg0, dwg1, sm, r_in, r_out, r_glu, r_small, send_sems, recv_sems, local_sems):
        x, y, c = lax.axis_index("x"), lax.axis_index("y"), lax.axis_index("c")
        me_idx = _dev_index((x, y, c))

        def src(j, dev_idx):
            if j < 2:
                return (dwi0, dwi1)[j].at[:, pl.ds(pl.multiple_of(dev_idx * SH_IN, 128), SH_IN)]
            if j < 4:
                return (dwo0, dwo1)[j - 2].at[pl.ds(pl.multiple_of(dev_idx * SH_OUT, SH_OUT), SH_OUT), :]
            if j < 6:
                return (dwg0, dwg1)[j - 4].at[pl.ds(pl.multiple_of(dev_idx * SH_GLU, SH_GLU), SH_GLU), :]
            return sm

        def dst(j, from_idx):
            if j < 2:
                return r_in.at[from_idx, j]
            if j < 4:
                return r_out.at[from_idx, j - 2]
            if j < 6:
                return r_glu.at[from_idx, j - 4]
            return r_small.at[from_idx]

        def peer(mask):
            return (1 - x if mask & 4 else x, 1 - y if mask & 2 else y, 1 - c if mask & 1 else c)

        local = [pltpu.make_async_copy(src(j, me_idx), dst(j, me_idx), local_sems.at[j]) for j in range(n_t)]
        for cp in local:
            cp.start()
        sends = []
        for mask in range(1, N_DEV):
            p_idx = _dev_index(peer(mask))
            for j in range(n_t):
                k = (mask - 1) * n_t + j
                sends.append(pltpu.make_async_remote_copy(
                    src_ref=src(j, p_idx), dst_ref=dst(j, me_idx), send_sem=send_sems.at[k], recv_sem=recv_sems.at[k],
                    device_id=peer(mask), device_id_type=MESH))
        for cp in sends:
            cp.start()
        for mask in range(1, N_DEV):
            p_idx = _dev_index(peer(mask))
            for j in range(n_t):
                k = (mask - 1) * n_t + j
                pltpu.make_async_remote_copy(
                    src_ref=src(j, p_idx), dst_ref=dst(j, p_idx), send_sem=send_sems.at[k], recv_sem=recv_sems.at[k],
                    device_id=peer(mask), device_id_type=MESH).wait_recv()
        for cp in sends:
            cp.wait_send()
        for cp in local:
            cp.wait()

    n_remote = (N_DEV - 1) * n_t
    return pl.pallas_call(
        body,
        name="exchange_grads",
        in_specs=[ANY] * n_t,
        out_specs=[ANY] * 4,
        out_shape=[
            jax.ShapeDtypeStruct((N_DEV, DEPTH, D_MODEL, SH_IN), BF16),
            jax.ShapeDtypeStruct((N_DEV, DEPTH, SH_OUT, D_MODEL), BF16),
            jax.ShapeDtypeStruct((N_DEV, DEPTH, SH_GLU, S5_W), BF16),
            jax.ShapeDtypeStruct((N_DEV, rows_small, 128), F32),
        ],
        scratch_shapes=[pltpu.SemaphoreType.DMA((n_remote,)), pltpu.SemaphoreType.DMA((n_remote,)),
                        pltpu.SemaphoreType.DMA((n_t,))],
    )(dwi[0], dwi[1], dwo[0], dwo[1], dwg[0], dwg[1], small)


def _s5_prep(lam_re, lam_im, b_re, b_im, c_re, c_im, d_skip, log_dt):
    dt = jnp.exp(log_dt)[:, None]
    mag = jnp.exp(lam_re * dt)
    a_re, a_im = mag * jnp.cos(lam_im * dt), mag * jnp.sin(lam_im * dt)
    den = lam_re * lam_re + lam_im * lam_im
    f_re = ((a_re - 1.0) * lam_re + a_im * lam_im) / den
    f_im = (a_im * lam_re - (a_re - 1.0) * lam_im) / den
    bb_re = f_re[..., None] * b_re - f_im[..., None] * b_im
    bb_im = f_re[..., None] * b_im + f_im[..., None] * b_re
    eye = jnp.eye(8, dtype=F32)

    def in_map(bb):
        return jnp.einsum("jgph,gk->ghjkp", bb.reshape(4, 8, S5_STATE, S5_CH), eye).reshape(128, N_STATE)

    def out_map(cm):
        return jnp.einsum("jghp,gk->ghjkp", cm.reshape(4, 8, S5_CH, S5_STATE), eye).reshape(128, N_STATE)

    a = jnp.stack([a_re.reshape(-1), a_im.reshape(-1)])
    bc = jnp.concatenate([in_map(bb_re), in_map(bb_im)], axis=1)
    cct = jnp.concatenate([out_map(c_re), -out_map(c_im)], axis=1)
    return a, bc, cct, d_skip.reshape(1, S5_W)


WEIGHTS = ["norm_g", "w_in", "lam_re", "lam_im", "b_re", "b_im", "c_re", "c_im", "d_skip", "log_dt", "w_glu", "b_glu",
           "ln_g", "ln_b", "w_s", "b_s", "w_pool", "pool_scale", "w_out", "final_g"]
SHARDED = ("w_in", "w_glu", "w_out")
SMALL = [n for n in WEIGHTS if n not in SHARDED]
INPUTS = ["x"] + WEIGHTS + ["loss_target"] + ["m_" + n for n in WEIGHTS] + ["v_" + n for n in WEIGHTS]
SMALL_TILE = 8 * 128


def _pack_small(arrays):
    flat = jnp.concatenate([a.reshape(-1) for a in arrays])
    pad = (-flat.shape[0]) % SMALL_TILE
    return jnp.pad(flat, (0, pad)).reshape(-1, 128)


def _unpack_small(packed, like):
    flat = packed.reshape(-1)
    out, off = [], 0
    for a in like:
        out.append(flat[off:off + a.size].reshape(a.shape))
        off += a.size
    return out


def _local_grads(p, x, target, wi_full, wo_full, wg_full):
    row = lambda v: v.reshape(1, -1)
    causal = jnp.tril(jnp.ones((CHUNK, CHUNK), dtype=bool))
    xs, saved = [x], []
    for l in range(DEPTH):
        (a, bc, cct, dvec), prep_vjp = jax.vjp(
            _s5_prep, p["lam_re"][l], p["lam_im"][l], p["b_re"][l], p["b_im"][l], p["c_re"][l], p["c_im"][l],
            p["d_skip"][l], p["log_dt"][l])
        ws_f32 = jnp.where(causal[None], p["w_s"][l], 0.0)
        c = dict(
            a=a, bc=bc.astype(BF16), cc=cct.T.astype(BF16), dvec=dvec, prep_vjp=prep_vjp,
            ws=ws_f32.astype(BF16), wst=jnp.swapaxes(ws_f32, 1, 2).astype(BF16),
            bsx=jnp.broadcast_to(p["b_s"][l][:, :, None], (SGU_HEADS, CHUNK, 128)),
            wpool=p["w_pool"][l].astype(BF16), scale=row(p["pool_scale"][l]),
            lng=row(p["ln_g"][l]), lnb=row(p["ln_b"][l]), bglu=row(p["b_glu"][l]), norm_g=row(p["norm_g"][l]))
        c["z"], c["h"] = _rms_inproj(xs[l], c["norm_g"], wi_full[l])
        c["ya"], c["s"], c["ys"] = _s5_fwd(c["z"], c["bc"], c["cc"], a, dvec, wg_full[l], c["bglu"])
        c["yb"] = _sgu_fwd(c["z"], c["lng"], c["lnb"], c["ws"], c["bsx"])
        c["yc"] = _pool_fwd(c["z"], c["wpool"], c["scale"])
        xs.append(_outproj(xs[l], c["ya"], c["yb"], c["yc"], wo_full[l]))
        saved.append(c)

    dx, loss_tile, dfinal = _final_loss(xs[DEPTH], row(p["final_g"]), target)
    small = {n: [None] * DEPTH for n in SMALL if n != "final_g"}
    dwi, dwo, dwg = [None] * DEPTH, [None] * DEPTH, [None] * DEPTH
    for l in reversed(range(DEPTH)):
        c = saved[l]
        dya, dyb, dyc = _outproj_bwd(dx, wo_full[l])
        dwo[l] = _wgrad_out(c["ya"], c["yb"], c["yc"], dx)
        dz_s5, dbc, dcct, da, dd, dwg[l], dbglu = _s5_bwd(
            dya, c["ys"], c["z"], c["s"], c["bc"], c["cc"], c["a"], c["dvec"], wg_full[l], c["bglu"])
        dz_sgu, dlng, dlnb, dws, dbsx = _sgu_bwd(dyb, c["z"], c["lng"], c["lnb"], c["ws"], c["wst"], c["bsx"])
        dz_pool, dwp, dsc = _pool_bwd(dyc, c["z"], c["wpool"], c["scale"])
        dwi[l] = _wgrad_in(c["h"], dz_s5, dz_sgu, dz_pool)
        dx, dnorm = _inproj_bwd(dz_s5, dz_sgu, dz_pool, wi_full[l], xs[l], c["norm_g"], dx)
        g_lam_re, g_lam_im, g_b_re, g_b_im, g_c_re, g_c_im, g_d, g_dt = c["prep_vjp"]((da, dbc, dcct, dd))
        for n, v in (("norm_g", dnorm.reshape(-1)), ("lam_re", g_lam_re), ("lam_im", g_lam_im), ("b_re", g_b_re),
                     ("b_im", g_b_im), ("c_re", g_c_re), ("c_im", g_c_im), ("d_skip", g_d), ("log_dt", g_dt),
                     ("b_glu", dbglu.reshape(-1)), ("ln_g", dlng.reshape(-1)), ("ln_b", dlnb.reshape(-1)),
                     ("w_s", dws), ("b_s", jnp.sum(dbsx, axis=-1)), ("w_pool", dwp), ("pool_scale", dsc.reshape(-1))):
            small[n][l] = v
    small = {n: jnp.stack(v) for n, v in small.items()}
    small["final_g"] = dfinal.reshape(-1)
    return loss_tile[0, 0], dx, dwi, dwo, dwg, small


def kernel(x, norm_g, w_in, lam_re, lam_im, b_re, b_im, c_re, c_im, d_skip, log_dt, w_glu, b_glu, ln_g, ln_b, w_s, b_s, w_pool, pool_scale, w_out, final_g, loss_target, m_norm_g, m_w_in, m_lam_re, m_lam_im, m_b_re, m_b_im, m_c_re, m_c_im, m_d_skip, m_log_dt, m_w_glu, m_b_glu, m_ln_g, m_ln_b, m_w_s, m_b_s, m_w_pool, m_pool_scale, m_w_out, m_final_g, v_norm_g, v_w_in, v_lam_re, v_lam_im, v_b_re, v_b_im, v_c_re, v_c_im, v_d_skip, v_log_dt, v_w_glu, v_b_glu, v_ln_g, v_ln_b, v_w_s, v_b_s, v_w_pool, v_pool_scale, v_w_out, v_final_g):
    p = dict(zip(INPUTS, (x, norm_g, w_in, lam_re, lam_im, b_re, b_im, c_re, c_im, d_skip, log_dt, w_glu, b_glu, ln_g, ln_b, w_s, b_s, w_pool, pool_scale, w_out, final_g, loss_target, m_norm_g, m_w_in, m_lam_re, m_lam_im, m_b_re, m_b_im, m_c_re, m_c_im, m_d_skip, m_log_dt, m_w_glu, m_b_glu, m_ln_g, m_ln_b, m_w_s, m_b_s, m_w_pool, m_pool_scale, m_w_out, m_final_g, v_norm_g, v_w_in, v_lam_re, v_lam_im, v_b_re, v_b_im, v_c_re, v_c_im, v_d_skip, v_log_dt, v_w_glu, v_b_glu, v_ln_g, v_ln_b, v_w_s, v_b_s, v_w_pool, v_pool_scale, v_w_out, v_final_g)))

    wi_full, wo_full, wg_full = _allgather_weights(w_in.astype(BF16), w_out.astype(BF16), w_glu.astype(BF16))
    loss_local, dx, dwi, dwo, dwg, small = _local_grads(p, x[0], loss_target[0], wi_full, wo_full, wg_full)
    loss = lax.psum(loss_local, ("x", "y", "c"))

    r_in, r_out, r_glu, r_small = _exchange_grads(
        dwi, dwo, [g.astype(BF16) for g in dwg], _pack_small([small[n] for n in SMALL]))

    out = {}

    def adam(name, shape2d, parts, tr):
        res = _adamw(p[name].reshape(shape2d), p["m_" + name].reshape(shape2d), p["v_" + name].reshape(shape2d),
                     parts.reshape((N_DEV,) + shape2d), tr, "adamw_" + name)
        out[name] = [r.reshape(p[name].shape) for r in res]

    adam("w_in", (DEPTH * D_MODEL, SH_IN), r_in, 256)
    adam("w_out", (DEPTH * SH_OUT, D_MODEL), r_out, 128)
    adam("w_glu", (DEPTH * SH_GLU, S5_W), r_glu, 128)
    like = [p[n] for n in SMALL]
    rows_small = r_small.shape[1]
    packed = _adamw(_pack_small(like), _pack_small([p["m_" + n] for n in SMALL]), _pack_small([p["v_" + n] for n in SMALL]),
                    r_small, rows_small // 8, "adamw_small")
    for n, vals in zip(SMALL, zip(*[_unpack_small(r, like) for r in packed])):
        out[n] = list(vals)

    return (loss, dx[None], *[out[n][0] for n in WEIGHTS], *[out[n][1] for n in WEIGHTS],
            *[out[n][2] for n in WEIGHTS], *[out[n][3] for n in WEIGHTS])
```

```python
import functools
import math

import jax
import jax.numpy as jnp
from jax import lax
from jax.experimental import pallas as pl
from jax.experimental.pallas import tpu as pltpu

F32 = jnp.float32
BF16 = jnp.bfloat16

D_MODEL = 2048
DEPTH = 2
S5_W, SGU_W, POOL_W = 512, 1024, 512
S5_GROUPS, S5_STATE, S5_CH = 32, 64, 16
N_STATE = S5_GROUPS * S5_STATE
CHUNK = 128
SGU_HEADS = 8
POOL_WINDOWS = (2, 4, 8, 16)
IN_COLS = 5120
RMS_EPS = 1e-6
LN_EPS = 1e-5
ADAM_LR, ADAM_B1, ADAM_B2, ADAM_EPS, ADAM_WD, ADAM_STEP = 0.001, 0.9, 0.999, 1e-08, 0.01, 10

CB = 512
N_CB = IN_COLS // CB
CB_XA, CB_U, CB_V, CB_XC, CB_GA, CB_GB, CB_GC = 0, 1, 3, 5, 6, 7, 9

N_DEV = 8
SH_IN = IN_COLS // N_DEV
SH_OUT = D_MODEL // N_DEV
SH_GLU = S5_W // N_DEV

VMEM_LIMIT = 52 * 1024 * 1024
HALO = 16
LANE_CH = 512

TN = (((0,), (0,)), ((), ()))
NT = (((1,), (1,)), ((), ()))


def _params(*sem):
    return pltpu.CompilerParams(dimension_semantics=sem if sem else None, vmem_limit_bytes=VMEM_LIMIT)


def _dot(a, b, dims=None):
    if dims is None:
        return jnp.dot(a, b, preferred_element_type=F32)
    return lax.dot_general(a, b, dims, preferred_element_type=F32)


_GELU_C = math.sqrt(2.0 / math.pi)


def _gelu(x):
    return 0.5 * x * (1.0 + jnp.tanh(_GELU_C * (x + 0.044715 * x * x * x)))


def _gelu_grad(x):
    t = jnp.tanh(_GELU_C * (x + 0.044715 * x * x * x))
    return 0.5 * (1.0 + t) + 0.5 * x * (1.0 - t * t) * _GELU_C * (1.0 + 3.0 * 0.044715 * x * x)


def _silu_and_grad(x):
    s = jax.nn.sigmoid(x)
    return x * s, s * (1.0 + x * (1.0 - s))


def _rowsum(x):
    return jnp.sum(x, axis=0, keepdims=True)


def _after(token):
    if token is None:
        return [], []
    return [pl.BlockSpec((8, 128), lambda *_: (0, 0))], [token]


def _rms_inproj(x, g, w, token=None, tm=512):
    t = x.shape[0]
    after_specs, after = _after(token)

    def body(x_ref, g_ref, w_ref, *rest):
        z_ref, h_ref = rest[-2:]

        @pl.when(pl.program_id(1) == 0)
        def _():
            xv = x_ref[...]
            r = lax.rsqrt(jnp.mean(xv * xv, axis=-1, keepdims=True) + RMS_EPS)
            h_ref[...] = (xv * r * g_ref[...]).astype(BF16)

        z_ref[...] = _dot(h_ref[...], w_ref[...])

    return pl.pallas_call(
        body,
        name="rms_inproj",
        grid=(t // tm, N_CB),
        in_specs=[
            pl.BlockSpec((tm, D_MODEL), lambda m, n: (m, 0)),
            pl.BlockSpec((1, D_MODEL), lambda m, n: (0, 0)),
            pl.BlockSpec((D_MODEL, CB), lambda m, n: (0, n)),
        ] + after_specs,
        out_specs=[
            pl.BlockSpec((tm, CB), lambda m, n: (m, n)),
            pl.BlockSpec((tm, D_MODEL), lambda m, n: (m, 0)),
        ],
        out_shape=[jax.ShapeDtypeStruct((t, IN_COLS), F32), jax.ShapeDtypeStruct((t, D_MODEL), BF16)],
        compiler_params=_params("arbitrary", "arbitrary"),
    )(x, g, w, *after)


def _outproj(x, ya, yb, yc, w, tm=512, tn=1024):
    t = x.shape[0]

    def body(x_ref, ya_ref, yb_ref, yc_ref, w_ref, o_ref):
        acc = _dot(ya_ref[...], w_ref[0:S5_W, :])
        acc += _dot(yb_ref[...], w_ref[S5_W:S5_W + SGU_W, :])
        acc += _dot(yc_ref[...], w_ref[S5_W + SGU_W:, :])
        o_ref[...] = x_ref[...] + acc

    return pl.pallas_call(
        body,
        name="outproj",
        grid=(t // tm, D_MODEL // tn),
        in_specs=[
            pl.BlockSpec((tm, tn), lambda m, n: (m, n)),
            pl.BlockSpec((tm, S5_W), lambda m, n: (m, 0)),
            pl.BlockSpec((tm, SGU_W), lambda m, n: (m, 0)),
            pl.BlockSpec((tm, POOL_W), lambda m, n: (m, 0)),
            pl.BlockSpec((D_MODEL, tn), lambda m, n: (0, n)),
        ],
        out_specs=pl.BlockSpec((tm, tn), lambda m, n: (m, n)),
        out_shape=jax.ShapeDtypeStruct((t, D_MODEL), F32),
        compiler_params=_params("arbitrary", "arbitrary"),
    )(x, ya, yb, yc, w)


def _outproj_bwd(dx, w, token=None, tm=512):
    t = dx.shape[0]
    after_specs, after = _after(token)

    def body(dx_ref, w_ref, *rest):
        dya_ref, dyb_ref, dyc_ref = rest[-3:]
        dy = _dot(dx_ref[...].astype(BF16), w_ref[...], NT)
        dya_ref[...] = dy[:, 0:S5_W]
        dyb_ref[...] = dy[:, S5_W:S5_W + SGU_W]
        dyc_ref[...] = dy[:, S5_W + SGU_W:]

    return pl.pallas_call(
        body,
        name="outproj_bwd",
        grid=(t // tm,),
        in_specs=[
            pl.BlockSpec((tm, D_MODEL), lambda m: (m, 0)),
            pl.BlockSpec((D_MODEL, D_MODEL), lambda m: (0, 0)),
        ] + after_specs,
        out_specs=[
            pl.BlockSpec((tm, S5_W), lambda m: (m, 0)),
            pl.BlockSpec((tm, SGU_W), lambda m: (m, 0)),
            pl.BlockSpec((tm, POOL_W), lambda m: (m, 0)),
        ],
        out_shape=[
            jax.ShapeDtypeStruct((t, S5_W), F32),
            jax.ShapeDtypeStruct((t, SGU_W), F32),
            jax.ShapeDtypeStruct((t, POOL_W), F32),
        ],
        compiler_params=_params("arbitrary"),
    )(dx, w, *after)


def _dz_piece_maps():
    s5_map = lambda j: jnp.where(j >= CB_GA, 1, 0)
    sgu_map = lambda j: jnp.clip(jnp.where(j <= 4, j - 1, j - 3), 0, 5)
    pool_map = lambda j: jnp.where(j >= CB_GC, 1, 0)
    return s5_map, sgu_map, pool_map


def _pick_piece(j):
    is_s5 = jnp.logical_or(j == CB_XA, j == CB_GA)
    is_pool = jnp.logical_or(j == CB_XC, j == CB_GC)
    return is_s5, is_pool, jnp.logical_not(jnp.logical_or(is_s5, is_pool))


def _inproj_bwd(dz_s5, dz_sgu, dz_pool, w, x, g, dxo, tm=512):
    t = x.shape[0]
    s5_map, sgu_map, pool_map = _dz_piece_maps()

    def body(s5_ref, sgu_ref, pool_ref, w_ref, x_ref, g_ref, dxo_ref, dx_ref, dg_ref, acc):
        m, j = pl.program_id(0), pl.program_id(1)

        @pl.when(jnp.logical_and(m == 0, j == 0))
        def _():
            dg_ref[...] = jnp.zeros_like(dg_ref)

        @pl.when(j == 0)
        def _():
            acc[...] = jnp.zeros_like(acc)

        is_s5, is_pool, is_sgu = _pick_piece(j)

        @pl.when(is_s5)
        def _():
            acc[...] += _dot(s5_ref[...], w_ref[...], NT)

        @pl.when(is_sgu)
        def _():
            acc[...] += _dot(sgu_ref[...], w_ref[...], NT)

        @pl.when(is_pool)
        def _():
            acc[...] += _dot(pool_ref[...], w_ref[...], NT)

        @pl.when(j == N_CB - 1)
        def _():
            xv = x_ref[...]
            r = lax.rsqrt(jnp.mean(xv * xv, axis=-1, keepdims=True) + RMS_EPS)
            n = xv * r
            dh = acc[...]
            dg_ref[...] += _rowsum(dh * n)
            dn = dh * g_ref[...]
            dx_ref[...] = dxo_ref[...] + r * (dn - n * jnp.mean(dn * n, axis=-1, keepdims=True))

    return pl.pallas_call(
        body,
        name="inproj_bwd",
        grid=(t // tm, N_CB),
        in_specs=[
            pl.BlockSpec((tm, CB), lambda m, j: (m, s5_map(j))),
            pl.BlockSpec((tm, CB), lambda m, j: (m, sgu_map(j))),
            pl.BlockSpec((tm, CB), lambda m, j: (m, pool_map(j))),
            pl.BlockSpec((D_MODEL, CB), lambda m, j: (0, j)),
            pl.BlockSpec((tm, D_MODEL), lambda m, j: (m, 0)),
            pl.BlockSpec((1, D_MODEL), lambda m, j: (0, 0)),
            pl.BlockSpec((tm, D_MODEL), lambda m, j: (m, 0)),
        ],
        out_specs=[
            pl.BlockSpec((tm, D_MODEL), lambda m, j: (m, 0)),
            pl.BlockSpec((1, D_MODEL), lambda m, j: (0, 0)),
        ],
        out_shape=[jax.ShapeDtypeStruct((t, D_MODEL), F32), jax.ShapeDtypeStruct((1, D_MODEL), F32)],
        scratch_shapes=[pltpu.VMEM((tm, D_MODEL), F32)],
        compiler_params=_params("arbitrary", "arbitrary"),
    )(dz_s5, dz_sgu, dz_pool, w, x, g, dxo)


def _wgrad_in(h, dz_s5, dz_sgu, dz_pool, tm=512):
    t = h.shape[0]
    s5_map, sgu_map, pool_map = _dz_piece_maps()

    def body(h_ref, s5_ref, sgu_ref, pool_ref, o_ref, acc):
        j, m = pl.program_id(0), pl.program_id(1)

        @pl.when(m == 0)
        def _():
            acc[...] = jnp.zeros_like(acc)

        is_s5, is_pool, is_sgu = _pick_piece(j)

        @pl.when(is_s5)
        def _():
            acc[...] += _dot(h_ref[...], s5_ref[...], TN)

        @pl.when(is_sgu)
        def _():
            acc[...] += _dot(h_ref[...], sgu_ref[...], TN)

        @pl.when(is_pool)
        def _():
            acc[...] += _dot(h_ref[...], pool_ref[...], TN)

        @pl.when(m == pl.num_programs(1) - 1)
        def _():
            o_ref[...] = acc[...].astype(BF16)

    return pl.pallas_call(
        body,
        name="wgrad_in",
        grid=(N_CB, t // tm),
        in_specs=[
            pl.BlockSpec((tm, D_MODEL), lambda j, m: (m, 0)),
            pl.BlockSpec((tm, CB), lambda j, m: (m, s5_map(j))),
            pl.BlockSpec((tm, CB), lambda j, m: (m, sgu_map(j))),
            pl.BlockSpec((tm, CB), lambda j, m: (m, pool_map(j))),
        ],
        out_specs=pl.BlockSpec((D_MODEL, CB), lambda j, m: (0, j)),
        out_shape=jax.ShapeDtypeStruct((D_MODEL, IN_COLS), BF16),
        scratch_shapes=[pltpu.VMEM((D_MODEL, CB), F32)],
        compiler_params=_params("arbitrary", "arbitrary"),
    )(h, dz_s5, dz_sgu, dz_pool)


def _wgrad_out(ya, yb, yc, dx, token=None, tm=512, tn=512):
    t = dx.shape[0]
    after_specs, after = _after(token)

    def body(ya_ref, yb_ref, yc_ref, dx_ref, *rest):
        o_ref, acc = rest[-2:]
        m = pl.program_id(1)

        @pl.when(m == 0)
        def _():
            acc[...] = jnp.zeros_like(acc)

        dxb = dx_ref[...].astype(BF16)
        acc[0:S5_W, :] += _dot(ya_ref[...], dxb, TN)
        acc[S5_W:S5_W + SGU_W, :] += _dot(yb_ref[...], dxb, TN)
        acc[S5_W + SGU_W:, :] += _dot(yc_ref[...], dxb, TN)

        @pl.when(m == pl.num_programs(1) - 1)
        def _():
            o_ref[...] = acc[...].astype(BF16)

    return pl.pallas_call(
        body,
        name="wgrad_out",
        grid=(D_MODEL // tn, t // tm),
        in_specs=[
            pl.BlockSpec((tm, S5_W), lambda n, m: (m, 0)),
            pl.BlockSpec((tm, SGU_W), lambda n, m: (m, 0)),
            pl.BlockSpec((tm, POOL_W), lambda n, m: (m, 0)),
            pl.BlockSpec((tm, tn), lambda n, m: (m, n)),
        ] + after_specs,
        out_specs=pl.BlockSpec((D_MODEL, tn), lambda n, m: (0, n)),
        out_shape=jax.ShapeDtypeStruct((D_MODEL, D_MODEL), BF16),
        scratch_shapes=[pltpu.VMEM((D_MODEL, tn), F32)],
        compiler_params=_params("arbitrary", "arbitrary"),
    )(ya, yb, yc, dx, *after)


def _final_loss(x, g, target, tm=512):
    t = x.shape[0]

    def body(x_ref, g_ref, t_ref, dx_ref, loss_ref, dg_ref):
        @pl.when(pl.program_id(0) == 0)
        def _():
            loss_ref[...] = jnp.zeros_like(loss_ref)
            dg_ref[...] = jnp.zeros_like(dg_ref)

        xv = x_ref[...]
        gv = g_ref[...]
        r = lax.rsqrt(jnp.mean(xv * xv, axis=-1, keepdims=True) + RMS_EPS)
        n = xv * r
        err = n * gv - t_ref[...]
        loss_ref[...] += 0.5 * jnp.sum(jnp.mean(err * err, axis=-1, keepdims=True))
        dy = err * (1.0 / D_MODEL)
        dg_ref[...] += _rowsum(dy * n)
        dn = dy * gv
        dx_ref[...] = r * (dn - n * jnp.mean(dn * n, axis=-1, keepdims=True))

    return pl.pallas_call(
        body,
        name="final_loss",
        grid=(t // tm,),
        in_specs=[
            pl.BlockSpec((tm, D_MODEL), lambda m: (m, 0)),
            pl.BlockSpec((1, D_MODEL), lambda m: (0, 0)),
            pl.BlockSpec((tm, D_MODEL), lambda m: (m, 0)),
        ],
        out_specs=[
            pl.BlockSpec((tm, D_MODEL), lambda m: (m, 0)),
            pl.BlockSpec((8, 128), lambda m: (0, 0)),
            pl.BlockSpec((1, D_MODEL), lambda m: (0, 0)),
        ],
        out_shape=[
            jax.ShapeDtypeStruct((t, D_MODEL), F32),
            jax.ShapeDtypeStruct((8, 128), F32),
            jax.ShapeDtypeStruct((1, D_MODEL), F32),
        ],
        compiler_params=_params("arbitrary"),
    )(x, g, target)


N_Q = 2 * N_STATE // LANE_CH


def _s5_scan(st, carry, a_ref, tb, reverse):
    for c0 in range(0, N_STATE, LANE_CH):
        re = pl.ds(c0, LANE_CH)
        im = pl.ds(N_STATE + c0, LANE_CH)
        a_re = a_ref[0:1, re]
        a_im = a_ref[1:2, re]

        def step(k, c, re=re, im=im, a_re=a_re, a_im=a_im):
            s_re, s_im = c
            t = (tb - 1 - k) if reverse else k
            row = pl.ds(t, 1)
            if reverse:
                n_re = st[row, re] + (a_re * s_re + a_im * s_im)
                n_im = st[row, im] + (a_re * s_im - a_im * s_re)
            else:
                n_re = st[row, re] + (a_re * s_re - a_im * s_im)
                n_im = st[row, im] + (a_re * s_im + a_im * s_re)
            st[row, re] = n_re
            st[row, im] = n_im
            return n_re, n_im

        s_re, s_im = lax.fori_loop(0, tb, step, (carry[0:1, re], carry[0:1, im]))
        carry[0:1, re] = s_re
        carry[0:1, im] = s_im


def _s5_fwd(z, bc, cc, a, dvec, wglu, bglu, tb=256):
    t = z.shape[0]

    def body(xa_ref, ga_ref, bc_ref, cc_ref, a_ref, d_ref, wglu_ref, bglu_ref, ya_ref, s_ref, ys_ref, st, carry):
        @pl.when(pl.program_id(0) == 0)
        def _():
            carry[...] = jnp.zeros_like(carry)

        xa = xa_ref[...]
        xab = xa.astype(BF16)
        for q in range(N_Q):
            cq = pl.ds(LANE_CH * q, LANE_CH)
            st[:, cq] = _dot(xab[:, 128 * (q % 4):128 * (q % 4) + 128], bc_ref[:, cq])
        _s5_scan(st, carry, a_ref, tb, reverse=False)
        sb = st[...].astype(BF16)
        s_ref[...] = sb
        cols = []
        for j in range(4):
            lo, hi = LANE_CH * j, N_STATE + LANE_CH * j
            cols.append(_dot(sb[:, lo:lo + LANE_CH], cc_ref[lo:lo + LANE_CH, :])
                        + _dot(sb[:, hi:hi + LANE_CH], cc_ref[hi:hi + LANE_CH, :]))
        ys = jnp.concatenate(cols, axis=1) + d_ref[...] * xa
        ys_ref[...] = ys
        ya1 = _gelu(ys)
        pre = _dot(ya1.astype(BF16), wglu_ref[...]) + bglu_ref[...]
        silu_ga, _ = _silu_and_grad(ga_ref[...])
        ya_ref[...] = (ya1 * jax.nn.sigmoid(pre) * silu_ga).astype(BF16)

    const = lambda shape: pl.BlockSpec(shape, lambda i: (0,) * len(shape))
    return pl.pallas_call(
        body,
        name="s5_fwd",
        grid=(t // tb,),
        in_specs=[
            pl.BlockSpec((tb, CB), lambda i: (i, CB_XA)),
            pl.BlockSpec((tb, CB), lambda i: (i, CB_GA)),
            const((128, 2 * N_STATE)),
            const((2 * N_STATE, 128)),
            const((2, N_STATE)),
            const((1, S5_W)),
            const((S5_W, S5_W)),
            const((1, S5_W)),
        ],
        out_specs=[
            pl.BlockSpec((tb, S5_W), lambda i: (i, 0)),
            pl.BlockSpec((tb, 2 * N_STATE), lambda i: (i, 0)),
            pl.BlockSpec((tb, S5_W), lambda i: (i, 0)),
        ],
        out_shape=[
            jax.ShapeDtypeStruct((t, S5_W), BF16),
            jax.ShapeDtypeStruct((t, 2 * N_STATE), BF16),
            jax.ShapeDtypeStruct((t, S5_W), F32),
        ],
        scratch_shapes=[pltpu.VMEM((tb, 2 * N_STATE), F32), pltpu.VMEM((8, 2 * N_STATE), F32)],
        compiler_params=_params("arbitrary"),
    )(z, z, bc, cc, a, dvec, wglu, bglu)


def _s5_bwd(dya, ys, z, s, bc, cc, a, dvec, wglu, bglu, tb=256):
    t = z.shape[0]
    nb = t // tb
    rev = lambda i: nb - 1 - i

    def body(dya_ref, ys_ref, xa_ref, ga_ref, s_ref, sp_ref, bc_ref, cc_ref, a_ref, d_ref, wglu_ref, bglu_ref,
             dz_ref, dbc_ref, dcct_ref, da_ref, dd_ref, dwglu_ref, dbglu_ref, g, carry):
        i = pl.program_id(0)

        @pl.when(i == 0)
        def _():
            carry[...] = jnp.zeros_like(carry)
            for r in (dbc_ref, dcct_ref, da_ref, dd_ref, dwglu_ref, dbglu_ref):
                r[...] = jnp.zeros_like(r)

        ys = ys_ref[...]
        xa = xa_ref[...]
        ga = ga_ref[...]
        dya = dya_ref[...]
        ya1 = _gelu(ys)
        ya1b = ya1.astype(BF16)
        sg = jax.nn.sigmoid(_dot(ya1b, wglu_ref[...]) + bglu_ref[...])
        silu_ga, silu_ga_grad = _silu_and_grad(ga)
        dz_ref[:, S5_W:] = (dya * (ya1 * sg) * silu_ga_grad).astype(BF16)
        dya2 = dya * silu_ga
        dpre = dya2 * ya1 * sg * (1.0 - sg)
        dbglu_ref[...] += _rowsum(dpre)
        dpreb = dpre.astype(BF16)
        dwglu_ref[...] += _dot(ya1b, dpreb, TN)
        dys = (dya2 * sg + _dot(dpreb, wglu_ref[...], NT)) * _gelu_grad(ys)
        dd_ref[...] += _rowsum(dys * xa)
        dysb = dys.astype(BF16)
        xab = xa.astype(BF16)

        for q in range(N_Q):
            cq = pl.ds(LANE_CH * q, LANE_CH)
            x0 = 128 * (q % 4)
            dcct_ref[:, cq] += _dot(dysb[:, x0:x0 + 128], s_ref[:, cq], TN)
            g[:, cq] = _dot(dysb[:, x0:x0 + 128], cc_ref[cq, :], NT)
        _s5_scan(g, carry, a_ref, tb, reverse=True)

        first = (lax.broadcasted_iota(jnp.int32, (tb, LANE_CH), 0) == 0)
        have_prev = i < nb - 1
        dxa_cols = []
        for j in range(4):
            re = pl.ds(LANE_CH * j, LANE_CH)
            im = pl.ds(N_STATE + LANE_CH * j, LANE_CH)
            x0 = 128 * j
            g_re, g_im = g[:, re], g[:, im]
            p_re = jnp.where(have_prev, sp_ref[:, re].astype(F32)[HALO - 1:HALO, :], 0.0)
            p_im = jnp.where(have_prev, sp_ref[:, im].astype(F32)[HALO - 1:HALO, :], 0.0)
            sp_re = jnp.where(first, p_re, pltpu.roll(s_ref[:, re].astype(F32), 1, 0))
            sp_im = jnp.where(first, p_im, pltpu.roll(s_ref[:, im].astype(F32), 1, 0))
            da_ref[0:1, re] += _rowsum(sp_re * g_re + sp_im * g_im)
            da_ref[1:2, re] += _rowsum(sp_re * g_im - sp_im * g_re)
            gb_re, gb_im = g_re.astype(BF16), g_im.astype(BF16)
            dbc_ref[:, re] += _dot(xab[:, x0:x0 + 128], gb_re, TN)
            dbc_ref[:, im] += _dot(xab[:, x0:x0 + 128], gb_im, TN)
            dxa_cols.append(_dot(gb_re, bc_ref[:, re], NT) + _dot(gb_im, bc_ref[:, im], NT))
        dz_ref[:, 0:S5_W] = (dys * d_ref[...] + jnp.concatenate(dxa_cols, axis=1)).astype(BF16)

    const = lambda shape: pl.BlockSpec(shape, lambda i: (0,) * len(shape))
    per_halo = tb // HALO
    return pl.pallas_call(
        body,
        name="s5_bwd",
        grid=(nb,),
        in_specs=[
            pl.BlockSpec((tb, S5_W), lambda i: (rev(i), 0)),
            pl.BlockSpec((tb, S5_W), lambda i: (rev(i), 0)),
            pl.BlockSpec((tb, CB), lambda i: (rev(i), CB_XA)),
            pl.BlockSpec((tb, CB), lambda i: (rev(i), CB_GA)),
            pl.BlockSpec((tb, 2 * N_STATE), lambda i: (rev(i), 0)),
            pl.BlockSpec((HALO, 2 * N_STATE), lambda i: (jnp.maximum(rev(i) * per_halo - 1, 0), 0)),
            const((128, 2 * N_STATE)),
            const((2 * N_STATE, 128)),
            const((2, N_STATE)),
            const((1, S5_W)),
            const((S5_W, S5_W)),
            const((1, S5_W)),
        ],
        out_specs=[
            pl.BlockSpec((tb, 2 * CB), lambda i: (rev(i), 0)),
            const((128, 2 * N_STATE)),
            const((128, 2 * N_STATE)),
            const((2, N_STATE)),
            const((1, S5_W)),
            const((S5_W, S5_W)),
            const((1, S5_W)),
        ],
        out_shape=[
            jax.ShapeDtypeStruct((t, 2 * CB), BF16),
            jax.ShapeDtypeStruct((128, 2 * N_STATE), F32),
            jax.ShapeDtypeStruct((128, 2 * N_STATE), F32),
            jax.ShapeDtypeStruct((2, N_STATE), F32),
            jax.ShapeDtypeStruct((1, S5_W), F32),
            jax.ShapeDtypeStruct((S5_W, S5_W), F32),
            jax.ShapeDtypeStruct((1, S5_W), F32),
        ],
        scratch_shapes=[pltpu.VMEM((tb, 2 * N_STATE), F32), pltpu.VMEM((8, 2 * N_STATE), F32)],
        compiler_params=_params("arbitrary"),
    )(dya, ys, z, z, s, s, bc, cc, a, dvec, wglu, bglu)


def _sgu_norm(v0, v1, lng_ref, lnb_ref):
    g0, g1 = _gelu(v0), _gelu(v1)
    mu = (jnp.sum(g0, axis=-1, keepdims=True) + jnp.sum(g1, axis=-1, keepdims=True)) * (1.0 / SGU_W)
    c0, c1 = g0 - mu, g1 - mu
    var = (jnp.sum(c0 * c0, axis=-1, keepdims=True) + jnp.sum(c1 * c1, axis=-1, keepdims=True)) * (1.0 / SGU_W)
    rstd = lax.rsqrt(var + LN_EPS)
    vh0, vh1 = c0 * rstd, c1 * rstd
    vn0 = vh0 * lng_ref[:, 0:CB] + lnb_ref[:, 0:CB]
    vn1 = vh1 * lng_ref[:, CB:] + lnb_ref[:, CB:]
    return (vh0, vh1), (vn0, vn1), rstd


def _sgu_fwd(z, lng, lnb, ws, bsx, tb=256):
    t = z.shape[0]

    def body(u0_ref, u1_ref, v0_ref, v1_ref, gb0_ref, gb1_ref, lng_ref, lnb_ref, ws_ref, bsx_ref, yb_ref):
        _, (vn0, vn1), _ = _sgu_norm(v0_ref[...], v1_ref[...], lng_ref, lnb_ref)
        for half, (vn, u_ref, gb_ref) in enumerate(((vn0, u0_ref, gb0_ref), (vn1, u1_ref, gb1_ref))):
            vnb = vn.astype(BF16)
            silu_gb, _ = _silu_and_grad(gb_ref[...])
            gate = _gelu(u_ref[...]) * silu_gb
            for hh in range(4):
                h = 4 * half + hh
                for c in range(tb // CHUNK):
                    rows, cols = slice(CHUNK * c, CHUNK * (c + 1)), slice(128 * hh, 128 * (hh + 1))
                    sp = _dot(ws_ref[h], vnb[rows, cols]) + bsx_ref[h]
                    yb_ref[rows, CB * half + 128 * hh:CB * half + 128 * (hh + 1)] = (gate[rows, cols] * sp).astype(BF16)

    zb = lambda j: pl.BlockSpec((tb, CB), lambda i, j=j: (i, j))
    const = lambda shape: pl.BlockSpec(shape, lambda i: (0,) * len(shape))
    return pl.pallas_call(
        body,
        name="sgu_fwd",
        grid=(t // tb,),
        in_specs=[zb(CB_U), zb(CB_U + 1), zb(CB_V), zb(CB_V + 1), zb(CB_GB), zb(CB_GB + 1),
                  const((1, SGU_W)), const((1, SGU_W)), const((SGU_HEADS, CHUNK, CHUNK)), const((SGU_HEADS, CHUNK, 128))],
        out_specs=pl.BlockSpec((tb, SGU_W), lambda i: (i, 0)),
        out_shape=jax.ShapeDtypeStruct((t, SGU_W), BF16),
        compiler_params=_params("arbitrary"),
    )(z, z, z, z, z, z, lng, lnb, ws, bsx)


def _sgu_bwd(dyb, z, lng, lnb, ws, wst, bsx, tb=256):
    t = z.shape[0]

    def body(dyb_ref, u0_ref, u1_ref, v0_ref, v1_ref, gb0_ref, gb1_ref, lng_ref, lnb_ref, ws_ref, wst_ref, bsx_ref,
             dz_ref, dlng_ref, dlnb_ref, dws_ref, dbs_ref, dvn):
        @pl.when(pl.program_id(0) == 0)
        def _():
            for r in (dlng_ref, dlnb_ref, dws_ref, dbs_ref):
                r[...] = jnp.zeros_like(r)

        v0, v1 = v0_ref[...], v1_ref[...]
        (vh0, vh1), (vn0, vn1), rstd = _sgu_norm(v0, v1, lng_ref, lnb_ref)
        causal = (lax.broadcasted_iota(jnp.int32, (CHUNK, CHUNK), 0) >= lax.broadcasted_iota(jnp.int32, (CHUNK, CHUNK), 1))
        for half, (vn, u_ref, gb_ref) in enumerate(((vn0, u0_ref, gb0_ref), (vn1, u1_ref, gb1_ref))):
            vnb = vn.astype(BF16)
            u = u_ref[...]
            ug = _gelu(u)
            silu_gb, silu_gb_grad = _silu_and_grad(gb_ref[...])
            dyb = dyb_ref[:, CB * half:CB * (half + 1)]
            dyb0 = dyb * silu_gb
            ds = dyb0 * ug
            sp_cols = []
            for hh in range(4):
                h = 4 * half + hh
                cols = slice(128 * hh, 128 * (hh + 1))
                sp_rows = []
                for c in range(tb // CHUNK):
                    rows = slice(CHUNK * c, CHUNK * (c + 1))
                    vt = vnb[rows, cols]
                    sp_rows.append(_dot(ws_ref[h], vt) + bsx_ref[h])
                    dst = ds[rows, cols]
                    dstb = dst.astype(BF16)
                    dbs_ref[h] += dst
                    dws_ref[h] += jnp.where(causal, _dot(dstb, vt, NT), 0.0)
                    dvn[rows, CB * half + 128 * hh:CB * half + 128 * (hh + 1)] = _dot(wst_ref[h], dstb)
                sp_cols.append(jnp.concatenate(sp_rows, axis=0))
            sp = jnp.concatenate(sp_cols, axis=1)
            dz_ref[:, CB * half:CB * (half + 1)] = (dyb0 * sp * _gelu_grad(u)).astype(BF16)
            dz_ref[:, 2 * SGU_W + CB * half:2 * SGU_W + CB * (half + 1)] = (dyb * (ug * sp) * silu_gb_grad).astype(BF16)

        dvn0, dvn1 = dvn[:, 0:CB], dvn[:, CB:]
        dlng_ref[:, 0:CB] += _rowsum(dvn0 * vh0)
        dlng_ref[:, CB:] += _rowsum(dvn1 * vh1)
        dlnb_ref[:, 0:CB] += _rowsum(dvn0)
        dlnb_ref[:, CB:] += _rowsum(dvn1)
        dh0, dh1 = dvn0 * lng_ref[:, 0:CB], dvn1 * lng_ref[:, CB:]
        m1 = (jnp.sum(dh0, axis=-1, keepdims=True) + jnp.sum(dh1, axis=-1, keepdims=True)) * (1.0 / SGU_W)
        m2 = (jnp.sum(dh0 * vh0, axis=-1, keepdims=True) + jnp.sum(dh1 * vh1, axis=-1, keepdims=True)) * (1.0 / SGU_W)
        dz_ref[:, SGU_W:SGU_W + CB] = (rstd * (dh0 - m1 - vh0 * m2) * _gelu_grad(v0)).astype(BF16)
        dz_ref[:, SGU_W + CB:2 * SGU_W] = (rstd * (dh1 - m1 - vh1 * m2) * _gelu_grad(v1)).astype(BF16)

    zb = lambda j: pl.BlockSpec((tb, CB), lambda i, j=j: (i, j))
    const = lambda shape: pl.BlockSpec(shape, lambda i: (0,) * len(shape))
    hmat = (SGU_HEADS, CHUNK, CHUNK)
    return pl.pallas_call(
        body,
        name="sgu_bwd",
        grid=(t // tb,),
        in_specs=[pl.BlockSpec((tb, SGU_W), lambda i: (i, 0)),
                  zb(CB_U), zb(CB_U + 1), zb(CB_V), zb(CB_V + 1), zb(CB_GB), zb(CB_GB + 1),
                  const((1, SGU_W)), const((1, SGU_W)), const(hmat), const(hmat), const(hmat)],
        out_specs=[pl.BlockSpec((tb, 3 * SGU_W), lambda i: (i, 0)),
                   const((1, SGU_W)), const((1, SGU_W)), const(hmat), const(hmat)],
        out_shape=[jax.ShapeDtypeStruct((t, 3 * SGU_W), BF16),
                   jax.ShapeDtypeStruct((1, SGU_W), F32), jax.ShapeDtypeStruct((1, SGU_W), F32),
                   jax.ShapeDtypeStruct(hmat, F32), jax.ShapeDtypeStruct(hmat, F32)],
        scratch_shapes=[pltpu.VMEM((tb, SGU_W), F32)],
        compiler_params=_params("arbitrary"),
    )(dyb, z, z, z, z, z, z, lng, lnb, ws, wst, bsx)


def _window_sums(ext, lookahead):
    n = ext.shape[0]
    out = []
    for gi, w in enumerate(POOL_WINDOWS):
        acc = ext[:, 128 * gi:128 * (gi + 1)]
        k = 1
        while k < w:
            acc = acc + pltpu.roll(acc, (n - k) if lookahead else k, 0)
            k *= 2
        out.append(acc)
    return jnp.concatenate(out, axis=1)


def _pool_counts(row0, tb):
    pos = (row0 + 1 + lax.broadcasted_iota(jnp.int32, (tb, POOL_W), 0)).astype(F32)
    lane = lax.broadcasted_iota(jnp.int32, (tb, POOL_W), 1)
    win = jnp.where(lane < 128, 2.0, jnp.where(lane < 256, 4.0, jnp.where(lane < 384, 8.0, 16.0)))
    return jnp.minimum(pos, win)


def _pool_fwd(z, wpool, scale, tb=256):
    t = z.shape[0]

    def body(xc_ref, gc_ref, wp_ref, sc_ref, yc_ref, halo):
        i = pl.program_id(0)

        @pl.when(i == 0)
        def _():
            halo[...] = jnp.zeros_like(halo)

        xc = xc_ref[...]
        sums = _window_sums(jnp.concatenate([halo[...], xc], axis=0), lookahead=False)[HALO:, :]
        halo[...] = xc[tb - HALO:, :]
        pb = (sums / _pool_counts(i * tb, tb) - xc).astype(BF16)
        q = jnp.concatenate([_dot(pb[:, 128 * gi:128 * (gi + 1)], wp_ref[gi]) for gi in range(4)], axis=1)
        silu_gc, _ = _silu_and_grad(gc_ref[...])
        yc_ref[...] = (q * sc_ref[...] * silu_gc).astype(BF16)

    const = lambda shape: pl.BlockSpec(shape, lambda i: (0,) * len(shape))
    return pl.pallas_call(
        body,
        name="pool_fwd",
        grid=(t // tb,),
        in_specs=[pl.BlockSpec((tb, CB), lambda i: (i, CB_XC)), pl.BlockSpec((tb, CB), lambda i: (i, CB_GC)),
                  const((4, 128, 128)), const((1, POOL_W))],
        out_specs=pl.BlockSpec((tb, POOL_W), lambda i: (i, 0)),
        out_shape=jax.ShapeDtypeStruct((t, POOL_W), BF16),
        scratch_shapes=[pltpu.VMEM((HALO, POOL_W), F32)],
        compiler_params=_params("arbitrary"),
    )(z, z, wpool, scale)


def _pool_bwd(dyc, z, wpool, scale, tb=256):
    t = z.shape[0]
    nb = t // tb
    rev = lambda i: nb - 1 - i
    per_halo = tb // HALO

    def body(dyc_ref, xc_ref, xp_ref, gc_ref, wp_ref, sc_ref, dz_ref, dwp_ref, dsc_ref, ehalo):
        i = pl.program_id(0)

        @pl.when(i == 0)
        def _():
            ehalo[...] = jnp.zeros_like(ehalo)
            dwp_ref[...] = jnp.zeros_like(dwp_ref)
            dsc_ref[...] = jnp.zeros_like(dsc_ref)

        xc = xc_ref[...]
        prev = jnp.where(i < nb - 1, xp_ref[...], 0.0)
        sums = _window_sums(jnp.concatenate([prev, xc], axis=0), lookahead=False)[HALO:, :]
        cnt = _pool_counts(rev(i) * tb, tb)
        pb = (sums / cnt - xc).astype(BF16)
        q = jnp.concatenate([_dot(pb[:, 128 * gi:128 * (gi + 1)], wp_ref[gi]) for gi in range(4)], axis=1)
        silu_gc, silu_gc_grad = _silu_and_grad(gc_ref[...])
        dyc = dyc_ref[...]
        dz_ref[:, POOL_W:] = (dyc * (q * sc_ref[...]) * silu_gc_grad).astype(BF16)
        dyc0 = dyc * silu_gc
        dsc_ref[...] += _rowsum(dyc0 * q)
        dqb = (dyc0 * sc_ref[...]).astype(BF16)
        dp_cols = []
        for gi in range(4):
            cols = slice(128 * gi, 128 * (gi + 1))
            dwp_ref[gi] += _dot(pb[:, cols], dqb[:, cols], TN)
            dp_cols.append(_dot(dqb[:, cols], wp_ref[gi], NT))
        dp = jnp.concatenate(dp_cols, axis=1)
        e = dp / cnt
        fut = _window_sums(jnp.concatenate([e, ehalo[...]], axis=0), lookahead=True)[:tb, :]
        ehalo[...] = e[:HALO, :]
        dz_ref[:, 0:POOL_W] = (fut - dp).astype(BF16)

    const = lambda shape: pl.BlockSpec(shape, lambda i: (0,) * len(shape))
    return pl.pallas_call(
        body,
        name="pool_bwd",
        grid=(nb,),
        in_specs=[pl.BlockSpec((tb, POOL_W), lambda i: (rev(i), 0)),
                  pl.BlockSpec((tb, CB), lambda i: (rev(i), CB_XC)),
                  pl.BlockSpec((HALO, CB), lambda i: (jnp.maximum(rev(i) * per_halo - 1, 0), CB_XC)),
                  pl.BlockSpec((tb, CB), lambda i: (rev(i), CB_GC)),
                  const((4, 128, 128)), const((1, POOL_W))],
        out_specs=[pl.BlockSpec((tb, 2 * POOL_W), lambda i: (rev(i), 0)), const((4, 128, 128)), const((1, POOL_W))],
        out_shape=[jax.ShapeDtypeStruct((t, 2 * POOL_W), BF16),
                   jax.ShapeDtypeStruct((4, 128, 128), F32), jax.ShapeDtypeStruct((1, POOL_W), F32)],
        scratch_shapes=[pltpu.VMEM((HALO, POOL_W), F32)],
        compiler_params=_params("arbitrary"),
    )(dyc, z, z, z, wpool, scale)


def _adamw(w, m, v, parts, tr, name):
    r, c = w.shape
    n_slab = len(parts)
    per_slab = r // n_slab // tr

    def body(w_ref, m_ref, v_ref, *refs):
        p_refs, (g_ref, d_ref, nm_ref, nv_ref) = refs[:n_slab], refs[n_slab:]
        for s, p_ref in enumerate(p_refs):
            @pl.when(pl.program_id(0) // per_slab == s)
            def _(p_ref=p_ref):
                g = p_ref[0].astype(F32)
                for k in range(1, N_DEV):
                    g = g + p_ref[k].astype(F32)
                g_ref[...] = g

        g = g_ref[...]
        nm = ADAM_B1 * m_ref[...] + (1.0 - ADAM_B1) * g
        nv = ADAM_B2 * v_ref[...] + (1.0 - ADAM_B2) * (g * g)
        m_hat = nm / (1.0 - ADAM_B1 ** ADAM_STEP)
        v_hat = nv / (1.0 - ADAM_B2 ** ADAM_STEP)
        nm_ref[...] = nm
        nv_ref[...] = nv
        d_ref[...] = -ADAM_LR * (m_hat / (jnp.sqrt(v_hat) + ADAM_EPS) + ADAM_WD * w_ref[...])

    blk = pl.BlockSpec((tr, c), lambda i: (i, 0))
    slab = lambda s: pl.BlockSpec((N_DEV, tr, c), lambda i, s=s: (0, jnp.clip(i - s * per_slab, 0, per_slab - 1), 0))
    return pl.pallas_call(
        body,
        name=name,
        grid=(r // tr,),
        in_specs=[blk, blk, blk] + [slab(s) for s in range(n_slab)],
        out_specs=[blk, blk, blk, blk],
        out_shape=[jax.ShapeDtypeStruct((r, c), F32)] * 4,
        compiler_params=_params("arbitrary"),
    )(w, m, v, *parts)


MESH = pl.DeviceIdType.MESH
ANY = pl.BlockSpec(memory_space=pl.ANY)


def _dev_index(dev):
    return 4 * dev[0] + 2 * dev[1] + dev[2]


WHOLE_SHAPES = ((D_MODEL, IN_COLS), (D_MODEL, D_MODEL), (S5_W, S5_W))
SHARD_SHAPES = ((D_MODEL, SH_IN), (SH_OUT, D_MODEL), (SH_GLU, S5_W))


def _shard_of(ref, ti, idx):
    if ti == 0:
        return ref.at[:, pl.ds(pl.multiple_of(idx * SH_IN, 128), SH_IN)]
    rows = SHARD_SHAPES[ti][0]
    return ref.at[pl.ds(pl.multiple_of(idx * rows, rows), rows), :]


def _peer(mask, x, y, c):
    return (1 - x if mask & 4 else x, 1 - y if mask & 2 else y, 1 - c if mask & 1 else c)


def _allgather_weights(wi, wo, wg):
    n_t = 3

    def body(wi_ref, wo_ref, wg_ref, gi_ref, go_ref, gg_ref, send_sems, recv_sems, local_sems):
        x, y, c = lax.axis_index("x"), lax.axis_index("y"), lax.axis_index("c")
        me, sibling = (x, y, c), (x, y, 1 - c)
        chips = [(1 - x, y), (x, 1 - y), (1 - x, 1 - y)]
        shards = (wi_ref, wo_ref, wg_ref)
        wholes = (gi_ref, go_ref, gg_ref)

        def slot(ti, dev):
            return _shard_of(wholes[ti], ti, _dev_index(dev))

        def copy(k, ti, block, to, own=False):
            return pltpu.make_async_remote_copy(
                src_ref=shards[ti] if own else slot(ti, block), dst_ref=slot(ti, block),
                send_sem=send_sems.at[n_t * k + ti], recv_sem=recv_sems.at[n_t * k + ti],
                device_id=to, device_id_type=MESH)

        mine = [pltpu.make_async_copy(shards[ti], slot(ti, me), local_sems.at[ti]) for ti in range(n_t)]
        for cp in mine:
            cp.start()
        first = [copy(0, ti, me, sibling, own=True) for ti in range(n_t)]
        first += [copy(1 + j, ti, me, (*chip, c), own=True) for j, chip in enumerate(chips) for ti in range(n_t)]
        for cp in first:
            cp.start()
        passed = []
        for j, chip in enumerate(chips):
            for ti in range(n_t):
                copy(1 + j, ti, (*chip, c), me).wait_recv()
            onward = [copy(4 + j, ti, (*chip, c), sibling) for ti in range(n_t)]
            for cp in onward:
                cp.start()
            passed += onward
        for ti in range(n_t):
            copy(0, ti, sibling, me).wait_recv()
        for j, chip in enumerate(chips):
            for ti in range(n_t):
                copy(4 + j, ti, (*chip, 1 - c), me).wait_recv()
        for cp in first + passed:
            cp.wait_send()
        for cp in mine:
            cp.wait()

    return pl.pallas_call(
        body,
        name="allgather_weights",
        in_specs=[ANY, ANY, ANY],
        out_specs=[ANY, ANY, ANY],
        out_shape=[jax.ShapeDtypeStruct(s, BF16) for s in WHOLE_SHAPES],
        scratch_shapes=[pltpu.SemaphoreType.DMA((7 * n_t,)), pltpu.SemaphoreType.DMA((7 * n_t,)),
                        pltpu.SemaphoreType.DMA((n_t,))],
    )(wi, wo, wg)


HBM = pl.BlockSpec(memory_space=pltpu.HBM)
SEM = pl.BlockSpec(memory_space=pltpu.SEMAPHORE)
N_SPLIT = (N_DEV - 1) * 3


def _split_routes(gather):
    def send(ti, me_idx, p_idx, src_ref, land_ref):
        if gather:
            return src_ref, _shard_of(land_ref, ti, me_idx)
        return _shard_of(src_ref, ti, p_idx), land_ref.at[me_idx]

    def recv(ti, me_idx, p_idx, src_ref, land_ref):
        if gather:
            return src_ref, _shard_of(land_ref, ti, p_idx)
        return _shard_of(src_ref, ti, p_idx), land_ref.at[p_idx]

    return send, recv


def _split_start(name, srcs, lands, gather):
    send, _ = _split_routes(gather)

    def body(*refs):
        src_refs, land_refs = refs[0:3], refs[3:6]
        send_sems, recv_sems, token = refs[6], refs[7], refs[-1]
        x, y, c = lax.axis_index("x"), lax.axis_index("y"), lax.axis_index("c")
        me_idx = _dev_index((x, y, c))
        for mask in range(1, N_DEV):
            p = _peer(mask, x, y, c)
            for ti in range(3):
                k = (mask - 1) * 3 + ti
                src, dst = send(ti, me_idx, _dev_index(p), src_refs[ti], land_refs[ti])
                pltpu.make_async_remote_copy(src_ref=src, dst_ref=dst, send_sem=send_sems.at[k], recv_sem=recv_sems.at[k],
                                             device_id=p, device_id_type=MESH).start()
        token[...] = jnp.zeros_like(token)

    arrays = list(srcs) + list(lands)
    return pl.pallas_call(
        body,
        name=name,
        in_specs=[HBM] * 6,
        out_specs=(SEM, SEM) + (HBM,) * 6 + (pl.BlockSpec(memory_space=pltpu.VMEM),),
        out_shape=(pltpu.SemaphoreType.DMA((N_SPLIT,)), pltpu.SemaphoreType.DMA((N_SPLIT,)))
        + tuple(pltpu.HBM(a.shape, a.dtype) for a in arrays) + (jax.ShapeDtypeStruct((8, 128), F32),),
        input_output_aliases={i: 2 + i for i in range(6)},
        compiler_params=pltpu.CompilerParams(has_side_effects=pltpu.SideEffectType.DATAFLOW_SIDE_EFFECTING),
    )(*[pltpu.with_memory_space_constraint(a, pltpu.HBM) for a in arrays])


def _split_wait(name, started, after, gather):
    send_sems, recv_sems, thru = started[0], started[1], started[2:8]
    _, recv = _split_routes(gather)

    def body(*refs):
        src_refs, land_refs = refs[0:3], refs[3:6]
        send_sems, recv_sems = refs[6], refs[7]
        x, y, c = lax.axis_index("x"), lax.axis_index("y"), lax.axis_index("c")
        me_idx = _dev_index((x, y, c))
        for mask in range(1, N_DEV):
            p = _peer(mask, x, y, c)
            for ti in range(3):
                k = (mask - 1) * 3 + ti
                src, dst = recv(ti, me_idx, _dev_index(p), src_refs[ti], land_refs[ti])
                cp = pltpu.make_async_remote_copy(src_ref=src, dst_ref=dst, send_sem=send_sems.at[k],
                                                  recv_sem=recv_sems.at[k], device_id=p, device_id_type=MESH)
                cp.wait_send()
                cp.wait_recv()

    res = pl.pallas_call(
        body,
        name=name,
        in_specs=[HBM] * 6 + [SEM, SEM, pl.BlockSpec(memory_space=pl.ANY)],
        out_specs=(HBM,) * 6,
        out_shape=tuple(pltpu.HBM(a.shape, a.dtype) for a in thru),
        input_output_aliases={i: i for i in range(6)},
        compiler_params=pltpu.CompilerParams(has_side_effects=pltpu.SideEffectType.DATAFLOW_SIDE_EFFECTING),
    )(*thru, send_sems, recv_sems, after)
    return res[3:6]


def _with_own(whole_shape, dtype, own, start):
    return lax.dynamic_update_slice(lax.empty(whole_shape, dtype), own, start)


def _exchange_grads(dwi, dwo, dwg, small):
    n_t = 4
    rows_small = small.shape[0]

    def body(dwi_ref, dwo_ref, dwg_ref, sm, r_in, r_out, r_glu, r_small, send_sems, recv_sems, local_sems):
        x, y, c = lax.axis_index("x"), lax.axis_index("y"), lax.axis_index("c")
        me_idx = _dev_index((x, y, c))
        partials = (dwi_ref, dwo_ref, dwg_ref)
        stacks = (r_in, r_out, r_glu, r_small)

        def src(j, dev_idx):
            return _shard_of(partials[j], j, dev_idx) if j < 3 else sm

        def dst(j, from_idx):
            return stacks[j].at[from_idx]

        def peer(mask):
            return _peer(mask, x, y, c)

        local = [pltpu.make_async_copy(src(j, me_idx), dst(j, me_idx), local_sems.at[j]) for j in range(n_t)]
        for cp in local:
            cp.start()
        sends = []
        for mask in range(1, N_DEV):
            p_idx = _dev_index(peer(mask))
            for j in range(n_t):
                k = (mask - 1) * n_t + j
                sends.append(pltpu.make_async_remote_copy(
                    src_ref=src(j, p_idx), dst_ref=dst(j, me_idx), send_sem=send_sems.at[k], recv_sem=recv_sems.at[k],
                    device_id=peer(mask), device_id_type=MESH))
        for cp in sends:
            cp.start()
        for mask in range(1, N_DEV):
            p_idx = _dev_index(peer(mask))
            for j in range(n_t):
                k = (mask - 1) * n_t + j
                pltpu.make_async_remote_copy(
                    src_ref=src(j, p_idx), dst_ref=dst(j, p_idx), send_sem=send_sems.at[k], recv_sem=recv_sems.at[k],
                    device_id=peer(mask), device_id_type=MESH).wait_recv()
        for cp in sends:
            cp.wait_send()
        for cp in local:
            cp.wait()

    n_remote = (N_DEV - 1) * n_t
    return pl.pallas_call(
        body,
        name="exchange_grads",
        in_specs=[ANY] * n_t,
        out_specs=[ANY] * 4,
        out_shape=[jax.ShapeDtypeStruct((N_DEV,) + s, BF16) for s in SHARD_SHAPES]
        + [jax.ShapeDtypeStruct((N_DEV, rows_small, 128), F32)],
        scratch_shapes=[pltpu.SemaphoreType.DMA((n_remote,)), pltpu.SemaphoreType.DMA((n_remote,)),
                        pltpu.SemaphoreType.DMA((n_t,))],
    )(dwi, dwo, dwg, small)


def _s5_prep(lam_re, lam_im, b_re, b_im, c_re, c_im, d_skip, log_dt):
    dt = jnp.exp(log_dt)[:, None]
    mag = jnp.exp(lam_re * dt)
    a_re, a_im = mag * jnp.cos(lam_im * dt), mag * jnp.sin(lam_im * dt)
    den = lam_re * lam_re + lam_im * lam_im
    f_re = ((a_re - 1.0) * lam_re + a_im * lam_im) / den
    f_im = (a_im * lam_re - (a_re - 1.0) * lam_im) / den
    bb_re = f_re[..., None] * b_re - f_im[..., None] * b_im
    bb_im = f_re[..., None] * b_im + f_im[..., None] * b_re
    eye = jnp.eye(8, dtype=F32)

    def in_map(bb):
        return jnp.einsum("jgph,gk->ghjkp", bb.reshape(4, 8, S5_STATE, S5_CH), eye).reshape(128, N_STATE)

    def out_map(cm):
        return jnp.einsum("jghp,gk->ghjkp", cm.reshape(4, 8, S5_CH, S5_STATE), eye).reshape(128, N_STATE)

    a = jnp.stack([a_re.reshape(-1), a_im.reshape(-1)])
    bc = jnp.concatenate([in_map(bb_re), in_map(bb_im)], axis=1)
    cct = jnp.concatenate([out_map(c_re), -out_map(c_im)], axis=1)
    return a, bc, cct, d_skip.reshape(1, S5_W)


WEIGHTS = ["norm_g", "w_in", "lam_re", "lam_im", "b_re", "b_im", "c_re", "c_im", "d_skip", "log_dt", "w_glu", "b_glu",
           "ln_g", "ln_b", "w_s", "b_s", "w_pool", "pool_scale", "w_out", "final_g"]
SHARDED = ("w_in", "w_glu", "w_out")
SMALL = [n for n in WEIGHTS if n not in SHARDED]
INPUTS = ["x"] + WEIGHTS + ["loss_target"] + ["m_" + n for n in WEIGHTS] + ["v_" + n for n in WEIGHTS]
SMALL_TILE = 8 * 128


def _pack_small(arrays):
    flat = jnp.concatenate([a.reshape(-1) for a in arrays])
    pad = (-flat.shape[0]) % SMALL_TILE
    return jnp.pad(flat, (0, pad)).reshape(-1, 128)


def _unpack_small(packed, like):
    flat = packed.reshape(-1)
    out, off = [], 0
    for a in like:
        out.append(flat[off:off + a.size].reshape(a.shape))
        off += a.size
    return out


def _layer_fwd(p, l, x, wi, wo, wg, token=None):
    row = lambda v: v.reshape(1, -1)
    causal = jnp.tril(jnp.ones((CHUNK, CHUNK), dtype=bool))
    (a, bc, cct, dvec), prep_vjp = jax.vjp(
        _s5_prep, p["lam_re"][l], p["lam_im"][l], p["b_re"][l], p["b_im"][l], p["c_re"][l], p["c_im"][l],
        p["d_skip"][l], p["log_dt"][l])
    ws_f32 = jnp.where(causal[None], p["w_s"][l], 0.0)
    c = dict(
        x=x, wi=wi, wo=wo, wg=wg, a=a, bc=bc.astype(BF16), cc=cct.T.astype(BF16), dvec=dvec, prep_vjp=prep_vjp,
        ws=ws_f32.astype(BF16), wst=jnp.swapaxes(ws_f32, 1, 2).astype(BF16),
        bsx=jnp.broadcast_to(p["b_s"][l][:, :, None], (SGU_HEADS, CHUNK, 128)),
        wpool=p["w_pool"][l].astype(BF16), scale=row(p["pool_scale"][l]),
        lng=row(p["ln_g"][l]), lnb=row(p["ln_b"][l]), bglu=row(p["b_glu"][l]), norm_g=row(p["norm_g"][l]))
    c["z"], c["h"] = _rms_inproj(x, c["norm_g"], wi, token)
    c["ya"], c["s"], c["ys"] = _s5_fwd(c["z"], c["bc"], c["cc"], a, dvec, wg, c["bglu"])
    c["yb"] = _sgu_fwd(c["z"], c["lng"], c["lnb"], c["ws"], c["bsx"])
    c["yc"] = _pool_fwd(c["z"], c["wpool"], c["scale"])
    return _outproj(x, c["ya"], c["yb"], c["yc"], wo), c


def _layer_bwd(c, dx, token=None):
    dya, dyb, dyc = _outproj_bwd(dx, c["wo"], token)
    dwo = _wgrad_out(c["ya"], c["yb"], c["yc"], dx, token)
    dz_s5, dbc, dcct, da, dd, dwg, dbglu = _s5_bwd(
        dya, c["ys"], c["z"], c["s"], c["bc"], c["cc"], c["a"], c["dvec"], c["wg"], c["bglu"])
    dz_sgu, dlng, dlnb, dws, dbsx = _sgu_bwd(dyb, c["z"], c["lng"], c["lnb"], c["ws"], c["wst"], c["bsx"])
    dz_pool, dwp, dsc = _pool_bwd(dyc, c["z"], c["wpool"], c["scale"])
    dwi = _wgrad_in(c["h"], dz_s5, dz_sgu, dz_pool)
    dx, dnorm = _inproj_bwd(dz_s5, dz_sgu, dz_pool, c["wi"], c["x"], c["norm_g"], dx)
    g_lam_re, g_lam_im, g_b_re, g_b_im, g_c_re, g_c_im, g_d, g_dt = c["prep_vjp"]((da, dbc, dcct, dd))
    small = dict(norm_g=dnorm.reshape(-1), lam_re=g_lam_re, lam_im=g_lam_im, b_re=g_b_re, b_im=g_b_im, c_re=g_c_re,
                 c_im=g_c_im, d_skip=g_d, log_dt=g_dt, b_glu=dbglu.reshape(-1), ln_g=dlng.reshape(-1),
                 ln_b=dlnb.reshape(-1), w_s=dws, b_s=jnp.sum(dbsx, axis=-1), w_pool=dwp, pool_scale=dsc.reshape(-1))
    return dx, dwi, dwo, dwg, small


def _own_index():
    return _dev_index((lax.axis_index("x"), lax.axis_index("y"), lax.axis_index("c")))


def _shard_start(ti, idx):
    return (0, idx * SH_IN) if ti == 0 else (idx * SHARD_SHAPES[ti][0], 0)


def kernel(x, norm_g, w_in, lam_re, lam_im, b_re, b_im, c_re, c_im, d_skip, log_dt, w_glu, b_glu, ln_g, ln_b, w_s, b_s, w_pool, pool_scale, w_out, final_g, loss_target, m_norm_g, m_w_in, m_lam_re, m_lam_im, m_b_re, m_b_im, m_c_re, m_c_im, m_d_skip, m_log_dt, m_w_glu, m_b_glu, m_ln_g, m_ln_b, m_w_s, m_b_s, m_w_pool, m_pool_scale, m_w_out, m_final_g, v_norm_g, v_w_in, v_lam_re, v_lam_im, v_b_re, v_b_im, v_c_re, v_c_im, v_d_skip, v_log_dt, v_w_glu, v_b_glu, v_ln_g, v_ln_b, v_w_s, v_b_s, v_w_pool, v_pool_scale, v_w_out, v_final_g):
    p = dict(zip(INPUTS, (x, norm_g, w_in, lam_re, lam_im, b_re, b_im, c_re, c_im, d_skip, log_dt, w_glu, b_glu, ln_g, ln_b, w_s, b_s, w_pool, pool_scale, w_out, final_g, loss_target, m_norm_g, m_w_in, m_lam_re, m_lam_im, m_b_re, m_b_im, m_c_re, m_c_im, m_d_skip, m_log_dt, m_w_glu, m_b_glu, m_ln_g, m_ln_b, m_w_s, m_b_s, m_w_pool, m_pool_scale, m_w_out, m_final_g, v_norm_g, v_w_in, v_lam_re, v_lam_im, v_b_re, v_b_im, v_c_re, v_c_im, v_d_skip, v_log_dt, v_w_glu, v_b_glu, v_ln_g, v_ln_b, v_w_s, v_b_s, v_w_pool, v_pool_scale, v_w_out, v_final_g)))

    me = _own_index()
    shards = [[w[l].astype(BF16) for w in (w_in, w_out, w_glu)] for l in range(DEPTH)]

    whole0 = _allgather_weights(*shards[0])
    lands = [_with_own(WHOLE_SHAPES[ti], BF16, shards[1][ti], _shard_start(ti, me)) for ti in range(3)]
    started = _split_start("gather1_start", shards[1], lands, gather=True)
    x1, c0 = _layer_fwd(p, 0, x[0], *whole0, token=started[-1])
    whole1 = _split_wait("gather1_wait", started, x1, gather=True)
    x2, c1 = _layer_fwd(p, 1, x1, *whole1)

    dx, loss_tile, dfinal = _final_loss(x2, final_g.reshape(1, -1), loss_target[0])
    loss = lax.psum(loss_tile[0, 0], ("x", "y", "c"))

    dx, dwi1, dwo1, dwg1, small1 = _layer_bwd(c1, dx)
    partials1 = [dwi1, dwo1, dwg1.astype(BF16)]
    lands = [_with_own((N_DEV,) + SHARD_SHAPES[ti], BF16,
                       lax.dynamic_slice(partials1[ti], _shard_start(ti, me), SHARD_SHAPES[ti])[None], (me, 0, 0))
             for ti in range(3)]
    started = _split_start("grads1_start", partials1, lands, gather=False)
    dx, dwi0, dwo0, dwg0, small0 = _layer_bwd(c0, dx, token=started[-1])
    parts1 = _split_wait("grads1_wait", started, dx, gather=False)
    small = {n: jnp.stack([small0[n], small1[n]]) for n in small0}
    small["final_g"] = dfinal.reshape(-1)
    *parts0, r_small = _exchange_grads(dwi0, dwo0, dwg0.astype(BF16), _pack_small([small[n] for n in SMALL]))

    out = {}

    def adam(name, ti, tr):
        shape2d = (DEPTH * SHARD_SHAPES[ti][0], SHARD_SHAPES[ti][1])
        res = _adamw(p[name].reshape(shape2d), p["m_" + name].reshape(shape2d), p["v_" + name].reshape(shape2d),
                     [parts0[ti], parts1[ti]], tr, "adamw_" + name)
        out[name] = [r.reshape(p[name].shape) for r in res]

    adam("w_in", 0, 256)
    adam("w_out", 1, 128)
    adam("w_glu", 2, 64)
    like = [p[n] for n in SMALL]
    rows_small = r_small.shape[1]
    packed = _adamw(_pack_small(like), _pack_small([p["m_" + n] for n in SMALL]), _pack_small([p["v_" + n] for n in SMALL]),
                    [r_small], rows_small // 8, "adamw_small")
    for n, vals in zip(SMALL, zip(*[_unpack_small(r, like) for r in packed])):
        out[n] = list(vals)

    return (loss, dx[None], *[out[n][0] for n in WEIGHTS], *[out[n][1] for n in WEIGHTS],
            *[out[n][2] for n in WEIGHTS], *[out[n][3] for n in WEIGHTS])
```

```python
import functools
import math

import jax
import jax.numpy as jnp
from jax import lax
from jax.experimental import pallas as pl
from jax.experimental.pallas import tpu as pltpu

F32 = jnp.float32
BF16 = jnp.bfloat16

D_MODEL = 2048
DEPTH = 2
S5_W, SGU_W, POOL_W = 512, 1024, 512
S5_GROUPS, S5_STATE, S5_CH = 32, 64, 16
N_STATE = S5_GROUPS * S5_STATE
CHUNK = 128
SGU_HEADS = 8
POOL_WINDOWS = (2, 4, 8, 16)
IN_COLS = 5120
RMS_EPS = 1e-6
LN_EPS = 1e-5
ADAM_LR, ADAM_B1, ADAM_B2, ADAM_EPS, ADAM_WD, ADAM_STEP = 0.001, 0.9, 0.999, 1e-08, 0.01, 10

CB = 512
N_CB = IN_COLS // CB
CB_XA, CB_U, CB_V, CB_XC, CB_GA, CB_GB, CB_GC = 0, 1, 3, 5, 6, 7, 9

N_DEV = 8
SH_IN = IN_COLS // N_DEV
SH_OUT = D_MODEL // N_DEV
SH_GLU = S5_W // N_DEV

VMEM_LIMIT = 52 * 1024 * 1024
HALO = 16
LANE_CH = 512

TN = (((0,), (0,)), ((), ()))
NT = (((1,), (1,)), ((), ()))


def _params(*sem):
    return pltpu.CompilerParams(dimension_semantics=sem if sem else None, vmem_limit_bytes=VMEM_LIMIT)


def _dot(a, b, dims=None):
    if dims is None:
        return jnp.dot(a, b, preferred_element_type=F32)
    return lax.dot_general(a, b, dims, preferred_element_type=F32)


_GELU_C = math.sqrt(2.0 / math.pi)


def _gelu(x):
    return 0.5 * x * (1.0 + jnp.tanh(_GELU_C * (x + 0.044715 * x * x * x)))


def _gelu_grad(x):
    t = jnp.tanh(_GELU_C * (x + 0.044715 * x * x * x))
    return 0.5 * (1.0 + t) + 0.5 * x * (1.0 - t * t) * _GELU_C * (1.0 + 3.0 * 0.044715 * x * x)


def _silu_and_grad(x):
    s = jax.nn.sigmoid(x)
    return x * s, s * (1.0 + x * (1.0 - s))


def _rowsum(x):
    return jnp.sum(x, axis=0, keepdims=True)


def _after(token):
    if token is None:
        return [], []
    return [pl.BlockSpec((8, 128), lambda *_: (0, 0))], [token]


def _rms_inproj(x, g, w, token=None, tm=512):
    t = x.shape[0]
    after_specs, after = _after(token)

    def body(x_ref, g_ref, w_ref, *rest):
        z_ref, h_ref = rest[-2:]

        @pl.when(pl.program_id(1) == 0)
        def _():
            xv = x_ref[...]
            r = lax.rsqrt(jnp.mean(xv * xv, axis=-1, keepdims=True) + RMS_EPS)
            h_ref[...] = (xv * r * g_ref[...]).astype(BF16)

        z_ref[...] = _dot(h_ref[...], w_ref[...])

    return pl.pallas_call(
        body,
        name="rms_inproj",
        grid=(t // tm, N_CB),
        in_specs=[
            pl.BlockSpec((tm, D_MODEL), lambda m, n: (m, 0)),
            pl.BlockSpec((1, D_MODEL), lambda m, n: (0, 0)),
            pl.BlockSpec((D_MODEL, CB), lambda m, n: (0, n)),
        ] + after_specs,
        out_specs=[
            pl.BlockSpec((tm, CB), lambda m, n: (m, n)),
            pl.BlockSpec((tm, D_MODEL), lambda m, n: (m, 0)),
        ],
        out_shape=[jax.ShapeDtypeStruct((t, IN_COLS), F32), jax.ShapeDtypeStruct((t, D_MODEL), BF16)],
        compiler_params=_params("arbitrary", "arbitrary"),
    )(x, g, w, *after)


def _outproj(x, ya, yb, yc, w, tm=512, tn=1024):
    t = x.shape[0]

    def body(x_ref, ya_ref, yb_ref, yc_ref, w_ref, o_ref):
        acc = _dot(ya_ref[...], w_ref[0:S5_W, :])
        acc += _dot(yb_ref[...], w_ref[S5_W:S5_W + SGU_W, :])
        acc += _dot(yc_ref[...], w_ref[S5_W + SGU_W:, :])
        o_ref[...] = x_ref[...] + acc

    return pl.pallas_call(
        body,
        name="outproj",
        grid=(t // tm, D_MODEL // tn),
        in_specs=[
            pl.BlockSpec((tm, tn), lambda m, n: (m, n)),
            pl.BlockSpec((tm, S5_W), lambda m, n: (m, 0)),
            pl.BlockSpec((tm, SGU_W), lambda m, n: (m, 0)),
            pl.BlockSpec((tm, POOL_W), lambda m, n: (m, 0)),
            pl.BlockSpec((D_MODEL, tn), lambda m, n: (0, n)),
        ],
        out_specs=pl.BlockSpec((tm, tn), lambda m, n: (m, n)),
        out_shape=jax.ShapeDtypeStruct((t, D_MODEL), F32),
        compiler_params=_params("arbitrary", "arbitrary"),
    )(x, ya, yb, yc, w)


def _outproj_bwd(dx, w, token=None, tm=512):
    t = dx.shape[0]
    after_specs, after = _after(token)

    def body(dx_ref, w_ref, *rest):
        dya_ref, dyb_ref, dyc_ref = rest[-3:]
        dy = _dot(dx_ref[...].astype(BF16), w_ref[...], NT)
        dya_ref[...] = dy[:, 0:S5_W]
        dyb_ref[...] = dy[:, S5_W:S5_W + SGU_W]
        dyc_ref[...] = dy[:, S5_W + SGU_W:]

    return pl.pallas_call(
        body,
        name="outproj_bwd",
        grid=(t // tm,),
        in_specs=[
            pl.BlockSpec((tm, D_MODEL), lambda m: (m, 0)),
            pl.BlockSpec((D_MODEL, D_MODEL), lambda m: (0, 0)),
        ] + after_specs,
        out_specs=[
            pl.BlockSpec((tm, S5_W), lambda m: (m, 0)),
            pl.BlockSpec((tm, SGU_W), lambda m: (m, 0)),
            pl.BlockSpec((tm, POOL_W), lambda m: (m, 0)),
        ],
        out_shape=[
            jax.ShapeDtypeStruct((t, S5_W), F32),
            jax.ShapeDtypeStruct((t, SGU_W), F32),
            jax.ShapeDtypeStruct((t, POOL_W), F32),
        ],
        compiler_params=_params("arbitrary"),
    )(dx, w, *after)


def _dz_piece_maps():
    s5_map = lambda j: jnp.where(j >= CB_GA, 1, 0)
    sgu_map = lambda j: jnp.clip(jnp.where(j <= 4, j - 1, j - 3), 0, 5)
    pool_map = lambda j: jnp.where(j >= CB_GC, 1, 0)
    return s5_map, sgu_map, pool_map


def _pick_piece(j):
    is_s5 = jnp.logical_or(j == CB_XA, j == CB_GA)
    is_pool = jnp.logical_or(j == CB_XC, j == CB_GC)
    return is_s5, is_pool, jnp.logical_not(jnp.logical_or(is_s5, is_pool))


def _inproj_bwd(dz_s5, dz_sgu, dz_pool, w, x, g, dxo, token=None, tm=512):
    t = x.shape[0]
    s5_map, sgu_map, pool_map = _dz_piece_maps()
    after_specs, after = _after(token)

    def body(s5_ref, sgu_ref, pool_ref, w_ref, x_ref, g_ref, dxo_ref, *rest):
        dx_ref, dg_ref, acc = rest[-3:]
        m, j = pl.program_id(0), pl.program_id(1)

        @pl.when(jnp.logical_and(m == 0, j == 0))
        def _():
            dg_ref[...] = jnp.zeros_like(dg_ref)

        @pl.when(j == 0)
        def _():
            acc[...] = jnp.zeros_like(acc)

        is_s5, is_pool, is_sgu = _pick_piece(j)

        @pl.when(is_s5)
        def _():
            acc[...] += _dot(s5_ref[...], w_ref[...], NT)

        @pl.when(is_sgu)
        def _():
            acc[...] += _dot(sgu_ref[...], w_ref[...], NT)

        @pl.when(is_pool)
        def _():
            acc[...] += _dot(pool_ref[...], w_ref[...], NT)

        @pl.when(j == N_CB - 1)
        def _():
            xv = x_ref[...]
            r = lax.rsqrt(jnp.mean(xv * xv, axis=-1, keepdims=True) + RMS_EPS)
            n = xv * r
            dh = acc[...]
            dg_ref[...] += _rowsum(dh * n)
            dn = dh * g_ref[...]
            dx_ref[...] = dxo_ref[...] + r * (dn - n * jnp.mean(dn * n, axis=-1, keepdims=True))

    return pl.pallas_call(
        body,
        name="inproj_bwd",
        grid=(t // tm, N_CB),
        in_specs=[
            pl.BlockSpec((tm, CB), lambda m, j: (m, s5_map(j))),
            pl.BlockSpec((tm, CB), lambda m, j: (m, sgu_map(j))),
            pl.BlockSpec((tm, CB), lambda m, j: (m, pool_map(j))),
            pl.BlockSpec((D_MODEL, CB), lambda m, j: (0, j)),
            pl.BlockSpec((tm, D_MODEL), lambda m, j: (m, 0)),
            pl.BlockSpec((1, D_MODEL), lambda m, j: (0, 0)),
            pl.BlockSpec((tm, D_MODEL), lambda m, j: (m, 0)),
        ] + after_specs,
        out_specs=[
            pl.BlockSpec((tm, D_MODEL), lambda m, j: (m, 0)),
            pl.BlockSpec((1, D_MODEL), lambda m, j: (0, 0)),
        ],
        out_shape=[jax.ShapeDtypeStruct((t, D_MODEL), F32), jax.ShapeDtypeStruct((1, D_MODEL), F32)],
        scratch_shapes=[pltpu.VMEM((tm, D_MODEL), F32)],
        compiler_params=_params("arbitrary", "arbitrary"),
    )(dz_s5, dz_sgu, dz_pool, w, x, g, dxo, *after)


def _wgrad_in(h, dz_s5, dz_sgu, dz_pool, token=None, tm=512):
    t = h.shape[0]
    s5_map, sgu_map, pool_map = _dz_piece_maps()
    after_specs, after = _after(token)

    def body(h_ref, s5_ref, sgu_ref, pool_ref, *rest):
        o_ref, acc = rest[-2:]
        j, m = pl.program_id(0), pl.program_id(1)

        @pl.when(m == 0)
        def _():
            acc[...] = jnp.zeros_like(acc)

        is_s5, is_pool, is_sgu = _pick_piece(j)

        @pl.when(is_s5)
        def _():
            acc[...] += _dot(h_ref[...], s5_ref[...], TN)

        @pl.when(is_sgu)
        def _():
            acc[...] += _dot(h_ref[...], sgu_ref[...], TN)

        @pl.when(is_pool)
        def _():
            acc[...] += _dot(h_ref[...], pool_ref[...], TN)

        @pl.when(m == pl.num_programs(1) - 1)
        def _():
            o_ref[...] = acc[...].astype(BF16)

    return pl.pallas_call(
        body,
        name="wgrad_in",
        grid=(N_CB, t // tm),
        in_specs=[
            pl.BlockSpec((tm, D_MODEL), lambda j, m: (m, 0)),
            pl.BlockSpec((tm, CB), lambda j, m: (m, s5_map(j))),
            pl.BlockSpec((tm, CB), lambda j, m: (m, sgu_map(j))),
            pl.BlockSpec((tm, CB), lambda j, m: (m, pool_map(j))),
        ] + after_specs,
        out_specs=pl.BlockSpec((D_MODEL, CB), lambda j, m: (0, j)),
        out_shape=jax.ShapeDtypeStruct((D_MODEL, IN_COLS), BF16),
        scratch_shapes=[pltpu.VMEM((D_MODEL, CB), F32)],
        compiler_params=_params("arbitrary", "arbitrary"),
    )(h, dz_s5, dz_sgu, dz_pool, *after)


def _wgrad_out(ya, yb, yc, dx, token=None, tm=512, tn=512):
    t = dx.shape[0]
    after_specs, after = _after(token)

    def body(ya_ref, yb_ref, yc_ref, dx_ref, *rest):
        o_ref, acc = rest[-2:]
        m = pl.program_id(1)

        @pl.when(m == 0)
        def _():
            acc[...] = jnp.zeros_like(acc)

        dxb = dx_ref[...].astype(BF16)
        acc[0:S5_W, :] += _dot(ya_ref[...], dxb, TN)
        acc[S5_W:S5_W + SGU_W, :] += _dot(yb_ref[...], dxb, TN)
        acc[S5_W + SGU_W:, :] += _dot(yc_ref[...], dxb, TN)

        @pl.when(m == pl.num_programs(1) - 1)
        def _():
            o_ref[...] = acc[...].astype(BF16)

    return pl.pallas_call(
        body,
        name="wgrad_out",
        grid=(D_MODEL // tn, t // tm),
        in_specs=[
            pl.BlockSpec((tm, S5_W), lambda n, m: (m, 0)),
            pl.BlockSpec((tm, SGU_W), lambda n, m: (m, 0)),
            pl.BlockSpec((tm, POOL_W), lambda n, m: (m, 0)),
            pl.BlockSpec((tm, tn), lambda n, m: (m, n)),
        ] + after_specs,
        out_specs=pl.BlockSpec((D_MODEL, tn), lambda n, m: (0, n)),
        out_shape=jax.ShapeDtypeStruct((D_MODEL, D_MODEL), BF16),
        scratch_shapes=[pltpu.VMEM((D_MODEL, tn), F32)],
        compiler_params=_params("arbitrary", "arbitrary"),
    )(ya, yb, yc, dx, *after)


def _final_loss(x, g, target, tm=512):
    t = x.shape[0]

    def body(x_ref, g_ref, t_ref, dx_ref, loss_ref, dg_ref):
        @pl.when(pl.program_id(0) == 0)
        def _():
            loss_ref[...] = jnp.zeros_like(loss_ref)
            dg_ref[...] = jnp.zeros_like(dg_ref)

        xv = x_ref[...]
        gv = g_ref[...]
        r = lax.rsqrt(jnp.mean(xv * xv, axis=-1, keepdims=True) + RMS_EPS)
        n = xv * r
        err = n * gv - t_ref[...]
        loss_ref[...] += 0.5 * jnp.sum(jnp.mean(err * err, axis=-1, keepdims=True))
        dy = err * (1.0 / D_MODEL)
        dg_ref[...] += _rowsum(dy * n)
        dn = dy * gv
        dx_ref[...] = r * (dn - n * jnp.mean(dn * n, axis=-1, keepdims=True))

    return pl.pallas_call(
        body,
        name="final_loss",
        grid=(t // tm,),
        in_specs=[
            pl.BlockSpec((tm, D_MODEL), lambda m: (m, 0)),
            pl.BlockSpec((1, D_MODEL), lambda m: (0, 0)),
            pl.BlockSpec((tm, D_MODEL), lambda m: (m, 0)),
        ],
        out_specs=[
            pl.BlockSpec((tm, D_MODEL), lambda m: (m, 0)),
            pl.BlockSpec((8, 128), lambda m: (0, 0)),
            pl.BlockSpec((1, D_MODEL), lambda m: (0, 0)),
        ],
        out_shape=[
            jax.ShapeDtypeStruct((t, D_MODEL), F32),
            jax.ShapeDtypeStruct((8, 128), F32),
            jax.ShapeDtypeStruct((1, D_MODEL), F32),
        ],
        compiler_params=_params("arbitrary"),
    )(x, g, target)


N_Q = 2 * N_STATE // LANE_CH


def _s5_scan(st, carry, a_ref, tb, reverse):
    for c0 in range(0, N_STATE, LANE_CH):
        re = pl.ds(c0, LANE_CH)
        im = pl.ds(N_STATE + c0, LANE_CH)
        a_re = a_ref[0:1, re]
        a_im = a_ref[1:2, re]

        def step(k, c, re=re, im=im, a_re=a_re, a_im=a_im):
            s_re, s_im = c
            t = (tb - 1 - k) if reverse else k
            row = pl.ds(t, 1)
            if reverse:
                n_re = st[row, re] + (a_re * s_re + a_im * s_im)
                n_im = st[row, im] + (a_re * s_im - a_im * s_re)
            else:
                n_re = st[row, re] + (a_re * s_re - a_im * s_im)
                n_im = st[row, im] + (a_re * s_im + a_im * s_re)
            st[row, re] = n_re
            st[row, im] = n_im
            return n_re, n_im

        s_re, s_im = lax.fori_loop(0, tb, step, (carry[0:1, re], carry[0:1, im]))
        carry[0:1, re] = s_re
        carry[0:1, im] = s_im


def _s5_fwd(z, bc, cc, a, dvec, wglu, bglu, tb=256):
    t = z.shape[0]

    def body(xa_ref, ga_ref, bc_ref, cc_ref, a_ref, d_ref, wglu_ref, bglu_ref, ya_ref, s_ref, ys_ref, st, carry):
        @pl.when(pl.program_id(0) == 0)
        def _():
            carry[...] = jnp.zeros_like(carry)

        xa = xa_ref[...]
        xab = xa.astype(BF16)
        for q in range(N_Q):
            cq = pl.ds(LANE_CH * q, LANE_CH)
            st[:, cq] = _dot(xab[:, 128 * (q % 4):128 * (q % 4) + 128], bc_ref[:, cq])
        _s5_scan(st, carry, a_ref, tb, reverse=False)
        sb = st[...].astype(BF16)
        s_ref[...] = sb
        cols = []
        for j in range(4):
            lo, hi = LANE_CH * j, N_STATE + LANE_CH * j
            cols.append(_dot(sb[:, lo:lo + LANE_CH], cc_ref[lo:lo + LANE_CH, :])
                        + _dot(sb[:, hi:hi + LANE_CH], cc_ref[hi:hi + LANE_CH, :]))
        ys = jnp.concatenate(cols, axis=1) + d_ref[...] * xa
        ys_ref[...] = ys
        ya1 = _gelu(ys)
        pre = _dot(ya1.astype(BF16), wglu_ref[...]) + bglu_ref[...]
        silu_ga, _ = _silu_and_grad(ga_ref[...])
        ya_ref[...] = (ya1 * jax.nn.sigmoid(pre) * silu_ga).astype(BF16)

    const = lambda shape: pl.BlockSpec(shape, lambda i: (0,) * len(shape))
    return pl.pallas_call(
        body,
        name="s5_fwd",
        grid=(t // tb,),
        in_specs=[
            pl.BlockSpec((tb, CB), lambda i: (i, CB_XA)),
            pl.BlockSpec((tb, CB), lambda i: (i, CB_GA)),
            const((128, 2 * N_STATE)),
            const((2 * N_STATE, 128)),
            const((2, N_STATE)),
            const((1, S5_W)),
            const((S5_W, S5_W)),
            const((1, S5_W)),
        ],
        out_specs=[
            pl.BlockSpec((tb, S5_W), lambda i: (i, 0)),
            pl.BlockSpec((tb, 2 * N_STATE), lambda i: (i, 0)),
            pl.BlockSpec((tb, S5_W), lambda i: (i, 0)),
        ],
        out_shape=[
            jax.ShapeDtypeStruct((t, S5_W), BF16),
            jax.ShapeDtypeStruct((t, 2 * N_STATE), BF16),
            jax.ShapeDtypeStruct((t, S5_W), F32),
        ],
        scratch_shapes=[pltpu.VMEM((tb, 2 * N_STATE), F32), pltpu.VMEM((8, 2 * N_STATE), F32)],
        compiler_params=_params("arbitrary"),
    )(z, z, bc, cc, a, dvec, wglu, bglu)


def _s5_bwd(dya, ys, z, s, bc, cc, a, dvec, wglu, bglu, tb=256):
    t = z.shape[0]
    nb = t // tb
    rev = lambda i: nb - 1 - i

    def body(dya_ref, ys_ref, xa_ref, ga_ref, s_ref, sp_ref, bc_ref, cc_ref, a_ref, d_ref, wglu_ref, bglu_ref,
             dz_ref, dbc_ref, dcct_ref, da_ref, dd_ref, dwglu_ref, dbglu_ref, g, carry):
        i = pl.program_id(0)

        @pl.when(i == 0)
        def _():
            carry[...] = jnp.zeros_like(carry)
            for r in (dbc_ref, dcct_ref, da_ref, dd_ref, dwglu_ref, dbglu_ref):
                r[...] = jnp.zeros_like(r)

        ys = ys_ref[...]
        xa = xa_ref[...]
        ga = ga_ref[...]
        dya = dya_ref[...]
        ya1 = _gelu(ys)
        ya1b = ya1.astype(BF16)
        sg = jax.nn.sigmoid(_dot(ya1b, wglu_ref[...]) + bglu_ref[...])
        silu_ga, silu_ga_grad = _silu_and_grad(ga)
        dz_ref[:, S5_W:] = (dya * (ya1 * sg) * silu_ga_grad).astype(BF16)
        dya2 = dya * silu_ga
        dpre = dya2 * ya1 * sg * (1.0 - sg)
        dbglu_ref[...] += _rowsum(dpre)
        dpreb = dpre.astype(BF16)
        dwglu_ref[...] += _dot(ya1b, dpreb, TN)
        dys = (dya2 * sg + _dot(dpreb, wglu_ref[...], NT)) * _gelu_grad(ys)
        dd_ref[...] += _rowsum(dys * xa)
        dysb = dys.astype(BF16)
        xab = xa.astype(BF16)

        for q in range(N_Q):
            cq = pl.ds(LANE_CH * q, LANE_CH)
            x0 = 128 * (q % 4)
            dcct_ref[:, cq] += _dot(dysb[:, x0:x0 + 128], s_ref[:, cq], TN)
            g[:, cq] = _dot(dysb[:, x0:x0 + 128], cc_ref[cq, :], NT)
        _s5_scan(g, carry, a_ref, tb, reverse=True)

        first = (lax.broadcasted_iota(jnp.int32, (tb, LANE_CH), 0) == 0)
        have_prev = i < nb - 1
        dxa_cols = []
        for j in range(4):
            re = pl.ds(LANE_CH * j, LANE_CH)
            im = pl.ds(N_STATE + LANE_CH * j, LANE_CH)
            x0 = 128 * j
            g_re, g_im = g[:, re], g[:, im]
            p_re = jnp.where(have_prev, sp_ref[:, re].astype(F32)[HALO - 1:HALO, :], 0.0)
            p_im = jnp.where(have_prev, sp_ref[:, im].astype(F32)[HALO - 1:HALO, :], 0.0)
            sp_re = jnp.where(first, p_re, pltpu.roll(s_ref[:, re].astype(F32), 1, 0))
            sp_im = jnp.where(first, p_im, pltpu.roll(s_ref[:, im].astype(F32), 1, 0))
            da_ref[0:1, re] += _rowsum(sp_re * g_re + sp_im * g_im)
            da_ref[1:2, re] += _rowsum(sp_re * g_im - sp_im * g_re)
            gb_re, gb_im = g_re.astype(BF16), g_im.astype(BF16)
            dbc_ref[:, re] += _dot(xab[:, x0:x0 + 128], gb_re, TN)
            dbc_ref[:, im] += _dot(xab[:, x0:x0 + 128], gb_im, TN)
            dxa_cols.append(_dot(gb_re, bc_ref[:, re], NT) + _dot(gb_im, bc_ref[:, im], NT))
        dz_ref[:, 0:S5_W] = (dys * d_ref[...] + jnp.concatenate(dxa_cols, axis=1)).astype(BF16)

    const = lambda shape: pl.BlockSpec(shape, lambda i: (0,) * len(shape))
    per_halo = tb // HALO
    return pl.pallas_call(
        body,
        name="s5_bwd",
        grid=(nb,),
        in_specs=[
            pl.BlockSpec((tb, S5_W), lambda i: (rev(i), 0)),
            pl.BlockSpec((tb, S5_W), lambda i: (rev(i), 0)),
            pl.BlockSpec((tb, CB), lambda i: (rev(i), CB_XA)),
            pl.BlockSpec((tb, CB), lambda i: (rev(i), CB_GA)),
            pl.BlockSpec((tb, 2 * N_STATE), lambda i: (rev(i), 0)),
            pl.BlockSpec((HALO, 2 * N_STATE), lambda i: (jnp.maximum(rev(i) * per_halo - 1, 0), 0)),
            const((128, 2 * N_STATE)),
            const((2 * N_STATE, 128)),
            const((2, N_STATE)),
            const((1, S5_W)),
            const((S5_W, S5_W)),
            const((1, S5_W)),
        ],
        out_specs=[
            pl.BlockSpec((tb, 2 * CB), lambda i: (rev(i), 0)),
            const((128, 2 * N_STATE)),
            const((128, 2 * N_STATE)),
            const((2, N_STATE)),
            const((1, S5_W)),
            const((S5_W, S5_W)),
            const((1, S5_W)),
        ],
        out_shape=[
            jax.ShapeDtypeStruct((t, 2 * CB), BF16),
            jax.ShapeDtypeStruct((128, 2 * N_STATE), F32),
            jax.ShapeDtypeStruct((128, 2 * N_STATE), F32),
            jax.ShapeDtypeStruct((2, N_STATE), F32),
            jax.ShapeDtypeStruct((1, S5_W), F32),
            jax.ShapeDtypeStruct((S5_W, S5_W), F32),
            jax.ShapeDtypeStruct((1, S5_W), F32),
        ],
        scratch_shapes=[pltpu.VMEM((tb, 2 * N_STATE), F32), pltpu.VMEM((8, 2 * N_STATE), F32)],
        compiler_params=_params("arbitrary"),
    )(dya, ys, z, z, s, s, bc, cc, a, dvec, wglu, bglu)


def _sgu_norm(v0, v1, lng_ref, lnb_ref):
    g0, g1 = _gelu(v0), _gelu(v1)
    mu = (jnp.sum(g0, axis=-1, keepdims=True) + jnp.sum(g1, axis=-1, keepdims=True)) * (1.0 / SGU_W)
    c0, c1 = g0 - mu, g1 - mu
    var = (jnp.sum(c0 * c0, axis=-1, keepdims=True) + jnp.sum(c1 * c1, axis=-1, keepdims=True)) * (1.0 / SGU_W)
    rstd = lax.rsqrt(var + LN_EPS)
    vh0, vh1 = c0 * rstd, c1 * rstd
    vn0 = vh0 * lng_ref[:, 0:CB] + lnb_ref[:, 0:CB]
    vn1 = vh1 * lng_ref[:, CB:] + lnb_ref[:, CB:]
    return (vh0, vh1), (vn0, vn1), rstd


def _sgu_fwd(z, lng, lnb, ws, bsx, tb=256):
    t = z.shape[0]

    def body(u0_ref, u1_ref, v0_ref, v1_ref, gb0_ref, gb1_ref, lng_ref, lnb_ref, ws_ref, bsx_ref, yb_ref):
        _, (vn0, vn1), _ = _sgu_norm(v0_ref[...], v1_ref[...], lng_ref, lnb_ref)
        for half, (vn, u_ref, gb_ref) in enumerate(((vn0, u0_ref, gb0_ref), (vn1, u1_ref, gb1_ref))):
            vnb = vn.astype(BF16)
            silu_gb, _ = _silu_and_grad(gb_ref[...])
            gate = _gelu(u_ref[...]) * silu_gb
            for hh in range(4):
                h = 4 * half + hh
                for c in range(tb // CHUNK):
                    rows, cols = slice(CHUNK * c, CHUNK * (c + 1)), slice(128 * hh, 128 * (hh + 1))
                    sp = _dot(ws_ref[h], vnb[rows, cols]) + bsx_ref[h]
                    yb_ref[rows, CB * half + 128 * hh:CB * half + 128 * (hh + 1)] = (gate[rows, cols] * sp).astype(BF16)

    zb = lambda j: pl.BlockSpec((tb, CB), lambda i, j=j: (i, j))
    const = lambda shape: pl.BlockSpec(shape, lambda i: (0,) * len(shape))
    return pl.pallas_call(
        body,
        name="sgu_fwd",
        grid=(t // tb,),
        in_specs=[zb(CB_U), zb(CB_U + 1), zb(CB_V), zb(CB_V + 1), zb(CB_GB), zb(CB_GB + 1),
                  const((1, SGU_W)), const((1, SGU_W)), const((SGU_HEADS, CHUNK, CHUNK)), const((SGU_HEADS, CHUNK, 128))],
        out_specs=pl.BlockSpec((tb, SGU_W), lambda i: (i, 0)),
        out_shape=jax.ShapeDtypeStruct((t, SGU_W), BF16),
        compiler_params=_params("arbitrary"),
    )(z, z, z, z, z, z, lng, lnb, ws, bsx)


def _sgu_bwd(dyb, z, lng, lnb, ws, wst, bsx, tb=256):
    t = z.shape[0]

    def body(dyb_ref, u0_ref, u1_ref, v0_ref, v1_ref, gb0_ref, gb1_ref, lng_ref, lnb_ref, ws_ref, wst_ref, bsx_ref,
             dz_ref, dlng_ref, dlnb_ref, dws_ref, dbs_ref, dvn):
        @pl.when(pl.program_id(0) == 0)
        def _():
            for r in (dlng_ref, dlnb_ref, dws_ref, dbs_ref):
                r[...] = jnp.zeros_like(r)

        v0, v1 = v0_ref[...], v1_ref[...]
        (vh0, vh1), (vn0, vn1), rstd = _sgu_norm(v0, v1, lng_ref, lnb_ref)
        causal = (lax.broadcasted_iota(jnp.int32, (CHUNK, CHUNK), 0) >= lax.broadcasted_iota(jnp.int32, (CHUNK, CHUNK), 1))
        for half, (vn, u_ref, gb_ref) in enumerate(((vn0, u0_ref, gb0_ref), (vn1, u1_ref, gb1_ref))):
            vnb = vn.astype(BF16)
            u = u_ref[...]
            ug = _gelu(u)
            silu_gb, silu_gb_grad = _silu_and_grad(gb_ref[...])
            dyb = dyb_ref[:, CB * half:CB * (half + 1)]
            dyb0 = dyb * silu_gb
            ds = dyb0 * ug
            sp_cols = []
            for hh in range(4):
                h = 4 * half + hh
                cols = slice(128 * hh, 128 * (hh + 1))
                sp_rows = []
                for c in range(tb // CHUNK):
                    rows = slice(CHUNK * c, CHUNK * (c + 1))
                    vt = vnb[rows, cols]
                    sp_rows.append(_dot(ws_ref[h], vt) + bsx_ref[h])
                    dst = ds[rows, cols]
                    dstb = dst.astype(BF16)
                    dbs_ref[h] += dst
                    dws_ref[h] += jnp.where(causal, _dot(dstb, vt, NT), 0.0)
                    dvn[rows, CB * half + 128 * hh:CB * half + 128 * (hh + 1)] = _dot(wst_ref[h], dstb)
                sp_cols.append(jnp.concatenate(sp_rows, axis=0))
            sp = jnp.concatenate(sp_cols, axis=1)
            dz_ref[:, CB * half:CB * (half + 1)] = (dyb0 * sp * _gelu_grad(u)).astype(BF16)
            dz_ref[:, 2 * SGU_W + CB * half:2 * SGU_W + CB * (half + 1)] = (dyb * (ug * sp) * silu_gb_grad).astype(BF16)

        dvn0, dvn1 = dvn[:, 0:CB], dvn[:, CB:]
        dlng_ref[:, 0:CB] += _rowsum(dvn0 * vh0)
        dlng_ref[:, CB:] += _rowsum(dvn1 * vh1)
        dlnb_ref[:, 0:CB] += _rowsum(dvn0)
        dlnb_ref[:, CB:] += _rowsum(dvn1)
        dh0, dh1 = dvn0 * lng_ref[:, 0:CB], dvn1 * lng_ref[:, CB:]
        m1 = (jnp.sum(dh0, axis=-1, keepdims=True) + jnp.sum(dh1, axis=-1, keepdims=True)) * (1.0 / SGU_W)
        m2 = (jnp.sum(dh0 * vh0, axis=-1, keepdims=True) + jnp.sum(dh1 * vh1, axis=-1, keepdims=True)) * (1.0 / SGU_W)
        dz_ref[:, SGU_W:SGU_W + CB] = (rstd * (dh0 - m1 - vh0 * m2) * _gelu_grad(v0)).astype(BF16)
        dz_ref[:, SGU_W + CB:2 * SGU_W] = (rstd * (dh1 - m1 - vh1 * m2) * _gelu_grad(v1)).astype(BF16)

    zb = lambda j: pl.BlockSpec((tb, CB), lambda i, j=j: (i, j))
    const = lambda shape: pl.BlockSpec(shape, lambda i: (0,) * len(shape))
    hmat = (SGU_HEADS, CHUNK, CHUNK)
    return pl.pallas_call(
        body,
        name="sgu_bwd",
        grid=(t // tb,),
        in_specs=[pl.BlockSpec((tb, SGU_W), lambda i: (i, 0)),
                  zb(CB_U), zb(CB_U + 1), zb(CB_V), zb(CB_V + 1), zb(CB_GB), zb(CB_GB + 1),
                  const((1, SGU_W)), const((1, SGU_W)), const(hmat), const(hmat), const(hmat)],
        out_specs=[pl.BlockSpec((tb, 3 * SGU_W), lambda i: (i, 0)),
                   const((1, SGU_W)), const((1, SGU_W)), const(hmat), const(hmat)],
        out_shape=[jax.ShapeDtypeStruct((t, 3 * SGU_W), BF16),
                   jax.ShapeDtypeStruct((1, SGU_W), F32), jax.ShapeDtypeStruct((1, SGU_W), F32),
                   jax.ShapeDtypeStruct(hmat, F32), jax.ShapeDtypeStruct(hmat, F32)],
        scratch_shapes=[pltpu.VMEM((tb, SGU_W), F32)],
        compiler_params=_params("arbitrary"),
    )(dyb, z, z, z, z, z, z, lng, lnb, ws, wst, bsx)


def _window_sums(ext, lookahead):
    n = ext.shape[0]
    out = []
    for gi, w in enumerate(POOL_WINDOWS):
        acc = ext[:, 128 * gi:128 * (gi + 1)]
        k = 1
        while k < w:
            acc = acc + pltpu.roll(acc, (n - k) if lookahead else k, 0)
            k *= 2
        out.append(acc)
    return jnp.concatenate(out, axis=1)


def _pool_counts(row0, tb):
    pos = (row0 + 1 + lax.broadcasted_iota(jnp.int32, (tb, POOL_W), 0)).astype(F32)
    lane = lax.broadcasted_iota(jnp.int32, (tb, POOL_W), 1)
    win = jnp.where(lane < 128, 2.0, jnp.where(lane < 256, 4.0, jnp.where(lane < 384, 8.0, 16.0)))
    return jnp.minimum(pos, win)


def _pool_fwd(z, wpool, scale, tb=256):
    t = z.shape[0]

    def body(xc_ref, gc_ref, wp_ref, sc_ref, yc_ref, halo):
        i = pl.program_id(0)

        @pl.when(i == 0)
        def _():
            halo[...] = jnp.zeros_like(halo)

        xc = xc_ref[...]
        sums = _window_sums(jnp.concatenate([halo[...], xc], axis=0), lookahead=False)[HALO:, :]
        halo[...] = xc[tb - HALO:, :]
        pb = (sums / _pool_counts(i * tb, tb) - xc).astype(BF16)
        q = jnp.concatenate([_dot(pb[:, 128 * gi:128 * (gi + 1)], wp_ref[gi]) for gi in range(4)], axis=1)
        silu_gc, _ = _silu_and_grad(gc_ref[...])
        yc_ref[...] = (q * sc_ref[...] * silu_gc).astype(BF16)

    const = lambda shape: pl.BlockSpec(shape, lambda i: (0,) * len(shape))
    return pl.pallas_call(
        body,
        name="pool_fwd",
        grid=(t // tb,),
        in_specs=[pl.BlockSpec((tb, CB), lambda i: (i, CB_XC)), pl.BlockSpec((tb, CB), lambda i: (i, CB_GC)),
                  const((4, 128, 128)), const((1, POOL_W))],
        out_specs=pl.BlockSpec((tb, POOL_W), lambda i: (i, 0)),
        out_shape=jax.ShapeDtypeStruct((t, POOL_W), BF16),
        scratch_shapes=[pltpu.VMEM((HALO, POOL_W), F32)],
        compiler_params=_params("arbitrary"),
    )(z, z, wpool, scale)


def _pool_bwd(dyc, z, wpool, scale, tb=256):
    t = z.shape[0]
    nb = t // tb
    rev = lambda i: nb - 1 - i
    per_halo = tb // HALO

    def body(dyc_ref, xc_ref, xp_ref, gc_ref, wp_ref, sc_ref, dz_ref, dwp_ref, dsc_ref, ehalo):
        i = pl.program_id(0)

        @pl.when(i == 0)
        def _():
            ehalo[...] = jnp.zeros_like(ehalo)
            dwp_ref[...] = jnp.zeros_like(dwp_ref)
            dsc_ref[...] = jnp.zeros_like(dsc_ref)

        xc = xc_ref[...]
        prev = jnp.where(i < nb - 1, xp_ref[...], 0.0)
        sums = _window_sums(jnp.concatenate([prev, xc], axis=0), lookahead=False)[HALO:, :]
        cnt = _pool_counts(rev(i) * tb, tb)
        pb = (sums / cnt - xc).astype(BF16)
        q = jnp.concatenate([_dot(pb[:, 128 * gi:128 * (gi + 1)], wp_ref[gi]) for gi in range(4)], axis=1)
        silu_gc, silu_gc_grad = _silu_and_grad(gc_ref[...])
        dyc = dyc_ref[...]
        dz_ref[:, POOL_W:] = (dyc * (q * sc_ref[...]) * silu_gc_grad).astype(BF16)
        dyc0 = dyc * silu_gc
        dsc_ref[...] += _rowsum(dyc0 * q)
        dqb = (dyc0 * sc_ref[...]).astype(BF16)
        dp_cols = []
        for gi in range(4):
            cols = slice(128 * gi, 128 * (gi + 1))
            dwp_ref[gi] += _dot(pb[:, cols], dqb[:, cols], TN)
            dp_cols.append(_dot(dqb[:, cols], wp_ref[gi], NT))
        dp = jnp.concatenate(dp_cols, axis=1)
        e = dp / cnt
        fut = _window_sums(jnp.concatenate([e, ehalo[...]], axis=0), lookahead=True)[:tb, :]
        ehalo[...] = e[:HALO, :]
        dz_ref[:, 0:POOL_W] = (fut - dp).astype(BF16)

    const = lambda shape: pl.BlockSpec(shape, lambda i: (0,) * len(shape))
    return pl.pallas_call(
        body,
        name="pool_bwd",
        grid=(nb,),
        in_specs=[pl.BlockSpec((tb, POOL_W), lambda i: (rev(i), 0)),
                  pl.BlockSpec((tb, CB), lambda i: (rev(i), CB_XC)),
                  pl.BlockSpec((HALO, CB), lambda i: (jnp.maximum(rev(i) * per_halo - 1, 0), CB_XC)),
                  pl.BlockSpec((tb, CB), lambda i: (rev(i), CB_GC)),
                  const((4, 128, 128)), const((1, POOL_W))],
        out_specs=[pl.BlockSpec((tb, 2 * POOL_W), lambda i: (rev(i), 0)), const((4, 128, 128)), const((1, POOL_W))],
        out_shape=[jax.ShapeDtypeStruct((t, 2 * POOL_W), BF16),
                   jax.ShapeDtypeStruct((4, 128, 128), F32), jax.ShapeDtypeStruct((1, POOL_W), F32)],
        scratch_shapes=[pltpu.VMEM((HALO, POOL_W), F32)],
        compiler_params=_params("arbitrary"),
    )(dyc, z, z, z, wpool, scale)


def _adamw(w, m, v, parts, tr, name):
    r, c = w.shape
    n_slab = len(parts)
    per_slab = r // n_slab // tr

    def body(w_ref, m_ref, v_ref, *refs):
        p_refs, (g_ref, d_ref, nm_ref, nv_ref) = refs[:n_slab], refs[n_slab:]
        for s, p_ref in enumerate(p_refs):
            @pl.when(pl.program_id(0) // per_slab == s)
            def _(p_ref=p_ref):
                g = p_ref[0].astype(F32)
                for k in range(1, N_DEV):
                    g = g + p_ref[k].astype(F32)
                g_ref[...] = g

        g = g_ref[...]
        nm = ADAM_B1 * m_ref[...] + (1.0 - ADAM_B1) * g
        nv = ADAM_B2 * v_ref[...] + (1.0 - ADAM_B2) * (g * g)
        m_hat = nm / (1.0 - ADAM_B1 ** ADAM_STEP)
        v_hat = nv / (1.0 - ADAM_B2 ** ADAM_STEP)
        nm_ref[...] = nm
        nv_ref[...] = nv
        d_ref[...] = -ADAM_LR * (m_hat / (jnp.sqrt(v_hat) + ADAM_EPS) + ADAM_WD * w_ref[...])

    blk = pl.BlockSpec((tr, c), lambda i: (i, 0))
    slab = lambda s: pl.BlockSpec((N_DEV, tr, c), lambda i, s=s: (0, jnp.clip(i - s * per_slab, 0, per_slab - 1), 0))
    return pl.pallas_call(
        body,
        name=name,
        grid=(r // tr,),
        in_specs=[blk, blk, blk] + [slab(s) for s in range(n_slab)],
        out_specs=[blk, blk, blk, blk],
        out_shape=[jax.ShapeDtypeStruct((r, c), F32)] * 4,
        compiler_params=_params("arbitrary"),
    )(w, m, v, *parts)


MESH = pl.DeviceIdType.MESH
ANY = pl.BlockSpec(memory_space=pl.ANY)


def _dev_index(dev):
    return 4 * dev[0] + 2 * dev[1] + dev[2]


WHOLE_SHAPES = ((D_MODEL, IN_COLS), (D_MODEL, D_MODEL), (S5_W, S5_W))
SHARD_SHAPES = ((D_MODEL, SH_IN), (SH_OUT, D_MODEL), (SH_GLU, S5_W))


def _shard_of(ref, ti, idx):
    if ti == 0:
        return ref.at[:, pl.ds(pl.multiple_of(idx * SH_IN, 128), SH_IN)]
    rows = SHARD_SHAPES[ti][0]
    return ref.at[pl.ds(pl.multiple_of(idx * rows, rows), rows), :]


def _peer(mask, x, y, c):
    return (1 - x if mask & 4 else x, 1 - y if mask & 2 else y, 1 - c if mask & 1 else c)


def _allgather_weights(wi, wo, wg):
    n_t = 3

    def body(wi_ref, wo_ref, wg_ref, gi_ref, go_ref, gg_ref, send_sems, recv_sems, local_sems):
        x, y, c = lax.axis_index("x"), lax.axis_index("y"), lax.axis_index("c")
        me, sibling = (x, y, c), (x, y, 1 - c)
        chips = [(1 - x, y), (x, 1 - y), (1 - x, 1 - y)]
        shards = (wi_ref, wo_ref, wg_ref)
        wholes = (gi_ref, go_ref, gg_ref)

        def slot(ti, dev):
            return _shard_of(wholes[ti], ti, _dev_index(dev))

        def copy(k, ti, block, to, own=False):
            return pltpu.make_async_remote_copy(
                src_ref=shards[ti] if own else slot(ti, block), dst_ref=slot(ti, block),
                send_sem=send_sems.at[n_t * k + ti], recv_sem=recv_sems.at[n_t * k + ti],
                device_id=to, device_id_type=MESH)

        mine = [pltpu.make_async_copy(shards[ti], slot(ti, me), local_sems.at[ti]) for ti in range(n_t)]
        for cp in mine:
            cp.start()
        first = [copy(0, ti, me, sibling, own=True) for ti in range(n_t)]
        first += [copy(1 + j, ti, me, (*chip, c), own=True) for j, chip in enumerate(chips) for ti in range(n_t)]
        for cp in first:
            cp.start()
        passed = []
        for j, chip in enumerate(chips):
            for ti in range(n_t):
                copy(1 + j, ti, (*chip, c), me).wait_recv()
            onward = [copy(4 + j, ti, (*chip, c), sibling) for ti in range(n_t)]
            for cp in onward:
                cp.start()
            passed += onward
        for ti in range(n_t):
            copy(0, ti, sibling, me).wait_recv()
        for j, chip in enumerate(chips):
            for ti in range(n_t):
                copy(4 + j, ti, (*chip, 1 - c), me).wait_recv()
        for cp in first + passed:
            cp.wait_send()
        for cp in mine:
            cp.wait()

    return pl.pallas_call(
        body,
        name="allgather_weights",
        in_specs=[ANY, ANY, ANY],
        out_specs=[ANY, ANY, ANY],
        out_shape=[jax.ShapeDtypeStruct(s, BF16) for s in WHOLE_SHAPES],
        scratch_shapes=[pltpu.SemaphoreType.DMA((7 * n_t,)), pltpu.SemaphoreType.DMA((7 * n_t,)),
                        pltpu.SemaphoreType.DMA((n_t,))],
    )(wi, wo, wg)


HBM = pl.BlockSpec(memory_space=pltpu.HBM)
SEM = pl.BlockSpec(memory_space=pltpu.SEMAPHORE)
GATHER, SCATTER, SHARE = "gather", "scatter", "share"


def _split_route(kind, ti, sending, me_idx, p_idx, src_ref, land_ref):
    owner = me_idx if sending else p_idx
    if kind == GATHER:
        return src_ref, _shard_of(land_ref, ti, owner)
    if kind == SCATTER:
        return _shard_of(src_ref, ti, p_idx), land_ref.at[owner]
    return src_ref, land_ref.at[owner]


def _split_start(name, srcs, lands, kinds, after=None):
    n = len(srcs)
    arrays = list(srcs) + list(lands) + ([] if after is None else [after])

    def body(*refs):
        src_refs, land_refs = refs[0:n], refs[n:2 * n]
        send_sems, recv_sems, token = refs[len(arrays)], refs[len(arrays) + 1], refs[-1]
        x, y, c = lax.axis_index("x"), lax.axis_index("y"), lax.axis_index("c")
        me_idx = _dev_index((x, y, c))
        for mask in range(1, N_DEV):
            p = _peer(mask, x, y, c)
            for i, (kind, ti) in enumerate(kinds):
                k = (mask - 1) * n + i
                src, dst = _split_route(kind, ti, True, me_idx, _dev_index(p), src_refs[i], land_refs[i])
                pltpu.make_async_remote_copy(src_ref=src, dst_ref=dst, send_sem=send_sems.at[k], recv_sem=recv_sems.at[k],
                                             device_id=p, device_id_type=MESH).start()
        token[...] = jnp.zeros_like(token)

    n_copies = (N_DEV - 1) * n
    return pl.pallas_call(
        body,
        name=name,
        in_specs=[HBM] * len(arrays),
        out_specs=(SEM, SEM) + (HBM,) * (2 * n) + (pl.BlockSpec(memory_space=pltpu.VMEM),),
        out_shape=(pltpu.SemaphoreType.DMA((n_copies,)), pltpu.SemaphoreType.DMA((n_copies,)))
        + tuple(pltpu.HBM(a.shape, a.dtype) for a in arrays[:2 * n]) + (jax.ShapeDtypeStruct((8, 128), F32),),
        input_output_aliases={i: 2 + i for i in range(2 * n)},
        compiler_params=pltpu.CompilerParams(has_side_effects=pltpu.SideEffectType.DATAFLOW_SIDE_EFFECTING),
    )(*[pltpu.with_memory_space_constraint(a, pltpu.HBM) for a in arrays])


def _split_wait(name, started, kinds, after):
    n = len(kinds)
    send_sems, recv_sems, thru = started[0], started[1], started[2:2 + 2 * n]

    def body(*refs):
        src_refs, land_refs = refs[0:n], refs[n:2 * n]
        send_sems, recv_sems = refs[2 * n], refs[2 * n + 1]
        x, y, c = lax.axis_index("x"), lax.axis_index("y"), lax.axis_index("c")
        me_idx = _dev_index((x, y, c))
        for mask in range(1, N_DEV):
            p = _peer(mask, x, y, c)
            for i, (kind, ti) in enumerate(kinds):
                k = (mask - 1) * n + i
                src, dst = _split_route(kind, ti, False, me_idx, _dev_index(p), src_refs[i], land_refs[i])
                cp = pltpu.make_async_remote_copy(src_ref=src, dst_ref=dst, send_sem=send_sems.at[k],
                                                  recv_sem=recv_sems.at[k], device_id=p, device_id_type=MESH)
                cp.wait_send()
                cp.wait_recv()

    res = pl.pallas_call(
        body,
        name=name,
        in_specs=[HBM] * (2 * n) + [SEM, SEM, pl.BlockSpec(memory_space=pl.ANY)],
        out_specs=(HBM,) * (2 * n),
        out_shape=tuple(pltpu.HBM(a.shape, a.dtype) for a in thru),
        input_output_aliases={i: i for i in range(2 * n)},
        compiler_params=pltpu.CompilerParams(has_side_effects=pltpu.SideEffectType.DATAFLOW_SIDE_EFFECTING),
    )(*thru, send_sems, recv_sems, after)
    return res[n:2 * n]


def _with_own(whole_shape, dtype, own, start):
    return lax.dynamic_update_slice(lax.empty(whole_shape, dtype), own, start)


def _share_small(buf):
    def body(b_ref, o_ref, send_sems, recv_sems, local_sem):
        x, y, c = lax.axis_index("x"), lax.axis_index("y"), lax.axis_index("c")
        me_idx = _dev_index((x, y, c))
        local = pltpu.make_async_copy(b_ref, o_ref.at[me_idx], local_sem)
        local.start()

        def copy(mask, owner):
            return pltpu.make_async_remote_copy(
                src_ref=b_ref, dst_ref=o_ref.at[owner], send_sem=send_sems.at[mask - 1], recv_sem=recv_sems.at[mask - 1],
                device_id=_peer(mask, x, y, c), device_id_type=MESH)

        sends = [copy(mask, me_idx) for mask in range(1, N_DEV)]
        for cp in sends:
            cp.start()
        for mask in range(1, N_DEV):
            copy(mask, _dev_index(_peer(mask, x, y, c))).wait_recv()
        for cp in sends:
            cp.wait_send()
        local.wait()

    return pl.pallas_call(
        body,
        name="share_small",
        in_specs=[ANY],
        out_specs=ANY,
        out_shape=jax.ShapeDtypeStruct((N_DEV,) + buf.shape, buf.dtype),
        scratch_shapes=[pltpu.SemaphoreType.DMA((N_DEV - 1,)), pltpu.SemaphoreType.DMA((N_DEV - 1,)),
                        pltpu.SemaphoreType.DMA],
    )(buf)


def _s5_prep(lam_re, lam_im, b_re, b_im, c_re, c_im, d_skip, log_dt):
    dt = jnp.exp(log_dt)[:, None]
    mag = jnp.exp(lam_re * dt)
    a_re, a_im = mag * jnp.cos(lam_im * dt), mag * jnp.sin(lam_im * dt)
    den = lam_re * lam_re + lam_im * lam_im
    f_re = ((a_re - 1.0) * lam_re + a_im * lam_im) / den
    f_im = (a_im * lam_re - (a_re - 1.0) * lam_im) / den
    bb_re = f_re[..., None] * b_re - f_im[..., None] * b_im
    bb_im = f_re[..., None] * b_im + f_im[..., None] * b_re
    eye = jnp.eye(8, dtype=F32)

    def in_map(bb):
        return jnp.einsum("jgph,gk->ghjkp", bb.reshape(4, 8, S5_STATE, S5_CH), eye).reshape(128, N_STATE)

    def out_map(cm):
        return jnp.einsum("jghp,gk->ghjkp", cm.reshape(4, 8, S5_CH, S5_STATE), eye).reshape(128, N_STATE)

    a = jnp.stack([a_re.reshape(-1), a_im.reshape(-1)])
    bc = jnp.concatenate([in_map(bb_re), in_map(bb_im)], axis=1)
    cct = jnp.concatenate([out_map(c_re), -out_map(c_im)], axis=1)
    return a, bc, cct, d_skip.reshape(1, S5_W)


WEIGHTS = ["norm_g", "w_in", "lam_re", "lam_im", "b_re", "b_im", "c_re", "c_im", "d_skip", "log_dt", "w_glu", "b_glu",
           "ln_g", "ln_b", "w_s", "b_s", "w_pool", "pool_scale", "w_out", "final_g"]
SHARDED = ("w_in", "w_glu", "w_out")
SMALL = [n for n in WEIGHTS if n not in SHARDED]
INPUTS = ["x"] + WEIGHTS + ["loss_target"] + ["m_" + n for n in WEIGHTS] + ["v_" + n for n in WEIGHTS]
SMALL_B = ["norm_g", "final_g"]
SMALL_A = [n for n in SMALL if n not in SMALL_B]
SMALL_TILE = 16 * 128


def _pack_small(arrays, dtype):
    flat = jnp.concatenate([a.reshape(-1) for a in arrays])
    pad = (-flat.shape[0]) % SMALL_TILE
    return jnp.pad(flat, (0, pad)).astype(dtype).reshape(-1, 128)


def _unpack_small(packed, like):
    flat = packed.reshape(-1)
    out, off = [], 0
    for a in like:
        out.append(flat[off:off + a.size].reshape(a.shape))
        off += a.size
    return out


def _layer_fwd(p, l, x, wi, wo, wg, token=None):
    row = lambda v: v.reshape(1, -1)
    causal = jnp.tril(jnp.ones((CHUNK, CHUNK), dtype=bool))
    (a, bc, cct, dvec), prep_vjp = jax.vjp(
        _s5_prep, p["lam_re"][l], p["lam_im"][l], p["b_re"][l], p["b_im"][l], p["c_re"][l], p["c_im"][l],
        p["d_skip"][l], p["log_dt"][l])
    ws_f32 = jnp.where(causal[None], p["w_s"][l], 0.0)
    c = dict(
        x=x, wi=wi, wo=wo, wg=wg, a=a, bc=bc.astype(BF16), cc=cct.T.astype(BF16), dvec=dvec, prep_vjp=prep_vjp,
        ws=ws_f32.astype(BF16), wst=jnp.swapaxes(ws_f32, 1, 2).astype(BF16),
        bsx=jnp.broadcast_to(p["b_s"][l][:, :, None], (SGU_HEADS, CHUNK, 128)),
        wpool=p["w_pool"][l].astype(BF16), scale=row(p["pool_scale"][l]),
        lng=row(p["ln_g"][l]), lnb=row(p["ln_b"][l]), bglu=row(p["b_glu"][l]), norm_g=row(p["norm_g"][l]))
    c["z"], c["h"] = _rms_inproj(x, c["norm_g"], wi, token)
    c["ya"], c["s"], c["ys"] = _s5_fwd(c["z"], c["bc"], c["cc"], a, dvec, wg, c["bglu"])
    c["yb"] = _sgu_fwd(c["z"], c["lng"], c["lnb"], c["ws"], c["bsx"])
    c["yc"] = _pool_fwd(c["z"], c["wpool"], c["scale"])
    return _outproj(x, c["ya"], c["yb"], c["yc"], wo), c


def _mixers_bwd(c, dx, token=None):
    dya, dyb, dyc = _outproj_bwd(dx, c["wo"], token)
    dwo = _wgrad_out(c["ya"], c["yb"], c["yc"], dx, token)
    dz_s5, dbc, dcct, da, dd, dwg, dbglu = _s5_bwd(
        dya, c["ys"], c["z"], c["s"], c["bc"], c["cc"], c["a"], c["dvec"], c["wg"], c["bglu"])
    dz_sgu, dlng, dlnb, dws, dbsx = _sgu_bwd(dyb, c["z"], c["lng"], c["lnb"], c["ws"], c["wst"], c["bsx"])
    dz_pool, dwp, dsc = _pool_bwd(dyc, c["z"], c["wpool"], c["scale"])
    g_lam_re, g_lam_im, g_b_re, g_b_im, g_c_re, g_c_im, g_d, g_dt = c["prep_vjp"]((da, dbc, dcct, dd))
    small = dict(lam_re=g_lam_re, lam_im=g_lam_im, b_re=g_b_re, b_im=g_b_im, c_re=g_c_re,
                 c_im=g_c_im, d_skip=g_d, log_dt=g_dt, b_glu=dbglu.reshape(-1), ln_g=dlng.reshape(-1),
                 ln_b=dlnb.reshape(-1), w_s=dws, b_s=jnp.sum(dbsx, axis=-1), w_pool=dwp, pool_scale=dsc.reshape(-1))
    return (dz_s5, dz_sgu, dz_pool), dwo, dwg, small


def _inproj_grads(c, dz, dx, token_w=None, token_x=None):
    dwi = _wgrad_in(c["h"], *dz, token_w)
    dx, dnorm = _inproj_bwd(*dz, c["wi"], c["x"], c["norm_g"], dx, token_x)
    return dwi, dx, dnorm.reshape(-1)


def _own_index():
    return _dev_index((lax.axis_index("x"), lax.axis_index("y"), lax.axis_index("c")))


def _shard_start(ti, idx):
    return (0, idx * SH_IN) if ti == 0 else (idx * SHARD_SHAPES[ti][0], 0)


def kernel(x, norm_g, w_in, lam_re, lam_im, b_re, b_im, c_re, c_im, d_skip, log_dt, w_glu, b_glu, ln_g, ln_b, w_s, b_s, w_pool, pool_scale, w_out, final_g, loss_target, m_norm_g, m_w_in, m_lam_re, m_lam_im, m_b_re, m_b_im, m_c_re, m_c_im, m_d_skip, m_log_dt, m_w_glu, m_b_glu, m_ln_g, m_ln_b, m_w_s, m_b_s, m_w_pool, m_pool_scale, m_w_out, m_final_g, v_norm_g, v_w_in, v_lam_re, v_lam_im, v_b_re, v_b_im, v_c_re, v_c_im, v_d_skip, v_log_dt, v_w_glu, v_b_glu, v_ln_g, v_ln_b, v_w_s, v_b_s, v_w_pool, v_pool_scale, v_w_out, v_final_g):
    p = dict(zip(INPUTS, (x, norm_g, w_in, lam_re, lam_im, b_re, b_im, c_re, c_im, d_skip, log_dt, w_glu, b_glu, ln_g, ln_b, w_s, b_s, w_pool, pool_scale, w_out, final_g, loss_target, m_norm_g, m_w_in, m_lam_re, m_lam_im, m_b_re, m_b_im, m_c_re, m_c_im, m_d_skip, m_log_dt, m_w_glu, m_b_glu, m_ln_g, m_ln_b, m_w_s, m_b_s, m_w_pool, m_pool_scale, m_w_out, m_final_g, v_norm_g, v_w_in, v_lam_re, v_lam_im, v_b_re, v_b_im, v_c_re, v_c_im, v_d_skip, v_log_dt, v_w_glu, v_b_glu, v_ln_g, v_ln_b, v_w_s, v_b_s, v_w_pool, v_pool_scale, v_w_out, v_final_g)))

    me = _own_index()
    shards = [[w[l].astype(BF16) for w in (w_in, w_out, w_glu)] for l in range(DEPTH)]
    gather3 = [(GATHER, ti) for ti in range(3)]
    scatter3 = [(SCATTER, ti) for ti in range(3)]

    def stack_with_own(partial, ti):
        own = lax.dynamic_slice(partial, _shard_start(ti, me), SHARD_SHAPES[ti])
        return _with_own((N_DEV,) + SHARD_SHAPES[ti], partial.dtype, own[None], (me, 0, 0))

    whole0 = _allgather_weights(*shards[0])
    lands = [_with_own(WHOLE_SHAPES[ti], BF16, shards[1][ti], _shard_start(ti, me)) for ti in range(3)]
    gather1 = _split_start("gather1_start", shards[1], lands, gather3, after=whole0[2])
    x1, c0 = _layer_fwd(p, 0, x[0], *whole0, token=gather1[-1])
    whole1 = _split_wait("gather1_wait", gather1, gather3, x1)
    x2, c1 = _layer_fwd(p, 1, x1, *whole1)

    dx, loss_tile, dfinal = _final_loss(x2, final_g.reshape(1, -1), loss_target[0])
    loss = lax.psum(loss_tile[0, 0], ("x", "y", "c"))

    dz1, dwo1, dwg1, small1 = _mixers_bwd(c1, dx)
    dwi1, dx, dnorm1 = _inproj_grads(c1, dz1, dx)
    partials1 = [dwi1, dwo1, dwg1.astype(BF16)]
    grads1 = _split_start("grads1_start", partials1, [stack_with_own(g, ti) for ti, g in enumerate(partials1)], scatter3)
    dz0, dwo0, dwg0, small0 = _mixers_bwd(c0, dx, token=grads1[-1])
    small_a = _pack_small([jnp.stack([small0[n], small1[n]]) for n in SMALL_A], BF16)
    srcs_a, kinds_a = [dwo0, dwg0.astype(BF16), small_a], [(SCATTER, 1), (SCATTER, 2), (SHARE, None)]
    lands_a = [stack_with_own(srcs_a[0], 1), stack_with_own(srcs_a[1], 2),
               _with_own((N_DEV,) + small_a.shape, BF16, small_a[None], (me, 0, 0))]
    grads0a = _split_start("grads0a_start", srcs_a, lands_a, kinds_a)
    dwi0 = _wgrad_in(c0["h"], *dz0, grads0a[-1])
    grads0b = _split_start("grads0b_start", [dwi0], [stack_with_own(dwi0, 0)], [(SCATTER, 0)])
    dx, dnorm0 = _inproj_bwd(*dz0, c0["wi"], c0["x"], c0["norm_g"], dx, grads0b[-1])
    parts1 = _split_wait("grads1_wait", grads1, scatter3, dx)
    r_out0, r_glu0, r_a = _split_wait("grads0a_wait", grads0a, kinds_a, dx)
    (r_in0,) = _split_wait("grads0b_wait", grads0b, [(SCATTER, 0)], dx)
    parts0 = [r_in0, r_out0, r_glu0]
    r_b = _share_small(_pack_small([jnp.stack([dnorm0.reshape(-1), dnorm1]), dfinal.reshape(-1)], F32))

    out = {}

    def adam(name, ti, tr):
        shape2d = (DEPTH * SHARD_SHAPES[ti][0], SHARD_SHAPES[ti][1])
        res = _adamw(p[name].reshape(shape2d), p["m_" + name].reshape(shape2d), p["v_" + name].reshape(shape2d),
                     [parts0[ti], parts1[ti]], tr, "adamw_" + name)
        out[name] = [r.reshape(p[name].shape) for r in res]

    adam("w_in", 0, 256)
    adam("w_out", 1, 128)
    adam("w_glu", 2, 64)
    for names, parts, steps, tag in ((SMALL_A, r_a, 7, "a"), (SMALL_B, r_b, 1, "b")):
        like = [p[n] for n in names]
        packed = _adamw(*[_pack_small([p[pre + n] for n in names], F32) for pre in ("", "m_", "v_")],
                        [parts], parts.shape[1] // steps, "adamw_small_" + tag)
        for n, vals in zip(names, zip(*[_unpack_small(r, like) for r in packed])):
            out[n] = list(vals)

    return (loss, dx[None], *[out[n][0] for n in WEIGHTS], *[out[n][1] for n in WEIGHTS],
            *[out[n][2] for n in WEIGHTS], *[out[n][3] for n in WEIGHTS])
```

```python
import functools
import math

import jax
import jax.numpy as jnp
from jax import lax
from jax.experimental import pallas as pl
from jax.experimental.pallas import tpu as pltpu

F32 = jnp.float32
BF16 = jnp.bfloat16

D_MODEL = 2048
DEPTH = 2
S5_W, SGU_W, POOL_W = 512, 1024, 512
S5_GROUPS, S5_STATE, S5_CH = 32, 64, 16
N_STATE = S5_GROUPS * S5_STATE
CHUNK = 128
SGU_HEADS = 8
POOL_WINDOWS = (2, 4, 8, 16)
IN_COLS = 5120
RMS_EPS = 1e-6
LN_EPS = 1e-5
ADAM_LR, ADAM_B1, ADAM_B2, ADAM_EPS, ADAM_WD, ADAM_STEP = 0.001, 0.9, 0.999, 1e-08, 0.01, 10

CB = 512
N_CB = IN_COLS // CB
CB_XA, CB_U, CB_V, CB_XC, CB_GA, CB_GB, CB_GC = 0, 1, 3, 5, 6, 7, 9

N_DEV = 8
SH_IN = IN_COLS // N_DEV
SH_OUT = D_MODEL // N_DEV
SH_GLU = S5_W // N_DEV

VMEM_LIMIT = 52 * 1024 * 1024
HALO = 16
LANE_CH = 512

TN = (((0,), (0,)), ((), ()))
NT = (((1,), (1,)), ((), ()))


def _params(*sem):
    return pltpu.CompilerParams(dimension_semantics=sem if sem else None, vmem_limit_bytes=VMEM_LIMIT)


def _dot(a, b, dims=None):
    if dims is None:
        return jnp.dot(a, b, preferred_element_type=F32)
    return lax.dot_general(a, b, dims, preferred_element_type=F32)


_GELU_C = math.sqrt(2.0 / math.pi)


def _gelu(x):
    return 0.5 * x * (1.0 + jnp.tanh(_GELU_C * (x + 0.044715 * x * x * x)))


def _gelu_grad(x):
    t = jnp.tanh(_GELU_C * (x + 0.044715 * x * x * x))
    return 0.5 * (1.0 + t) + 0.5 * x * (1.0 - t * t) * _GELU_C * (1.0 + 3.0 * 0.044715 * x * x)


def _silu_and_grad(x):
    s = jax.nn.sigmoid(x)
    return x * s, s * (1.0 + x * (1.0 - s))


def _rowsum(x):
    return jnp.sum(x, axis=0, keepdims=True)


def _after(token):
    if token is None:
        return [], []
    return [pl.BlockSpec((8, 128), lambda *_: (0, 0))], [token]


def _rms_inproj(x, g, w, token=None, tm=1024):
    t = x.shape[0]
    tm = min(tm, t)
    after_specs, after = _after(token)

    def body(x_ref, g_ref, w_ref, *rest):
        z_ref, h_ref = rest[-2:]

        @pl.when(pl.program_id(1) == 0)
        def _():
            xv = x_ref[...]
            r = lax.rsqrt(jnp.mean(xv * xv, axis=-1, keepdims=True) + RMS_EPS)
            h_ref[...] = (xv * r * g_ref[...]).astype(BF16)

        z_ref[...] = _dot(h_ref[...], w_ref[...])

    return pl.pallas_call(
        body,
        name="rms_inproj",
        grid=(t // tm, N_CB),
        in_specs=[
            pl.BlockSpec((tm, D_MODEL), lambda m, n: (m, 0)),
            pl.BlockSpec((1, D_MODEL), lambda m, n: (0, 0)),
            pl.BlockSpec((D_MODEL, CB), lambda m, n: (0, n)),
        ] + after_specs,
        out_specs=[
            pl.BlockSpec((tm, CB), lambda m, n: (m, n)),
            pl.BlockSpec((tm, D_MODEL), lambda m, n: (m, 0)),
        ],
        out_shape=[jax.ShapeDtypeStruct((t, IN_COLS), F32), jax.ShapeDtypeStruct((t, D_MODEL), BF16)],
        compiler_params=_params("arbitrary", "arbitrary"),
    )(x, g, w, *after)


def _outproj(x, ya, yb, yc, w, tm=1024, tn=1024):
    t = x.shape[0]
    tm = min(tm, t)

    def body(x_ref, ya_ref, yb_ref, yc_ref, w_ref, o_ref):
        acc = _dot(ya_ref[...], w_ref[0:S5_W, :])
        acc += _dot(yb_ref[...], w_ref[S5_W:S5_W + SGU_W, :])
        acc += _dot(yc_ref[...], w_ref[S5_W + SGU_W:, :])
        o_ref[...] = x_ref[...] + acc

    return pl.pallas_call(
        body,
        name="outproj",
        grid=(t // tm, D_MODEL // tn),
        in_specs=[
            pl.BlockSpec((tm, tn), lambda m, n: (m, n)),
            pl.BlockSpec((tm, S5_W), lambda m, n: (m, 0)),
            pl.BlockSpec((tm, SGU_W), lambda m, n: (m, 0)),
            pl.BlockSpec((tm, POOL_W), lambda m, n: (m, 0)),
            pl.BlockSpec((D_MODEL, tn), lambda m, n: (0, n)),
        ],
        out_specs=pl.BlockSpec((tm, tn), lambda m, n: (m, n)),
        out_shape=jax.ShapeDtypeStruct((t, D_MODEL), F32),
        compiler_params=_params("arbitrary", "arbitrary"),
    )(x, ya, yb, yc, w)


def _outproj_bwd(dx, w, token=None, tm=512):
    t = dx.shape[0]
    after_specs, after = _after(token)

    def body(dx_ref, w_ref, *rest):
        dya_ref, dyb_ref, dyc_ref = rest[-3:]
        dy = _dot(dx_ref[...].astype(BF16), w_ref[...], NT)
        dya_ref[...] = dy[:, 0:S5_W]
        dyb_ref[...] = dy[:, S5_W:S5_W + SGU_W]
        dyc_ref[...] = dy[:, S5_W + SGU_W:]

    return pl.pallas_call(
        body,
        name="outproj_bwd",
        grid=(t // tm,),
        in_specs=[
            pl.BlockSpec((tm, D_MODEL), lambda m: (m, 0)),
            pl.BlockSpec((D_MODEL, D_MODEL), lambda m: (0, 0)),
        ] + after_specs,
        out_specs=[
            pl.BlockSpec((tm, S5_W), lambda m: (m, 0)),
            pl.BlockSpec((tm, SGU_W), lambda m: (m, 0)),
            pl.BlockSpec((tm, POOL_W), lambda m: (m, 0)),
        ],
        out_shape=[
            jax.ShapeDtypeStruct((t, S5_W), F32),
            jax.ShapeDtypeStruct((t, SGU_W), F32),
            jax.ShapeDtypeStruct((t, POOL_W), F32),
        ],
        compiler_params=_params("arbitrary"),
    )(dx, w, *after)


def _dz_piece_maps():
    s5_map = lambda j: jnp.where(j >= CB_GA, 1, 0)
    sgu_map = lambda j: jnp.clip(jnp.where(j <= 4, j - 1, j - 3), 0, 5)
    pool_map = lambda j: jnp.where(j >= CB_GC, 1, 0)
    return s5_map, sgu_map, pool_map


def _pick_piece(j):
    is_s5 = jnp.logical_or(j == CB_XA, j == CB_GA)
    is_pool = jnp.logical_or(j == CB_XC, j == CB_GC)
    return is_s5, is_pool, jnp.logical_not(jnp.logical_or(is_s5, is_pool))


def _inproj_bwd(dz_s5, dz_sgu, dz_pool, w, x, g, dxo, token=None, tm=512):
    t = x.shape[0]
    s5_map, sgu_map, pool_map = _dz_piece_maps()
    after_specs, after = _after(token)

    def body(s5_ref, sgu_ref, pool_ref, w_ref, x_ref, g_ref, dxo_ref, *rest):
        dx_ref, dg_ref, acc = rest[-3:]
        m, j = pl.program_id(0), pl.program_id(1)

        @pl.when(jnp.logical_and(m == 0, j == 0))
        def _():
            dg_ref[...] = jnp.zeros_like(dg_ref)

        @pl.when(j == 0)
        def _():
            acc[...] = jnp.zeros_like(acc)

        is_s5, is_pool, is_sgu = _pick_piece(j)

        @pl.when(is_s5)
        def _():
            acc[...] += _dot(s5_ref[...], w_ref[...], NT)

        @pl.when(is_sgu)
        def _():
            acc[...] += _dot(sgu_ref[...], w_ref[...], NT)

        @pl.when(is_pool)
        def _():
            acc[...] += _dot(pool_ref[...], w_ref[...], NT)

        @pl.when(j == N_CB - 1)
        def _():
            xv = x_ref[...]
            r = lax.rsqrt(jnp.mean(xv * xv, axis=-1, keepdims=True) + RMS_EPS)
            n = xv * r
            dh = acc[...]
            dg_ref[...] += _rowsum(dh * n)
            dn = dh * g_ref[...]
            dx_ref[...] = dxo_ref[...] + r * (dn - n * jnp.mean(dn * n, axis=-1, keepdims=True))

    return pl.pallas_call(
        body,
        name="inproj_bwd",
        grid=(t // tm, N_CB),
        in_specs=[
            pl.BlockSpec((tm, CB), lambda m, j: (m, s5_map(j))),
            pl.BlockSpec((tm, CB), lambda m, j: (m, sgu_map(j))),
            pl.BlockSpec((tm, CB), lambda m, j: (m, pool_map(j))),
            pl.BlockSpec((D_MODEL, CB), lambda m, j: (0, j)),
            pl.BlockSpec((tm, D_MODEL), lambda m, j: (m, 0)),
            pl.BlockSpec((1, D_MODEL), lambda m, j: (0, 0)),
            pl.BlockSpec((tm, D_MODEL), lambda m, j: (m, 0)),
        ] + after_specs,
        out_specs=[
            pl.BlockSpec((tm, D_MODEL), lambda m, j: (m, 0)),
            pl.BlockSpec((1, D_MODEL), lambda m, j: (0, 0)),
        ],
        out_shape=[jax.ShapeDtypeStruct((t, D_MODEL), F32), jax.ShapeDtypeStruct((1, D_MODEL), F32)],
        scratch_shapes=[pltpu.VMEM((tm, D_MODEL), F32)],
        compiler_params=_params("arbitrary", "arbitrary"),
    )(dz_s5, dz_sgu, dz_pool, w, x, g, dxo, *after)


def _wgrad_in(h, dz_s5, dz_sgu, dz_pool, token=None, tm=512):
    t = h.shape[0]
    s5_map, sgu_map, pool_map = _dz_piece_maps()
    after_specs, after = _after(token)

    def body(h_ref, s5_ref, sgu_ref, pool_ref, *rest):
        o_ref, acc = rest[-2:]
        j, m = pl.program_id(0), pl.program_id(1)

        @pl.when(m == 0)
        def _():
            acc[...] = jnp.zeros_like(acc)

        is_s5, is_pool, is_sgu = _pick_piece(j)

        @pl.when(is_s5)
        def _():
            acc[...] += _dot(s5_ref[...], h_ref[...], TN)

        @pl.when(is_sgu)
        def _():
            acc[...] += _dot(sgu_ref[...], h_ref[...], TN)

        @pl.when(is_pool)
        def _():
            acc[...] += _dot(pool_ref[...], h_ref[...], TN)

        @pl.when(m == pl.num_programs(1) - 1)
        def _():
            o_ref[...] = acc[...].T.astype(BF16)

    return pl.pallas_call(
        body,
        name="wgrad_in",
        grid=(N_CB, t // tm),
        in_specs=[
            pl.BlockSpec((tm, D_MODEL), lambda j, m: (m, 0)),
            pl.BlockSpec((tm, CB), lambda j, m: (m, s5_map(j))),
            pl.BlockSpec((tm, CB), lambda j, m: (m, sgu_map(j))),
            pl.BlockSpec((tm, CB), lambda j, m: (m, pool_map(j))),
        ] + after_specs,
        out_specs=pl.BlockSpec((D_MODEL, CB), lambda j, m: (0, j)),
        out_shape=jax.ShapeDtypeStruct((D_MODEL, IN_COLS), BF16),
        scratch_shapes=[pltpu.VMEM((CB, D_MODEL), F32)],
        compiler_params=_params("arbitrary", "arbitrary"),
    )(h, dz_s5, dz_sgu, dz_pool, *after)


def _wgrad_out(ya, yb, yc, dx, token=None, tm=512, tn=512):
    t = dx.shape[0]
    after_specs, after = _after(token)

    def body(ya_ref, yb_ref, yc_ref, dx_ref, *rest):
        o_ref, acc = rest[-2:]
        m = pl.program_id(1)

        @pl.when(m == 0)
        def _():
            acc[...] = jnp.zeros_like(acc)

        dxb = dx_ref[...].astype(BF16)
        acc[:, 0:S5_W] += _dot(dxb, ya_ref[...], TN)
        acc[:, S5_W:S5_W + SGU_W] += _dot(dxb, yb_ref[...], TN)
        acc[:, S5_W + SGU_W:] += _dot(dxb, yc_ref[...], TN)

        @pl.when(m == pl.num_programs(1) - 1)
        def _():
            o_ref[...] = acc[...].T.astype(BF16)

    return pl.pallas_call(
        body,
        name="wgrad_out",
        grid=(D_MODEL // tn, t // tm),
        in_specs=[
            pl.BlockSpec((tm, S5_W), lambda n, m: (m, 0)),
            pl.BlockSpec((tm, SGU_W), lambda n, m: (m, 0)),
            pl.BlockSpec((tm, POOL_W), lambda n, m: (m, 0)),
            pl.BlockSpec((tm, tn), lambda n, m: (m, n)),
        ] + after_specs,
        out_specs=pl.BlockSpec((D_MODEL, tn), lambda n, m: (0, n)),
        out_shape=jax.ShapeDtypeStruct((D_MODEL, D_MODEL), BF16),
        scratch_shapes=[pltpu.VMEM((tn, D_MODEL), F32)],
        compiler_params=_params("arbitrary", "arbitrary"),
    )(ya, yb, yc, dx, *after)


def _final_loss(x, g, target, tm=512):
    t = x.shape[0]

    def body(x_ref, g_ref, t_ref, dx_ref, loss_ref, dg_ref):
        @pl.when(pl.program_id(0) == 0)
        def _():
            loss_ref[...] = jnp.zeros_like(loss_ref)
            dg_ref[...] = jnp.zeros_like(dg_ref)

        xv = x_ref[...]
        gv = g_ref[...]
        r = lax.rsqrt(jnp.mean(xv * xv, axis=-1, keepdims=True) + RMS_EPS)
        n = xv * r
        err = n * gv - t_ref[...]
        loss_ref[...] += 0.5 * jnp.sum(jnp.mean(err * err, axis=-1, keepdims=True))
        dy = err * (1.0 / D_MODEL)
        dg_ref[...] += _rowsum(dy * n)
        dn = dy * gv
        dx_ref[...] = r * (dn - n * jnp.mean(dn * n, axis=-1, keepdims=True))

    return pl.pallas_call(
        body,
        name="final_loss",
        grid=(t // tm,),
        in_specs=[
            pl.BlockSpec((tm, D_MODEL), lambda m: (m, 0)),
            pl.BlockSpec((1, D_MODEL), lambda m: (0, 0)),
            pl.BlockSpec((tm, D_MODEL), lambda m: (m, 0)),
        ],
        out_specs=[
            pl.BlockSpec((tm, D_MODEL), lambda m: (m, 0)),
            pl.BlockSpec((8, 128), lambda m: (0, 0)),
            pl.BlockSpec((1, D_MODEL), lambda m: (0, 0)),
        ],
        out_shape=[
            jax.ShapeDtypeStruct((t, D_MODEL), F32),
            jax.ShapeDtypeStruct((8, 128), F32),
            jax.ShapeDtypeStruct((1, D_MODEL), F32),
        ],
        compiler_params=_params("arbitrary"),
    )(x, g, target)


N_Q = 2 * N_STATE // LANE_CH
N_LT = 2 * N_STATE // 128
N_PAIR = N_LT // 2
SEG = 8
PAIR_GROUP = 8
S5_TB = 256


def _cmul_add(b_re, b_im, a_re, a_im, s_re, s_im):
    return b_re + (a_re * s_re - a_im * s_im), b_im + (a_re * s_im + a_im * s_re)


def _s5_scan(st, carry, a_ref, pw_ref, tb, reverse):
    seg_len = tb // SEG
    sign = -1.0 if reverse else 1.0
    sub = lax.broadcasted_iota(jnp.int32, (SEG, 128), 0)
    chain_row = 0 if reverse else seg_len - 1
    full = lambda row: jnp.broadcast_to(row, (SEG, 128))
    for p0 in range(0, N_PAIR, PAIR_GROUP):
        pairs = list(range(p0, p0 + PAIR_GROUP))
        a_re = [full(a_ref[p:p + 1, :]) for p in pairs]
        a_im = [sign * full(a_ref[N_PAIR + p:N_PAIR + p + 1, :]) for p in pairs]

        def step(k, c, pairs=pairs, a_re=a_re, a_im=a_im):
            rows = pl.ds((seg_len - 1 - k) if reverse else k, SEG, stride=seg_len)
            out = []
            for i, p in enumerate(pairs):
                t_re, t_im = pl.ds(p, 1), pl.ds(N_PAIR + p, 1)
                n_re, n_im = _cmul_add(st[t_re, rows, :][0], st[t_im, rows, :][0], a_re[i], a_im[i], c[2 * i], c[2 * i + 1])
                st[t_re, rows, :] = n_re[None]
                st[t_im, rows, :] = n_im[None]
                out += [n_re, n_im]
            return tuple(out)

        ends = lax.fori_loop(0, seg_len, step, tuple(jnp.zeros((SEG, 128), F32) for _ in range(2 * PAIR_GROUP)))
        for i, p in enumerate(pairs):
            e_re, e_im = ends[2 * i], ends[2 * i + 1]
            w_re = full(pw_ref[p, chain_row:chain_row + 1, :])
            w_im = full(pw_ref[N_PAIR + p, chain_row:chain_row + 1, :])
            c_re, c_im = full(carry[p:p + 1, :]), full(carry[N_PAIR + p:N_PAIR + p + 1, :])
            for hop in range(SEG - 1):
                n_re, n_im = _cmul_add(e_re, e_im, w_re, w_im, c_re, c_im)
                target = SEG - 2 - hop if reverse else hop + 1
                shift = SEG - 1 if reverse else 1
                c_re = jnp.where(sub == target, pltpu.roll(n_re, shift, 0), c_re)
                c_im = jnp.where(sub == target, pltpu.roll(n_im, shift, 0), c_im)
            n_re, n_im = _cmul_add(e_re, e_im, w_re, w_im, c_re, c_im)
            last = 0 if reverse else SEG - 1
            carry[p:p + 1, :] = n_re[last:last + 1, :]
            carry[N_PAIR + p:N_PAIR + p + 1, :] = n_im[last:last + 1, :]
            pw_re, pw_im = pw_ref[p], pw_ref[N_PAIR + p]
            for s in range(SEG):
                rows = slice(seg_len * s, seg_len * (s + 1))
                in_re = jnp.broadcast_to(c_re[s:s + 1, :], (seg_len, 128))
                in_im = jnp.broadcast_to(c_im[s:s + 1, :], (seg_len, 128))
                st[p, rows, :], st[N_PAIR + p, rows, :] = _cmul_add(
                    st[p, rows, :], st[N_PAIR + p, rows, :], pw_re, pw_im, in_re, in_im)


def _lane_chunk(ref, q):
    return jnp.concatenate([ref[4 * q + i] for i in range(4)], axis=1)


def _put_lane_chunk(ref, q, value):
    for i in range(4):
        ref[4 * q + i] = value[:, 128 * i:128 * (i + 1)]


def _s5_fwd(z, bc, cc, a, pw, dvec, wglu, bglu, tb=256):
    t = z.shape[0]

    def body(xa_ref, ga_ref, bc_ref, cc_ref, a_ref, pw_ref, d_ref, wglu_ref, bglu_ref, ya_ref, s_ref, ys_ref, st, carry):
        @pl.when(pl.program_id(0) == 0)
        def _():
            carry[...] = jnp.zeros_like(carry)

        xa = xa_ref[...]
        xab = xa.astype(BF16)
        for q in range(N_Q):
            _put_lane_chunk(st, q, _dot(xab[:, 128 * (q % 4):128 * (q % 4) + 128], bc_ref[:, pl.ds(LANE_CH * q, LANE_CH)]))
        _s5_scan(st, carry, a_ref, pw_ref, tb, reverse=False)
        s_ref[...] = st[...].astype(BF16)
        cols = []
        for j in range(4):
            lo, hi = LANE_CH * j, N_STATE + LANE_CH * j
            cols.append(_dot(_lane_chunk(s_ref, j), cc_ref[lo:lo + LANE_CH, :])
                        + _dot(_lane_chunk(s_ref, 4 + j), cc_ref[hi:hi + LANE_CH, :]))
        ys = jnp.concatenate(cols, axis=1) + d_ref[...] * xa
        ys_ref[...] = ys
        ya1 = _gelu(ys)
        pre = _dot(ya1.astype(BF16), wglu_ref[...]) + bglu_ref[...]
        silu_ga, _ = _silu_and_grad(ga_ref[...])
        ya_ref[...] = (ya1 * jax.nn.sigmoid(pre) * silu_ga).astype(BF16)

    const = lambda shape: pl.BlockSpec(shape, lambda i: (0,) * len(shape))
    return pl.pallas_call(
        body,
        name="s5_fwd",
        grid=(t // tb,),
        in_specs=[
            pl.BlockSpec((tb, CB), lambda i: (i, CB_XA)),
            pl.BlockSpec((tb, CB), lambda i: (i, CB_GA)),
            const((128, 2 * N_STATE)),
            const((2 * N_STATE, 128)),
            const((N_LT, 128)),
            const((N_LT, tb // SEG, 128)),
            const((1, S5_W)),
            const((S5_W, S5_W)),
            const((1, S5_W)),
        ],
        out_specs=[
            pl.BlockSpec((tb, S5_W), lambda i: (i, 0)),
            pl.BlockSpec((N_LT, tb, 128), lambda i: (0, i, 0)),
            pl.BlockSpec((tb, S5_W), lambda i: (i, 0)),
        ],
        out_shape=[
            jax.ShapeDtypeStruct((t, S5_W), BF16),
            jax.ShapeDtypeStruct((N_LT, t, 128), BF16),
            jax.ShapeDtypeStruct((t, S5_W), F32),
        ],
        scratch_shapes=[pltpu.VMEM((N_LT, tb, 128), F32), pltpu.VMEM((N_LT, 128), F32)],
        compiler_params=_params("arbitrary"),
    )(z, z, bc, cc, a, pw, dvec, wglu, bglu)


def _s5_bwd(dya, ys, z, s, bc, cc, a, pw, dvec, wglu, bglu, tb=256):
    t = z.shape[0]
    nb = t // tb
    rev = lambda i: nb - 1 - i

    def body(dya_ref, ys_ref, xa_ref, ga_ref, s_ref, sp_ref, bc_ref, cc_ref, a_ref, pw_ref, d_ref, wglu_ref, bglu_ref,
             dz_ref, dbc_ref, dcct_ref, da_ref, dd_ref, dwglu_ref, dbglu_ref, g, carry):
        i = pl.program_id(0)

        @pl.when(i == 0)
        def _():
            carry[...] = jnp.zeros_like(carry)
            for r in (dbc_ref, dcct_ref, da_ref, dd_ref, dwglu_ref, dbglu_ref):
                r[...] = jnp.zeros_like(r)

        ys = ys_ref[...]
        xa = xa_ref[...]
        ga = ga_ref[...]
        dya = dya_ref[...]
        ya1 = _gelu(ys)
        ya1b = ya1.astype(BF16)
        sg = jax.nn.sigmoid(_dot(ya1b, wglu_ref[...]) + bglu_ref[...])
        silu_ga, silu_ga_grad = _silu_and_grad(ga)
        dz_ref[:, S5_W:] = (dya * (ya1 * sg) * silu_ga_grad).astype(BF16)
        dya2 = dya * silu_ga
        dpre = dya2 * ya1 * sg * (1.0 - sg)
        dbglu_ref[...] += _rowsum(dpre)
        dpreb = dpre.astype(BF16)
        dwglu_ref[...] += _dot(ya1b, dpreb, TN)
        dys = (dya2 * sg + _dot(dpreb, wglu_ref[...], NT)) * _gelu_grad(ys)
        dd_ref[...] += _rowsum(dys * xa)
        dysb = dys.astype(BF16)
        xab = xa.astype(BF16)

        for q in range(N_Q):
            cq = pl.ds(LANE_CH * q, LANE_CH)
            x0 = 128 * (q % 4)
            dcct_ref[:, cq] += _dot(dysb[:, x0:x0 + 128], _lane_chunk(s_ref, q), TN)
            _put_lane_chunk(g, q, _dot(dysb[:, x0:x0 + 128], cc_ref[cq, :], NT))
        _s5_scan(g, carry, a_ref, pw_ref, tb, reverse=True)

        first = (lax.broadcasted_iota(jnp.int32, (tb, 128), 0) == 0)
        have_prev = i < nb - 1
        for p in range(N_PAIR):
            g_re, g_im = g[p], g[N_PAIR + p]
            p_re = jnp.where(have_prev, sp_ref[p].astype(F32)[HALO - 1:HALO, :], 0.0)
            p_im = jnp.where(have_prev, sp_ref[N_PAIR + p].astype(F32)[HALO - 1:HALO, :], 0.0)
            sp_re = jnp.where(first, p_re, pltpu.roll(s_ref[p].astype(F32), 1, 0))
            sp_im = jnp.where(first, p_im, pltpu.roll(s_ref[N_PAIR + p].astype(F32), 1, 0))
            da_ref[p:p + 1, :] += _rowsum(sp_re * g_re + sp_im * g_im)
            da_ref[N_PAIR + p:N_PAIR + p + 1, :] += _rowsum(sp_re * g_im - sp_im * g_re)
        dxa_cols = []
        for j in range(4):
            re = pl.ds(LANE_CH * j, LANE_CH)
            im = pl.ds(N_STATE + LANE_CH * j, LANE_CH)
            x0 = 128 * j
            gb_re, gb_im = _lane_chunk(g, j).astype(BF16), _lane_chunk(g, 4 + j).astype(BF16)
            dbc_ref[:, re] += _dot(xab[:, x0:x0 + 128], gb_re, TN)
            dbc_ref[:, im] += _dot(xab[:, x0:x0 + 128], gb_im, TN)
            dxa_cols.append(_dot(gb_re, bc_ref[:, re], NT) + _dot(gb_im, bc_ref[:, im], NT))
        dz_ref[:, 0:S5_W] = (dys * d_ref[...] + jnp.concatenate(dxa_cols, axis=1)).astype(BF16)

    const = lambda shape: pl.BlockSpec(shape, lambda i: (0,) * len(shape))
    per_halo = tb // HALO
    return pl.pallas_call(
        body,
        name="s5_bwd",
        grid=(nb,),
        in_specs=[
            pl.BlockSpec((tb, S5_W), lambda i: (rev(i), 0)),
            pl.BlockSpec((tb, S5_W), lambda i: (rev(i), 0)),
            pl.BlockSpec((tb, CB), lambda i: (rev(i), CB_XA)),
            pl.BlockSpec((tb, CB), lambda i: (rev(i), CB_GA)),
            pl.BlockSpec((N_LT, tb, 128), lambda i: (0, rev(i), 0)),
            pl.BlockSpec((N_LT, HALO, 128), lambda i: (0, jnp.maximum(rev(i) * per_halo - 1, 0), 0)),
            const((128, 2 * N_STATE)),
            const((2 * N_STATE, 128)),
            const((N_LT, 128)),
            const((N_LT, tb // SEG, 128)),
            const((1, S5_W)),
            const((S5_W, S5_W)),
            const((1, S5_W)),
        ],
        out_specs=[
            pl.BlockSpec((tb, 2 * CB), lambda i: (rev(i), 0)),
            const((128, 2 * N_STATE)),
            const((128, 2 * N_STATE)),
            const((N_LT, 128)),
            const((1, S5_W)),
            const((S5_W, S5_W)),
            const((1, S5_W)),
        ],
        out_shape=[
            jax.ShapeDtypeStruct((t, 2 * CB), BF16),
            jax.ShapeDtypeStruct((128, 2 * N_STATE), F32),
            jax.ShapeDtypeStruct((128, 2 * N_STATE), F32),
            jax.ShapeDtypeStruct((N_LT, 128), F32),
            jax.ShapeDtypeStruct((1, S5_W), F32),
            jax.ShapeDtypeStruct((S5_W, S5_W), F32),
            jax.ShapeDtypeStruct((1, S5_W), F32),
        ],
        scratch_shapes=[pltpu.VMEM((N_LT, tb, 128), F32), pltpu.VMEM((N_LT, 128), F32)],
        compiler_params=_params("arbitrary"),
    )(dya, ys, z, z, s, s, bc, cc, a, pw, dvec, wglu, bglu)


def _sgu_norm(v0, v1, lng_ref, lnb_ref):
    g0, g1 = _gelu(v0), _gelu(v1)
    mu = (jnp.sum(g0, axis=-1, keepdims=True) + jnp.sum(g1, axis=-1, keepdims=True)) * (1.0 / SGU_W)
    c0, c1 = g0 - mu, g1 - mu
    var = (jnp.sum(c0 * c0, axis=-1, keepdims=True) + jnp.sum(c1 * c1, axis=-1, keepdims=True)) * (1.0 / SGU_W)
    rstd = lax.rsqrt(var + LN_EPS)
    vh0, vh1 = c0 * rstd, c1 * rstd
    vn0 = vh0 * lng_ref[:, 0:CB] + lnb_ref[:, 0:CB]
    vn1 = vh1 * lng_ref[:, CB:] + lnb_ref[:, CB:]
    return (vh0, vh1), (vn0, vn1), rstd


def _sgu_fwd(z, lng, lnb, ws, bsx, tb=256):
    t = z.shape[0]

    def body(u0_ref, u1_ref, v0_ref, v1_ref, gb0_ref, gb1_ref, lng_ref, lnb_ref, ws_ref, bsx_ref, yb_ref):
        _, (vn0, vn1), _ = _sgu_norm(v0_ref[...], v1_ref[...], lng_ref, lnb_ref)
        for half, (vn, u_ref, gb_ref) in enumerate(((vn0, u0_ref, gb0_ref), (vn1, u1_ref, gb1_ref))):
            vnb = vn.astype(BF16)
            silu_gb, _ = _silu_and_grad(gb_ref[...])
            gate = _gelu(u_ref[...]) * silu_gb
            for hh in range(4):
                h = 4 * half + hh
                for c in range(tb // CHUNK):
                    rows, cols = slice(CHUNK * c, CHUNK * (c + 1)), slice(128 * hh, 128 * (hh + 1))
                    sp = _dot(ws_ref[h], vnb[rows, cols]) + bsx_ref[h]
                    yb_ref[rows, CB * half + 128 * hh:CB * half + 128 * (hh + 1)] = (gate[rows, cols] * sp).astype(BF16)

    zb = lambda j: pl.BlockSpec((tb, CB), lambda i, j=j: (i, j))
    const = lambda shape: pl.BlockSpec(shape, lambda i: (0,) * len(shape))
    return pl.pallas_call(
        body,
        name="sgu_fwd",
        grid=(t // tb,),
        in_specs=[zb(CB_U), zb(CB_U + 1), zb(CB_V), zb(CB_V + 1), zb(CB_GB), zb(CB_GB + 1),
                  const((1, SGU_W)), const((1, SGU_W)), const((SGU_HEADS, CHUNK, CHUNK)), const((SGU_HEADS, CHUNK, 128))],
        out_specs=pl.BlockSpec((tb, SGU_W), lambda i: (i, 0)),
        out_shape=jax.ShapeDtypeStruct((t, SGU_W), BF16),
        compiler_params=_params("arbitrary"),
    )(z, z, z, z, z, z, lng, lnb, ws, bsx)


def _sgu_bwd(dyb, z, lng, lnb, ws, wst, bsx, tb=256):
    t = z.shape[0]

    def body(dyb_ref, u0_ref, u1_ref, v0_ref, v1_ref, gb0_ref, gb1_ref, lng_ref, lnb_ref, ws_ref, wst_ref, bsx_ref,
             dz_ref, dlng_ref, dlnb_ref, dws_ref, dbs_ref, dvn):
        @pl.when(pl.program_id(0) == 0)
        def _():
            for r in (dlng_ref, dlnb_ref, dws_ref, dbs_ref):
                r[...] = jnp.zeros_like(r)

        v0, v1 = v0_ref[...], v1_ref[...]
        (vh0, vh1), (vn0, vn1), rstd = _sgu_norm(v0, v1, lng_ref, lnb_ref)
        causal = (lax.broadcasted_iota(jnp.int32, (CHUNK, CHUNK), 0) >= lax.broadcasted_iota(jnp.int32, (CHUNK, CHUNK), 1))
        for half, (vn, u_ref, gb_ref) in enumerate(((vn0, u0_ref, gb0_ref), (vn1, u1_ref, gb1_ref))):
            vnb = vn.astype(BF16)
            u = u_ref[...]
            ug = _gelu(u)
            silu_gb, silu_gb_grad = _silu_and_grad(gb_ref[...])
            dyb = dyb_ref[:, CB * half:CB * (half + 1)]
            dyb0 = dyb * silu_gb
            ds = dyb0 * ug
            sp_cols = []
            for hh in range(4):
                h = 4 * half + hh
                cols = slice(128 * hh, 128 * (hh + 1))
                sp_rows = []
                for c in range(tb // CHUNK):
                    rows = slice(CHUNK * c, CHUNK * (c + 1))
                    vt = vnb[rows, cols]
                    sp_rows.append(_dot(ws_ref[h], vt) + bsx_ref[h])
                    dst = ds[rows, cols]
                    dstb = dst.astype(BF16)
                    dbs_ref[h] += dst
                    dws_ref[h] += jnp.where(causal, _dot(dstb, vt, NT), 0.0)
                    dvn[rows, CB * half + 128 * hh:CB * half + 128 * (hh + 1)] = _dot(wst_ref[h], dstb)
                sp_cols.append(jnp.concatenate(sp_rows, axis=0))
            sp = jnp.concatenate(sp_cols, axis=1)
            dz_ref[:, CB * half:CB * (half + 1)] = (dyb0 * sp * _gelu_grad(u)).astype(BF16)
            dz_ref[:, 2 * SGU_W + CB * half:2 * SGU_W + CB * (half + 1)] = (dyb * (ug * sp) * silu_gb_grad).astype(BF16)

        dvn0, dvn1 = dvn[:, 0:CB], dvn[:, CB:]
        dlng_ref[:, 0:CB] += _rowsum(dvn0 * vh0)
        dlng_ref[:, CB:] += _rowsum(dvn1 * vh1)
        dlnb_ref[:, 0:CB] += _rowsum(dvn0)
        dlnb_ref[:, CB:] += _rowsum(dvn1)
        dh0, dh1 = dvn0 * lng_ref[:, 0:CB], dvn1 * lng_ref[:, CB:]
        m1 = (jnp.sum(dh0, axis=-1, keepdims=True) + jnp.sum(dh1, axis=-1, keepdims=True)) * (1.0 / SGU_W)
        m2 = (jnp.sum(dh0 * vh0, axis=-1, keepdims=True) + jnp.sum(dh1 * vh1, axis=-1, keepdims=True)) * (1.0 / SGU_W)
        dz_ref[:, SGU_W:SGU_W + CB] = (rstd * (dh0 - m1 - vh0 * m2) * _gelu_grad(v0)).astype(BF16)
        dz_ref[:, SGU_W + CB:2 * SGU_W] = (rstd * (dh1 - m1 - vh1 * m2) * _gelu_grad(v1)).astype(BF16)

    zb = lambda j: pl.BlockSpec((tb, CB), lambda i, j=j: (i, j))
    const = lambda shape: pl.BlockSpec(shape, lambda i: (0,) * len(shape))
    hmat = (SGU_HEADS, CHUNK, CHUNK)
    return pl.pallas_call(
        body,
        name="sgu_bwd",
        grid=(t // tb,),
        in_specs=[pl.BlockSpec((tb, SGU_W), lambda i: (i, 0)),
                  zb(CB_U), zb(CB_U + 1), zb(CB_V), zb(CB_V + 1), zb(CB_GB), zb(CB_GB + 1),
                  const((1, SGU_W)), const((1, SGU_W)), const(hmat), const(hmat), const(hmat)],
        out_specs=[pl.BlockSpec((tb, 3 * SGU_W), lambda i: (i, 0)),
                   const((1, SGU_W)), const((1, SGU_W)), const(hmat), const(hmat)],
        out_shape=[jax.ShapeDtypeStruct((t, 3 * SGU_W), BF16),
                   jax.ShapeDtypeStruct((1, SGU_W), F32), jax.ShapeDtypeStruct((1, SGU_W), F32),
                   jax.ShapeDtypeStruct(hmat, F32), jax.ShapeDtypeStruct(hmat, F32)],
        scratch_shapes=[pltpu.VMEM((tb, SGU_W), F32)],
        compiler_params=_params("arbitrary"),
    )(dyb, z, z, z, z, z, z, lng, lnb, ws, wst, bsx)


def _window_sums(ext, lookahead):
    n = ext.shape[0]
    out = []
    for gi, w in enumerate(POOL_WINDOWS):
        acc = ext[:, 128 * gi:128 * (gi + 1)]
        k = 1
        while k < w:
            acc = acc + pltpu.roll(acc, (n - k) if lookahead else k, 0)
            k *= 2
        out.append(acc)
    return jnp.concatenate(out, axis=1)


def _pool_counts(row0, tb):
    pos = (row0 + 1 + lax.broadcasted_iota(jnp.int32, (tb, POOL_W), 0)).astype(F32)
    lane = lax.broadcasted_iota(jnp.int32, (tb, POOL_W), 1)
    win = jnp.where(lane < 128, 2.0, jnp.where(lane < 256, 4.0, jnp.where(lane < 384, 8.0, 16.0)))
    return jnp.minimum(pos, win)


def _pool_fwd(z, wpool, scale, tb=256):
    t = z.shape[0]

    def body(xc_ref, gc_ref, wp_ref, sc_ref, yc_ref, halo):
        i = pl.program_id(0)

        @pl.when(i == 0)
        def _():
            halo[...] = jnp.zeros_like(halo)

        xc = xc_ref[...]
        sums = _window_sums(jnp.concatenate([halo[...], xc], axis=0), lookahead=False)[HALO:, :]
        halo[...] = xc[tb - HALO:, :]
        pb = (sums / _pool_counts(i * tb, tb) - xc).astype(BF16)
        q = jnp.concatenate([_dot(pb[:, 128 * gi:128 * (gi + 1)], wp_ref[gi]) for gi in range(4)], axis=1)
        silu_gc, _ = _silu_and_grad(gc_ref[...])
        yc_ref[...] = (q * sc_ref[...] * silu_gc).astype(BF16)

    const = lambda shape: pl.BlockSpec(shape, lambda i: (0,) * len(shape))
    return pl.pallas_call(
        body,
        name="pool_fwd",
        grid=(t // tb,),
        in_specs=[pl.BlockSpec((tb, CB), lambda i: (i, CB_XC)), pl.BlockSpec((tb, CB), lambda i: (i, CB_GC)),
                  const((4, 128, 128)), const((1, POOL_W))],
        out_specs=pl.BlockSpec((tb, POOL_W), lambda i: (i, 0)),
        out_shape=jax.ShapeDtypeStruct((t, POOL_W), BF16),
        scratch_shapes=[pltpu.VMEM((HALO, POOL_W), F32)],
        compiler_params=_params("arbitrary"),
    )(z, z, wpool, scale)


def _pool_bwd(dyc, z, wpool, scale, tb=256):
    t = z.shape[0]
    nb = t // tb
    rev = lambda i: nb - 1 - i
    per_halo = tb // HALO

    def body(dyc_ref, xc_ref, xp_ref, gc_ref, wp_ref, sc_ref, dz_ref, dwp_ref, dsc_ref, ehalo):
        i = pl.program_id(0)

        @pl.when(i == 0)
        def _():
            ehalo[...] = jnp.zeros_like(ehalo)
            dwp_ref[...] = jnp.zeros_like(dwp_ref)
            dsc_ref[...] = jnp.zeros_like(dsc_ref)

        xc = xc_ref[...]
        prev = jnp.where(i < nb - 1, xp_ref[...], 0.0)
        sums = _window_sums(jnp.concatenate([prev, xc], axis=0), lookahead=False)[HALO:, :]
        cnt = _pool_counts(rev(i) * tb, tb)
        pb = (sums / cnt - xc).astype(BF16)
        q = jnp.concatenate([_dot(pb[:, 128 * gi:128 * (gi + 1)], wp_ref[gi]) for gi in range(4)], axis=1)
        silu_gc, silu_gc_grad = _silu_and_grad(gc_ref[...])
        dyc = dyc_ref[...]
        dz_ref[:, POOL_W:] = (dyc * (q * sc_ref[...]) * silu_gc_grad).astype(BF16)
        dyc0 = dyc * silu_gc
        dsc_ref[...] += _rowsum(dyc0 * q)
        dqb = (dyc0 * sc_ref[...]).astype(BF16)
        dp_cols = []
        for gi in range(4):
            cols = slice(128 * gi, 128 * (gi + 1))
            dwp_ref[gi] += _dot(pb[:, cols], dqb[:, cols], TN)
            dp_cols.append(_dot(dqb[:, cols], wp_ref[gi], NT))
        dp = jnp.concatenate(dp_cols, axis=1)
        e = dp / cnt
        fut = _window_sums(jnp.concatenate([e, ehalo[...]], axis=0), lookahead=True)[:tb, :]
        ehalo[...] = e[:HALO, :]
        dz_ref[:, 0:POOL_W] = (fut - dp).astype(BF16)

    const = lambda shape: pl.BlockSpec(shape, lambda i: (0,) * len(shape))
    return pl.pallas_call(
        body,
        name="pool_bwd",
        grid=(nb,),
        in_specs=[pl.BlockSpec((tb, POOL_W), lambda i: (rev(i), 0)),
                  pl.BlockSpec((tb, CB), lambda i: (rev(i), CB_XC)),
                  pl.BlockSpec((HALO, CB), lambda i: (jnp.maximum(rev(i) * per_halo - 1, 0), CB_XC)),
                  pl.BlockSpec((tb, CB), lambda i: (rev(i), CB_GC)),
                  const((4, 128, 128)), const((1, POOL_W))],
        out_specs=[pl.BlockSpec((tb, 2 * POOL_W), lambda i: (rev(i), 0)), const((4, 128, 128)), const((1, POOL_W))],
        out_shape=[jax.ShapeDtypeStruct((t, 2 * POOL_W), BF16),
                   jax.ShapeDtypeStruct((4, 128, 128), F32), jax.ShapeDtypeStruct((1, POOL_W), F32)],
        scratch_shapes=[pltpu.VMEM((HALO, POOL_W), F32)],
        compiler_params=_params("arbitrary"),
    )(dyc, z, z, z, wpool, scale)


def _adamw(w, m, v, parts, tr, name):
    r, c = w.shape
    n_slab = len(parts)
    per_slab = r // n_slab // tr

    def body(w_ref, m_ref, v_ref, *refs):
        p_refs, (g_ref, d_ref, nm_ref, nv_ref) = refs[:n_slab], refs[n_slab:]
        for s, p_ref in enumerate(p_refs):
            @pl.when(pl.program_id(0) // per_slab == s)
            def _(p_ref=p_ref):
                g = p_ref[0].astype(F32)
                for k in range(1, N_DEV):
                    g = g + p_ref[k].astype(F32)
                g_ref[...] = g

        g = g_ref[...]
        nm = ADAM_B1 * m_ref[...] + (1.0 - ADAM_B1) * g
        nv = ADAM_B2 * v_ref[...] + (1.0 - ADAM_B2) * (g * g)
        m_hat = nm / (1.0 - ADAM_B1 ** ADAM_STEP)
        v_hat = nv / (1.0 - ADAM_B2 ** ADAM_STEP)
        nm_ref[...] = nm
        nv_ref[...] = nv
        d_ref[...] = -ADAM_LR * (m_hat / (jnp.sqrt(v_hat) + ADAM_EPS) + ADAM_WD * w_ref[...])

    blk = pl.BlockSpec((tr, c), lambda i: (i, 0))
    slab = lambda s: pl.BlockSpec((N_DEV, tr, c), lambda i, s=s: (0, jnp.clip(i - s * per_slab, 0, per_slab - 1), 0))
    return pl.pallas_call(
        body,
        name=name,
        grid=(r // tr,),
        in_specs=[blk, blk, blk] + [slab(s) for s in range(n_slab)],
        out_specs=[blk, blk, blk, blk],
        out_shape=[jax.ShapeDtypeStruct((r, c), F32)] * 4,
        compiler_params=_params("arbitrary"),
    )(w, m, v, *parts)


MESH = pl.DeviceIdType.MESH
ANY = pl.BlockSpec(memory_space=pl.ANY)


def _dev_index(dev):
    return 4 * dev[0] + 2 * dev[1] + dev[2]


WHOLE_SHAPES = ((D_MODEL, IN_COLS), (D_MODEL, D_MODEL), (S5_W, S5_W))
SHARD_SHAPES = ((D_MODEL, SH_IN), (SH_OUT, D_MODEL), (SH_GLU, S5_W))


def _shard_of(ref, ti, idx):
    if ti == 0:
        return ref.at[:, pl.ds(pl.multiple_of(idx * SH_IN, 128), SH_IN)]
    rows = SHARD_SHAPES[ti][0]
    return ref.at[pl.ds(pl.multiple_of(idx * rows, rows), rows), :]


def _peer(mask, x, y, c):
    return (1 - x if mask & 4 else x, 1 - y if mask & 2 else y, 1 - c if mask & 1 else c)


def _allgather_weights(wi, wo, wg):
    n_t = 3

    def body(wi_ref, wo_ref, wg_ref, gi_ref, go_ref, gg_ref, send_sems, recv_sems, local_sems):
        x, y, c = lax.axis_index("x"), lax.axis_index("y"), lax.axis_index("c")
        me, sibling = (x, y, c), (x, y, 1 - c)
        chips = [(1 - x, y), (x, 1 - y), (1 - x, 1 - y)]
        shards = (wi_ref, wo_ref, wg_ref)
        wholes = (gi_ref, go_ref, gg_ref)

        def slot(ti, dev):
            return _shard_of(wholes[ti], ti, _dev_index(dev))

        def copy(k, ti, block, to, own=False):
            return pltpu.make_async_remote_copy(
                src_ref=shards[ti] if own else slot(ti, block), dst_ref=slot(ti, block),
                send_sem=send_sems.at[n_t * k + ti], recv_sem=recv_sems.at[n_t * k + ti],
                device_id=to, device_id_type=MESH)

        mine = [pltpu.make_async_copy(shards[ti], slot(ti, me), local_sems.at[ti]) for ti in range(n_t)]
        for cp in mine:
            cp.start()
        first = [copy(0, ti, me, sibling, own=True) for ti in range(n_t)]
        first += [copy(1 + j, ti, me, (*chip, c), own=True) for j, chip in enumerate(chips) for ti in range(n_t)]
        for cp in first:
            cp.start()
        passed = []
        for j, chip in enumerate(chips):
            for ti in range(n_t):
                copy(1 + j, ti, (*chip, c), me).wait_recv()
            onward = [copy(4 + j, ti, (*chip, c), sibling) for ti in range(n_t)]
            for cp in onward:
                cp.start()
            passed += onward
        for ti in range(n_t):
            copy(0, ti, sibling, me).wait_recv()
        for j, chip in enumerate(chips):
            for ti in range(n_t):
                copy(4 + j, ti, (*chip, 1 - c), me).wait_recv()
        for cp in first + passed:
            cp.wait_send()
        for cp in mine:
            cp.wait()

    return pl.pallas_call(
        body,
        name="allgather_weights",
        in_specs=[ANY, ANY, ANY],
        out_specs=[ANY, ANY, ANY],
        out_shape=[jax.ShapeDtypeStruct(s, BF16) for s in WHOLE_SHAPES],
        scratch_shapes=[pltpu.SemaphoreType.DMA((7 * n_t,)), pltpu.SemaphoreType.DMA((7 * n_t,)),
                        pltpu.SemaphoreType.DMA((n_t,))],
    )(wi, wo, wg)


HBM = pl.BlockSpec(memory_space=pltpu.HBM)
SEM = pl.BlockSpec(memory_space=pltpu.SEMAPHORE)
GATHER, SCATTER, SHARE = "gather", "scatter", "share"


def _split_route(kind, ti, sending, me_idx, p_idx, src_ref, land_ref):
    owner = me_idx if sending else p_idx
    if kind == GATHER:
        return src_ref, _shard_of(land_ref, ti, owner)
    if kind == SCATTER:
        return _shard_of(src_ref, ti, p_idx), land_ref.at[owner]
    return src_ref, land_ref.at[owner]


def _split_start(name, srcs, lands, kinds, after=None):
    n = len(srcs)
    arrays = list(srcs) + list(lands) + ([] if after is None else [after])

    def body(*refs):
        src_refs, land_refs = refs[0:n], refs[n:2 * n]
        send_sems, recv_sems, token = refs[len(arrays)], refs[len(arrays) + 1], refs[-1]
        x, y, c = lax.axis_index("x"), lax.axis_index("y"), lax.axis_index("c")
        me_idx = _dev_index((x, y, c))
        for mask in range(1, N_DEV):
            p = _peer(mask, x, y, c)
            for i, (kind, ti) in enumerate(kinds):
                k = (mask - 1) * n + i
                src, dst = _split_route(kind, ti, True, me_idx, _dev_index(p), src_refs[i], land_refs[i])
                pltpu.make_async_remote_copy(src_ref=src, dst_ref=dst, send_sem=send_sems.at[k], recv_sem=recv_sems.at[k],
                                             device_id=p, device_id_type=MESH).start()
        token[...] = jnp.zeros_like(token)

    n_copies = (N_DEV - 1) * n
    return pl.pallas_call(
        body,
        name=name,
        in_specs=[HBM] * len(arrays),
        out_specs=(SEM, SEM) + (HBM,) * (2 * n) + (pl.BlockSpec(memory_space=pltpu.VMEM),),
        out_shape=(pltpu.SemaphoreType.DMA((n_copies,)), pltpu.SemaphoreType.DMA((n_copies,)))
        + tuple(pltpu.HBM(a.shape, a.dtype) for a in arrays[:2 * n]) + (jax.ShapeDtypeStruct((8, 128), F32),),
        input_output_aliases={i: 2 + i for i in range(2 * n)},
        compiler_params=pltpu.CompilerParams(has_side_effects=pltpu.SideEffectType.DATAFLOW_SIDE_EFFECTING),
    )(*[pltpu.with_memory_space_constraint(a, pltpu.HBM) for a in arrays])


def _split_wait(name, started, kinds, after):
    n = len(kinds)
    send_sems, recv_sems, thru = started[0], started[1], started[2:2 + 2 * n]

    def body(*refs):
        src_refs, land_refs = refs[0:n], refs[n:2 * n]
        send_sems, recv_sems = refs[2 * n], refs[2 * n + 1]
        x, y, c = lax.axis_index("x"), lax.axis_index("y"), lax.axis_index("c")
        me_idx = _dev_index((x, y, c))
        for mask in range(1, N_DEV):
            p = _peer(mask, x, y, c)
            for i, (kind, ti) in enumerate(kinds):
                k = (mask - 1) * n + i
                src, dst = _split_route(kind, ti, False, me_idx, _dev_index(p), src_refs[i], land_refs[i])
                cp = pltpu.make_async_remote_copy(src_ref=src, dst_ref=dst, send_sem=send_sems.at[k],
                                                  recv_sem=recv_sems.at[k], device_id=p, device_id_type=MESH)
                cp.wait_send()
                cp.wait_recv()

    res = pl.pallas_call(
        body,
        name=name,
        in_specs=[HBM] * (2 * n) + [SEM, SEM, pl.BlockSpec(memory_space=pl.ANY)],
        out_specs=(HBM,) * (2 * n),
        out_shape=tuple(pltpu.HBM(a.shape, a.dtype) for a in thru),
        input_output_aliases={i: i for i in range(2 * n)},
        compiler_params=pltpu.CompilerParams(has_side_effects=pltpu.SideEffectType.DATAFLOW_SIDE_EFFECTING),
    )(*thru, send_sems, recv_sems, after)
    return res[n:2 * n]


def _with_own(whole_shape, dtype, own, start):
    return lax.dynamic_update_slice(lax.empty(whole_shape, dtype), own, start)


def _share_small(buf):
    def body(b_ref, o_ref, send_sems, recv_sems, local_sem):
        x, y, c = lax.axis_index("x"), lax.axis_index("y"), lax.axis_index("c")
        me_idx = _dev_index((x, y, c))
        local = pltpu.make_async_copy(b_ref, o_ref.at[me_idx], local_sem)
        local.start()

        def copy(mask, owner):
            return pltpu.make_async_remote_copy(
                src_ref=b_ref, dst_ref=o_ref.at[owner], send_sem=send_sems.at[mask - 1], recv_sem=recv_sems.at[mask - 1],
                device_id=_peer(mask, x, y, c), device_id_type=MESH)

        sends = [copy(mask, me_idx) for mask in range(1, N_DEV)]
        for cp in sends:
            cp.start()
        for mask in range(1, N_DEV):
            copy(mask, _dev_index(_peer(mask, x, y, c))).wait_recv()
        for cp in sends:
            cp.wait_send()
        local.wait()

    return pl.pallas_call(
        body,
        name="share_small",
        in_specs=[ANY],
        out_specs=ANY,
        out_shape=jax.ShapeDtypeStruct((N_DEV,) + buf.shape, buf.dtype),
        scratch_shapes=[pltpu.SemaphoreType.DMA((N_DEV - 1,)), pltpu.SemaphoreType.DMA((N_DEV - 1,)),
                        pltpu.SemaphoreType.DMA],
    )(buf)


def _s5_prep(lam_re, lam_im, b_re, b_im, c_re, c_im, d_skip, log_dt):
    dt = jnp.exp(log_dt)[:, None]
    mag = jnp.exp(lam_re * dt)
    a_re, a_im = mag * jnp.cos(lam_im * dt), mag * jnp.sin(lam_im * dt)
    den = lam_re * lam_re + lam_im * lam_im
    f_re = ((a_re - 1.0) * lam_re + a_im * lam_im) / den
    f_im = (a_im * lam_re - (a_re - 1.0) * lam_im) / den
    bb_re = f_re[..., None] * b_re - f_im[..., None] * b_im
    bb_im = f_re[..., None] * b_im + f_im[..., None] * b_re
    eye = jnp.eye(8, dtype=F32)

    def in_map(bb):
        return jnp.einsum("jgph,gk->ghjkp", bb.reshape(4, 8, S5_STATE, S5_CH), eye).reshape(128, N_STATE)

    def out_map(cm):
        return jnp.einsum("jghp,gk->ghjkp", cm.reshape(4, 8, S5_CH, S5_STATE), eye).reshape(128, N_STATE)

    a = jnp.concatenate([a_re.reshape(N_PAIR, 128), a_im.reshape(N_PAIR, 128)])
    bc = jnp.concatenate([in_map(bb_re), in_map(bb_im)], axis=1)
    cct = jnp.concatenate([out_map(c_re), -out_map(c_im)], axis=1)
    return a, bc, cct, d_skip.reshape(1, S5_W)


def _s5_powers(lam_re, lam_im, log_dt, seg_len):
    dt = jnp.exp(log_dt)[:, None]
    k = jnp.arange(1, seg_len + 1, dtype=F32)[:, None, None]
    mag = jnp.exp(k * (lam_re * dt))
    ang = k * (lam_im * dt)
    tiles = lambda v: jnp.swapaxes(v.reshape(seg_len, N_PAIR, 128), 0, 1)
    p_re, p_im = tiles(mag * jnp.cos(ang)), tiles(mag * jnp.sin(ang))
    fwd = jnp.concatenate([p_re, p_im])
    bwd = jnp.concatenate([p_re[:, ::-1], -p_im[:, ::-1]])
    return fwd, bwd


WEIGHTS = ["norm_g", "w_in", "lam_re", "lam_im", "b_re", "b_im", "c_re", "c_im", "d_skip", "log_dt", "w_glu", "b_glu",
           "ln_g", "ln_b", "w_s", "b_s", "w_pool", "pool_scale", "w_out", "final_g"]
SHARDED = ("w_in", "w_glu", "w_out")
SMALL = [n for n in WEIGHTS if n not in SHARDED]
INPUTS = ["x"] + WEIGHTS + ["loss_target"] + ["m_" + n for n in WEIGHTS] + ["v_" + n for n in WEIGHTS]
SMALL_B = ["norm_g", "final_g"]
SMALL_A = [n for n in SMALL if n not in SMALL_B]
SMALL_TILE = 16 * 128


def _pack_small(arrays, dtype):
    flat = jnp.concatenate([a.reshape(-1) for a in arrays])
    pad = (-flat.shape[0]) % SMALL_TILE
    return jnp.pad(flat, (0, pad)).astype(dtype).reshape(-1, 128)


def _unpack_small(packed, like):
    flat = packed.reshape(-1)
    out, off = [], 0
    for a in like:
        out.append(flat[off:off + a.size].reshape(a.shape))
        off += a.size
    return out


def _layer_fwd(p, l, x, wi, wo, wg, token=None):
    row = lambda v: v.reshape(1, -1)
    causal = jnp.tril(jnp.ones((CHUNK, CHUNK), dtype=bool))
    (a, bc, cct, dvec), prep_vjp = jax.vjp(
        _s5_prep, p["lam_re"][l], p["lam_im"][l], p["b_re"][l], p["b_im"][l], p["c_re"][l], p["c_im"][l],
        p["d_skip"][l], p["log_dt"][l])
    ws_f32 = jnp.where(causal[None], p["w_s"][l], 0.0)
    pw_fwd, pw_bwd = _s5_powers(p["lam_re"][l], p["lam_im"][l], p["log_dt"][l], S5_TB // SEG)
    c = dict(
        x=x, wi=wi, wo=wo, wg=wg, a=a, bc=bc.astype(BF16), cc=cct.T.astype(BF16), dvec=dvec, prep_vjp=prep_vjp,
        pw_bwd=pw_bwd,
        ws=ws_f32.astype(BF16), wst=jnp.swapaxes(ws_f32, 1, 2).astype(BF16),
        bsx=jnp.broadcast_to(p["b_s"][l][:, :, None], (SGU_HEADS, CHUNK, 128)),
        wpool=p["w_pool"][l].astype(BF16), scale=row(p["pool_scale"][l]),
        lng=row(p["ln_g"][l]), lnb=row(p["ln_b"][l]), bglu=row(p["b_glu"][l]), norm_g=row(p["norm_g"][l]))
    c["z"], c["h"] = _rms_inproj(x, c["norm_g"], wi, token)
    c["ya"], c["s"], c["ys"] = _s5_fwd(c["z"], c["bc"], c["cc"], a, pw_fwd, dvec, wg, c["bglu"], S5_TB)
    c["yb"] = _sgu_fwd(c["z"], c["lng"], c["lnb"], c["ws"], c["bsx"])
    c["yc"] = _pool_fwd(c["z"], c["wpool"], c["scale"])
    return _outproj(x, c["ya"], c["yb"], c["yc"], wo), c


def _mixers_bwd(c, dx, token=None):
    dya, dyb, dyc = _outproj_bwd(dx, c["wo"], token)
    dwo = _wgrad_out(c["ya"], c["yb"], c["yc"], dx, token)
    dz_s5, dbc, dcct, da, dd, dwg, dbglu = _s5_bwd(
        dya, c["ys"], c["z"], c["s"], c["bc"], c["cc"], c["a"], c["pw_bwd"], c["dvec"], c["wg"], c["bglu"], S5_TB)
    dz_sgu, dlng, dlnb, dws, dbsx = _sgu_bwd(dyb, c["z"], c["lng"], c["lnb"], c["ws"], c["wst"], c["bsx"])
    dz_pool, dwp, dsc = _pool_bwd(dyc, c["z"], c["wpool"], c["scale"])
    g_lam_re, g_lam_im, g_b_re, g_b_im, g_c_re, g_c_im, g_d, g_dt = c["prep_vjp"]((da, dbc, dcct, dd))
    small = dict(lam_re=g_lam_re, lam_im=g_lam_im, b_re=g_b_re, b_im=g_b_im, c_re=g_c_re,
                 c_im=g_c_im, d_skip=g_d, log_dt=g_dt, b_glu=dbglu.reshape(-1), ln_g=dlng.reshape(-1),
                 ln_b=dlnb.reshape(-1), w_s=dws, b_s=jnp.sum(dbsx, axis=-1), w_pool=dwp, pool_scale=dsc.reshape(-1))
    return (dz_s5, dz_sgu, dz_pool), dwo, dwg, small


def _inproj_grads(c, dz, dx, token_w=None, token_x=None):
    dwi = _wgrad_in(c["h"], *dz, token_w)
    dx, dnorm = _inproj_bwd(*dz, c["wi"], c["x"], c["norm_g"], dx, token_x)
    return dwi, dx, dnorm.reshape(-1)


def _own_index():
    return _dev_index((lax.axis_index("x"), lax.axis_index("y"), lax.axis_index("c")))


def _shard_start(ti, idx):
    return (0, idx * SH_IN) if ti == 0 else (idx * SHARD_SHAPES[ti][0], 0)


def kernel(x, norm_g, w_in, lam_re, lam_im, b_re, b_im, c_re, c_im, d_skip, log_dt, w_glu, b_glu, ln_g, ln_b, w_s, b_s, w_pool, pool_scale, w_out, final_g, loss_target, m_norm_g, m_w_in, m_lam_re, m_lam_im, m_b_re, m_b_im, m_c_re, m_c_im, m_d_skip, m_log_dt, m_w_glu, m_b_glu, m_ln_g, m_ln_b, m_w_s, m_b_s, m_w_pool, m_pool_scale, m_w_out, m_final_g, v_norm_g, v_w_in, v_lam_re, v_lam_im, v_b_re, v_b_im, v_c_re, v_c_im, v_d_skip, v_log_dt, v_w_glu, v_b_glu, v_ln_g, v_ln_b, v_w_s, v_b_s, v_w_pool, v_pool_scale, v_w_out, v_final_g):
    p = dict(zip(INPUTS, (x, norm_g, w_in, lam_re, lam_im, b_re, b_im, c_re, c_im, d_skip, log_dt, w_glu, b_glu, ln_g, ln_b, w_s, b_s, w_pool, pool_scale, w_out, final_g, loss_target, m_norm_g, m_w_in, m_lam_re, m_lam_im, m_b_re, m_b_im, m_c_re, m_c_im, m_d_skip, m_log_dt, m_w_glu, m_b_glu, m_ln_g, m_ln_b, m_w_s, m_b_s, m_w_pool, m_pool_scale, m_w_out, m_final_g, v_norm_g, v_w_in, v_lam_re, v_lam_im, v_b_re, v_b_im, v_c_re, v_c_im, v_d_skip, v_log_dt, v_w_glu, v_b_glu, v_ln_g, v_ln_b, v_w_s, v_b_s, v_w_pool, v_pool_scale, v_w_out, v_final_g)))

    me = _own_index()
    shards = [[w[l].astype(BF16) for w in (w_in, w_out, w_glu)] for l in range(DEPTH)]
    gather3 = [(GATHER, ti) for ti in range(3)]
    scatter3 = [(SCATTER, ti) for ti in range(3)]

    def stack_with_own(partial, ti):
        own = lax.dynamic_slice(partial, _shard_start(ti, me), SHARD_SHAPES[ti])
        return _with_own((N_DEV,) + SHARD_SHAPES[ti], partial.dtype, own[None], (me, 0, 0))

    whole0 = _allgather_weights(*shards[0])
    lands = [_with_own(WHOLE_SHAPES[ti], BF16, shards[1][ti], _shard_start(ti, me)) for ti in range(3)]
    gather1 = _split_start("gather1_start", shards[1], lands, gather3, after=whole0[2])
    x1, c0 = _layer_fwd(p, 0, x[0], *whole0, token=gather1[-1])
    whole1 = _split_wait("gather1_wait", gather1, gather3, x1)
    x2, c1 = _layer_fwd(p, 1, x1, *whole1)

    dx, loss_tile, dfinal = _final_loss(x2, final_g.reshape(1, -1), loss_target[0])
    loss = lax.psum(loss_tile[0, 0], ("x", "y", "c"))

    dz1, dwo1, dwg1, small1 = _mixers_bwd(c1, dx)
    dwi1, dx, dnorm1 = _inproj_grads(c1, dz1, dx)
    partials1 = [dwi1, dwo1, dwg1.astype(BF16)]
    grads1 = _split_start("grads1_start", partials1, [stack_with_own(g, ti) for ti, g in enumerate(partials1)], scatter3)
    dz0, dwo0, dwg0, small0 = _mixers_bwd(c0, dx, token=grads1[-1])
    small_a = _pack_small([jnp.stack([small0[n], small1[n]]) for n in SMALL_A], BF16)
    srcs_a, kinds_a = [dwo0, dwg0.astype(BF16), small_a], [(SCATTER, 1), (SCATTER, 2), (SHARE, None)]
    lands_a = [stack_with_own(srcs_a[0], 1), stack_with_own(srcs_a[1], 2),
               _with_own((N_DEV,) + small_a.shape, BF16, small_a[None], (me, 0, 0))]
    grads0a = _split_start("grads0a_start", srcs_a, lands_a, kinds_a)
    dwi0 = _wgrad_in(c0["h"], *dz0, grads0a[-1])
    grads0b = _split_start("grads0b_start", [dwi0], [stack_with_own(dwi0, 0)], [(SCATTER, 0)])
    dx, dnorm0 = _inproj_bwd(*dz0, c0["wi"], c0["x"], c0["norm_g"], dx, grads0b[-1])
    parts1 = _split_wait("grads1_wait", grads1, scatter3, dx)
    r_out0, r_glu0, r_a = _split_wait("grads0a_wait", grads0a, kinds_a, dx)
    (r_in0,) = _split_wait("grads0b_wait", grads0b, [(SCATTER, 0)], dx)
    parts0 = [r_in0, r_out0, r_glu0]
    r_b = _share_small(_pack_small([jnp.stack([dnorm0.reshape(-1), dnorm1]), dfinal.reshape(-1)], F32))

    out = {}

    def adam(name, ti, tr):
        shape2d = (DEPTH * SHARD_SHAPES[ti][0], SHARD_SHAPES[ti][1])
        res = _adamw(p[name].reshape(shape2d), p["m_" + name].reshape(shape2d), p["v_" + name].reshape(shape2d),
                     [parts0[ti], parts1[ti]], tr, "adamw_" + name)
        out[name] = [r.reshape(p[name].shape) for r in res]

    adam("w_in", 0, 256)
    adam("w_out", 1, 128)
    adam("w_glu", 2, 64)
    for names, parts, steps, tag in ((SMALL_A, r_a, 7, "a"), (SMALL_B, r_b, 1, "b")):
        like = [p[n] for n in names]
        packed = _adamw(*[_pack_small([p[pre + n] for n in names], F32) for pre in ("", "m_", "v_")],
                        [parts], parts.shape[1] // steps, "adamw_small_" + tag)
        for n, vals in zip(names, zip(*[_unpack_small(r, like) for r in packed])):
            out[n] = list(vals)

    return (loss, dx[None], *[out[n][0] for n in WEIGHTS], *[out[n][1] for n in WEIGHTS],
            *[out[n][2] for n in WEIGHTS], *[out[n][3] for n in WEIGHTS])
```

```python
import functools
import math

import jax
import jax.numpy as jnp
import numpy as np
from jax import lax
from jax.experimental import pallas as pl
from jax.experimental.pallas import tpu as pltpu

F32 = jnp.float32
BF16 = jnp.bfloat16

D_MODEL = 2048
DEPTH = 2
S5_W, SGU_W, POOL_W = 512, 1024, 512
S5_GROUPS, S5_STATE, S5_CH = 32, 64, 16
N_STATE = S5_GROUPS * S5_STATE
CHUNK = 128
SGU_HEADS = 8
POOL_WINDOWS = (2, 4, 8, 16)
IN_COLS = 5120
RMS_EPS = 1e-6
LN_EPS = 1e-5
ADAM_LR, ADAM_B1, ADAM_B2, ADAM_EPS, ADAM_WD, ADAM_STEP = 0.001, 0.9, 0.999, 1e-08, 0.01, 10

CB = 512
N_CB = IN_COLS // CB
CB_XA, CB_U, CB_V, CB_XC, CB_GA, CB_GB, CB_GC = 0, 1, 3, 5, 6, 7, 9

N_DEV = 8
SH_IN = IN_COLS // N_DEV
SH_OUT = D_MODEL // N_DEV
SH_GLU = S5_W // N_DEV

VMEM_LIMIT = 52 * 1024 * 1024
HALO = 16
LANE_CH = 512

TN = (((0,), (0,)), ((), ()))
NT = (((1,), (1,)), ((), ()))


def _params(*sem):
    return pltpu.CompilerParams(dimension_semantics=sem if sem else None, vmem_limit_bytes=VMEM_LIMIT)


def _dot(a, b, dims=None):
    if dims is None:
        return jnp.dot(a, b, preferred_element_type=F32)
    return lax.dot_general(a, b, dims, preferred_element_type=F32)


_GELU_C = math.sqrt(2.0 / math.pi)


def _gelu(x):
    return 0.5 * x * (1.0 + jnp.tanh(_GELU_C * (x + 0.044715 * x * x * x)))


def _gelu_grad(x):
    t = jnp.tanh(_GELU_C * (x + 0.044715 * x * x * x))
    return 0.5 * (1.0 + t) + 0.5 * x * (1.0 - t * t) * _GELU_C * (1.0 + 3.0 * 0.044715 * x * x)


def _silu_and_grad(x):
    s = jax.nn.sigmoid(x)
    return x * s, s * (1.0 + x * (1.0 - s))


def _rowsum(x):
    return jnp.sum(x, axis=0, keepdims=True)


def _after(token):
    if token is None:
        return [], []
    return [pl.BlockSpec((8, 128), lambda *_: (0, 0))], [token]


def _rms_inproj(x, g, w, token=None, tm=1024):
    t = x.shape[0]
    tm = min(tm, t)
    after_specs, after = _after(token)

    def body(x_ref, g_ref, w_ref, *rest):
        z_ref, h_ref = rest[-2:]

        @pl.when(pl.program_id(1) == 0)
        def _():
            xv = x_ref[...]
            r = lax.rsqrt(jnp.mean(xv * xv, axis=-1, keepdims=True) + RMS_EPS)
            h_ref[...] = (xv * r * g_ref[...]).astype(BF16)

        z_ref[...] = _dot(h_ref[...], w_ref[...])

    return pl.pallas_call(
        body,
        name="rms_inproj",
        grid=(t // tm, N_CB),
        in_specs=[
            pl.BlockSpec((tm, D_MODEL), lambda m, n: (m, 0)),
            pl.BlockSpec((1, D_MODEL), lambda m, n: (0, 0)),
            pl.BlockSpec((D_MODEL, CB), lambda m, n: (0, n)),
        ] + after_specs,
        out_specs=[
            pl.BlockSpec((tm, CB), lambda m, n: (m, n)),
            pl.BlockSpec((tm, D_MODEL), lambda m, n: (m, 0)),
        ],
        out_shape=[jax.ShapeDtypeStruct((t, IN_COLS), F32), jax.ShapeDtypeStruct((t, D_MODEL), BF16)],
        compiler_params=_params("arbitrary", "arbitrary"),
    )(x, g, w, *after)


def _outproj(x, ya, yb, yc, w, tm=1024, tn=1024):
    t = x.shape[0]
    tm = min(tm, t)

    def body(x_ref, ya_ref, yb_ref, yc_ref, w_ref, o_ref):
        acc = _dot(ya_ref[...], w_ref[0:S5_W, :])
        acc += _dot(yb_ref[...], w_ref[S5_W:S5_W + SGU_W, :])
        acc += _dot(yc_ref[...], w_ref[S5_W + SGU_W:, :])
        o_ref[...] = x_ref[...] + acc

    return pl.pallas_call(
        body,
        name="outproj",
        grid=(t // tm, D_MODEL // tn),
        in_specs=[
            pl.BlockSpec((tm, tn), lambda m, n: (m, n)),
            pl.BlockSpec((tm, S5_W), lambda m, n: (m, 0)),
            pl.BlockSpec((tm, SGU_W), lambda m, n: (m, 0)),
            pl.BlockSpec((tm, POOL_W), lambda m, n: (m, 0)),
            pl.BlockSpec((D_MODEL, tn), lambda m, n: (0, n)),
        ],
        out_specs=pl.BlockSpec((tm, tn), lambda m, n: (m, n)),
        out_shape=jax.ShapeDtypeStruct((t, D_MODEL), F32),
        compiler_params=_params("arbitrary", "arbitrary"),
    )(x, ya, yb, yc, w)


def _outproj_bwd(dx, w, token=None, tm=512):
    t = dx.shape[0]
    after_specs, after = _after(token)

    def body(dx_ref, w_ref, *rest):
        dya_ref, dyb_ref, dyc_ref = rest[-3:]
        dy = _dot(dx_ref[...].astype(BF16), w_ref[...], NT)
        dya_ref[...] = dy[:, 0:S5_W]
        dyb_ref[...] = dy[:, S5_W:S5_W + SGU_W]
        dyc_ref[...] = dy[:, S5_W + SGU_W:]

    return pl.pallas_call(
        body,
        name="outproj_bwd",
        grid=(t // tm,),
        in_specs=[
            pl.BlockSpec((tm, D_MODEL), lambda m: (m, 0)),
            pl.BlockSpec((D_MODEL, D_MODEL), lambda m: (0, 0)),
        ] + after_specs,
        out_specs=[
            pl.BlockSpec((tm, S5_W), lambda m: (m, 0)),
            pl.BlockSpec((tm, SGU_W), lambda m: (m, 0)),
            pl.BlockSpec((tm, POOL_W), lambda m: (m, 0)),
        ],
        out_shape=[
            jax.ShapeDtypeStruct((t, S5_W), F32),
            jax.ShapeDtypeStruct((t, SGU_W), F32),
            jax.ShapeDtypeStruct((t, POOL_W), F32),
        ],
        compiler_params=_params("arbitrary"),
    )(dx, w, *after)


def _dz_piece_maps():
    s5_map = lambda j: jnp.where(j >= CB_GA, 1, 0)
    sgu_map = lambda j: jnp.clip(jnp.where(j <= 4, j - 1, j - 3), 0, 5)
    pool_map = lambda j: jnp.where(j >= CB_GC, 1, 0)
    return s5_map, sgu_map, pool_map


def _pick_piece(j):
    is_s5 = jnp.logical_or(j == CB_XA, j == CB_GA)
    is_pool = jnp.logical_or(j == CB_XC, j == CB_GC)
    return is_s5, is_pool, jnp.logical_not(jnp.logical_or(is_s5, is_pool))


def _inproj_bwd(dz_s5, dz_sgu, dz_pool, w, x, g, dxo, token=None, tm=512):
    t = x.shape[0]
    s5_map, sgu_map, pool_map = _dz_piece_maps()
    after_specs, after = _after(token)

    def body(s5_ref, sgu_ref, pool_ref, w_ref, x_ref, g_ref, dxo_ref, *rest):
        dx_ref, dg_ref, acc = rest[-3:]
        m, j = pl.program_id(0), pl.program_id(1)

        @pl.when(jnp.logical_and(m == 0, j == 0))
        def _():
            dg_ref[...] = jnp.zeros_like(dg_ref)

        @pl.when(j == 0)
        def _():
            acc[...] = jnp.zeros_like(acc)

        is_s5, is_pool, is_sgu = _pick_piece(j)

        @pl.when(is_s5)
        def _():
            acc[...] += _dot(s5_ref[...], w_ref[...], NT)

        @pl.when(is_sgu)
        def _():
            acc[...] += _dot(sgu_ref[...], w_ref[...], NT)

        @pl.when(is_pool)
        def _():
            acc[...] += _dot(pool_ref[...], w_ref[...], NT)

        @pl.when(j == N_CB - 1)
        def _():
            xv = x_ref[...]
            r = lax.rsqrt(jnp.mean(xv * xv, axis=-1, keepdims=True) + RMS_EPS)
            n = xv * r
            dh = acc[...]
            dg_ref[...] += _rowsum(dh * n)
            dn = dh * g_ref[...]
            dx_ref[...] = dxo_ref[...] + r * (dn - n * jnp.mean(dn * n, axis=-1, keepdims=True))

    return pl.pallas_call(
        body,
        name="inproj_bwd",
        grid=(t // tm, N_CB),
        in_specs=[
            pl.BlockSpec((tm, CB), lambda m, j: (m, s5_map(j))),
            pl.BlockSpec((tm, CB), lambda m, j: (m, sgu_map(j))),
            pl.BlockSpec((tm, CB), lambda m, j: (m, pool_map(j))),
            pl.BlockSpec((D_MODEL, CB), lambda m, j: (0, j)),
            pl.BlockSpec((tm, D_MODEL), lambda m, j: (m, 0)),
            pl.BlockSpec((1, D_MODEL), lambda m, j: (0, 0)),
            pl.BlockSpec((tm, D_MODEL), lambda m, j: (m, 0)),
        ] + after_specs,
        out_specs=[
            pl.BlockSpec((tm, D_MODEL), lambda m, j: (m, 0)),
            pl.BlockSpec((1, D_MODEL), lambda m, j: (0, 0)),
        ],
        out_shape=[jax.ShapeDtypeStruct((t, D_MODEL), F32), jax.ShapeDtypeStruct((1, D_MODEL), F32)],
        scratch_shapes=[pltpu.VMEM((tm, D_MODEL), F32)],
        compiler_params=_params("arbitrary", "arbitrary"),
    )(dz_s5, dz_sgu, dz_pool, w, x, g, dxo, *after)


def _wgrad_in(h, dz_s5, dz_sgu, dz_pool, token=None, tm=512):
    t = h.shape[0]
    s5_map, sgu_map, pool_map = _dz_piece_maps()
    after_specs, after = _after(token)

    def body(h_ref, s5_ref, sgu_ref, pool_ref, *rest):
        o_ref, acc = rest[-2:]
        j, m = pl.program_id(0), pl.program_id(1)

        @pl.when(m == 0)
        def _():
            acc[...] = jnp.zeros_like(acc)

        is_s5, is_pool, is_sgu = _pick_piece(j)

        @pl.when(is_s5)
        def _():
            acc[...] += _dot(s5_ref[...], h_ref[...], TN)

        @pl.when(is_sgu)
        def _():
            acc[...] += _dot(sgu_ref[...], h_ref[...], TN)

        @pl.when(is_pool)
        def _():
            acc[...] += _dot(pool_ref[...], h_ref[...], TN)

        @pl.when(m == pl.num_programs(1) - 1)
        def _():
            o_ref[...] = acc[...].T.astype(BF16)

    return pl.pallas_call(
        body,
        name="wgrad_in",
        grid=(N_CB, t // tm),
        in_specs=[
            pl.BlockSpec((tm, D_MODEL), lambda j, m: (m, 0)),
            pl.BlockSpec((tm, CB), lambda j, m: (m, s5_map(j))),
            pl.BlockSpec((tm, CB), lambda j, m: (m, sgu_map(j))),
            pl.BlockSpec((tm, CB), lambda j, m: (m, pool_map(j))),
        ] + after_specs,
        out_specs=pl.BlockSpec((D_MODEL, CB), lambda j, m: (0, j)),
        out_shape=jax.ShapeDtypeStruct((D_MODEL, IN_COLS), BF16),
        scratch_shapes=[pltpu.VMEM((CB, D_MODEL), F32)],
        compiler_params=_params("arbitrary", "arbitrary"),
    )(h, dz_s5, dz_sgu, dz_pool, *after)


def _wgrad_out(ya, yb, yc, dx, token=None, tm=512, tn=512):
    t = dx.shape[0]
    after_specs, after = _after(token)

    def body(ya_ref, yb_ref, yc_ref, dx_ref, *rest):
        o_ref, acc = rest[-2:]
        m = pl.program_id(1)

        @pl.when(m == 0)
        def _():
            acc[...] = jnp.zeros_like(acc)

        dxb = dx_ref[...].astype(BF16)
        acc[:, 0:S5_W] += _dot(dxb, ya_ref[...], TN)
        acc[:, S5_W:S5_W + SGU_W] += _dot(dxb, yb_ref[...], TN)
        acc[:, S5_W + SGU_W:] += _dot(dxb, yc_ref[...], TN)

        @pl.when(m == pl.num_programs(1) - 1)
        def _():
            o_ref[...] = acc[...].T.astype(BF16)

    return pl.pallas_call(
        body,
        name="wgrad_out",
        grid=(D_MODEL // tn, t // tm),
        in_specs=[
            pl.BlockSpec((tm, S5_W), lambda n, m: (m, 0)),
            pl.BlockSpec((tm, SGU_W), lambda n, m: (m, 0)),
            pl.BlockSpec((tm, POOL_W), lambda n, m: (m, 0)),
            pl.BlockSpec((tm, tn), lambda n, m: (m, n)),
        ] + after_specs,
        out_specs=pl.BlockSpec((D_MODEL, tn), lambda n, m: (0, n)),
        out_shape=jax.ShapeDtypeStruct((D_MODEL, D_MODEL), BF16),
        scratch_shapes=[pltpu.VMEM((tn, D_MODEL), F32)],
        compiler_params=_params("arbitrary", "arbitrary"),
    )(ya, yb, yc, dx, *after)


def _final_loss(x, g, target, tm=512):
    t = x.shape[0]

    def body(x_ref, g_ref, t_ref, dx_ref, loss_ref, dg_ref):
        @pl.when(pl.program_id(0) == 0)
        def _():
            loss_ref[...] = jnp.zeros_like(loss_ref)
            dg_ref[...] = jnp.zeros_like(dg_ref)

        xv = x_ref[...]
        gv = g_ref[...]
        r = lax.rsqrt(jnp.mean(xv * xv, axis=-1, keepdims=True) + RMS_EPS)
        n = xv * r
        err = n * gv - t_ref[...]
        loss_ref[...] += 0.5 * jnp.sum(jnp.mean(err * err, axis=-1, keepdims=True))
        dy = err * (1.0 / D_MODEL)
        dg_ref[...] += _rowsum(dy * n)
        dn = dy * gv
        dx_ref[...] = r * (dn - n * jnp.mean(dn * n, axis=-1, keepdims=True))

    return pl.pallas_call(
        body,
        name="final_loss",
        grid=(t // tm,),
        in_specs=[
            pl.BlockSpec((tm, D_MODEL), lambda m: (m, 0)),
            pl.BlockSpec((1, D_MODEL), lambda m: (0, 0)),
            pl.BlockSpec((tm, D_MODEL), lambda m: (m, 0)),
        ],
        out_specs=[
            pl.BlockSpec((tm, D_MODEL), lambda m: (m, 0)),
            pl.BlockSpec((8, 128), lambda m: (0, 0)),
            pl.BlockSpec((1, D_MODEL), lambda m: (0, 0)),
        ],
        out_shape=[
            jax.ShapeDtypeStruct((t, D_MODEL), F32),
            jax.ShapeDtypeStruct((8, 128), F32),
            jax.ShapeDtypeStruct((1, D_MODEL), F32),
        ],
        compiler_params=_params("arbitrary"),
    )(x, g, target)


N_Q = 2 * N_STATE // LANE_CH
N_LT = 2 * N_STATE // 128
N_PAIR = N_LT // 2
SEG = 8
PAIR_GROUP = 8
S5_TB = 256


def _cmul_add(b_re, b_im, a_re, a_im, s_re, s_im):
    return b_re + (a_re * s_re - a_im * s_im), b_im + (a_re * s_im + a_im * s_re)


def _s5_scan(st, carry, a_ref, pw_ref, tb, reverse):
    seg_len = tb // SEG
    sign = -1.0 if reverse else 1.0
    sub = lax.broadcasted_iota(jnp.int32, (SEG, 128), 0)
    chain = (0 if reverse else seg_len - 1) * SEG
    full = lambda row: jnp.broadcast_to(row, (SEG, 128))
    for p0 in range(0, N_PAIR, PAIR_GROUP):
        pairs = list(range(p0, p0 + PAIR_GROUP))
        a_re = [full(a_ref[p:p + 1, :]) for p in pairs]
        a_im = [sign * full(a_ref[N_PAIR + p:N_PAIR + p + 1, :]) for p in pairs]

        def step(k, c, pairs=pairs, a_re=a_re, a_im=a_im):
            rows = pl.ds(pl.multiple_of(((seg_len - 1 - k) if reverse else k) * SEG, SEG), SEG)
            out = []
            for i, p in enumerate(pairs):
                n_re, n_im = _cmul_add(st[p, rows, :], st[N_PAIR + p, rows, :], a_re[i], a_im[i], c[2 * i], c[2 * i + 1])
                st[p, rows, :] = n_re
                st[N_PAIR + p, rows, :] = n_im
                out += [n_re, n_im]
            return tuple(out)

        ends = lax.fori_loop(0, seg_len, step, tuple(jnp.zeros((SEG, 128), F32) for _ in range(2 * PAIR_GROUP)))
        for i, p in enumerate(pairs):
            e_re, e_im = ends[2 * i], ends[2 * i + 1]
            w_re, w_im = pw_ref[p, chain:chain + SEG, :], pw_ref[N_PAIR + p, chain:chain + SEG, :]
            c_re, c_im = full(carry[p:p + 1, :]), full(carry[N_PAIR + p:N_PAIR + p + 1, :])
            for hop in range(SEG - 1):
                n_re, n_im = _cmul_add(e_re, e_im, w_re, w_im, c_re, c_im)
                target = SEG - 2 - hop if reverse else hop + 1
                shift = SEG - 1 if reverse else 1
                c_re = jnp.where(sub == target, pltpu.roll(n_re, shift, 0), c_re)
                c_im = jnp.where(sub == target, pltpu.roll(n_im, shift, 0), c_im)
            n_re, n_im = _cmul_add(e_re, e_im, w_re, w_im, c_re, c_im)
            last = 0 if reverse else SEG - 1
            carry[p:p + 1, :] = n_re[last:last + 1, :]
            carry[N_PAIR + p:N_PAIR + p + 1, :] = n_im[last:last + 1, :]
            in_re, in_im = jnp.tile(c_re, (seg_len, 1)), jnp.tile(c_im, (seg_len, 1))
            st[p], st[N_PAIR + p] = _cmul_add(st[p], st[N_PAIR + p], pw_ref[p], pw_ref[N_PAIR + p], in_re, in_im)


def _lane_chunk(ref, q):
    return jnp.concatenate([ref[4 * q + i] for i in range(4)], axis=1)


def _put_lane_chunk(ref, q, value):
    for i in range(4):
        ref[4 * q + i] = value[:, 128 * i:128 * (i + 1)]


def _step_major(tb):
    r = np.arange(tb)
    pm = np.zeros((tb, tb), np.float32)
    pm[r, (r % SEG) * (tb // SEG) + r // SEG] = 1.0
    return jnp.asarray(pm, BF16), jnp.asarray(pm.T, F32)


def _unpermute(pt_ref, v):
    return jnp.dot(pt_ref[...], v, preferred_element_type=F32, precision=lax.Precision.HIGHEST)


def _s5_fwd(z, bc, cc, a, pw, pm, pt, dvec, wglu, bglu, tb=256):
    t = z.shape[0]

    def body(xa_ref, ga_ref, bc_ref, cc_ref, a_ref, pw_ref, pm_ref, pt_ref, d_ref, wglu_ref, bglu_ref,
             ya_ref, s_ref, ys_ref, st, carry):
        @pl.when(pl.program_id(0) == 0)
        def _():
            carry[...] = jnp.zeros_like(carry)

        xa = xa_ref[...]
        xab = _dot(pm_ref[...], xa.astype(BF16)).astype(BF16)
        for q in range(N_Q):
            _put_lane_chunk(st, q, _dot(xab[:, 128 * (q % 4):128 * (q % 4) + 128], bc_ref[:, pl.ds(LANE_CH * q, LANE_CH)]))
        _s5_scan(st, carry, a_ref, pw_ref, tb, reverse=False)
        s_ref[...] = st[...].astype(BF16)
        cols = []
        for j in range(4):
            lo, hi = LANE_CH * j, N_STATE + LANE_CH * j
            cols.append(_dot(_lane_chunk(s_ref, j), cc_ref[lo:lo + LANE_CH, :])
                        + _dot(_lane_chunk(s_ref, 4 + j), cc_ref[hi:hi + LANE_CH, :]))
        ys = _unpermute(pt_ref, jnp.concatenate(cols, axis=1)) + d_ref[...] * xa
        ys_ref[...] = ys
        ya1 = _gelu(ys)
        pre = _dot(ya1.astype(BF16), wglu_ref[...]) + bglu_ref[...]
        silu_ga, _ = _silu_and_grad(ga_ref[...])
        ya_ref[...] = (ya1 * jax.nn.sigmoid(pre) * silu_ga).astype(BF16)

    const = lambda shape: pl.BlockSpec(shape, lambda i: (0,) * len(shape))
    return pl.pallas_call(
        body,
        name="s5_fwd",
        grid=(t // tb,),
        in_specs=[
            pl.BlockSpec((tb, CB), lambda i: (i, CB_XA)),
            pl.BlockSpec((tb, CB), lambda i: (i, CB_GA)),
            const((128, 2 * N_STATE)),
            const((2 * N_STATE, 128)),
            const((N_LT, 128)),
            const((N_LT, tb, 128)),
            const((tb, tb)),
            const((tb, tb)),
            const((1, S5_W)),
            const((S5_W, S5_W)),
            const((1, S5_W)),
        ],
        out_specs=[
            pl.BlockSpec((tb, S5_W), lambda i: (i, 0)),
            pl.BlockSpec((N_LT, tb, 128), lambda i: (0, i, 0)),
            pl.BlockSpec((tb, S5_W), lambda i: (i, 0)),
        ],
        out_shape=[
            jax.ShapeDtypeStruct((t, S5_W), BF16),
            jax.ShapeDtypeStruct((N_LT, t, 128), BF16),
            jax.ShapeDtypeStruct((t, S5_W), F32),
        ],
        scratch_shapes=[pltpu.VMEM((N_LT, tb, 128), F32), pltpu.VMEM((N_LT, 128), F32)],
        compiler_params=_params("arbitrary"),
    )(z, z, bc, cc, a, pw, pm, pt, dvec, wglu, bglu)


def _s5_bwd(dya, ys, z, s, bc, cc, a, pw, pm, pt, dvec, wglu, bglu, tb=256):
    t = z.shape[0]
    nb = t // tb
    rev = lambda i: nb - 1 - i

    def body(dya_ref, ys_ref, xa_ref, ga_ref, s_ref, sp_ref, bc_ref, cc_ref, a_ref, pw_ref, pm_ref, pt_ref, d_ref,
             wglu_ref, bglu_ref, dz_ref, dbc_ref, dcct_ref, da_ref, dd_ref, dwglu_ref, dbglu_ref, g, carry):
        i = pl.program_id(0)

        @pl.when(i == 0)
        def _():
            carry[...] = jnp.zeros_like(carry)
            for r in (dbc_ref, dcct_ref, da_ref, dd_ref, dwglu_ref, dbglu_ref):
                r[...] = jnp.zeros_like(r)

        ys = ys_ref[...]
        xa = xa_ref[...]
        ga = ga_ref[...]
        dya = dya_ref[...]
        ya1 = _gelu(ys)
        ya1b = ya1.astype(BF16)
        sg = jax.nn.sigmoid(_dot(ya1b, wglu_ref[...]) + bglu_ref[...])
        silu_ga, silu_ga_grad = _silu_and_grad(ga)
        dz_ref[:, S5_W:] = (dya * (ya1 * sg) * silu_ga_grad).astype(BF16)
        dya2 = dya * silu_ga
        dpre = dya2 * ya1 * sg * (1.0 - sg)
        dbglu_ref[...] += _rowsum(dpre)
        dpreb = dpre.astype(BF16)
        dwglu_ref[...] += _dot(ya1b, dpreb, TN)
        dys = (dya2 * sg + _dot(dpreb, wglu_ref[...], NT)) * _gelu_grad(ys)
        dd_ref[...] += _rowsum(dys * xa)
        dysb = _dot(pm_ref[...], dys.astype(BF16)).astype(BF16)
        xab = _dot(pm_ref[...], xa.astype(BF16)).astype(BF16)

        for q in range(N_Q):
            cq = pl.ds(LANE_CH * q, LANE_CH)
            x0 = 128 * (q % 4)
            dcct_ref[:, cq] += _dot(dysb[:, x0:x0 + 128], _lane_chunk(s_ref, q), TN)
            _put_lane_chunk(g, q, _dot(dysb[:, x0:x0 + 128], cc_ref[cq, :], NT))
        _s5_scan(g, carry, a_ref, pw_ref, tb, reverse=True)

        seg0 = (lax.broadcasted_iota(jnp.int32, (SEG, 128), 0) == 0)
        have_prev = i < nb - 1

        def before(tile, halo):
            tile = tile.astype(F32)
            prev_last = jnp.where(have_prev, halo.astype(F32)[HALO - 1:HALO, :], 0.0)
            step0 = jnp.where(seg0, prev_last, pltpu.roll(tile[tb - SEG:, :], 1, 0))
            return jnp.concatenate([step0, tile[:tb - SEG, :]], axis=0)

        for p in range(N_PAIR):
            g_re, g_im = g[p], g[N_PAIR + p]
            sp_re, sp_im = before(s_ref[p], sp_ref[p]), before(s_ref[N_PAIR + p], sp_ref[N_PAIR + p])
            da_ref[p:p + 1, :] += _rowsum(sp_re * g_re + sp_im * g_im)
            da_ref[N_PAIR + p:N_PAIR + p + 1, :] += _rowsum(sp_re * g_im - sp_im * g_re)
        dxa_cols = []
        for j in range(4):
            re = pl.ds(LANE_CH * j, LANE_CH)
            im = pl.ds(N_STATE + LANE_CH * j, LANE_CH)
            x0 = 128 * j
            gb_re, gb_im = _lane_chunk(g, j).astype(BF16), _lane_chunk(g, 4 + j).astype(BF16)
            dbc_ref[:, re] += _dot(xab[:, x0:x0 + 128], gb_re, TN)
            dbc_ref[:, im] += _dot(xab[:, x0:x0 + 128], gb_im, TN)
            dxa_cols.append(_dot(gb_re, bc_ref[:, re], NT) + _dot(gb_im, bc_ref[:, im], NT))
        dz_ref[:, 0:S5_W] = (dys * d_ref[...] + _unpermute(pt_ref, jnp.concatenate(dxa_cols, axis=1))).astype(BF16)

    const = lambda shape: pl.BlockSpec(shape, lambda i: (0,) * len(shape))
    per_halo = tb // HALO
    return pl.pallas_call(
        body,
        name="s5_bwd",
        grid=(nb,),
        in_specs=[
            pl.BlockSpec((tb, S5_W), lambda i: (rev(i), 0)),
            pl.BlockSpec((tb, S5_W), lambda i: (rev(i), 0)),
            pl.BlockSpec((tb, CB), lambda i: (rev(i), CB_XA)),
            pl.BlockSpec((tb, CB), lambda i: (rev(i), CB_GA)),
            pl.BlockSpec((N_LT, tb, 128), lambda i: (0, rev(i), 0)),
            pl.BlockSpec((N_LT, HALO, 128), lambda i: (0, jnp.maximum(rev(i) * per_halo - 1, 0), 0)),
            const((128, 2 * N_STATE)),
            const((2 * N_STATE, 128)),
            const((N_LT, 128)),
            const((N_LT, tb, 128)),
            const((tb, tb)),
            const((tb, tb)),
            const((1, S5_W)),
            const((S5_W, S5_W)),
            const((1, S5_W)),
        ],
        out_specs=[
            pl.BlockSpec((tb, 2 * CB), lambda i: (rev(i), 0)),
            const((128, 2 * N_STATE)),
            const((128, 2 * N_STATE)),
            const((N_LT, 128)),
            const((1, S5_W)),
            const((S5_W, S5_W)),
            const((1, S5_W)),
        ],
        out_shape=[
            jax.ShapeDtypeStruct((t, 2 * CB), BF16),
            jax.ShapeDtypeStruct((128, 2 * N_STATE), F32),
            jax.ShapeDtypeStruct((128, 2 * N_STATE), F32),
            jax.ShapeDtypeStruct((N_LT, 128), F32),
            jax.ShapeDtypeStruct((1, S5_W), F32),
            jax.ShapeDtypeStruct((S5_W, S5_W), F32),
            jax.ShapeDtypeStruct((1, S5_W), F32),
        ],
        scratch_shapes=[pltpu.VMEM((N_LT, tb, 128), F32), pltpu.VMEM((N_LT, 128), F32)],
        compiler_params=_params("arbitrary"),
    )(dya, ys, z, z, s, s, bc, cc, a, pw, pm, pt, dvec, wglu, bglu)


def _sgu_norm(v0, v1, lng_ref, lnb_ref):
    g0, g1 = _gelu(v0), _gelu(v1)
    mu = (jnp.sum(g0, axis=-1, keepdims=True) + jnp.sum(g1, axis=-1, keepdims=True)) * (1.0 / SGU_W)
    c0, c1 = g0 - mu, g1 - mu
    var = (jnp.sum(c0 * c0, axis=-1, keepdims=True) + jnp.sum(c1 * c1, axis=-1, keepdims=True)) * (1.0 / SGU_W)
    rstd = lax.rsqrt(var + LN_EPS)
    vh0, vh1 = c0 * rstd, c1 * rstd
    vn0 = vh0 * lng_ref[:, 0:CB] + lnb_ref[:, 0:CB]
    vn1 = vh1 * lng_ref[:, CB:] + lnb_ref[:, CB:]
    return (vh0, vh1), (vn0, vn1), rstd


def _sgu_fwd(z, lng, lnb, ws, bsx, tb=256):
    t = z.shape[0]

    def body(u0_ref, u1_ref, v0_ref, v1_ref, gb0_ref, gb1_ref, lng_ref, lnb_ref, ws_ref, bsx_ref, yb_ref):
        _, (vn0, vn1), _ = _sgu_norm(v0_ref[...], v1_ref[...], lng_ref, lnb_ref)
        for half, (vn, u_ref, gb_ref) in enumerate(((vn0, u0_ref, gb0_ref), (vn1, u1_ref, gb1_ref))):
            vnb = vn.astype(BF16)
            silu_gb, _ = _silu_and_grad(gb_ref[...])
            gate = _gelu(u_ref[...]) * silu_gb
            for hh in range(4):
                h = 4 * half + hh
                for c in range(tb // CHUNK):
                    rows, cols = slice(CHUNK * c, CHUNK * (c + 1)), slice(128 * hh, 128 * (hh + 1))
                    sp = _dot(ws_ref[h], vnb[rows, cols]) + bsx_ref[h]
                    yb_ref[rows, CB * half + 128 * hh:CB * half + 128 * (hh + 1)] = (gate[rows, cols] * sp).astype(BF16)

    zb = lambda j: pl.BlockSpec((tb, CB), lambda i, j=j: (i, j))
    const = lambda shape: pl.BlockSpec(shape, lambda i: (0,) * len(shape))
    return pl.pallas_call(
        body,
        name="sgu_fwd",
        grid=(t // tb,),
        in_specs=[zb(CB_U), zb(CB_U + 1), zb(CB_V), zb(CB_V + 1), zb(CB_GB), zb(CB_GB + 1),
                  const((1, SGU_W)), const((1, SGU_W)), const((SGU_HEADS, CHUNK, CHUNK)), const((SGU_HEADS, CHUNK, 128))],
        out_specs=pl.BlockSpec((tb, SGU_W), lambda i: (i, 0)),
        out_shape=jax.ShapeDtypeStruct((t, SGU_W), BF16),
        compiler_params=_params("arbitrary"),
    )(z, z, z, z, z, z, lng, lnb, ws, bsx)


def _sgu_bwd(dyb, z, lng, lnb, ws, wst, bsx, tb=256):
    t = z.shape[0]

    def body(dyb_ref, u0_ref, u1_ref, v0_ref, v1_ref, gb0_ref, gb1_ref, lng_ref, lnb_ref, ws_ref, wst_ref, bsx_ref,
             dz_ref, dlng_ref, dlnb_ref, dws_ref, dbs_ref, dvn):
        @pl.when(pl.program_id(0) == 0)
        def _():
            for r in (dlng_ref, dlnb_ref, dws_ref, dbs_ref):
                r[...] = jnp.zeros_like(r)

        v0, v1 = v0_ref[...], v1_ref[...]
        (vh0, vh1), (vn0, vn1), rstd = _sgu_norm(v0, v1, lng_ref, lnb_ref)
        causal = (lax.broadcasted_iota(jnp.int32, (CHUNK, CHUNK), 0) >= lax.broadcasted_iota(jnp.int32, (CHUNK, CHUNK), 1))
        for half, (vn, u_ref, gb_ref) in enumerate(((vn0, u0_ref, gb0_ref), (vn1, u1_ref, gb1_ref))):
            vnb = vn.astype(BF16)
            u = u_ref[...]
            ug = _gelu(u)
            silu_gb, silu_gb_grad = _silu_and_grad(gb_ref[...])
            dyb = dyb_ref[:, CB * half:CB * (half + 1)]
            dyb0 = dyb * silu_gb
            ds = dyb0 * ug
            sp_cols = []
            for hh in range(4):
                h = 4 * half + hh
                cols = slice(128 * hh, 128 * (hh + 1))
                sp_rows = []
                for c in range(tb // CHUNK):
                    rows = slice(CHUNK * c, CHUNK * (c + 1))
                    vt = vnb[rows, cols]
                    sp_rows.append(_dot(ws_ref[h], vt) + bsx_ref[h])
                    dst = ds[rows, cols]
                    dstb = dst.astype(BF16)
                    dbs_ref[h] += dst
                    dws_ref[h] += jnp.where(causal, _dot(dstb, vt, NT), 0.0)
                    dvn[rows, CB * half + 128 * hh:CB * half + 128 * (hh + 1)] = _dot(wst_ref[h], dstb)
                sp_cols.append(jnp.concatenate(sp_rows, axis=0))
            sp = jnp.concatenate(sp_cols, axis=1)
            dz_ref[:, CB * half:CB * (half + 1)] = (dyb0 * sp * _gelu_grad(u)).astype(BF16)
            dz_ref[:, 2 * SGU_W + CB * half:2 * SGU_W + CB * (half + 1)] = (dyb * (ug * sp) * silu_gb_grad).astype(BF16)

        dvn0, dvn1 = dvn[:, 0:CB], dvn[:, CB:]
        dlng_ref[:, 0:CB] += _rowsum(dvn0 * vh0)
        dlng_ref[:, CB:] += _rowsum(dvn1 * vh1)
        dlnb_ref[:, 0:CB] += _rowsum(dvn0)
        dlnb_ref[:, CB:] += _rowsum(dvn1)
        dh0, dh1 = dvn0 * lng_ref[:, 0:CB], dvn1 * lng_ref[:, CB:]
        m1 = (jnp.sum(dh0, axis=-1, keepdims=True) + jnp.sum(dh1, axis=-1, keepdims=True)) * (1.0 / SGU_W)
        m2 = (jnp.sum(dh0 * vh0, axis=-1, keepdims=True) + jnp.sum(dh1 * vh1, axis=-1, keepdims=True)) * (1.0 / SGU_W)
        dz_ref[:, SGU_W:SGU_W + CB] = (rstd * (dh0 - m1 - vh0 * m2) * _gelu_grad(v0)).astype(BF16)
        dz_ref[:, SGU_W + CB:2 * SGU_W] = (rstd * (dh1 - m1 - vh1 * m2) * _gelu_grad(v1)).astype(BF16)

    zb = lambda j: pl.BlockSpec((tb, CB), lambda i, j=j: (i, j))
    const = lambda shape: pl.BlockSpec(shape, lambda i: (0,) * len(shape))
    hmat = (SGU_HEADS, CHUNK, CHUNK)
    return pl.pallas_call(
        body,
        name="sgu_bwd",
        grid=(t // tb,),
        in_specs=[pl.BlockSpec((tb, SGU_W), lambda i: (i, 0)),
                  zb(CB_U), zb(CB_U + 1), zb(CB_V), zb(CB_V + 1), zb(CB_GB), zb(CB_GB + 1),
                  const((1, SGU_W)), const((1, SGU_W)), const(hmat), const(hmat), const(hmat)],
        out_specs=[pl.BlockSpec((tb, 3 * SGU_W), lambda i: (i, 0)),
                   const((1, SGU_W)), const((1, SGU_W)), const(hmat), const(hmat)],
        out_shape=[jax.ShapeDtypeStruct((t, 3 * SGU_W), BF16),
                   jax.ShapeDtypeStruct((1, SGU_W), F32), jax.ShapeDtypeStruct((1, SGU_W), F32),
                   jax.ShapeDtypeStruct(hmat, F32), jax.ShapeDtypeStruct(hmat, F32)],
        scratch_shapes=[pltpu.VMEM((tb, SGU_W), F32)],
        compiler_params=_params("arbitrary"),
    )(dyb, z, z, z, z, z, z, lng, lnb, ws, wst, bsx)


def _window_sums(ext, lookahead):
    n = ext.shape[0]
    out = []
    for gi, w in enumerate(POOL_WINDOWS):
        acc = ext[:, 128 * gi:128 * (gi + 1)]
        k = 1
        while k < w:
            acc = acc + pltpu.roll(acc, (n - k) if lookahead else k, 0)
            k *= 2
        out.append(acc)
    return jnp.concatenate(out, axis=1)


def _pool_counts(row0, tb):
    pos = (row0 + 1 + lax.broadcasted_iota(jnp.int32, (tb, POOL_W), 0)).astype(F32)
    lane = lax.broadcasted_iota(jnp.int32, (tb, POOL_W), 1)
    win = jnp.where(lane < 128, 2.0, jnp.where(lane < 256, 4.0, jnp.where(lane < 384, 8.0, 16.0)))
    return jnp.minimum(pos, win)


def _pool_fwd(z, wpool, scale, tb=256):
    t = z.shape[0]

    def body(xc_ref, gc_ref, wp_ref, sc_ref, yc_ref, halo):
        i = pl.program_id(0)

        @pl.when(i == 0)
        def _():
            halo[...] = jnp.zeros_like(halo)

        xc = xc_ref[...]
        sums = _window_sums(jnp.concatenate([halo[...], xc], axis=0), lookahead=False)[HALO:, :]
        halo[...] = xc[tb - HALO:, :]
        pb = (sums / _pool_counts(i * tb, tb) - xc).astype(BF16)
        q = jnp.concatenate([_dot(pb[:, 128 * gi:128 * (gi + 1)], wp_ref[gi]) for gi in range(4)], axis=1)
        silu_gc, _ = _silu_and_grad(gc_ref[...])
        yc_ref[...] = (q * sc_ref[...] * silu_gc).astype(BF16)

    const = lambda shape: pl.BlockSpec(shape, lambda i: (0,) * len(shape))
    return pl.pallas_call(
        body,
        name="pool_fwd",
        grid=(t // tb,),
        in_specs=[pl.BlockSpec((tb, CB), lambda i: (i, CB_XC)), pl.BlockSpec((tb, CB), lambda i: (i, CB_GC)),
                  const((4, 128, 128)), const((1, POOL_W))],
        out_specs=pl.BlockSpec((tb, POOL_W), lambda i: (i, 0)),
        out_shape=jax.ShapeDtypeStruct((t, POOL_W), BF16),
        scratch_shapes=[pltpu.VMEM((HALO, POOL_W), F32)],
        compiler_params=_params("arbitrary"),
    )(z, z, wpool, scale)


def _pool_bwd(dyc, z, wpool, scale, tb=256):
    t = z.shape[0]
    nb = t // tb
    rev = lambda i: nb - 1 - i
    per_halo = tb // HALO

    def body(dyc_ref, xc_ref, xp_ref, gc_ref, wp_ref, sc_ref, dz_ref, dwp_ref, dsc_ref, ehalo):
        i = pl.program_id(0)

        @pl.when(i == 0)
        def _():
            ehalo[...] = jnp.zeros_like(ehalo)
            dwp_ref[...] = jnp.zeros_like(dwp_ref)
            dsc_ref[...] = jnp.zeros_like(dsc_ref)

        xc = xc_ref[...]
        prev = jnp.where(i < nb - 1, xp_ref[...], 0.0)
        sums = _window_sums(jnp.concatenate([prev, xc], axis=0), lookahead=False)[HALO:, :]
        cnt = _pool_counts(rev(i) * tb, tb)
        pb = (sums / cnt - xc).astype(BF16)
        q = jnp.concatenate([_dot(pb[:, 128 * gi:128 * (gi + 1)], wp_ref[gi]) for gi in range(4)], axis=1)
        silu_gc, silu_gc_grad = _silu_and_grad(gc_ref[...])
        dyc = dyc_ref[...]
        dz_ref[:, POOL_W:] = (dyc * (q * sc_ref[...]) * silu_gc_grad).astype(BF16)
        dyc0 = dyc * silu_gc
        dsc_ref[...] += _rowsum(dyc0 * q)
        dqb = (dyc0 * sc_ref[...]).astype(BF16)
        dp_cols = []
        for gi in range(4):
            cols = slice(128 * gi, 128 * (gi + 1))
            dwp_ref[gi] += _dot(pb[:, cols], dqb[:, cols], TN)
            dp_cols.append(_dot(dqb[:, cols], wp_ref[gi], NT))
        dp = jnp.concatenate(dp_cols, axis=1)
        e = dp / cnt
        fut = _window_sums(jnp.concatenate([e, ehalo[...]], axis=0), lookahead=True)[:tb, :]
        ehalo[...] = e[:HALO, :]
        dz_ref[:, 0:POOL_W] = (fut - dp).astype(BF16)

    const = lambda shape: pl.BlockSpec(shape, lambda i: (0,) * len(shape))
    return pl.pallas_call(
        body,
        name="pool_bwd",
        grid=(nb,),
        in_specs=[pl.BlockSpec((tb, POOL_W), lambda i: (rev(i), 0)),
                  pl.BlockSpec((tb, CB), lambda i: (rev(i), CB_XC)),
                  pl.BlockSpec((HALO, CB), lambda i: (jnp.maximum(rev(i) * per_halo - 1, 0), CB_XC)),
                  pl.BlockSpec((tb, CB), lambda i: (rev(i), CB_GC)),
                  const((4, 128, 128)), const((1, POOL_W))],
        out_specs=[pl.BlockSpec((tb, 2 * POOL_W), lambda i: (rev(i), 0)), const((4, 128, 128)), const((1, POOL_W))],
        out_shape=[jax.ShapeDtypeStruct((t, 2 * POOL_W), BF16),
                   jax.ShapeDtypeStruct((4, 128, 128), F32), jax.ShapeDtypeStruct((1, POOL_W), F32)],
        scratch_shapes=[pltpu.VMEM((HALO, POOL_W), F32)],
        compiler_params=_params("arbitrary"),
    )(dyc, z, z, z, wpool, scale)


def _adamw(w, m, v, parts, tr, name):
    r, c = w.shape
    n_slab = len(parts)
    per_slab = r // n_slab // tr

    def body(w_ref, m_ref, v_ref, *refs):
        p_refs, (g_ref, d_ref, nm_ref, nv_ref) = refs[:n_slab], refs[n_slab:]
        for s, p_ref in enumerate(p_refs):
            @pl.when(pl.program_id(0) // per_slab == s)
            def _(p_ref=p_ref):
                g = p_ref[0].astype(F32)
                for k in range(1, N_DEV):
                    g = g + p_ref[k].astype(F32)
                g_ref[...] = g

        g = g_ref[...]
        nm = ADAM_B1 * m_ref[...] + (1.0 - ADAM_B1) * g
        nv = ADAM_B2 * v_ref[...] + (1.0 - ADAM_B2) * (g * g)
        m_hat = nm / (1.0 - ADAM_B1 ** ADAM_STEP)
        v_hat = nv / (1.0 - ADAM_B2 ** ADAM_STEP)
        nm_ref[...] = nm
        nv_ref[...] = nv
        d_ref[...] = -ADAM_LR * (m_hat / (jnp.sqrt(v_hat) + ADAM_EPS) + ADAM_WD * w_ref[...])

    blk = pl.BlockSpec((tr, c), lambda i: (i, 0))
    slab = lambda s: pl.BlockSpec((N_DEV, tr, c), lambda i, s=s: (0, jnp.clip(i - s * per_slab, 0, per_slab - 1), 0))
    return pl.pallas_call(
        body,
        name=name,
        grid=(r // tr,),
        in_specs=[blk, blk, blk] + [slab(s) for s in range(n_slab)],
        out_specs=[blk, blk, blk, blk],
        out_shape=[jax.ShapeDtypeStruct((r, c), F32)] * 4,
        compiler_params=_params("arbitrary"),
    )(w, m, v, *parts)


MESH = pl.DeviceIdType.MESH
ANY = pl.BlockSpec(memory_space=pl.ANY)


def _dev_index(dev):
    return 4 * dev[0] + 2 * dev[1] + dev[2]


WHOLE_SHAPES = ((D_MODEL, IN_COLS), (D_MODEL, D_MODEL), (S5_W, S5_W))
SHARD_SHAPES = ((D_MODEL, SH_IN), (SH_OUT, D_MODEL), (SH_GLU, S5_W))


def _shard_of(ref, ti, idx):
    if ti == 0:
        return ref.at[:, pl.ds(pl.multiple_of(idx * SH_IN, 128), SH_IN)]
    rows = SHARD_SHAPES[ti][0]
    return ref.at[pl.ds(pl.multiple_of(idx * rows, rows), rows), :]


def _peer(mask, x, y, c):
    return (1 - x if mask & 4 else x, 1 - y if mask & 2 else y, 1 - c if mask & 1 else c)


def _allgather_weights(wi, wo, wg):
    n_t = 3

    def body(wi_ref, wo_ref, wg_ref, gi_ref, go_ref, gg_ref, send_sems, recv_sems, local_sems):
        x, y, c = lax.axis_index("x"), lax.axis_index("y"), lax.axis_index("c")
        me, sibling = (x, y, c), (x, y, 1 - c)
        chips = [(1 - x, y), (x, 1 - y), (1 - x, 1 - y)]
        shards = (wi_ref, wo_ref, wg_ref)
        wholes = (gi_ref, go_ref, gg_ref)

        def slot(ti, dev):
            return _shard_of(wholes[ti], ti, _dev_index(dev))

        def copy(k, ti, block, to, own=False):
            return pltpu.make_async_remote_copy(
                src_ref=shards[ti] if own else slot(ti, block), dst_ref=slot(ti, block),
                send_sem=send_sems.at[n_t * k + ti], recv_sem=recv_sems.at[n_t * k + ti],
                device_id=to, device_id_type=MESH)

        mine = [pltpu.make_async_copy(shards[ti], slot(ti, me), local_sems.at[ti]) for ti in range(n_t)]
        for cp in mine:
            cp.start()
        first = [copy(0, ti, me, sibling, own=True) for ti in range(n_t)]
        first += [copy(1 + j, ti, me, (*chip, c), own=True) for j, chip in enumerate(chips) for ti in range(n_t)]
        for cp in first:
            cp.start()
        passed = []
        for j, chip in enumerate(chips):
            for ti in range(n_t):
                copy(1 + j, ti, (*chip, c), me).wait_recv()
            onward = [copy(4 + j, ti, (*chip, c), sibling) for ti in range(n_t)]
            for cp in onward:
                cp.start()
            passed += onward
        for ti in range(n_t):
            copy(0, ti, sibling, me).wait_recv()
        for j, chip in enumerate(chips):
            for ti in range(n_t):
                copy(4 + j, ti, (*chip, 1 - c), me).wait_recv()
        for cp in first + passed:
            cp.wait_send()
        for cp in mine:
            cp.wait()

    return pl.pallas_call(
        body,
        name="allgather_weights",
        in_specs=[ANY, ANY, ANY],
        out_specs=[ANY, ANY, ANY],
        out_shape=[jax.ShapeDtypeStruct(s, BF16) for s in WHOLE_SHAPES],
        scratch_shapes=[pltpu.SemaphoreType.DMA((7 * n_t,)), pltpu.SemaphoreType.DMA((7 * n_t,)),
                        pltpu.SemaphoreType.DMA((n_t,))],
    )(wi, wo, wg)


HBM = pl.BlockSpec(memory_space=pltpu.HBM)
SEM = pl.BlockSpec(memory_space=pltpu.SEMAPHORE)
GATHER, SCATTER, SHARE = "gather", "scatter", "share"


def _split_route(kind, ti, sending, me_idx, p_idx, src_ref, land_ref):
    owner = me_idx if sending else p_idx
    if kind == GATHER:
        return src_ref, _shard_of(land_ref, ti, owner)
    if kind == SCATTER:
        return _shard_of(src_ref, ti, p_idx), land_ref.at[owner]
    return src_ref, land_ref.at[owner]


def _split_start(name, srcs, lands, kinds, after=None):
    n = len(srcs)
    arrays = list(srcs) + list(lands) + ([] if after is None else [after])

    def body(*refs):
        src_refs, land_refs = refs[0:n], refs[n:2 * n]
        send_sems, recv_sems, token = refs[len(arrays)], refs[len(arrays) + 1], refs[-1]
        x, y, c = lax.axis_index("x"), lax.axis_index("y"), lax.axis_index("c")
        me_idx = _dev_index((x, y, c))
        for mask in range(1, N_DEV):
            p = _peer(mask, x, y, c)
            for i, (kind, ti) in enumerate(kinds):
                k = (mask - 1) * n + i
                src, dst = _split_route(kind, ti, True, me_idx, _dev_index(p), src_refs[i], land_refs[i])
                pltpu.make_async_remote_copy(src_ref=src, dst_ref=dst, send_sem=send_sems.at[k], recv_sem=recv_sems.at[k],
                                             device_id=p, device_id_type=MESH).start()
        token[...] = jnp.zeros_like(token)

    n_copies = (N_DEV - 1) * n
    return pl.pallas_call(
        body,
        name=name,
        in_specs=[HBM] * len(arrays),
        out_specs=(SEM, SEM) + (HBM,) * (2 * n) + (pl.BlockSpec(memory_space=pltpu.VMEM),),
        out_shape=(pltpu.SemaphoreType.DMA((n_copies,)), pltpu.SemaphoreType.DMA((n_copies,)))
        + tuple(pltpu.HBM(a.shape, a.dtype) for a in arrays[:2 * n]) + (jax.ShapeDtypeStruct((8, 128), F32),),
        input_output_aliases={i: 2 + i for i in range(2 * n)},
        compiler_params=pltpu.CompilerParams(has_side_effects=pltpu.SideEffectType.DATAFLOW_SIDE_EFFECTING),
    )(*[pltpu.with_memory_space_constraint(a, pltpu.HBM) for a in arrays])


def _split_wait(name, started, kinds, after):
    n = len(kinds)
    send_sems, recv_sems, thru = started[0], started[1], started[2:2 + 2 * n]

    def body(*refs):
        src_refs, land_refs = refs[0:n], refs[n:2 * n]
        send_sems, recv_sems = refs[2 * n], refs[2 * n + 1]
        x, y, c = lax.axis_index("x"), lax.axis_index("y"), lax.axis_index("c")
        me_idx = _dev_index((x, y, c))
        for mask in range(1, N_DEV):
            p = _peer(mask, x, y, c)
            for i, (kind, ti) in enumerate(kinds):
                k = (mask - 1) * n + i
                src, dst = _split_route(kind, ti, False, me_idx, _dev_index(p), src_refs[i], land_refs[i])
                cp = pltpu.make_async_remote_copy(src_ref=src, dst_ref=dst, send_sem=send_sems.at[k],
                                                  recv_sem=recv_sems.at[k], device_id=p, device_id_type=MESH)
                cp.wait_send()
                cp.wait_recv()

    res = pl.pallas_call(
        body,
        name=name,
        in_specs=[HBM] * (2 * n) + [SEM, SEM, pl.BlockSpec(memory_space=pl.ANY)],
        out_specs=(HBM,) * (2 * n),
        out_shape=tuple(pltpu.HBM(a.shape, a.dtype) for a in thru),
        input_output_aliases={i: i for i in range(2 * n)},
        compiler_params=pltpu.CompilerParams(has_side_effects=pltpu.SideEffectType.DATAFLOW_SIDE_EFFECTING),
    )(*thru, send_sems, recv_sems, after)
    return res[n:2 * n]


def _with_own(whole_shape, dtype, own, start):
    return lax.dynamic_update_slice(lax.empty(whole_shape, dtype), own, start)


def _share_small(buf):
    def body(b_ref, o_ref, send_sems, recv_sems, local_sem):
        x, y, c = lax.axis_index("x"), lax.axis_index("y"), lax.axis_index("c")
        me_idx = _dev_index((x, y, c))
        local = pltpu.make_async_copy(b_ref, o_ref.at[me_idx], local_sem)
        local.start()

        def copy(mask, owner):
            return pltpu.make_async_remote_copy(
                src_ref=b_ref, dst_ref=o_ref.at[owner], send_sem=send_sems.at[mask - 1], recv_sem=recv_sems.at[mask - 1],
                device_id=_peer(mask, x, y, c), device_id_type=MESH)

        sends = [copy(mask, me_idx) for mask in range(1, N_DEV)]
        for cp in sends:
            cp.start()
        for mask in range(1, N_DEV):
            copy(mask, _dev_index(_peer(mask, x, y, c))).wait_recv()
        for cp in sends:
            cp.wait_send()
        local.wait()

    return pl.pallas_call(
        body,
        name="share_small",
        in_specs=[ANY],
        out_specs=ANY,
        out_shape=jax.ShapeDtypeStruct((N_DEV,) + buf.shape, buf.dtype),
        scratch_shapes=[pltpu.SemaphoreType.DMA((N_DEV - 1,)), pltpu.SemaphoreType.DMA((N_DEV - 1,)),
                        pltpu.SemaphoreType.DMA],
    )(buf)


def _s5_prep(lam_re, lam_im, b_re, b_im, c_re, c_im, d_skip, log_dt):
    dt = jnp.exp(log_dt)[:, None]
    mag = jnp.exp(lam_re * dt)
    a_re, a_im = mag * jnp.cos(lam_im * dt), mag * jnp.sin(lam_im * dt)
    den = lam_re * lam_re + lam_im * lam_im
    f_re = ((a_re - 1.0) * lam_re + a_im * lam_im) / den
    f_im = (a_im * lam_re - (a_re - 1.0) * lam_im) / den
    bb_re = f_re[..., None] * b_re - f_im[..., None] * b_im
    bb_im = f_re[..., None] * b_im + f_im[..., None] * b_re
    eye = jnp.eye(8, dtype=F32)

    def in_map(bb):
        return jnp.einsum("jgph,gk->ghjkp", bb.reshape(4, 8, S5_STATE, S5_CH), eye).reshape(128, N_STATE)

    def out_map(cm):
        return jnp.einsum("jghp,gk->ghjkp", cm.reshape(4, 8, S5_CH, S5_STATE), eye).reshape(128, N_STATE)

    a = jnp.concatenate([a_re.reshape(N_PAIR, 128), a_im.reshape(N_PAIR, 128)])
    bc = jnp.concatenate([in_map(bb_re), in_map(bb_im)], axis=1)
    cct = jnp.concatenate([out_map(c_re), -out_map(c_im)], axis=1)
    return a, bc, cct, d_skip.reshape(1, S5_W)


def _s5_powers(lam_re, lam_im, log_dt, seg_len):
    dt = jnp.exp(log_dt)[:, None]
    k = jnp.repeat(jnp.arange(1, seg_len + 1, dtype=F32), SEG)[:, None, None]
    mag = jnp.exp(k * (lam_re * dt))
    ang = k * (lam_im * dt)
    tiles = lambda v: jnp.swapaxes(v.reshape(SEG * seg_len, N_PAIR, 128), 0, 1)
    p_re, p_im = tiles(mag * jnp.cos(ang)), tiles(mag * jnp.sin(ang))
    fwd = jnp.concatenate([p_re, p_im])
    bwd = jnp.concatenate([p_re[:, ::-1], -p_im[:, ::-1]])
    return fwd, bwd


WEIGHTS = ["norm_g", "w_in", "lam_re", "lam_im", "b_re", "b_im", "c_re", "c_im", "d_skip", "log_dt", "w_glu", "b_glu",
           "ln_g", "ln_b", "w_s", "b_s", "w_pool", "pool_scale", "w_out", "final_g"]
SHARDED = ("w_in", "w_glu", "w_out")
SMALL = [n for n in WEIGHTS if n not in SHARDED]
INPUTS = ["x"] + WEIGHTS + ["loss_target"] + ["m_" + n for n in WEIGHTS] + ["v_" + n for n in WEIGHTS]
SMALL_B = ["norm_g", "final_g"]
SMALL_A = [n for n in SMALL if n not in SMALL_B]
SMALL_TILE = 16 * 128


def _pack_small(arrays, dtype):
    flat = jnp.concatenate([a.reshape(-1) for a in arrays])
    pad = (-flat.shape[0]) % SMALL_TILE
    return jnp.pad(flat, (0, pad)).astype(dtype).reshape(-1, 128)


def _unpack_small(packed, like):
    flat = packed.reshape(-1)
    out, off = [], 0
    for a in like:
        out.append(flat[off:off + a.size].reshape(a.shape))
        off += a.size
    return out


def _layer_fwd(p, l, x, wi, wo, wg, token=None):
    row = lambda v: v.reshape(1, -1)
    causal = jnp.tril(jnp.ones((CHUNK, CHUNK), dtype=bool))
    (a, bc, cct, dvec), prep_vjp = jax.vjp(
        _s5_prep, p["lam_re"][l], p["lam_im"][l], p["b_re"][l], p["b_im"][l], p["c_re"][l], p["c_im"][l],
        p["d_skip"][l], p["log_dt"][l])
    ws_f32 = jnp.where(causal[None], p["w_s"][l], 0.0)
    pw_fwd, pw_bwd = _s5_powers(p["lam_re"][l], p["lam_im"][l], p["log_dt"][l], S5_TB // SEG)
    pm, pt = _step_major(S5_TB)
    c = dict(
        x=x, wi=wi, wo=wo, wg=wg, a=a, bc=bc.astype(BF16), cc=cct.T.astype(BF16), dvec=dvec, prep_vjp=prep_vjp,
        pw_bwd=pw_bwd, pm=pm, pt=pt,
        ws=ws_f32.astype(BF16), wst=jnp.swapaxes(ws_f32, 1, 2).astype(BF16),
        bsx=jnp.broadcast_to(p["b_s"][l][:, :, None], (SGU_HEADS, CHUNK, 128)),
        wpool=p["w_pool"][l].astype(BF16), scale=row(p["pool_scale"][l]),
        lng=row(p["ln_g"][l]), lnb=row(p["ln_b"][l]), bglu=row(p["b_glu"][l]), norm_g=row(p["norm_g"][l]))
    c["z"], c["h"] = _rms_inproj(x, c["norm_g"], wi, token)
    c["ya"], c["s"], c["ys"] = _s5_fwd(c["z"], c["bc"], c["cc"], a, pw_fwd, pm, pt, dvec, wg, c["bglu"], S5_TB)
    c["yb"] = _sgu_fwd(c["z"], c["lng"], c["lnb"], c["ws"], c["bsx"])
    c["yc"] = _pool_fwd(c["z"], c["wpool"], c["scale"])
    return _outproj(x, c["ya"], c["yb"], c["yc"], wo), c


def _mixers_bwd(c, dx, token=None):
    dya, dyb, dyc = _outproj_bwd(dx, c["wo"], token)
    dwo = _wgrad_out(c["ya"], c["yb"], c["yc"], dx, token)
    dz_s5, dbc, dcct, da, dd, dwg, dbglu = _s5_bwd(
        dya, c["ys"], c["z"], c["s"], c["bc"], c["cc"], c["a"], c["pw_bwd"], c["pm"], c["pt"], c["dvec"], c["wg"],
        c["bglu"], S5_TB)
    dz_sgu, dlng, dlnb, dws, dbsx = _sgu_bwd(dyb, c["z"], c["lng"], c["lnb"], c["ws"], c["wst"], c["bsx"])
    dz_pool, dwp, dsc = _pool_bwd(dyc, c["z"], c["wpool"], c["scale"])
    g_lam_re, g_lam_im, g_b_re, g_b_im, g_c_re, g_c_im, g_d, g_dt = c["prep_vjp"]((da, dbc, dcct, dd))
    small = dict(lam_re=g_lam_re, lam_im=g_lam_im, b_re=g_b_re, b_im=g_b_im, c_re=g_c_re,
                 c_im=g_c_im, d_skip=g_d, log_dt=g_dt, b_glu=dbglu.reshape(-1), ln_g=dlng.reshape(-1),
                 ln_b=dlnb.reshape(-1), w_s=dws, b_s=jnp.sum(dbsx, axis=-1), w_pool=dwp, pool_scale=dsc.reshape(-1))
    return (dz_s5, dz_sgu, dz_pool), dwo, dwg, small


def _inproj_grads(c, dz, dx, token_w=None, token_x=None):
    dwi = _wgrad_in(c["h"], *dz, token_w)
    dx, dnorm = _inproj_bwd(*dz, c["wi"], c["x"], c["norm_g"], dx, token_x)
    return dwi, dx, dnorm.reshape(-1)


def _own_index():
    return _dev_index((lax.axis_index("x"), lax.axis_index("y"), lax.axis_index("c")))


def _shard_start(ti, idx):
    return (0, idx * SH_IN) if ti == 0 else (idx * SHARD_SHAPES[ti][0], 0)


def kernel(x, norm_g, w_in, lam_re, lam_im, b_re, b_im, c_re, c_im, d_skip, log_dt, w_glu, b_glu, ln_g, ln_b, w_s, b_s, w_pool, pool_scale, w_out, final_g, loss_target, m_norm_g, m_w_in, m_lam_re, m_lam_im, m_b_re, m_b_im, m_c_re, m_c_im, m_d_skip, m_log_dt, m_w_glu, m_b_glu, m_ln_g, m_ln_b, m_w_s, m_b_s, m_w_pool, m_pool_scale, m_w_out, m_final_g, v_norm_g, v_w_in, v_lam_re, v_lam_im, v_b_re, v_b_im, v_c_re, v_c_im, v_d_skip, v_log_dt, v_w_glu, v_b_glu, v_ln_g, v_ln_b, v_w_s, v_b_s, v_w_pool, v_pool_scale, v_w_out, v_final_g):
    p = dict(zip(INPUTS, (x, norm_g, w_in, lam_re, lam_im, b_re, b_im, c_re, c_im, d_skip, log_dt, w_glu, b_glu, ln_g, ln_b, w_s, b_s, w_pool, pool_scale, w_out, final_g, loss_target, m_norm_g, m_w_in, m_lam_re, m_lam_im, m_b_re, m_b_im, m_c_re, m_c_im, m_d_skip, m_log_dt, m_w_glu, m_b_glu, m_ln_g, m_ln_b, m_w_s, m_b_s, m_w_pool, m_pool_scale, m_w_out, m_final_g, v_norm_g, v_w_in, v_lam_re, v_lam_im, v_b_re, v_b_im, v_c_re, v_c_im, v_d_skip, v_log_dt, v_w_glu, v_b_glu, v_ln_g, v_ln_b, v_w_s, v_b_s, v_w_pool, v_pool_scale, v_w_out, v_final_g)))

    me = _own_index()
    shards = [[w[l].astype(BF16) for w in (w_in, w_out, w_glu)] for l in range(DEPTH)]
    gather3 = [(GATHER, ti) for ti in range(3)]
    scatter3 = [(SCATTER, ti) for ti in range(3)]

    def stack_with_own(partial, ti):
        own = lax.dynamic_slice(partial, _shard_start(ti, me), SHARD_SHAPES[ti])
        return _with_own((N_DEV,) + SHARD_SHAPES[ti], partial.dtype, own[None], (me, 0, 0))

    whole0 = _allgather_weights(*shards[0])
    lands = [_with_own(WHOLE_SHAPES[ti], BF16, shards[1][ti], _shard_start(ti, me)) for ti in range(3)]
    gather1 = _split_start("gather1_start", shards[1], lands, gather3, after=whole0[2])
    x1, c0 = _layer_fwd(p, 0, x[0], *whole0, token=gather1[-1])
    whole1 = _split_wait("gather1_wait", gather1, gather3, x1)
    x2, c1 = _layer_fwd(p, 1, x1, *whole1)

    dx, loss_tile, dfinal = _final_loss(x2, final_g.reshape(1, -1), loss_target[0])
    loss = lax.psum(loss_tile[0, 0], ("x", "y", "c"))

    dz1, dwo1, dwg1, small1 = _mixers_bwd(c1, dx)
    dwi1, dx, dnorm1 = _inproj_grads(c1, dz1, dx)
    partials1 = [dwi1, dwo1, dwg1.astype(BF16)]
    grads1 = _split_start("grads1_start", partials1, [stack_with_own(g, ti) for ti, g in enumerate(partials1)], scatter3)
    dz0, dwo0, dwg0, small0 = _mixers_bwd(c0, dx, token=grads1[-1])
    small_a = _pack_small([jnp.stack([small0[n], small1[n]]) for n in SMALL_A], BF16)
    srcs_a, kinds_a = [dwo0, dwg0.astype(BF16), small_a], [(SCATTER, 1), (SCATTER, 2), (SHARE, None)]
    lands_a = [stack_with_own(srcs_a[0], 1), stack_with_own(srcs_a[1], 2),
               _with_own((N_DEV,) + small_a.shape, BF16, small_a[None], (me, 0, 0))]
    grads0a = _split_start("grads0a_start", srcs_a, lands_a, kinds_a)
    dwi0 = _wgrad_in(c0["h"], *dz0, grads0a[-1])
    grads0b = _split_start("grads0b_start", [dwi0], [stack_with_own(dwi0, 0)], [(SCATTER, 0)])
    dx, dnorm0 = _inproj_bwd(*dz0, c0["wi"], c0["x"], c0["norm_g"], dx, grads0b[-1])
    parts1 = _split_wait("grads1_wait", grads1, scatter3, dx)
    r_out0, r_glu0, r_a = _split_wait("grads0a_wait", grads0a, kinds_a, dx)
    (r_in0,) = _split_wait("grads0b_wait", grads0b, [(SCATTER, 0)], dx)
    parts0 = [r_in0, r_out0, r_glu0]
    r_b = _share_small(_pack_small([jnp.stack([dnorm0.reshape(-1), dnorm1]), dfinal.reshape(-1)], F32))

    out = {}

    def adam(name, ti, tr):
        shape2d = (DEPTH * SHARD_SHAPES[ti][0], SHARD_SHAPES[ti][1])
        res = _adamw(p[name].reshape(shape2d), p["m_" + name].reshape(shape2d), p["v_" + name].reshape(shape2d),
                     [parts0[ti], parts1[ti]], tr, "adamw_" + name)
        out[name] = [r.reshape(p[name].shape) for r in res]

    adam("w_in", 0, 256)
    adam("w_out", 1, 128)
    adam("w_glu", 2, 64)
    for names, parts, steps, tag in ((SMALL_A, r_a, 7, "a"), (SMALL_B, r_b, 1, "b")):
        like = [p[n] for n in names]
        packed = _adamw(*[_pack_small([p[pre + n] for n in names], F32) for pre in ("", "m_", "v_")],
                        [parts], parts.shape[1] // steps, "adamw_small_" + tag)
        for n, vals in zip(names, zip(*[_unpack_small(r, like) for r in packed])):
            out[n] = list(vals)

    return (loss, dx[None], *[out[n][0] for n in WEIGHTS], *[out[n][1] for n in WEIGHTS],
            *[out[n][2] for n in WEIGHTS], *[out[n][3] for n in WEIGHTS])
```

```python
import functools
import math

import jax
import jax.numpy as jnp
import numpy as np
from jax import lax
from jax.experimental import pallas as pl
from jax.experimental.pallas import tpu as pltpu

F32 = jnp.float32
BF16 = jnp.bfloat16

D_MODEL = 2048
DEPTH = 2
S5_W, SGU_W, POOL_W = 512, 1024, 512
S5_GROUPS, S5_STATE, S5_CH = 32, 64, 16
N_STATE = S5_GROUPS * S5_STATE
CHUNK = 128
SGU_HEADS = 8
POOL_WINDOWS = (2, 4, 8, 16)
IN_COLS = 5120
RMS_EPS = 1e-6
LN_EPS = 1e-5
ADAM_LR, ADAM_B1, ADAM_B2, ADAM_EPS, ADAM_WD, ADAM_STEP = 0.001, 0.9, 0.999, 1e-08, 0.01, 10

CB = 512
N_CB = IN_COLS // CB
CB_XA, CB_U, CB_V, CB_XC, CB_GA, CB_GB, CB_GC = 0, 1, 3, 5, 6, 7, 9

N_DEV = 8
SH_IN = IN_COLS // N_DEV
SH_OUT = D_MODEL // N_DEV
SH_GLU = S5_W // N_DEV

VMEM_LIMIT = 52 * 1024 * 1024
HALO = 16
LANE_CH = 512

TN = (((0,), (0,)), ((), ()))
NT = (((1,), (1,)), ((), ()))


def _params(*sem):
    return pltpu.CompilerParams(dimension_semantics=sem if sem else None, vmem_limit_bytes=VMEM_LIMIT)


def _dot(a, b, dims=None):
    if dims is None:
        return jnp.dot(a, b, preferred_element_type=F32)
    return lax.dot_general(a, b, dims, preferred_element_type=F32)


_GELU_C = math.sqrt(2.0 / math.pi)


def _gelu(x):
    return 0.5 * x * (1.0 + jnp.tanh(_GELU_C * (x + 0.044715 * x * x * x)))


def _gelu_grad(x):
    t = jnp.tanh(_GELU_C * (x + 0.044715 * x * x * x))
    return 0.5 * (1.0 + t) + 0.5 * x * (1.0 - t * t) * _GELU_C * (1.0 + 3.0 * 0.044715 * x * x)


def _silu_and_grad(x):
    s = jax.nn.sigmoid(x)
    return x * s, s * (1.0 + x * (1.0 - s))


def _rowsum(x):
    return jnp.sum(x, axis=0, keepdims=True)


def _after(token):
    if token is None:
        return [], []
    return [pl.BlockSpec((8, 128), lambda *_: (0, 0))], [token]


def _rms_inproj(x, g, w, token=None, tm=1024):
    t = x.shape[0]
    tm = min(tm, t)
    after_specs, after = _after(token)

    def body(x_ref, g_ref, w_ref, *rest):
        z_ref, h_ref = rest[-2:]

        @pl.when(pl.program_id(1) == 0)
        def _():
            xv = x_ref[...]
            r = lax.rsqrt(jnp.mean(xv * xv, axis=-1, keepdims=True) + RMS_EPS)
            h_ref[...] = (xv * r * g_ref[...]).astype(BF16)

        z_ref[...] = _dot(h_ref[...], w_ref[...])

    return pl.pallas_call(
        body,
        name="rms_inproj",
        grid=(t // tm, N_CB),
        in_specs=[
            pl.BlockSpec((tm, D_MODEL), lambda m, n: (m, 0)),
            pl.BlockSpec((1, D_MODEL), lambda m, n: (0, 0)),
            pl.BlockSpec((D_MODEL, CB), lambda m, n: (0, n)),
        ] + after_specs,
        out_specs=[
            pl.BlockSpec((tm, CB), lambda m, n: (m, n)),
            pl.BlockSpec((tm, D_MODEL), lambda m, n: (m, 0)),
        ],
        out_shape=[jax.ShapeDtypeStruct((t, IN_COLS), F32), jax.ShapeDtypeStruct((t, D_MODEL), BF16)],
        compiler_params=_params("arbitrary", "arbitrary"),
    )(x, g, w, *after)


def _outproj(x, ya, yb, yc, w, tm=1024, tn=1024):
    t = x.shape[0]
    tm = min(tm, t)

    def body(x_ref, ya_ref, yb_ref, yc_ref, w_ref, o_ref):
        acc = _dot(ya_ref[...], w_ref[0:S5_W, :])
        acc += _dot(yb_ref[...], w_ref[S5_W:S5_W + SGU_W, :])
        acc += _dot(yc_ref[...], w_ref[S5_W + SGU_W:, :])
        o_ref[...] = x_ref[...] + acc

    return pl.pallas_call(
        body,
        name="outproj",
        grid=(t // tm, D_MODEL // tn),
        in_specs=[
            pl.BlockSpec((tm, tn), lambda m, n: (m, n)),
            pl.BlockSpec((tm, S5_W), lambda m, n: (m, 0)),
            pl.BlockSpec((tm, SGU_W), lambda m, n: (m, 0)),
            pl.BlockSpec((tm, POOL_W), lambda m, n: (m, 0)),
            pl.BlockSpec((D_MODEL, tn), lambda m, n: (0, n)),
        ],
        out_specs=pl.BlockSpec((tm, tn), lambda m, n: (m, n)),
        out_shape=jax.ShapeDtypeStruct((t, D_MODEL), F32),
        compiler_params=_params("arbitrary", "arbitrary"),
    )(x, ya, yb, yc, w)


def _outproj_bwd(dx, w, token=None, tm=512):
    t = dx.shape[0]
    after_specs, after = _after(token)

    def body(dx_ref, w_ref, *rest):
        dya_ref, dyb_ref, dyc_ref = rest[-3:]
        dy = _dot(dx_ref[...].astype(BF16), w_ref[...], NT)
        dya_ref[...] = dy[:, 0:S5_W]
        dyb_ref[...] = dy[:, S5_W:S5_W + SGU_W]
        dyc_ref[...] = dy[:, S5_W + SGU_W:]

    return pl.pallas_call(
        body,
        name="outproj_bwd",
        grid=(t // tm,),
        in_specs=[
            pl.BlockSpec((tm, D_MODEL), lambda m: (m, 0)),
            pl.BlockSpec((D_MODEL, D_MODEL), lambda m: (0, 0)),
        ] + after_specs,
        out_specs=[
            pl.BlockSpec((tm, S5_W), lambda m: (m, 0)),
            pl.BlockSpec((tm, SGU_W), lambda m: (m, 0)),
            pl.BlockSpec((tm, POOL_W), lambda m: (m, 0)),
        ],
        out_shape=[
            jax.ShapeDtypeStruct((t, S5_W), F32),
            jax.ShapeDtypeStruct((t, SGU_W), F32),
            jax.ShapeDtypeStruct((t, POOL_W), F32),
        ],
        compiler_params=_params("arbitrary"),
    )(dx, w, *after)


def _dz_piece_maps():
    s5_map = lambda j: jnp.where(j >= CB_GA, 1, 0)
    sgu_map = lambda j: jnp.clip(jnp.where(j <= 4, j - 1, j - 3), 0, 5)
    pool_map = lambda j: jnp.where(j >= CB_GC, 1, 0)
    return s5_map, sgu_map, pool_map


def _pick_piece(j):
    is_s5 = jnp.logical_or(j == CB_XA, j == CB_GA)
    is_pool = jnp.logical_or(j == CB_XC, j == CB_GC)
    return is_s5, is_pool, jnp.logical_not(jnp.logical_or(is_s5, is_pool))


def _inproj_bwd(dz_s5, dz_sgu, dz_pool, w, x, g, dxo, token=None, tm=512):
    t = x.shape[0]
    s5_map, sgu_map, pool_map = _dz_piece_maps()
    after_specs, after = _after(token)

    def body(s5_ref, sgu_ref, pool_ref, w_ref, x_ref, g_ref, dxo_ref, *rest):
        dx_ref, dg_ref, acc = rest[-3:]
        m, j = pl.program_id(0), pl.program_id(1)

        @pl.when(jnp.logical_and(m == 0, j == 0))
        def _():
            dg_ref[...] = jnp.zeros_like(dg_ref)

        @pl.when(j == 0)
        def _():
            acc[...] = jnp.zeros_like(acc)

        is_s5, is_pool, is_sgu = _pick_piece(j)

        @pl.when(is_s5)
        def _():
            acc[...] += _dot(s5_ref[...], w_ref[...], NT)

        @pl.when(is_sgu)
        def _():
            acc[...] += _dot(sgu_ref[...], w_ref[...], NT)

        @pl.when(is_pool)
        def _():
            acc[...] += _dot(pool_ref[...], w_ref[...], NT)

        @pl.when(j == N_CB - 1)
        def _():
            xv = x_ref[...]
            r = lax.rsqrt(jnp.mean(xv * xv, axis=-1, keepdims=True) + RMS_EPS)
            n = xv * r
            dh = acc[...]
            dg_ref[...] += _rowsum(dh * n)
            dn = dh * g_ref[...]
            dx_ref[...] = dxo_ref[...] + r * (dn - n * jnp.mean(dn * n, axis=-1, keepdims=True))

    return pl.pallas_call(
        body,
        name="inproj_bwd",
        grid=(t // tm, N_CB),
        in_specs=[
            pl.BlockSpec((tm, CB), lambda m, j: (m, s5_map(j))),
            pl.BlockSpec((tm, CB), lambda m, j: (m, sgu_map(j))),
            pl.BlockSpec((tm, CB), lambda m, j: (m, pool_map(j))),
            pl.BlockSpec((D_MODEL, CB), lambda m, j: (0, j)),
            pl.BlockSpec((tm, D_MODEL), lambda m, j: (m, 0)),
            pl.BlockSpec((1, D_MODEL), lambda m, j: (0, 0)),
            pl.BlockSpec((tm, D_MODEL), lambda m, j: (m, 0)),
        ] + after_specs,
        out_specs=[
            pl.BlockSpec((tm, D_MODEL), lambda m, j: (m, 0)),
            pl.BlockSpec((1, D_MODEL), lambda m, j: (0, 0)),
        ],
        out_shape=[jax.ShapeDtypeStruct((t, D_MODEL), F32), jax.ShapeDtypeStruct((1, D_MODEL), F32)],
        scratch_shapes=[pltpu.VMEM((tm, D_MODEL), F32)],
        compiler_params=_params("arbitrary", "arbitrary"),
    )(dz_s5, dz_sgu, dz_pool, w, x, g, dxo, *after)


def _wgrad_in(h, dz_s5, dz_sgu, dz_pool, token=None, tm=512):
    t = h.shape[0]
    s5_map, sgu_map, pool_map = _dz_piece_maps()
    after_specs, after = _after(token)

    def body(h_ref, s5_ref, sgu_ref, pool_ref, *rest):
        o_ref, acc = rest[-2:]
        j, m = pl.program_id(0), pl.program_id(1)

        @pl.when(m == 0)
        def _():
            acc[...] = jnp.zeros_like(acc)

        is_s5, is_pool, is_sgu = _pick_piece(j)

        @pl.when(is_s5)
        def _():
            acc[...] += _dot(s5_ref[...], h_ref[...], TN)

        @pl.when(is_sgu)
        def _():
            acc[...] += _dot(sgu_ref[...], h_ref[...], TN)

        @pl.when(is_pool)
        def _():
            acc[...] += _dot(pool_ref[...], h_ref[...], TN)

        @pl.when(m == pl.num_programs(1) - 1)
        def _():
            o_ref[...] = acc[...].T.astype(BF16)

    return pl.pallas_call(
        body,
        name="wgrad_in",
        grid=(N_CB, t // tm),
        in_specs=[
            pl.BlockSpec((tm, D_MODEL), lambda j, m: (m, 0)),
            pl.BlockSpec((tm, CB), lambda j, m: (m, s5_map(j))),
            pl.BlockSpec((tm, CB), lambda j, m: (m, sgu_map(j))),
            pl.BlockSpec((tm, CB), lambda j, m: (m, pool_map(j))),
        ] + after_specs,
        out_specs=pl.BlockSpec((D_MODEL, CB), lambda j, m: (0, j)),
        out_shape=jax.ShapeDtypeStruct((D_MODEL, IN_COLS), BF16),
        scratch_shapes=[pltpu.VMEM((CB, D_MODEL), F32)],
        compiler_params=_params("arbitrary", "arbitrary"),
    )(h, dz_s5, dz_sgu, dz_pool, *after)


def _wgrad_out(ya, yb, yc, dx, token=None, tm=512, tn=512):
    t = dx.shape[0]
    after_specs, after = _after(token)

    def body(ya_ref, yb_ref, yc_ref, dx_ref, *rest):
        o_ref, acc = rest[-2:]
        m = pl.program_id(1)

        @pl.when(m == 0)
        def _():
            acc[...] = jnp.zeros_like(acc)

        dxb = dx_ref[...].astype(BF16)
        acc[:, 0:S5_W] += _dot(dxb, ya_ref[...], TN)
        acc[:, S5_W:S5_W + SGU_W] += _dot(dxb, yb_ref[...], TN)
        acc[:, S5_W + SGU_W:] += _dot(dxb, yc_ref[...], TN)

        @pl.when(m == pl.num_programs(1) - 1)
        def _():
            o_ref[...] = acc[...].T.astype(BF16)

    return pl.pallas_call(
        body,
        name="wgrad_out",
        grid=(D_MODEL // tn, t // tm),
        in_specs=[
            pl.BlockSpec((tm, S5_W), lambda n, m: (m, 0)),
            pl.BlockSpec((tm, SGU_W), lambda n, m: (m, 0)),
            pl.BlockSpec((tm, POOL_W), lambda n, m: (m, 0)),
            pl.BlockSpec((tm, tn), lambda n, m: (m, n)),
        ] + after_specs,
        out_specs=pl.BlockSpec((D_MODEL, tn), lambda n, m: (0, n)),
        out_shape=jax.ShapeDtypeStruct((D_MODEL, D_MODEL), BF16),
        scratch_shapes=[pltpu.VMEM((tn, D_MODEL), F32)],
        compiler_params=_params("arbitrary", "arbitrary"),
    )(ya, yb, yc, dx, *after)


def _final_loss(x, g, target, tm=512):
    t = x.shape[0]

    def body(x_ref, g_ref, t_ref, dx_ref, loss_ref, dg_ref):
        @pl.when(pl.program_id(0) == 0)
        def _():
            loss_ref[...] = jnp.zeros_like(loss_ref)
            dg_ref[...] = jnp.zeros_like(dg_ref)

        xv = x_ref[...]
        gv = g_ref[...]
        r = lax.rsqrt(jnp.mean(xv * xv, axis=-1, keepdims=True) + RMS_EPS)
        n = xv * r
        err = n * gv - t_ref[...]
        loss_ref[...] += 0.5 * jnp.sum(jnp.mean(err * err, axis=-1, keepdims=True))
        dy = err * (1.0 / D_MODEL)
        dg_ref[...] += _rowsum(dy * n)
        dn = dy * gv
        dx_ref[...] = r * (dn - n * jnp.mean(dn * n, axis=-1, keepdims=True))

    return pl.pallas_call(
        body,
        name="final_loss",
        grid=(t // tm,),
        in_specs=[
            pl.BlockSpec((tm, D_MODEL), lambda m: (m, 0)),
            pl.BlockSpec((1, D_MODEL), lambda m: (0, 0)),
            pl.BlockSpec((tm, D_MODEL), lambda m: (m, 0)),
        ],
        out_specs=[
            pl.BlockSpec((tm, D_MODEL), lambda m: (m, 0)),
            pl.BlockSpec((8, 128), lambda m: (0, 0)),
            pl.BlockSpec((1, D_MODEL), lambda m: (0, 0)),
        ],
        out_shape=[
            jax.ShapeDtypeStruct((t, D_MODEL), F32),
            jax.ShapeDtypeStruct((8, 128), F32),
            jax.ShapeDtypeStruct((1, D_MODEL), F32),
        ],
        compiler_params=_params("arbitrary"),
    )(x, g, target)


N_Q = 2 * N_STATE // LANE_CH
N_LT = 2 * N_STATE // 128
N_PAIR = N_LT // 2
SEG = 8
PAIR_GROUP = 8
S5_TB = 256


def _cmul_add(b_re, b_im, a_re, a_im, s_re, s_im):
    return b_re + (a_re * s_re - a_im * s_im), b_im + (a_re * s_im + a_im * s_re)


def _s5_fill_powers(pw, a_ref, tb, reverse):
    seg_len = tb // SEG
    sign = -1.0 if reverse else 1.0
    for p in range(N_PAIR):
        a_re = jnp.broadcast_to(a_ref[p:p + 1, :], (SEG, 128))
        a_im = sign * jnp.broadcast_to(a_ref[N_PAIR + p:N_PAIR + p + 1, :], (SEG, 128))

        def step(k, c, p=p, a_re=a_re, a_im=a_im):
            rows = pl.ds(pl.multiple_of(((seg_len - 1 - k) if reverse else k) * SEG, SEG), SEG)
            pw[p, rows, :] = c[0]
            pw[N_PAIR + p, rows, :] = c[1]
            return c[0] * a_re - c[1] * a_im, c[0] * a_im + c[1] * a_re

        lax.fori_loop(0, seg_len, step, (a_re, a_im))


def _s5_scan(st, carry, a_ref, pw_ref, tb, reverse):
    seg_len = tb // SEG
    sign = -1.0 if reverse else 1.0
    sub = lax.broadcasted_iota(jnp.int32, (SEG, 128), 0)
    chain = (0 if reverse else seg_len - 1) * SEG
    full = lambda row: jnp.broadcast_to(row, (SEG, 128))
    for p0 in range(0, N_PAIR, PAIR_GROUP):
        pairs = list(range(p0, p0 + PAIR_GROUP))
        a_re = [full(a_ref[p:p + 1, :]) for p in pairs]
        a_im = [sign * full(a_ref[N_PAIR + p:N_PAIR + p + 1, :]) for p in pairs]

        def step(k, c, pairs=pairs, a_re=a_re, a_im=a_im):
            rows = pl.ds(pl.multiple_of(((seg_len - 1 - k) if reverse else k) * SEG, SEG), SEG)
            out = []
            for i, p in enumerate(pairs):
                n_re, n_im = _cmul_add(st[p, rows, :], st[N_PAIR + p, rows, :], a_re[i], a_im[i], c[2 * i], c[2 * i + 1])
                st[p, rows, :] = n_re
                st[N_PAIR + p, rows, :] = n_im
                out += [n_re, n_im]
            return tuple(out)

        ends = lax.fori_loop(0, seg_len, step, tuple(jnp.zeros((SEG, 128), F32) for _ in range(2 * PAIR_GROUP)))
        for i, p in enumerate(pairs):
            e_re, e_im = ends[2 * i], ends[2 * i + 1]
            w_re, w_im = pw_ref[p, chain:chain + SEG, :], pw_ref[N_PAIR + p, chain:chain + SEG, :]
            c_re, c_im = full(carry[p:p + 1, :]), full(carry[N_PAIR + p:N_PAIR + p + 1, :])
            for hop in range(SEG - 1):
                n_re, n_im = _cmul_add(e_re, e_im, w_re, w_im, c_re, c_im)
                target = SEG - 2 - hop if reverse else hop + 1
                shift = SEG - 1 if reverse else 1
                c_re = jnp.where(sub == target, pltpu.roll(n_re, shift, 0), c_re)
                c_im = jnp.where(sub == target, pltpu.roll(n_im, shift, 0), c_im)
            n_re, n_im = _cmul_add(e_re, e_im, w_re, w_im, c_re, c_im)
            last = 0 if reverse else SEG - 1
            carry[p:p + 1, :] = n_re[last:last + 1, :]
            carry[N_PAIR + p:N_PAIR + p + 1, :] = n_im[last:last + 1, :]
            in_re, in_im = jnp.tile(c_re, (seg_len, 1)), jnp.tile(c_im, (seg_len, 1))
            st[p], st[N_PAIR + p] = _cmul_add(st[p], st[N_PAIR + p], pw_ref[p], pw_ref[N_PAIR + p], in_re, in_im)


def _lane_chunk(ref, q):
    return jnp.concatenate([ref[4 * q + i] for i in range(4)], axis=1)


def _put_lane_chunk(ref, q, value):
    for i in range(4):
        ref[4 * q + i] = value[:, 128 * i:128 * (i + 1)]


def _step_major(tb):
    r = np.arange(tb)
    pm = np.zeros((tb, tb), np.float32)
    pm[r, (r % SEG) * (tb // SEG) + r // SEG] = 1.0
    return jnp.asarray(pm, BF16), jnp.asarray(pm.T, BF16)


def _unpermute(pt_ref, v):
    hi = v.astype(BF16)
    rest = v - hi.astype(F32)
    mid = rest.astype(BF16)
    lo = (rest - mid.astype(F32)).astype(BF16)
    return _dot(pt_ref[...], hi) + _dot(pt_ref[...], mid) + _dot(pt_ref[...], lo)


def _s5_fwd(z, bc, cc, a, pm, pt, dvec, wglu, bglu, tb=256):
    t = z.shape[0]

    def body(xa_ref, ga_ref, bc_ref, cc_ref, a_ref, pm_ref, pt_ref, d_ref, wglu_ref, bglu_ref,
             ya_ref, s_ref, ys_ref, st, carry, pw_ref):
        @pl.when(pl.program_id(0) == 0)
        def _():
            carry[...] = jnp.zeros_like(carry)
            _s5_fill_powers(pw_ref, a_ref, tb, reverse=False)

        xa = xa_ref[...]
        xab = _dot(pm_ref[...], xa.astype(BF16)).astype(BF16)
        for q in range(N_Q):
            _put_lane_chunk(st, q, _dot(xab[:, 128 * (q % 4):128 * (q % 4) + 128], bc_ref[:, pl.ds(LANE_CH * q, LANE_CH)]))
        _s5_scan(st, carry, a_ref, pw_ref, tb, reverse=False)
        s_ref[...] = st[...].astype(BF16)
        cols = []
        for j in range(4):
            lo, hi = LANE_CH * j, N_STATE + LANE_CH * j
            cols.append(_dot(_lane_chunk(s_ref, j), cc_ref[lo:lo + LANE_CH, :])
                        + _dot(_lane_chunk(s_ref, 4 + j), cc_ref[hi:hi + LANE_CH, :]))
        ys = _unpermute(pt_ref, jnp.concatenate(cols, axis=1)) + d_ref[...] * xa
        ys_ref[...] = ys
        ya1 = _gelu(ys)
        pre = _dot(ya1.astype(BF16), wglu_ref[...]) + bglu_ref[...]
        silu_ga, _ = _silu_and_grad(ga_ref[...])
        ya_ref[...] = (ya1 * jax.nn.sigmoid(pre) * silu_ga).astype(BF16)

    const = lambda shape: pl.BlockSpec(shape, lambda i: (0,) * len(shape))
    return pl.pallas_call(
        body,
        name="s5_fwd",
        grid=(t // tb,),
        in_specs=[
            pl.BlockSpec((tb, CB), lambda i: (i, CB_XA)),
            pl.BlockSpec((tb, CB), lambda i: (i, CB_GA)),
            const((128, 2 * N_STATE)),
            const((2 * N_STATE, 128)),
            const((N_LT, 128)),
            const((tb, tb)),
            const((tb, tb)),
            const((1, S5_W)),
            const((S5_W, S5_W)),
            const((1, S5_W)),
        ],
        out_specs=[
            pl.BlockSpec((tb, S5_W), lambda i: (i, 0)),
            pl.BlockSpec((N_LT, tb, 128), lambda i: (0, i, 0)),
            pl.BlockSpec((tb, S5_W), lambda i: (i, 0)),
        ],
        out_shape=[
            jax.ShapeDtypeStruct((t, S5_W), BF16),
            jax.ShapeDtypeStruct((N_LT, t, 128), BF16),
            jax.ShapeDtypeStruct((t, S5_W), F32),
        ],
        scratch_shapes=[pltpu.VMEM((N_LT, tb, 128), F32), pltpu.VMEM((N_LT, 128), F32), pltpu.VMEM((N_LT, tb, 128), F32)],
        compiler_params=_params("arbitrary"),
    )(z, z, bc, cc, a, pm, pt, dvec, wglu, bglu)


def _s5_bwd(dya, ys, z, s, bc, cc, a, pm, pt, dvec, wglu, bglu, tb=256):
    t = z.shape[0]
    nb = t // tb
    rev = lambda i: nb - 1 - i

    def body(dya_ref, ys_ref, xa_ref, ga_ref, s_ref, sp_ref, bc_ref, cc_ref, a_ref, pm_ref, pt_ref, d_ref,
             wglu_ref, bglu_ref, dz_ref, dbc_ref, dcct_ref, da_ref, dd_ref, dwglu_ref, dbglu_ref, g, carry, pw_ref):
        i = pl.program_id(0)

        @pl.when(i == 0)
        def _():
            carry[...] = jnp.zeros_like(carry)
            _s5_fill_powers(pw_ref, a_ref, tb, reverse=True)
            for r in (dbc_ref, dcct_ref, da_ref, dd_ref, dwglu_ref, dbglu_ref):
                r[...] = jnp.zeros_like(r)

        ys = ys_ref[...]
        xa = xa_ref[...]
        ga = ga_ref[...]
        dya = dya_ref[...]
        ya1 = _gelu(ys)
        ya1b = ya1.astype(BF16)
        sg = jax.nn.sigmoid(_dot(ya1b, wglu_ref[...]) + bglu_ref[...])
        silu_ga, silu_ga_grad = _silu_and_grad(ga)
        dz_ref[:, S5_W:] = (dya * (ya1 * sg) * silu_ga_grad).astype(BF16)
        dya2 = dya * silu_ga
        dpre = dya2 * ya1 * sg * (1.0 - sg)
        dbglu_ref[...] += _rowsum(dpre)
        dpreb = dpre.astype(BF16)
        dwglu_ref[...] += _dot(ya1b, dpreb, TN)
        dys = (dya2 * sg + _dot(dpreb, wglu_ref[...], NT)) * _gelu_grad(ys)
        dd_ref[...] += _rowsum(dys * xa)
        dysb = _dot(pm_ref[...], dys.astype(BF16)).astype(BF16)
        xab = _dot(pm_ref[...], xa.astype(BF16)).astype(BF16)

        for q in range(N_Q):
            cq = pl.ds(LANE_CH * q, LANE_CH)
            x0 = 128 * (q % 4)
            dcct_ref[:, cq] += _dot(dysb[:, x0:x0 + 128], _lane_chunk(s_ref, q), TN)
            _put_lane_chunk(g, q, _dot(dysb[:, x0:x0 + 128], cc_ref[cq, :], NT))
        _s5_scan(g, carry, a_ref, pw_ref, tb, reverse=True)

        seg0 = (lax.broadcasted_iota(jnp.int32, (SEG, 128), 0) == 0)
        have_prev = i < nb - 1

        def before(tile, halo):
            tile = tile.astype(F32)
            prev_last = jnp.where(have_prev, halo.astype(F32)[HALO - 1:HALO, :], 0.0)
            step0 = jnp.where(seg0, prev_last, pltpu.roll(tile[tb - SEG:, :], 1, 0))
            return jnp.concatenate([step0, tile[:tb - SEG, :]], axis=0)

        for p in range(N_PAIR):
            g_re, g_im = g[p], g[N_PAIR + p]
            sp_re, sp_im = before(s_ref[p], sp_ref[p]), before(s_ref[N_PAIR + p], sp_ref[N_PAIR + p])
            da_ref[p:p + 1, :] += _rowsum(sp_re * g_re + sp_im * g_im)
            da_ref[N_PAIR + p:N_PAIR + p + 1, :] += _rowsum(sp_re * g_im - sp_im * g_re)
        dxa_cols = []
        for j in range(4):
            re = pl.ds(LANE_CH * j, LANE_CH)
            im = pl.ds(N_STATE + LANE_CH * j, LANE_CH)
            x0 = 128 * j
            gb_re, gb_im = _lane_chunk(g, j).astype(BF16), _lane_chunk(g, 4 + j).astype(BF16)
            dbc_ref[:, re] += _dot(xab[:, x0:x0 + 128], gb_re, TN)
            dbc_ref[:, im] += _dot(xab[:, x0:x0 + 128], gb_im, TN)
            dxa_cols.append(_dot(gb_re, bc_ref[:, re], NT) + _dot(gb_im, bc_ref[:, im], NT))
        dz_ref[:, 0:S5_W] = (dys * d_ref[...] + _unpermute(pt_ref, jnp.concatenate(dxa_cols, axis=1))).astype(BF16)

    const = lambda shape: pl.BlockSpec(shape, lambda i: (0,) * len(shape))
    per_halo = tb // HALO
    return pl.pallas_call(
        body,
        name="s5_bwd",
        grid=(nb,),
        in_specs=[
            pl.BlockSpec((tb, S5_W), lambda i: (rev(i), 0)),
            pl.BlockSpec((tb, S5_W), lambda i: (rev(i), 0)),
            pl.BlockSpec((tb, CB), lambda i: (rev(i), CB_XA)),
            pl.BlockSpec((tb, CB), lambda i: (rev(i), CB_GA)),
            pl.BlockSpec((N_LT, tb, 128), lambda i: (0, rev(i), 0)),
            pl.BlockSpec((N_LT, HALO, 128), lambda i: (0, jnp.maximum(rev(i) * per_halo - 1, 0), 0)),
            const((128, 2 * N_STATE)),
            const((2 * N_STATE, 128)),
            const((N_LT, 128)),
            const((tb, tb)),
            const((tb, tb)),
            const((1, S5_W)),
            const((S5_W, S5_W)),
            const((1, S5_W)),
        ],
        out_specs=[
            pl.BlockSpec((tb, 2 * CB), lambda i: (rev(i), 0)),
            const((128, 2 * N_STATE)),
            const((128, 2 * N_STATE)),
            const((N_LT, 128)),
            const((1, S5_W)),
            const((S5_W, S5_W)),
            const((1, S5_W)),
        ],
        out_shape=[
            jax.ShapeDtypeStruct((t, 2 * CB), BF16),
            jax.ShapeDtypeStruct((128, 2 * N_STATE), F32),
            jax.ShapeDtypeStruct((128, 2 * N_STATE), F32),
            jax.ShapeDtypeStruct((N_LT, 128), F32),
            jax.ShapeDtypeStruct((1, S5_W), F32),
            jax.ShapeDtypeStruct((S5_W, S5_W), F32),
            jax.ShapeDtypeStruct((1, S5_W), F32),
        ],
        scratch_shapes=[pltpu.VMEM((N_LT, tb, 128), F32), pltpu.VMEM((N_LT, 128), F32), pltpu.VMEM((N_LT, tb, 128), F32)],
        compiler_params=_params("arbitrary"),
    )(dya, ys, z, z, s, s, bc, cc, a, pm, pt, dvec, wglu, bglu)


def _sgu_norm(v0, v1, lng_ref, lnb_ref):
    g0, g1 = _gelu(v0), _gelu(v1)
    mu = (jnp.sum(g0, axis=-1, keepdims=True) + jnp.sum(g1, axis=-1, keepdims=True)) * (1.0 / SGU_W)
    c0, c1 = g0 - mu, g1 - mu
    var = (jnp.sum(c0 * c0, axis=-1, keepdims=True) + jnp.sum(c1 * c1, axis=-1, keepdims=True)) * (1.0 / SGU_W)
    rstd = lax.rsqrt(var + LN_EPS)
    vh0, vh1 = c0 * rstd, c1 * rstd
    vn0 = vh0 * lng_ref[:, 0:CB] + lnb_ref[:, 0:CB]
    vn1 = vh1 * lng_ref[:, CB:] + lnb_ref[:, CB:]
    return (vh0, vh1), (vn0, vn1), rstd


def _sgu_fwd(z, lng, lnb, ws, bsx, tb=256):
    t = z.shape[0]

    def body(u0_ref, u1_ref, v0_ref, v1_ref, gb0_ref, gb1_ref, lng_ref, lnb_ref, ws_ref, bsx_ref, yb_ref):
        _, (vn0, vn1), _ = _sgu_norm(v0_ref[...], v1_ref[...], lng_ref, lnb_ref)
        for half, (vn, u_ref, gb_ref) in enumerate(((vn0, u0_ref, gb0_ref), (vn1, u1_ref, gb1_ref))):
            vnb = vn.astype(BF16)
            silu_gb, _ = _silu_and_grad(gb_ref[...])
            gate = _gelu(u_ref[...]) * silu_gb
            for hh in range(4):
                h = 4 * half + hh
                for c in range(tb // CHUNK):
                    rows, cols = slice(CHUNK * c, CHUNK * (c + 1)), slice(128 * hh, 128 * (hh + 1))
                    sp = _dot(ws_ref[h], vnb[rows, cols]) + bsx_ref[h]
                    yb_ref[rows, CB * half + 128 * hh:CB * half + 128 * (hh + 1)] = (gate[rows, cols] * sp).astype(BF16)

    zb = lambda j: pl.BlockSpec((tb, CB), lambda i, j=j: (i, j))
    const = lambda shape: pl.BlockSpec(shape, lambda i: (0,) * len(shape))
    return pl.pallas_call(
        body,
        name="sgu_fwd",
        grid=(t // tb,),
        in_specs=[zb(CB_U), zb(CB_U + 1), zb(CB_V), zb(CB_V + 1), zb(CB_GB), zb(CB_GB + 1),
                  const((1, SGU_W)), const((1, SGU_W)), const((SGU_HEADS, CHUNK, CHUNK)), const((SGU_HEADS, CHUNK, 128))],
        out_specs=pl.BlockSpec((tb, SGU_W), lambda i: (i, 0)),
        out_shape=jax.ShapeDtypeStruct((t, SGU_W), BF16),
        compiler_params=_params("arbitrary"),
    )(z, z, z, z, z, z, lng, lnb, ws, bsx)


def _sgu_bwd(dyb, z, lng, lnb, ws, wst, bsx, tb=256):
    t = z.shape[0]

    def body(dyb_ref, u0_ref, u1_ref, v0_ref, v1_ref, gb0_ref, gb1_ref, lng_ref, lnb_ref, ws_ref, wst_ref, bsx_ref,
             dz_ref, dlng_ref, dlnb_ref, dws_ref, dbs_ref, dvn):
        @pl.when(pl.program_id(0) == 0)
        def _():
            for r in (dlng_ref, dlnb_ref, dws_ref, dbs_ref):
                r[...] = jnp.zeros_like(r)

        v0, v1 = v0_ref[...], v1_ref[...]
        (vh0, vh1), (vn0, vn1), rstd = _sgu_norm(v0, v1, lng_ref, lnb_ref)
        causal = (lax.broadcasted_iota(jnp.int32, (CHUNK, CHUNK), 0) >= lax.broadcasted_iota(jnp.int32, (CHUNK, CHUNK), 1))
        for half, (vn, u_ref, gb_ref) in enumerate(((vn0, u0_ref, gb0_ref), (vn1, u1_ref, gb1_ref))):
            vnb = vn.astype(BF16)
            u = u_ref[...]
            ug = _gelu(u)
            silu_gb, silu_gb_grad = _silu_and_grad(gb_ref[...])
            dyb = dyb_ref[:, CB * half:CB * (half + 1)]
            dyb0 = dyb * silu_gb
            ds = dyb0 * ug
            sp_cols = []
            for hh in range(4):
                h = 4 * half + hh
                cols = slice(128 * hh, 128 * (hh + 1))
                sp_rows = []
                for c in range(tb // CHUNK):
                    rows = slice(CHUNK * c, CHUNK * (c + 1))
                    vt = vnb[rows, cols]
                    sp_rows.append(_dot(ws_ref[h], vt) + bsx_ref[h])
                    dst = ds[rows, cols]
                    dstb = dst.astype(BF16)
                    dbs_ref[h] += dst
                    dws_ref[h] += jnp.where(causal, _dot(dstb, vt, NT), 0.0)
                    dvn[rows, CB * half + 128 * hh:CB * half + 128 * (hh + 1)] = _dot(wst_ref[h], dstb)
                sp_cols.append(jnp.concatenate(sp_rows, axis=0))
            sp = jnp.concatenate(sp_cols, axis=1)
            dz_ref[:, CB * half:CB * (half + 1)] = (dyb0 * sp * _gelu_grad(u)).astype(BF16)
            dz_ref[:, 2 * SGU_W + CB * half:2 * SGU_W + CB * (half + 1)] = (dyb * (ug * sp) * silu_gb_grad).astype(BF16)

        dvn0, dvn1 = dvn[:, 0:CB], dvn[:, CB:]
        dlng_ref[:, 0:CB] += _rowsum(dvn0 * vh0)
        dlng_ref[:, CB:] += _rowsum(dvn1 * vh1)
        dlnb_ref[:, 0:CB] += _rowsum(dvn0)
        dlnb_ref[:, CB:] += _rowsum(dvn1)
        dh0, dh1 = dvn0 * lng_ref[:, 0:CB], dvn1 * lng_ref[:, CB:]
        m1 = (jnp.sum(dh0, axis=-1, keepdims=True) + jnp.sum(dh1, axis=-1, keepdims=True)) * (1.0 / SGU_W)
        m2 = (jnp.sum(dh0 * vh0, axis=-1, keepdims=True) + jnp.sum(dh1 * vh1, axis=-1, keepdims=True)) * (1.0 / SGU_W)
        dz_ref[:, SGU_W:SGU_W + CB] = (rstd * (dh0 - m1 - vh0 * m2) * _gelu_grad(v0)).astype(BF16)
        dz_ref[:, SGU_W + CB:2 * SGU_W] = (rstd * (dh1 - m1 - vh1 * m2) * _gelu_grad(v1)).astype(BF16)

    zb = lambda j: pl.BlockSpec((tb, CB), lambda i, j=j: (i, j))
    const = lambda shape: pl.BlockSpec(shape, lambda i: (0,) * len(shape))
    hmat = (SGU_HEADS, CHUNK, CHUNK)
    return pl.pallas_call(
        body,
        name="sgu_bwd",
        grid=(t // tb,),
        in_specs=[pl.BlockSpec((tb, SGU_W), lambda i: (i, 0)),
                  zb(CB_U), zb(CB_U + 1), zb(CB_V), zb(CB_V + 1), zb(CB_GB), zb(CB_GB + 1),
                  const((1, SGU_W)), const((1, SGU_W)), const(hmat), const(hmat), const(hmat)],
        out_specs=[pl.BlockSpec((tb, 3 * SGU_W), lambda i: (i, 0)),
                   const((1, SGU_W)), const((1, SGU_W)), const(hmat), const(hmat)],
        out_shape=[jax.ShapeDtypeStruct((t, 3 * SGU_W), BF16),
                   jax.ShapeDtypeStruct((1, SGU_W), F32), jax.ShapeDtypeStruct((1, SGU_W), F32),
                   jax.ShapeDtypeStruct(hmat, F32), jax.ShapeDtypeStruct(hmat, F32)],
        scratch_shapes=[pltpu.VMEM((tb, SGU_W), F32)],
        compiler_params=_params("arbitrary"),
    )(dyb, z, z, z, z, z, z, lng, lnb, ws, wst, bsx)


def _window_sums(ext, lookahead):
    n = ext.shape[0]
    out = []
    for gi, w in enumerate(POOL_WINDOWS):
        acc = ext[:, 128 * gi:128 * (gi + 1)]
        k = 1
        while k < w:
            acc = acc + pltpu.roll(acc, (n - k) if lookahead else k, 0)
            k *= 2
        out.append(acc)
    return jnp.concatenate(out, axis=1)


def _pool_counts(row0, tb):
    pos = (row0 + 1 + lax.broadcasted_iota(jnp.int32, (tb, POOL_W), 0)).astype(F32)
    lane = lax.broadcasted_iota(jnp.int32, (tb, POOL_W), 1)
    win = jnp.where(lane < 128, 2.0, jnp.where(lane < 256, 4.0, jnp.where(lane < 384, 8.0, 16.0)))
    return jnp.minimum(pos, win)


def _pool_fwd(z, wpool, scale, tb=256):
    t = z.shape[0]

    def body(xc_ref, gc_ref, wp_ref, sc_ref, yc_ref, halo):
        i = pl.program_id(0)

        @pl.when(i == 0)
        def _():
            halo[...] = jnp.zeros_like(halo)

        xc = xc_ref[...]
        sums = _window_sums(jnp.concatenate([halo[...], xc], axis=0), lookahead=False)[HALO:, :]
        halo[...] = xc[tb - HALO:, :]
        pb = (sums / _pool_counts(i * tb, tb) - xc).astype(BF16)
        q = jnp.concatenate([_dot(pb[:, 128 * gi:128 * (gi + 1)], wp_ref[gi]) for gi in range(4)], axis=1)
        silu_gc, _ = _silu_and_grad(gc_ref[...])
        yc_ref[...] = (q * sc_ref[...] * silu_gc).astype(BF16)

    const = lambda shape: pl.BlockSpec(shape, lambda i: (0,) * len(shape))
    return pl.pallas_call(
        body,
        name="pool_fwd",
        grid=(t // tb,),
        in_specs=[pl.BlockSpec((tb, CB), lambda i: (i, CB_XC)), pl.BlockSpec((tb, CB), lambda i: (i, CB_GC)),
                  const((4, 128, 128)), const((1, POOL_W))],
        out_specs=pl.BlockSpec((tb, POOL_W), lambda i: (i, 0)),
        out_shape=jax.ShapeDtypeStruct((t, POOL_W), BF16),
        scratch_shapes=[pltpu.VMEM((HALO, POOL_W), F32)],
        compiler_params=_params("arbitrary"),
    )(z, z, wpool, scale)


def _pool_bwd(dyc, z, wpool, scale, tb=256):
    t = z.shape[0]
    nb = t // tb
    rev = lambda i: nb - 1 - i
    per_halo = tb // HALO

    def body(dyc_ref, xc_ref, xp_ref, gc_ref, wp_ref, sc_ref, dz_ref, dwp_ref, dsc_ref, ehalo):
        i = pl.program_id(0)

        @pl.when(i == 0)
        def _():
            ehalo[...] = jnp.zeros_like(ehalo)
            dwp_ref[...] = jnp.zeros_like(dwp_ref)
            dsc_ref[...] = jnp.zeros_like(dsc_ref)

        xc = xc_ref[...]
        prev = jnp.where(i < nb - 1, xp_ref[...], 0.0)
        sums = _window_sums(jnp.concatenate([prev, xc], axis=0), lookahead=False)[HALO:, :]
        cnt = _pool_counts(rev(i) * tb, tb)
        pb = (sums / cnt - xc).astype(BF16)
        q = jnp.concatenate([_dot(pb[:, 128 * gi:128 * (gi + 1)], wp_ref[gi]) for gi in range(4)], axis=1)
        silu_gc, silu_gc_grad = _silu_and_grad(gc_ref[...])
        dyc = dyc_ref[...]
        dz_ref[:, POOL_W:] = (dyc * (q * sc_ref[...]) * silu_gc_grad).astype(BF16)
        dyc0 = dyc * silu_gc
        dsc_ref[...] += _rowsum(dyc0 * q)
        dqb = (dyc0 * sc_ref[...]).astype(BF16)
        dp_cols = []
        for gi in range(4):
            cols = slice(128 * gi, 128 * (gi + 1))
            dwp_ref[gi] += _dot(pb[:, cols], dqb[:, cols], TN)
            dp_cols.append(_dot(dqb[:, cols], wp_ref[gi], NT))
        dp = jnp.concatenate(dp_cols, axis=1)
        e = dp / cnt
        fut = _window_sums(jnp.concatenate([e, ehalo[...]], axis=0), lookahead=True)[:tb, :]
        ehalo[...] = e[:HALO, :]
        dz_ref[:, 0:POOL_W] = (fut - dp).astype(BF16)

    const = lambda shape: pl.BlockSpec(shape, lambda i: (0,) * len(shape))
    return pl.pallas_call(
        body,
        name="pool_bwd",
        grid=(nb,),
        in_specs=[pl.BlockSpec((tb, POOL_W), lambda i: (rev(i), 0)),
                  pl.BlockSpec((tb, CB), lambda i: (rev(i), CB_XC)),
                  pl.BlockSpec((HALO, CB), lambda i: (jnp.maximum(rev(i) * per_halo - 1, 0), CB_XC)),
                  pl.BlockSpec((tb, CB), lambda i: (rev(i), CB_GC)),
                  const((4, 128, 128)), const((1, POOL_W))],
        out_specs=[pl.BlockSpec((tb, 2 * POOL_W), lambda i: (rev(i), 0)), const((4, 128, 128)), const((1, POOL_W))],
        out_shape=[jax.ShapeDtypeStruct((t, 2 * POOL_W), BF16),
                   jax.ShapeDtypeStruct((4, 128, 128), F32), jax.ShapeDtypeStruct((1, POOL_W), F32)],
        scratch_shapes=[pltpu.VMEM((HALO, POOL_W), F32)],
        compiler_params=_params("arbitrary"),
    )(dyc, z, z, z, wpool, scale)


def _adamw(w, m, v, parts, tr, name):
    r, c = w.shape
    n_slab = len(parts)
    per_slab = r // n_slab // tr

    def body(w_ref, m_ref, v_ref, *refs):
        p_refs, (g_ref, d_ref, nm_ref, nv_ref) = refs[:n_slab], refs[n_slab:]
        for s, p_ref in enumerate(p_refs):
            @pl.when(pl.program_id(0) // per_slab == s)
            def _(p_ref=p_ref):
                g = p_ref[0].astype(F32)
                for k in range(1, N_DEV):
                    g = g + p_ref[k].astype(F32)
                g_ref[...] = g

        g = g_ref[...]
        nm = ADAM_B1 * m_ref[...] + (1.0 - ADAM_B1) * g
        nv = ADAM_B2 * v_ref[...] + (1.0 - ADAM_B2) * (g * g)
        m_hat = nm / (1.0 - ADAM_B1 ** ADAM_STEP)
        v_hat = nv / (1.0 - ADAM_B2 ** ADAM_STEP)
        nm_ref[...] = nm
        nv_ref[...] = nv
        d_ref[...] = -ADAM_LR * (m_hat / (jnp.sqrt(v_hat) + ADAM_EPS) + ADAM_WD * w_ref[...])

    blk = pl.BlockSpec((tr, c), lambda i: (i, 0))
    slab = lambda s: pl.BlockSpec((N_DEV, tr, c), lambda i, s=s: (0, jnp.clip(i - s * per_slab, 0, per_slab - 1), 0))
    return pl.pallas_call(
        body,
        name=name,
        grid=(r // tr,),
        in_specs=[blk, blk, blk] + [slab(s) for s in range(n_slab)],
        out_specs=[blk, blk, blk, blk],
        out_shape=[jax.ShapeDtypeStruct((r, c), F32)] * 4,
        compiler_params=_params("arbitrary"),
    )(w, m, v, *parts)


MESH = pl.DeviceIdType.MESH
ANY = pl.BlockSpec(memory_space=pl.ANY)


def _dev_index(dev):
    return 4 * dev[0] + 2 * dev[1] + dev[2]


WHOLE_SHAPES = ((D_MODEL, IN_COLS), (D_MODEL, D_MODEL), (S5_W, S5_W))
SHARD_SHAPES = ((D_MODEL, SH_IN), (SH_OUT, D_MODEL), (SH_GLU, S5_W))


def _shard_of(ref, ti, idx):
    if ti == 0:
        return ref.at[:, pl.ds(pl.multiple_of(idx * SH_IN, 128), SH_IN)]
    rows = SHARD_SHAPES[ti][0]
    return ref.at[pl.ds(pl.multiple_of(idx * rows, rows), rows), :]


def _peer(mask, x, y, c):
    return (1 - x if mask & 4 else x, 1 - y if mask & 2 else y, 1 - c if mask & 1 else c)


def _allgather_weights(wi, wo, wg):
    n_t = 3

    def body(wi_ref, wo_ref, wg_ref, gi_ref, go_ref, gg_ref, send_sems, recv_sems, local_sems):
        x, y, c = lax.axis_index("x"), lax.axis_index("y"), lax.axis_index("c")
        me, sibling = (x, y, c), (x, y, 1 - c)
        chips = [(1 - x, y), (x, 1 - y), (1 - x, 1 - y)]
        shards = (wi_ref, wo_ref, wg_ref)
        wholes = (gi_ref, go_ref, gg_ref)

        def slot(ti, dev):
            return _shard_of(wholes[ti], ti, _dev_index(dev))

        def copy(k, ti, block, to, own=False):
            return pltpu.make_async_remote_copy(
                src_ref=shards[ti] if own else slot(ti, block), dst_ref=slot(ti, block),
                send_sem=send_sems.at[n_t * k + ti], recv_sem=recv_sems.at[n_t * k + ti],
                device_id=to, device_id_type=MESH)

        mine = [pltpu.make_async_copy(shards[ti], slot(ti, me), local_sems.at[ti]) for ti in range(n_t)]
        for cp in mine:
            cp.start()
        first = [copy(0, ti, me, sibling, own=True) for ti in range(n_t)]
        first += [copy(1 + j, ti, me, (*chip, c), own=True) for j, chip in enumerate(chips) for ti in range(n_t)]
        for cp in first:
            cp.start()
        passed = []
        for j, chip in enumerate(chips):
            for ti in range(n_t):
                copy(1 + j, ti, (*chip, c), me).wait_recv()
            onward = [copy(4 + j, ti, (*chip, c), sibling) for ti in range(n_t)]
            for cp in onward:
                cp.start()
            passed += onward
        for ti in range(n_t):
            copy(0, ti, sibling, me).wait_recv()
        for j, chip in enumerate(chips):
            for ti in range(n_t):
                copy(4 + j, ti, (*chip, 1 - c), me).wait_recv()
        for cp in first + passed:
            cp.wait_send()
        for cp in mine:
            cp.wait()

    return pl.pallas_call(
        body,
        name="allgather_weights",
        in_specs=[ANY, ANY, ANY],
        out_specs=[ANY, ANY, ANY],
        out_shape=[jax.ShapeDtypeStruct(s, BF16) for s in WHOLE_SHAPES],
        scratch_shapes=[pltpu.SemaphoreType.DMA((7 * n_t,)), pltpu.SemaphoreType.DMA((7 * n_t,)),
                        pltpu.SemaphoreType.DMA((n_t,))],
    )(wi, wo, wg)


HBM = pl.BlockSpec(memory_space=pltpu.HBM)
SEM = pl.BlockSpec(memory_space=pltpu.SEMAPHORE)
GATHER, SCATTER, SHARE = "gather", "scatter", "share"


def _split_route(kind, ti, sending, me_idx, p_idx, src_ref, land_ref):
    owner = me_idx if sending else p_idx
    if kind == GATHER:
        return src_ref, _shard_of(land_ref, ti, owner)
    if kind == SCATTER:
        return _shard_of(src_ref, ti, p_idx), land_ref.at[owner]
    return src_ref, land_ref.at[owner]


def _split_start(name, srcs, lands, kinds, after=None):
    n = len(srcs)
    arrays = list(srcs) + list(lands) + ([] if after is None else [after])

    def body(*refs):
        src_refs, land_refs = refs[0:n], refs[n:2 * n]
        send_sems, recv_sems, token = refs[len(arrays)], refs[len(arrays) + 1], refs[-1]
        x, y, c = lax.axis_index("x"), lax.axis_index("y"), lax.axis_index("c")
        me_idx = _dev_index((x, y, c))
        for mask in range(1, N_DEV):
            p = _peer(mask, x, y, c)
            for i, (kind, ti) in enumerate(kinds):
                k = (mask - 1) * n + i
                src, dst = _split_route(kind, ti, True, me_idx, _dev_index(p), src_refs[i], land_refs[i])
                pltpu.make_async_remote_copy(src_ref=src, dst_ref=dst, send_sem=send_sems.at[k], recv_sem=recv_sems.at[k],
                                             device_id=p, device_id_type=MESH).start()
        token[...] = jnp.zeros_like(token)

    n_copies = (N_DEV - 1) * n
    return pl.pallas_call(
        body,
        name=name,
        in_specs=[HBM] * len(arrays),
        out_specs=(SEM, SEM) + (HBM,) * (2 * n) + (pl.BlockSpec(memory_space=pltpu.VMEM),),
        out_shape=(pltpu.SemaphoreType.DMA((n_copies,)), pltpu.SemaphoreType.DMA((n_copies,)))
        + tuple(pltpu.HBM(a.shape, a.dtype) for a in arrays[:2 * n]) + (jax.ShapeDtypeStruct((8, 128), F32),),
        input_output_aliases={i: 2 + i for i in range(2 * n)},
        compiler_params=pltpu.CompilerParams(has_side_effects=pltpu.SideEffectType.DATAFLOW_SIDE_EFFECTING),
    )(*[pltpu.with_memory_space_constraint(a, pltpu.HBM) for a in arrays])


def _split_wait(name, started, kinds, after):
    n = len(kinds)
    send_sems, recv_sems, thru = started[0], started[1], started[2:2 + 2 * n]

    def body(*refs):
        src_refs, land_refs = refs[0:n], refs[n:2 * n]
        send_sems, recv_sems = refs[2 * n], refs[2 * n + 1]
        x, y, c = lax.axis_index("x"), lax.axis_index("y"), lax.axis_index("c")
        me_idx = _dev_index((x, y, c))
        for mask in range(1, N_DEV):
            p = _peer(mask, x, y, c)
            for i, (kind, ti) in enumerate(kinds):
                k = (mask - 1) * n + i
                src, dst = _split_route(kind, ti, False, me_idx, _dev_index(p), src_refs[i], land_refs[i])
                cp = pltpu.make_async_remote_copy(src_ref=src, dst_ref=dst, send_sem=send_sems.at[k],
                                                  recv_sem=recv_sems.at[k], device_id=p, device_id_type=MESH)
                cp.wait_send()
                cp.wait_recv()

    res = pl.pallas_call(
        body,
        name=name,
        in_specs=[HBM] * (2 * n) + [SEM, SEM, pl.BlockSpec(memory_space=pl.ANY)],
        out_specs=(HBM,) * (2 * n),
        out_shape=tuple(pltpu.HBM(a.shape, a.dtype) for a in thru),
        input_output_aliases={i: i for i in range(2 * n)},
        compiler_params=pltpu.CompilerParams(has_side_effects=pltpu.SideEffectType.DATAFLOW_SIDE_EFFECTING),
    )(*thru, send_sems, recv_sems, after)
    return res[n:2 * n]


def _landing_zones(name, srcs, kinds):
    def shape(src, kind, ti):
        if kind == GATHER:
            return WHOLE_SHAPES[ti]
        return (N_DEV,) + (SHARD_SHAPES[ti] if kind == SCATTER else src.shape)

    n = len(srcs)

    def body(*refs):
        src_refs, land_refs, sems = refs[0:n], refs[n:2 * n], refs[2 * n]
        me_idx = _own_index()
        copies = []
        for i, (kind, ti) in enumerate(kinds):
            src, dst = _split_route(kind, ti, True, me_idx, me_idx, src_refs[i], land_refs[i])
            copies.append(pltpu.make_async_copy(src, dst, sems.at[i]))
        for cp in copies:
            cp.start()
        for cp in copies:
            cp.wait()

    return pl.pallas_call(
        body,
        name=name,
        in_specs=[ANY] * n,
        out_specs=[ANY] * n,
        out_shape=[jax.ShapeDtypeStruct(shape(s, *k), s.dtype) for s, k in zip(srcs, kinds)],
        scratch_shapes=[pltpu.SemaphoreType.DMA((n,))],
    )(*srcs)


def _share_small(buf):
    def body(b_ref, o_ref, send_sems, recv_sems, local_sem):
        x, y, c = lax.axis_index("x"), lax.axis_index("y"), lax.axis_index("c")
        me_idx = _dev_index((x, y, c))
        local = pltpu.make_async_copy(b_ref, o_ref.at[me_idx], local_sem)
        local.start()

        def copy(mask, owner):
            return pltpu.make_async_remote_copy(
                src_ref=b_ref, dst_ref=o_ref.at[owner], send_sem=send_sems.at[mask - 1], recv_sem=recv_sems.at[mask - 1],
                device_id=_peer(mask, x, y, c), device_id_type=MESH)

        sends = [copy(mask, me_idx) for mask in range(1, N_DEV)]
        for cp in sends:
            cp.start()
        for mask in range(1, N_DEV):
            copy(mask, _dev_index(_peer(mask, x, y, c))).wait_recv()
        for cp in sends:
            cp.wait_send()
        local.wait()

    return pl.pallas_call(
        body,
        name="share_small",
        in_specs=[ANY],
        out_specs=ANY,
        out_shape=jax.ShapeDtypeStruct((N_DEV,) + buf.shape, buf.dtype),
        scratch_shapes=[pltpu.SemaphoreType.DMA((N_DEV - 1,)), pltpu.SemaphoreType.DMA((N_DEV - 1,)),
                        pltpu.SemaphoreType.DMA],
    )(buf)


def _s5_prep(lam_re, lam_im, b_re, b_im, c_re, c_im, d_skip, log_dt):
    dt = jnp.exp(log_dt)[:, None]
    mag = jnp.exp(lam_re * dt)
    a_re, a_im = mag * jnp.cos(lam_im * dt), mag * jnp.sin(lam_im * dt)
    den = lam_re * lam_re + lam_im * lam_im
    f_re = ((a_re - 1.0) * lam_re + a_im * lam_im) / den
    f_im = (a_im * lam_re - (a_re - 1.0) * lam_im) / den
    bb_re = f_re[..., None] * b_re - f_im[..., None] * b_im
    bb_im = f_re[..., None] * b_im + f_im[..., None] * b_re
    eye = jnp.eye(8, dtype=F32)

    def in_map(bb):
        return jnp.einsum("jgph,gk->ghjkp", bb.reshape(4, 8, S5_STATE, S5_CH), eye).reshape(128, N_STATE)

    def out_map(cm):
        return jnp.einsum("jghp,gk->ghjkp", cm.reshape(4, 8, S5_CH, S5_STATE), eye).reshape(128, N_STATE)

    a = jnp.concatenate([a_re.reshape(N_PAIR, 128), a_im.reshape(N_PAIR, 128)])
    bc = jnp.concatenate([in_map(bb_re), in_map(bb_im)], axis=1)
    cct = jnp.concatenate([out_map(c_re), -out_map(c_im)], axis=1)
    return a, bc, cct, d_skip.reshape(1, S5_W)


WEIGHTS = ["norm_g", "w_in", "lam_re", "lam_im", "b_re", "b_im", "c_re", "c_im", "d_skip", "log_dt", "w_glu", "b_glu",
           "ln_g", "ln_b", "w_s", "b_s", "w_pool", "pool_scale", "w_out", "final_g"]
SHARDED = ("w_in", "w_glu", "w_out")
SMALL = [n for n in WEIGHTS if n not in SHARDED]
INPUTS = ["x"] + WEIGHTS + ["loss_target"] + ["m_" + n for n in WEIGHTS] + ["v_" + n for n in WEIGHTS]
SMALL_B = ["norm_g", "final_g"]
SMALL_A = [n for n in SMALL if n not in SMALL_B]
SMALL_TILE = 16 * 128
LOSS_AT = (DEPTH + 1) * D_MODEL


def _pack_small(arrays, dtype):
    flat = jnp.concatenate([a.reshape(-1) for a in arrays])
    pad = (-flat.shape[0]) % SMALL_TILE
    return jnp.pad(flat, (0, pad)).astype(dtype).reshape(-1, 128)


def _unpack_small(packed, like):
    flat = packed.reshape(-1)
    out, off = [], 0
    for a in like:
        out.append(flat[off:off + a.size].reshape(a.shape))
        off += a.size
    return out


def _layer_fwd(p, l, x, wi, wo, wg, token=None):
    row = lambda v: v.reshape(1, -1)
    causal = jnp.tril(jnp.ones((CHUNK, CHUNK), dtype=bool))
    (a, bc, cct, dvec), prep_vjp = jax.vjp(
        _s5_prep, p["lam_re"][l], p["lam_im"][l], p["b_re"][l], p["b_im"][l], p["c_re"][l], p["c_im"][l],
        p["d_skip"][l], p["log_dt"][l])
    ws_f32 = jnp.where(causal[None], p["w_s"][l], 0.0)
    pm, pt = _step_major(S5_TB)
    c = dict(
        x=x, wi=wi, wo=wo, wg=wg, a=a, bc=bc.astype(BF16), cc=cct.T.astype(BF16), dvec=dvec, prep_vjp=prep_vjp,
        pm=pm, pt=pt,
        ws=ws_f32.astype(BF16), wst=jnp.swapaxes(ws_f32, 1, 2).astype(BF16),
        bsx=jnp.broadcast_to(p["b_s"][l][:, :, None], (SGU_HEADS, CHUNK, 128)),
        wpool=p["w_pool"][l].astype(BF16), scale=row(p["pool_scale"][l]),
        lng=row(p["ln_g"][l]), lnb=row(p["ln_b"][l]), bglu=row(p["b_glu"][l]), norm_g=row(p["norm_g"][l]))
    c["z"], c["h"] = _rms_inproj(x, c["norm_g"], wi, token)
    c["ya"], c["s"], c["ys"] = _s5_fwd(c["z"], c["bc"], c["cc"], a, pm, pt, dvec, wg, c["bglu"], S5_TB)
    c["yb"] = _sgu_fwd(c["z"], c["lng"], c["lnb"], c["ws"], c["bsx"])
    c["yc"] = _pool_fwd(c["z"], c["wpool"], c["scale"])
    return _outproj(x, c["ya"], c["yb"], c["yc"], wo), c


def _mixers_bwd(c, dx, token=None):
    dya, dyb, dyc = _outproj_bwd(dx, c["wo"], token)
    dwo = _wgrad_out(c["ya"], c["yb"], c["yc"], dx, token)
    dz_s5, dbc, dcct, da, dd, dwg, dbglu = _s5_bwd(
        dya, c["ys"], c["z"], c["s"], c["bc"], c["cc"], c["a"], c["pm"], c["pt"], c["dvec"], c["wg"], c["bglu"], S5_TB)
    dz_sgu, dlng, dlnb, dws, dbsx = _sgu_bwd(dyb, c["z"], c["lng"], c["lnb"], c["ws"], c["wst"], c["bsx"])
    dz_pool, dwp, dsc = _pool_bwd(dyc, c["z"], c["wpool"], c["scale"])
    g_lam_re, g_lam_im, g_b_re, g_b_im, g_c_re, g_c_im, g_d, g_dt = c["prep_vjp"]((da, dbc, dcct, dd))
    small = dict(lam_re=g_lam_re, lam_im=g_lam_im, b_re=g_b_re, b_im=g_b_im, c_re=g_c_re,
                 c_im=g_c_im, d_skip=g_d, log_dt=g_dt, b_glu=dbglu.reshape(-1), ln_g=dlng.reshape(-1),
                 ln_b=dlnb.reshape(-1), w_s=dws, b_s=jnp.sum(dbsx, axis=-1), w_pool=dwp, pool_scale=dsc.reshape(-1))
    return (dz_s5, dz_sgu, dz_pool), dwo, dwg, small


def _inproj_grads(c, dz, dx, token_w=None, token_x=None):
    dwi = _wgrad_in(c["h"], *dz, token_w)
    dx, dnorm = _inproj_bwd(*dz, c["wi"], c["x"], c["norm_g"], dx, token_x)
    return dwi, dx, dnorm.reshape(-1)


def _own_index():
    return _dev_index((lax.axis_index("x"), lax.axis_index("y"), lax.axis_index("c")))


def kernel(x, norm_g, w_in, lam_re, lam_im, b_re, b_im, c_re, c_im, d_skip, log_dt, w_glu, b_glu, ln_g, ln_b, w_s, b_s, w_pool, pool_scale, w_out, final_g, loss_target, m_norm_g, m_w_in, m_lam_re, m_lam_im, m_b_re, m_b_im, m_c_re, m_c_im, m_d_skip, m_log_dt, m_w_glu, m_b_glu, m_ln_g, m_ln_b, m_w_s, m_b_s, m_w_pool, m_pool_scale, m_w_out, m_final_g, v_norm_g, v_w_in, v_lam_re, v_lam_im, v_b_re, v_b_im, v_c_re, v_c_im, v_d_skip, v_log_dt, v_w_glu, v_b_glu, v_ln_g, v_ln_b, v_w_s, v_b_s, v_w_pool, v_pool_scale, v_w_out, v_final_g):
    p = dict(zip(INPUTS, (x, norm_g, w_in, lam_re, lam_im, b_re, b_im, c_re, c_im, d_skip, log_dt, w_glu, b_glu, ln_g, ln_b, w_s, b_s, w_pool, pool_scale, w_out, final_g, loss_target, m_norm_g, m_w_in, m_lam_re, m_lam_im, m_b_re, m_b_im, m_c_re, m_c_im, m_d_skip, m_log_dt, m_w_glu, m_b_glu, m_ln_g, m_ln_b, m_w_s, m_b_s, m_w_pool, m_pool_scale, m_w_out, m_final_g, v_norm_g, v_w_in, v_lam_re, v_lam_im, v_b_re, v_b_im, v_c_re, v_c_im, v_d_skip, v_log_dt, v_w_glu, v_b_glu, v_ln_g, v_ln_b, v_w_s, v_b_s, v_w_pool, v_pool_scale, v_w_out, v_final_g)))

    shards = [[w[l].astype(BF16) for w in (w_in, w_out, w_glu)] for l in range(DEPTH)]
    gather3 = [(GATHER, ti) for ti in range(3)]
    scatter3 = [(SCATTER, ti) for ti in range(3)]

    whole0 = _allgather_weights(*shards[0])
    lands = _landing_zones("gather1_lands", shards[1], gather3)
    gather1 = _split_start("gather1_start", shards[1], lands, gather3, after=whole0[2])
    x1, c0 = _layer_fwd(p, 0, x[0], *whole0, token=gather1[-1])
    whole1 = _split_wait("gather1_wait", gather1, gather3, x1)
    x2, c1 = _layer_fwd(p, 1, x1, *whole1)

    dx, loss_tile, dfinal = _final_loss(x2, final_g.reshape(1, -1), loss_target[0])

    dz1, dwo1, dwg1, small1 = _mixers_bwd(c1, dx)
    dwi1, dx, dnorm1 = _inproj_grads(c1, dz1, dx)
    partials1 = [dwi1, dwo1, dwg1.astype(BF16)]
    grads1 = _split_start("grads1_start", partials1, _landing_zones("grads1_lands", partials1, scatter3), scatter3)
    dz0, dwo0, dwg0, small0 = _mixers_bwd(c0, dx, token=grads1[-1])
    small_a = _pack_small([jnp.stack([small0[n], small1[n]]) for n in SMALL_A], BF16)
    srcs_a, kinds_a = [dwo0, dwg0.astype(BF16), small_a], [(SCATTER, 1), (SCATTER, 2), (SHARE, None)]
    grads0a = _split_start("grads0a_start", srcs_a, _landing_zones("grads0a_lands", srcs_a, kinds_a), kinds_a)
    dwi0 = _wgrad_in(c0["h"], *dz0, grads0a[-1])
    kinds_b = [(SCATTER, 0)]
    grads0b = _split_start("grads0b_start", [dwi0], _landing_zones("grads0b_lands", [dwi0], kinds_b), kinds_b)
    dx, dnorm0 = _inproj_bwd(*dz0, c0["wi"], c0["x"], c0["norm_g"], dx, grads0b[-1])
    parts1 = _split_wait("grads1_wait", grads1, scatter3, dx)
    r_out0, r_glu0, r_a = _split_wait("grads0a_wait", grads0a, kinds_a, dx)
    (r_in0,) = _split_wait("grads0b_wait", grads0b, kinds_b, dx)
    parts0 = [r_in0, r_out0, r_glu0]
    r_b = _share_small(_pack_small(
        [jnp.stack([dnorm0.reshape(-1), dnorm1]), dfinal.reshape(-1), loss_tile[0, 0:1]], F32))
    loss = jnp.sum(r_b.reshape(N_DEV, -1)[:, LOSS_AT])

    out = {}

    def adam(name, ti, tr):
        shape2d = (DEPTH * SHARD_SHAPES[ti][0], SHARD_SHAPES[ti][1])
        res = _adamw(p[name].reshape(shape2d), p["m_" + name].reshape(shape2d), p["v_" + name].reshape(shape2d),
                     [parts0[ti], parts1[ti]], tr, "adamw_" + name)
        out[name] = [r.reshape(p[name].shape) for r in res]

    adam("w_in", 0, 256)
    adam("w_out", 1, 128)
    adam("w_glu", 2, 64)
    for names, parts, steps, tag in ((SMALL_A, r_a, 7, "a"), (SMALL_B, r_b, 1, "b")):
        like = [p[n] for n in names]
        pad = [jnp.zeros((1,), F32)] if tag == "b" else []
        packed = _adamw(*[_pack_small([p[pre + n] for n in names] + pad, F32) for pre in ("", "m_", "v_")],
                        [parts], parts.shape[1] // steps, "adamw_small_" + tag)
        for n, vals in zip(names, zip(*[_unpack_small(r, like) for r in packed])):
            out[n] = list(vals)

    return (loss, dx[None], *[out[n][0] for n in WEIGHTS], *[out[n][1] for n in WEIGHTS],
            *[out[n][2] for n in WEIGHTS], *[out[n][3] for n in WEIGHTS])
```

```python
import functools
import math

import jax
import jax.numpy as jnp
import numpy as np
from jax import lax
from jax.experimental import pallas as pl
from jax.experimental.pallas import tpu as pltpu

F32 = jnp.float32
BF16 = jnp.bfloat16

D_MODEL = 2048
DEPTH = 2
S5_W, SGU_W, POOL_W = 512, 1024, 512
S5_GROUPS, S5_STATE, S5_CH = 32, 64, 16
N_STATE = S5_GROUPS * S5_STATE
CHUNK = 128
SGU_HEADS = 8
POOL_WINDOWS = (2, 4, 8, 16)
IN_COLS = 5120
RMS_EPS = 1e-6
LN_EPS = 1e-5
ADAM_LR, ADAM_B1, ADAM_B2, ADAM_EPS, ADAM_WD, ADAM_STEP = 0.001, 0.9, 0.999, 1e-08, 0.01, 10

CB = 512
N_CB = IN_COLS // CB
CB_XA, CB_U, CB_V, CB_XC, CB_GA, CB_GB, CB_GC = 0, 1, 3, 5, 6, 7, 9

N_DEV = 8
SH_IN = IN_COLS // N_DEV
SH_OUT = D_MODEL // N_DEV
SH_GLU = S5_W // N_DEV

VMEM_LIMIT = 52 * 1024 * 1024
HALO = 16
LANE_CH = 512

TN = (((0,), (0,)), ((), ()))
NT = (((1,), (1,)), ((), ()))


def _params(*sem):
    return pltpu.CompilerParams(dimension_semantics=sem if sem else None, vmem_limit_bytes=VMEM_LIMIT)


def _dot(a, b, dims=None):
    if dims is None:
        return jnp.dot(a, b, preferred_element_type=F32)
    return lax.dot_general(a, b, dims, preferred_element_type=F32)


_GELU_C = math.sqrt(2.0 / math.pi)


def _gelu(x):
    return 0.5 * x * (1.0 + jnp.tanh(_GELU_C * (x + 0.044715 * x * x * x)))


def _gelu_and_grad(x):
    t = jnp.tanh(_GELU_C * (x + 0.044715 * x * x * x))
    half = 0.5 * (1.0 + t)
    return x * half, half + 0.5 * x * (1.0 - t * t) * _GELU_C * (1.0 + 3.0 * 0.044715 * x * x)


def _gelu_grad(x):
    return _gelu_and_grad(x)[1]


def _silu_and_grad(x):
    s = jax.nn.sigmoid(x)
    return x * s, s * (1.0 + x * (1.0 - s))


def _rowsum(x):
    return jnp.sum(x, axis=0, keepdims=True)


def _after(token):
    if token is None:
        return [], []
    return [pl.BlockSpec((8, 128), lambda *_: (0, 0))], [token]


def _rms_inproj(x, g, w, token=None, tm=1024, tn=1024):
    t = x.shape[0]
    tm = min(tm, t)
    after_specs, after = _after(token)

    def body(x_ref, g_ref, w_ref, *rest):
        z_ref, h_ref = rest[-2:]

        @pl.when(pl.program_id(1) == 0)
        def _():
            xv = x_ref[...]
            r = lax.rsqrt(jnp.mean(xv * xv, axis=-1, keepdims=True) + RMS_EPS)
            h_ref[...] = (xv * r * g_ref[...]).astype(BF16)

        z_ref[...] = _dot(h_ref[...], w_ref[...])

    return pl.pallas_call(
        body,
        name="rms_inproj",
        grid=(t // tm, IN_COLS // tn),
        in_specs=[
            pl.BlockSpec((tm, D_MODEL), lambda m, n: (m, 0)),
            pl.BlockSpec((1, D_MODEL), lambda m, n: (0, 0)),
            pl.BlockSpec((D_MODEL, tn), lambda m, n: (0, n)),
        ] + after_specs,
        out_specs=[
            pl.BlockSpec((tm, tn), lambda m, n: (m, n)),
            pl.BlockSpec((tm, D_MODEL), lambda m, n: (m, 0)),
        ],
        out_shape=[jax.ShapeDtypeStruct((t, IN_COLS), F32), jax.ShapeDtypeStruct((t, D_MODEL), BF16)],
        compiler_params=_params("arbitrary", "arbitrary"),
    )(x, g, w, *after)


def _outproj(x, ya, yb, yc, w, tm=1024, tn=1024):
    t = x.shape[0]
    tm = min(tm, t)

    def body(x_ref, ya_ref, yb_ref, yc_ref, w_ref, o_ref):
        acc = _dot(ya_ref[...], w_ref[0:S5_W, :])
        acc += _dot(yb_ref[...], w_ref[S5_W:S5_W + SGU_W, :])
        acc += _dot(yc_ref[...], w_ref[S5_W + SGU_W:, :])
        o_ref[...] = x_ref[...] + acc

    return pl.pallas_call(
        body,
        name="outproj",
        grid=(t // tm, D_MODEL // tn),
        in_specs=[
            pl.BlockSpec((tm, tn), lambda m, n: (m, n)),
            pl.BlockSpec((tm, S5_W), lambda m, n: (m, 0)),
            pl.BlockSpec((tm, SGU_W), lambda m, n: (m, 0)),
            pl.BlockSpec((tm, POOL_W), lambda m, n: (m, 0)),
            pl.BlockSpec((D_MODEL, tn), lambda m, n: (0, n)),
        ],
        out_specs=pl.BlockSpec((tm, tn), lambda m, n: (m, n)),
        out_shape=jax.ShapeDtypeStruct((t, D_MODEL), F32),
        compiler_params=_params("arbitrary", "arbitrary"),
    )(x, ya, yb, yc, w)


def _outproj_bwd(dx, w, token=None, tm=1024):
    t = dx.shape[0]
    tm = min(tm, t)
    after_specs, after = _after(token)

    def body(dx_ref, w_ref, *rest):
        dya_ref, dyb_ref, dyc_ref = rest[-3:]
        dy = _dot(dx_ref[...].astype(BF16), w_ref[...], NT)
        dya_ref[...] = dy[:, 0:S5_W]
        dyb_ref[...] = dy[:, S5_W:S5_W + SGU_W]
        dyc_ref[...] = dy[:, S5_W + SGU_W:]

    return pl.pallas_call(
        body,
        name="outproj_bwd",
        grid=(t // tm,),
        in_specs=[
            pl.BlockSpec((tm, D_MODEL), lambda m: (m, 0)),
            pl.BlockSpec((D_MODEL, D_MODEL), lambda m: (0, 0)),
        ] + after_specs,
        out_specs=[
            pl.BlockSpec((tm, S5_W), lambda m: (m, 0)),
            pl.BlockSpec((tm, SGU_W), lambda m: (m, 0)),
            pl.BlockSpec((tm, POOL_W), lambda m: (m, 0)),
        ],
        out_shape=[
            jax.ShapeDtypeStruct((t, S5_W), F32),
            jax.ShapeDtypeStruct((t, SGU_W), F32),
            jax.ShapeDtypeStruct((t, POOL_W), F32),
        ],
        compiler_params=_params("arbitrary"),
    )(dx, w, *after)


def _dz_piece_maps():
    s5_map = lambda j: jnp.where(j >= CB_GA, 1, 0)
    sgu_map = lambda j: jnp.clip(jnp.where(j <= 4, j - 1, j - 3), 0, 5)
    pool_map = lambda j: jnp.where(j >= CB_GC, 1, 0)
    return s5_map, sgu_map, pool_map


def _pick_piece(j):
    is_s5 = jnp.logical_or(j == CB_XA, j == CB_GA)
    is_pool = jnp.logical_or(j == CB_XC, j == CB_GC)
    return is_s5, is_pool, jnp.logical_not(jnp.logical_or(is_s5, is_pool))


def _inproj_bwd(dz_s5, dz_sgu, dz_pool, w, x, g, dxo, token=None, tm=512):
    t = x.shape[0]
    s5_map, sgu_map, pool_map = _dz_piece_maps()
    after_specs, after = _after(token)

    def body(s5_ref, sgu_ref, pool_ref, w_ref, x_ref, g_ref, dxo_ref, *rest):
        dx_ref, dg_ref, acc = rest[-3:]
        m, j = pl.program_id(0), pl.program_id(1)

        @pl.when(jnp.logical_and(m == 0, j == 0))
        def _():
            dg_ref[...] = jnp.zeros_like(dg_ref)

        @pl.when(j == 0)
        def _():
            acc[...] = jnp.zeros_like(acc)

        is_s5, is_pool, is_sgu = _pick_piece(j)

        @pl.when(is_s5)
        def _():
            acc[...] += _dot(w_ref[...], s5_ref[...], NT)

        @pl.when(is_sgu)
        def _():
            acc[...] += _dot(w_ref[...], sgu_ref[...], NT)

        @pl.when(is_pool)
        def _():
            acc[...] += _dot(w_ref[...], pool_ref[...], NT)

        @pl.when(j == N_CB - 1)
        def _():
            xv = x_ref[...]
            r = lax.rsqrt(jnp.mean(xv * xv, axis=-1, keepdims=True) + RMS_EPS)
            n = xv * r
            dh = acc[...].T
            dg_ref[...] += _rowsum(dh * n)
            dn = dh * g_ref[...]
            dx_ref[...] = dxo_ref[...] + r * (dn - n * jnp.mean(dn * n, axis=-1, keepdims=True))

    return pl.pallas_call(
        body,
        name="inproj_bwd",
        grid=(t // tm, N_CB),
        in_specs=[
            pl.BlockSpec((tm, CB), lambda m, j: (m, s5_map(j))),
            pl.BlockSpec((tm, CB), lambda m, j: (m, sgu_map(j))),
            pl.BlockSpec((tm, CB), lambda m, j: (m, pool_map(j))),
            pl.BlockSpec((D_MODEL, CB), lambda m, j: (0, j)),
            pl.BlockSpec((tm, D_MODEL), lambda m, j: (m, 0)),
            pl.BlockSpec((1, D_MODEL), lambda m, j: (0, 0)),
            pl.BlockSpec((tm, D_MODEL), lambda m, j: (m, 0)),
        ] + after_specs,
        out_specs=[
            pl.BlockSpec((tm, D_MODEL), lambda m, j: (m, 0)),
            pl.BlockSpec((1, D_MODEL), lambda m, j: (0, 0)),
        ],
        out_shape=[jax.ShapeDtypeStruct((t, D_MODEL), F32), jax.ShapeDtypeStruct((1, D_MODEL), F32)],
        scratch_shapes=[pltpu.VMEM((D_MODEL, tm), F32)],
        compiler_params=_params("arbitrary", "arbitrary"),
    )(dz_s5, dz_sgu, dz_pool, w, x, g, dxo, *after)


def _wgrad_in(h, dz_s5, dz_sgu, dz_pool, token=None, tm=2048):
    t = h.shape[0]
    tm = min(tm, t)
    s5_map, sgu_map, pool_map = _dz_piece_maps()
    after_specs, after = _after(token)

    def body(h_ref, s5_ref, sgu_ref, pool_ref, *rest):
        o_ref, acc = rest[-2:]
        j, m = pl.program_id(0), pl.program_id(1)

        @pl.when(m == 0)
        def _():
            acc[...] = jnp.zeros_like(acc)

        is_s5, is_pool, is_sgu = _pick_piece(j)

        @pl.when(is_s5)
        def _():
            acc[...] += _dot(s5_ref[...], h_ref[...], TN)

        @pl.when(is_sgu)
        def _():
            acc[...] += _dot(sgu_ref[...], h_ref[...], TN)

        @pl.when(is_pool)
        def _():
            acc[...] += _dot(pool_ref[...], h_ref[...], TN)

        @pl.when(m == pl.num_programs(1) - 1)
        def _():
            o_ref[...] = acc[...].T.astype(BF16)

    return pl.pallas_call(
        body,
        name="wgrad_in",
        grid=(N_CB, t // tm),
        in_specs=[
            pl.BlockSpec((tm, D_MODEL), lambda j, m: (m, 0)),
            pl.BlockSpec((tm, CB), lambda j, m: (m, s5_map(j))),
            pl.BlockSpec((tm, CB), lambda j, m: (m, sgu_map(j))),
            pl.BlockSpec((tm, CB), lambda j, m: (m, pool_map(j))),
        ] + after_specs,
        out_specs=pl.BlockSpec((D_MODEL, CB), lambda j, m: (0, j)),
        out_shape=jax.ShapeDtypeStruct((D_MODEL, IN_COLS), BF16),
        scratch_shapes=[pltpu.VMEM((CB, D_MODEL), F32)],
        compiler_params=_params("arbitrary", "arbitrary"),
    )(h, dz_s5, dz_sgu, dz_pool, *after)


def _wgrad_out(ya, yb, yc, dx, token=None, tm=2048, tn=512):
    t = dx.shape[0]
    tm = min(tm, t)
    after_specs, after = _after(token)

    def body(ya_ref, yb_ref, yc_ref, dx_ref, *rest):
        o_ref, acc = rest[-2:]
        m = pl.program_id(1)

        @pl.when(m == 0)
        def _():
            acc[...] = jnp.zeros_like(acc)

        dxb = dx_ref[...].astype(BF16)
        acc[:, 0:S5_W] += _dot(dxb, ya_ref[...], TN)
        acc[:, S5_W:S5_W + SGU_W] += _dot(dxb, yb_ref[...], TN)
        acc[:, S5_W + SGU_W:] += _dot(dxb, yc_ref[...], TN)

        @pl.when(m == pl.num_programs(1) - 1)
        def _():
            o_ref[...] = acc[...].T.astype(BF16)

    return pl.pallas_call(
        body,
        name="wgrad_out",
        grid=(D_MODEL // tn, t // tm),
        in_specs=[
            pl.BlockSpec((tm, S5_W), lambda n, m: (m, 0)),
            pl.BlockSpec((tm, SGU_W), lambda n, m: (m, 0)),
            pl.BlockSpec((tm, POOL_W), lambda n, m: (m, 0)),
            pl.BlockSpec((tm, tn), lambda n, m: (m, n)),
        ] + after_specs,
        out_specs=pl.BlockSpec((D_MODEL, tn), lambda n, m: (0, n)),
        out_shape=jax.ShapeDtypeStruct((D_MODEL, D_MODEL), BF16),
        scratch_shapes=[pltpu.VMEM((tn, D_MODEL), F32)],
        compiler_params=_params("arbitrary", "arbitrary"),
    )(ya, yb, yc, dx, *after)


def _final_loss(x, g, target, tm=512):
    t = x.shape[0]

    def body(x_ref, g_ref, t_ref, dx_ref, loss_ref, dg_ref):
        @pl.when(pl.program_id(0) == 0)
        def _():
            loss_ref[...] = jnp.zeros_like(loss_ref)
            dg_ref[...] = jnp.zeros_like(dg_ref)

        xv = x_ref[...]
        gv = g_ref[...]
        r = lax.rsqrt(jnp.mean(xv * xv, axis=-1, keepdims=True) + RMS_EPS)
        n = xv * r
        err = n * gv - t_ref[...]
        loss_ref[...] += 0.5 * jnp.sum(jnp.mean(err * err, axis=-1, keepdims=True))
        dy = err * (1.0 / D_MODEL)
        dg_ref[...] += _rowsum(dy * n)
        dn = dy * gv
        dx_ref[...] = r * (dn - n * jnp.mean(dn * n, axis=-1, keepdims=True))

    return pl.pallas_call(
        body,
        name="final_loss",
        grid=(t // tm,),
        in_specs=[
            pl.BlockSpec((tm, D_MODEL), lambda m: (m, 0)),
            pl.BlockSpec((1, D_MODEL), lambda m: (0, 0)),
            pl.BlockSpec((tm, D_MODEL), lambda m: (m, 0)),
        ],
        out_specs=[
            pl.BlockSpec((tm, D_MODEL), lambda m: (m, 0)),
            pl.BlockSpec((8, 128), lambda m: (0, 0)),
            pl.BlockSpec((1, D_MODEL), lambda m: (0, 0)),
        ],
        out_shape=[
            jax.ShapeDtypeStruct((t, D_MODEL), F32),
            jax.ShapeDtypeStruct((8, 128), F32),
            jax.ShapeDtypeStruct((1, D_MODEL), F32),
        ],
        compiler_params=_params("arbitrary"),
    )(x, g, target)


N_Q = 2 * N_STATE // LANE_CH
N_LT = 2 * N_STATE // 128
N_PAIR = N_LT // 2
SEG = 8
PAIR_GROUP = 8
S5_TB = 512


def _cmul_add(b_re, b_im, a_re, a_im, s_re, s_im):
    return b_re + (a_re * s_re - a_im * s_im), b_im + (a_re * s_im + a_im * s_re)


def _s5_fill_powers(pw, a_ref, tb, reverse):
    seg_len = tb // SEG
    sign = -1.0 if reverse else 1.0
    for p in range(N_PAIR):
        a_re = jnp.broadcast_to(a_ref[p:p + 1, :], (SEG, 128))
        a_im = sign * jnp.broadcast_to(a_ref[N_PAIR + p:N_PAIR + p + 1, :], (SEG, 128))

        def step(k, c, p=p, a_re=a_re, a_im=a_im):
            rows = pl.ds(pl.multiple_of(((seg_len - 1 - k) if reverse else k) * SEG, SEG), SEG)
            pw[p, rows, :] = c[0]
            pw[N_PAIR + p, rows, :] = c[1]
            return c[0] * a_re - c[1] * a_im, c[0] * a_im + c[1] * a_re

        lax.fori_loop(0, seg_len, step, (a_re, a_im))


def _s5_scan(st, carry, a_ref, pw_ref, tb, reverse):
    seg_len = tb // SEG
    sign = -1.0 if reverse else 1.0
    sub = lax.broadcasted_iota(jnp.int32, (SEG, 128), 0)
    chain = (0 if reverse else seg_len - 1) * SEG
    full = lambda row: jnp.broadcast_to(row, (SEG, 128))
    for p0 in range(0, N_PAIR, PAIR_GROUP):
        pairs = list(range(p0, p0 + PAIR_GROUP))
        a_re = [full(a_ref[p:p + 1, :]) for p in pairs]
        a_im = [sign * full(a_ref[N_PAIR + p:N_PAIR + p + 1, :]) for p in pairs]

        def step(k, c, pairs=pairs, a_re=a_re, a_im=a_im):
            rows = pl.ds(pl.multiple_of(((seg_len - 1 - k) if reverse else k) * SEG, SEG), SEG)
            out = []
            for i, p in enumerate(pairs):
                n_re, n_im = _cmul_add(st[p, rows, :], st[N_PAIR + p, rows, :], a_re[i], a_im[i], c[2 * i], c[2 * i + 1])
                st[p, rows, :] = n_re
                st[N_PAIR + p, rows, :] = n_im
                out += [n_re, n_im]
            return tuple(out)

        ends = lax.fori_loop(0, seg_len, step, tuple(jnp.zeros((SEG, 128), F32) for _ in range(2 * PAIR_GROUP)))
        for i, p in enumerate(pairs):
            e_re, e_im = ends[2 * i], ends[2 * i + 1]
            w_re, w_im = pw_ref[p, chain:chain + SEG, :], pw_ref[N_PAIR + p, chain:chain + SEG, :]
            c_re, c_im = full(carry[p:p + 1, :]), full(carry[N_PAIR + p:N_PAIR + p + 1, :])
            for hop in range(SEG - 1):
                n_re, n_im = _cmul_add(e_re, e_im, w_re, w_im, c_re, c_im)
                target = SEG - 2 - hop if reverse else hop + 1
                shift = SEG - 1 if reverse else 1
                c_re = jnp.where(sub == target, pltpu.roll(n_re, shift, 0), c_re)
                c_im = jnp.where(sub == target, pltpu.roll(n_im, shift, 0), c_im)
            n_re, n_im = _cmul_add(e_re, e_im, w_re, w_im, c_re, c_im)
            last = 0 if reverse else SEG - 1
            carry[p:p + 1, :] = n_re[last:last + 1, :]
            carry[N_PAIR + p:N_PAIR + p + 1, :] = n_im[last:last + 1, :]
            in_re, in_im = jnp.tile(c_re, (seg_len, 1)), jnp.tile(c_im, (seg_len, 1))
            st[p], st[N_PAIR + p] = _cmul_add(st[p], st[N_PAIR + p], pw_ref[p], pw_ref[N_PAIR + p], in_re, in_im)


def _lane_chunk(ref, q):
    return jnp.concatenate([ref[4 * q + i] for i in range(4)], axis=1)


def _put_lane_chunk(ref, q, value):
    for i in range(4):
        ref[4 * q + i] = value[:, 128 * i:128 * (i + 1)]


def _step_major(tb):
    r = np.arange(tb)
    pm = np.zeros((tb, tb), np.float32)
    pm[r, (r % SEG) * (tb // SEG) + r // SEG] = 1.0
    return jnp.asarray(pm, BF16), jnp.asarray(pm.T, BF16)


def _unpermute(pt_ref, v):
    hi = v.astype(BF16)
    rest = v - hi.astype(F32)
    mid = rest.astype(BF16)
    lo = (rest - mid.astype(F32)).astype(BF16)
    return _dot(pt_ref[...], hi) + _dot(pt_ref[...], mid) + _dot(pt_ref[...], lo)


def _s5_fwd(z, bc, cc, a, pm, pt, dvec, wglu, bglu, tb=256):
    t = z.shape[0]

    def body(xa_ref, ga_ref, bc_ref, cc_ref, a_ref, pm_ref, pt_ref, d_ref, wglu_ref, bglu_ref,
             ya_ref, s_ref, ys_ref, st, carry, pw_ref):
        @pl.when(pl.program_id(0) == 0)
        def _():
            carry[...] = jnp.zeros_like(carry)
            _s5_fill_powers(pw_ref, a_ref, tb, reverse=False)

        xa = xa_ref[...]
        xab = _dot(pm_ref[...], xa.astype(BF16)).astype(BF16)
        for q in range(N_Q):
            _put_lane_chunk(st, q, _dot(xab[:, 128 * (q % 4):128 * (q % 4) + 128], bc_ref[:, pl.ds(LANE_CH * q, LANE_CH)]))
        _s5_scan(st, carry, a_ref, pw_ref, tb, reverse=False)
        s_ref[...] = st[...].astype(BF16)
        cols = []
        for j in range(4):
            lo, hi = LANE_CH * j, N_STATE + LANE_CH * j
            cols.append(_dot(_lane_chunk(s_ref, j), cc_ref[lo:lo + LANE_CH, :])
                        + _dot(_lane_chunk(s_ref, 4 + j), cc_ref[hi:hi + LANE_CH, :]))
        ys = _unpermute(pt_ref, jnp.concatenate(cols, axis=1)) + d_ref[...] * xa
        ys_ref[...] = ys
        ya1 = _gelu(ys)
        pre = _dot(ya1.astype(BF16), wglu_ref[...]) + bglu_ref[...]
        silu_ga, _ = _silu_and_grad(ga_ref[...])
        ya_ref[...] = (ya1 * jax.nn.sigmoid(pre) * silu_ga).astype(BF16)

    const = lambda shape: pl.BlockSpec(shape, lambda i: (0,) * len(shape))
    return pl.pallas_call(
        body,
        name="s5_fwd",
        grid=(t // tb,),
        in_specs=[
            pl.BlockSpec((tb, CB), lambda i: (i, CB_XA)),
            pl.BlockSpec((tb, CB), lambda i: (i, CB_GA)),
            const((128, 2 * N_STATE)),
            const((2 * N_STATE, 128)),
            const((N_LT, 128)),
            const((tb, tb)),
            const((tb, tb)),
            const((1, S5_W)),
            const((S5_W, S5_W)),
            const((1, S5_W)),
        ],
        out_specs=[
            pl.BlockSpec((tb, S5_W), lambda i: (i, 0)),
            pl.BlockSpec((N_LT, tb, 128), lambda i: (0, i, 0)),
            pl.BlockSpec((tb, S5_W), lambda i: (i, 0)),
        ],
        out_shape=[
            jax.ShapeDtypeStruct((t, S5_W), BF16),
            jax.ShapeDtypeStruct((N_LT, t, 128), BF16),
            jax.ShapeDtypeStruct((t, S5_W), F32),
        ],
        scratch_shapes=[pltpu.VMEM((N_LT, tb, 128), F32), pltpu.VMEM((N_LT, 128), F32), pltpu.VMEM((N_LT, tb, 128), F32)],
        compiler_params=_params("arbitrary"),
    )(z, z, bc, cc, a, pm, pt, dvec, wglu, bglu)


def _s5_bwd(dya, ys, z, s, bc, cc, a, pm, pt, dvec, wglu, bglu, tb=256):
    t = z.shape[0]
    nb = t // tb
    rev = lambda i: nb - 1 - i

    def body(dya_ref, ys_ref, xa_ref, ga_ref, s_ref, sp_ref, bc_ref, cc_ref, a_ref, pm_ref, pt_ref, d_ref,
             wglu_ref, bglu_ref, dz_ref, dbc_ref, dcct_ref, da_ref, dd_ref, dwglu_ref, dbglu_ref, g, carry, pw_ref):
        i = pl.program_id(0)

        @pl.when(i == 0)
        def _():
            carry[...] = jnp.zeros_like(carry)
            _s5_fill_powers(pw_ref, a_ref, tb, reverse=True)
            for r in (dbc_ref, dcct_ref, da_ref, dd_ref, dwglu_ref, dbglu_ref):
                r[...] = jnp.zeros_like(r)

        ys = ys_ref[...]
        xa = xa_ref[...]
        ga = ga_ref[...]
        dya = dya_ref[...]
        ya1, ya1_grad = _gelu_and_grad(ys)
        ya1b = ya1.astype(BF16)
        sg = jax.nn.sigmoid(_dot(ya1b, wglu_ref[...]) + bglu_ref[...])
        silu_ga, silu_ga_grad = _silu_and_grad(ga)
        dz_ref[:, S5_W:] = (dya * (ya1 * sg) * silu_ga_grad).astype(BF16)
        dya2 = dya * silu_ga
        dpre = dya2 * ya1 * sg * (1.0 - sg)
        dbglu_ref[...] += _rowsum(dpre)
        dpreb = dpre.astype(BF16)
        dwglu_ref[...] += _dot(ya1b, dpreb, TN)
        dys = (dya2 * sg + _dot(dpreb, wglu_ref[...], NT)) * ya1_grad
        dd_ref[...] += _rowsum(dys * xa)
        dysb = _dot(pm_ref[...], dys.astype(BF16)).astype(BF16)
        xab = _dot(pm_ref[...], xa.astype(BF16)).astype(BF16)

        for q in range(N_Q):
            cq = pl.ds(LANE_CH * q, LANE_CH)
            x0 = 128 * (q % 4)
            dcct_ref[:, cq] += _dot(dysb[:, x0:x0 + 128], _lane_chunk(s_ref, q), TN)
            _put_lane_chunk(g, q, _dot(dysb[:, x0:x0 + 128], cc_ref[cq, :], NT))
        _s5_scan(g, carry, a_ref, pw_ref, tb, reverse=True)

        seg0 = (lax.broadcasted_iota(jnp.int32, (SEG, 128), 0) == 0)
        have_prev = i < nb - 1

        def before(tile, halo):
            tile = tile.astype(F32)
            prev_last = jnp.where(have_prev, halo.astype(F32)[HALO - 1:HALO, :], 0.0)
            step0 = jnp.where(seg0, prev_last, pltpu.roll(tile[tb - SEG:, :], 1, 0))
            return jnp.concatenate([step0, tile[:tb - SEG, :]], axis=0)

        for p in range(N_PAIR):
            g_re, g_im = g[p], g[N_PAIR + p]
            sp_re, sp_im = before(s_ref[p], sp_ref[p]), before(s_ref[N_PAIR + p], sp_ref[N_PAIR + p])
            da_ref[p:p + 1, :] += _rowsum(sp_re * g_re + sp_im * g_im)
            da_ref[N_PAIR + p:N_PAIR + p + 1, :] += _rowsum(sp_re * g_im - sp_im * g_re)
        dxa_cols = []
        for j in range(4):
            re = pl.ds(LANE_CH * j, LANE_CH)
            im = pl.ds(N_STATE + LANE_CH * j, LANE_CH)
            x0 = 128 * j
            gb_re, gb_im = _lane_chunk(g, j).astype(BF16), _lane_chunk(g, 4 + j).astype(BF16)
            dbc_ref[:, re] += _dot(xab[:, x0:x0 + 128], gb_re, TN)
            dbc_ref[:, im] += _dot(xab[:, x0:x0 + 128], gb_im, TN)
            dxa_cols.append(_dot(gb_re, bc_ref[:, re], NT) + _dot(gb_im, bc_ref[:, im], NT))
        dz_ref[:, 0:S5_W] = (dys * d_ref[...] + _unpermute(pt_ref, jnp.concatenate(dxa_cols, axis=1))).astype(BF16)

    const = lambda shape: pl.BlockSpec(shape, lambda i: (0,) * len(shape))
    per_halo = tb // HALO
    return pl.pallas_call(
        body,
        name="s5_bwd",
        grid=(nb,),
        in_specs=[
            pl.BlockSpec((tb, S5_W), lambda i: (rev(i), 0)),
            pl.BlockSpec((tb, S5_W), lambda i: (rev(i), 0)),
            pl.BlockSpec((tb, CB), lambda i: (rev(i), CB_XA)),
            pl.BlockSpec((tb, CB), lambda i: (rev(i), CB_GA)),
            pl.BlockSpec((N_LT, tb, 128), lambda i: (0, rev(i), 0)),
            pl.BlockSpec((N_LT, HALO, 128), lambda i: (0, jnp.maximum(rev(i) * per_halo - 1, 0), 0)),
            const((128, 2 * N_STATE)),
            const((2 * N_STATE, 128)),
            const((N_LT, 128)),
            const((tb, tb)),
            const((tb, tb)),
            const((1, S5_W)),
            const((S5_W, S5_W)),
            const((1, S5_W)),
        ],
        out_specs=[
            pl.BlockSpec((tb, 2 * CB), lambda i: (rev(i), 0)),
            const((128, 2 * N_STATE)),
            const((128, 2 * N_STATE)),
            const((N_LT, 128)),
            const((1, S5_W)),
            const((S5_W, S5_W)),
            const((1, S5_W)),
        ],
        out_shape=[
            jax.ShapeDtypeStruct((t, 2 * CB), BF16),
            jax.ShapeDtypeStruct((128, 2 * N_STATE), F32),
            jax.ShapeDtypeStruct((128, 2 * N_STATE), F32),
            jax.ShapeDtypeStruct((N_LT, 128), F32),
            jax.ShapeDtypeStruct((1, S5_W), F32),
            jax.ShapeDtypeStruct((S5_W, S5_W), F32),
            jax.ShapeDtypeStruct((1, S5_W), F32),
        ],
        scratch_shapes=[pltpu.VMEM((N_LT, tb, 128), F32), pltpu.VMEM((N_LT, 128), F32), pltpu.VMEM((N_LT, tb, 128), F32)],
        compiler_params=_params("arbitrary"),
    )(dya, ys, z, z, s, s, bc, cc, a, pm, pt, dvec, wglu, bglu)


def _sgu_norm(v0, v1, lng_ref, lnb_ref):
    g0, g1 = _gelu(v0), _gelu(v1)
    mu = (jnp.sum(g0, axis=-1, keepdims=True) + jnp.sum(g1, axis=-1, keepdims=True)) * (1.0 / SGU_W)
    c0, c1 = g0 - mu, g1 - mu
    var = (jnp.sum(c0 * c0, axis=-1, keepdims=True) + jnp.sum(c1 * c1, axis=-1, keepdims=True)) * (1.0 / SGU_W)
    rstd = lax.rsqrt(var + LN_EPS)
    vh0, vh1 = c0 * rstd, c1 * rstd
    vn0 = vh0 * lng_ref[:, 0:CB] + lnb_ref[:, 0:CB]
    vn1 = vh1 * lng_ref[:, CB:] + lnb_ref[:, CB:]
    return (vh0, vh1), (vn0, vn1), rstd


def _sgu_fwd(z, lng, lnb, ws, bsx, tb=256):
    t = z.shape[0]

    def body(u0_ref, u1_ref, v0_ref, v1_ref, gb0_ref, gb1_ref, lng_ref, lnb_ref, ws_ref, bsx_ref, yb_ref):
        _, (vn0, vn1), _ = _sgu_norm(v0_ref[...], v1_ref[...], lng_ref, lnb_ref)
        for half, (vn, u_ref, gb_ref) in enumerate(((vn0, u0_ref, gb0_ref), (vn1, u1_ref, gb1_ref))):
            vnb = vn.astype(BF16)
            silu_gb, _ = _silu_and_grad(gb_ref[...])
            gate = _gelu(u_ref[...]) * silu_gb
            for hh in range(4):
                h = 4 * half + hh
                for c in range(tb // CHUNK):
                    rows, cols = slice(CHUNK * c, CHUNK * (c + 1)), slice(128 * hh, 128 * (hh + 1))
                    sp = _dot(ws_ref[h], vnb[rows, cols]) + bsx_ref[h]
                    yb_ref[rows, CB * half + 128 * hh:CB * half + 128 * (hh + 1)] = (gate[rows, cols] * sp).astype(BF16)

    zb = lambda j: pl.BlockSpec((tb, CB), lambda i, j=j: (i, j))
    const = lambda shape: pl.BlockSpec(shape, lambda i: (0,) * len(shape))
    return pl.pallas_call(
        body,
        name="sgu_fwd",
        grid=(t // tb,),
        in_specs=[zb(CB_U), zb(CB_U + 1), zb(CB_V), zb(CB_V + 1), zb(CB_GB), zb(CB_GB + 1),
                  const((1, SGU_W)), const((1, SGU_W)), const((SGU_HEADS, CHUNK, CHUNK)), const((SGU_HEADS, CHUNK, 128))],
        out_specs=pl.BlockSpec((tb, SGU_W), lambda i: (i, 0)),
        out_shape=jax.ShapeDtypeStruct((t, SGU_W), BF16),
        compiler_params=_params("arbitrary"),
    )(z, z, z, z, z, z, lng, lnb, ws, bsx)


def _sgu_bwd(dyb, z, lng, lnb, ws, wst, bsx, tb=256):
    t = z.shape[0]

    def body(dyb_ref, u0_ref, u1_ref, v0_ref, v1_ref, gb0_ref, gb1_ref, lng_ref, lnb_ref, ws_ref, wst_ref, bsx_ref,
             dz_ref, dlng_ref, dlnb_ref, dws_ref, dbs_ref, dvn):
        @pl.when(pl.program_id(0) == 0)
        def _():
            for r in (dlng_ref, dlnb_ref, dws_ref, dbs_ref):
                r[...] = jnp.zeros_like(r)

        v0, v1 = v0_ref[...], v1_ref[...]
        (vh0, vh1), (vn0, vn1), rstd = _sgu_norm(v0, v1, lng_ref, lnb_ref)
        causal = (lax.broadcasted_iota(jnp.int32, (CHUNK, CHUNK), 0) >= lax.broadcasted_iota(jnp.int32, (CHUNK, CHUNK), 1))
        for half, (vn, u_ref, gb_ref) in enumerate(((vn0, u0_ref, gb0_ref), (vn1, u1_ref, gb1_ref))):
            vnb = vn.astype(BF16)
            u = u_ref[...]
            ug, ug_grad = _gelu_and_grad(u)
            silu_gb, silu_gb_grad = _silu_and_grad(gb_ref[...])
            dyb = dyb_ref[:, CB * half:CB * (half + 1)]
            dyb0 = dyb * silu_gb
            ds = dyb0 * ug
            sp_cols = []
            for hh in range(4):
                h = 4 * half + hh
                cols = slice(128 * hh, 128 * (hh + 1))
                sp_rows = []
                for c in range(tb // CHUNK):
                    rows = slice(CHUNK * c, CHUNK * (c + 1))
                    vt = vnb[rows, cols]
                    sp_rows.append(_dot(ws_ref[h], vt) + bsx_ref[h])
                    dst = ds[rows, cols]
                    dstb = dst.astype(BF16)
                    dbs_ref[h] += dst
                    dws_ref[h] += jnp.where(causal, _dot(dstb, vt, NT), 0.0)
                    dvn[rows, CB * half + 128 * hh:CB * half + 128 * (hh + 1)] = _dot(wst_ref[h], dstb)
                sp_cols.append(jnp.concatenate(sp_rows, axis=0))
            sp = jnp.concatenate(sp_cols, axis=1)
            dz_ref[:, CB * half:CB * (half + 1)] = (dyb0 * sp * ug_grad).astype(BF16)
            dz_ref[:, 2 * SGU_W + CB * half:2 * SGU_W + CB * (half + 1)] = (dyb * (ug * sp) * silu_gb_grad).astype(BF16)

        dvn0, dvn1 = dvn[:, 0:CB], dvn[:, CB:]
        dlng_ref[:, 0:CB] += _rowsum(dvn0 * vh0)
        dlng_ref[:, CB:] += _rowsum(dvn1 * vh1)
        dlnb_ref[:, 0:CB] += _rowsum(dvn0)
        dlnb_ref[:, CB:] += _rowsum(dvn1)
        dh0, dh1 = dvn0 * lng_ref[:, 0:CB], dvn1 * lng_ref[:, CB:]
        m1 = (jnp.sum(dh0, axis=-1, keepdims=True) + jnp.sum(dh1, axis=-1, keepdims=True)) * (1.0 / SGU_W)
        m2 = (jnp.sum(dh0 * vh0, axis=-1, keepdims=True) + jnp.sum(dh1 * vh1, axis=-1, keepdims=True)) * (1.0 / SGU_W)
        dz_ref[:, SGU_W:SGU_W + CB] = (rstd * (dh0 - m1 - vh0 * m2) * _gelu_grad(v0)).astype(BF16)
        dz_ref[:, SGU_W + CB:2 * SGU_W] = (rstd * (dh1 - m1 - vh1 * m2) * _gelu_grad(v1)).astype(BF16)

    zb = lambda j: pl.BlockSpec((tb, CB), lambda i, j=j: (i, j))
    const = lambda shape: pl.BlockSpec(shape, lambda i: (0,) * len(shape))
    hmat = (SGU_HEADS, CHUNK, CHUNK)
    return pl.pallas_call(
        body,
        name="sgu_bwd",
        grid=(t // tb,),
        in_specs=[pl.BlockSpec((tb, SGU_W), lambda i: (i, 0)),
                  zb(CB_U), zb(CB_U + 1), zb(CB_V), zb(CB_V + 1), zb(CB_GB), zb(CB_GB + 1),
                  const((1, SGU_W)), const((1, SGU_W)), const(hmat), const(hmat), const(hmat)],
        out_specs=[pl.BlockSpec((tb, 3 * SGU_W), lambda i: (i, 0)),
                   const((1, SGU_W)), const((1, SGU_W)), const(hmat), const(hmat)],
        out_shape=[jax.ShapeDtypeStruct((t, 3 * SGU_W), BF16),
                   jax.ShapeDtypeStruct((1, SGU_W), F32), jax.ShapeDtypeStruct((1, SGU_W), F32),
                   jax.ShapeDtypeStruct(hmat, F32), jax.ShapeDtypeStruct(hmat, F32)],
        scratch_shapes=[pltpu.VMEM((tb, SGU_W), F32)],
        compiler_params=_params("arbitrary"),
    )(dyb, z, z, z, z, z, z, lng, lnb, ws, wst, bsx)


def _window_sums(ext, lookahead):
    n = ext.shape[0]
    out = []
    for gi, w in enumerate(POOL_WINDOWS):
        acc = ext[:, 128 * gi:128 * (gi + 1)]
        k = 1
        while k < w:
            acc = acc + pltpu.roll(acc, (n - k) if lookahead else k, 0)
            k *= 2
        out.append(acc)
    return jnp.concatenate(out, axis=1)


def _pool_counts(row0, tb):
    pos = (row0 + 1 + lax.broadcasted_iota(jnp.int32, (tb, POOL_W), 0)).astype(F32)
    lane = lax.broadcasted_iota(jnp.int32, (tb, POOL_W), 1)
    win = jnp.where(lane < 128, 2.0, jnp.where(lane < 256, 4.0, jnp.where(lane < 384, 8.0, 16.0)))
    return jnp.minimum(pos, win)


def _pool_fwd(z, wpool, scale, tb=256):
    t = z.shape[0]

    def body(xc_ref, gc_ref, wp_ref, sc_ref, yc_ref, halo):
        i = pl.program_id(0)

        @pl.when(i == 0)
        def _():
            halo[...] = jnp.zeros_like(halo)

        xc = xc_ref[...]
        sums = _window_sums(jnp.concatenate([halo[...], xc], axis=0), lookahead=False)[HALO:, :]
        halo[...] = xc[tb - HALO:, :]
        pb = (sums / _pool_counts(i * tb, tb) - xc).astype(BF16)
        q = jnp.concatenate([_dot(pb[:, 128 * gi:128 * (gi + 1)], wp_ref[gi]) for gi in range(4)], axis=1)
        silu_gc, _ = _silu_and_grad(gc_ref[...])
        yc_ref[...] = (q * sc_ref[...] * silu_gc).astype(BF16)

    const = lambda shape: pl.BlockSpec(shape, lambda i: (0,) * len(shape))
    return pl.pallas_call(
        body,
        name="pool_fwd",
        grid=(t // tb,),
        in_specs=[pl.BlockSpec((tb, CB), lambda i: (i, CB_XC)), pl.BlockSpec((tb, CB), lambda i: (i, CB_GC)),
                  const((4, 128, 128)), const((1, POOL_W))],
        out_specs=pl.BlockSpec((tb, POOL_W), lambda i: (i, 0)),
        out_shape=jax.ShapeDtypeStruct((t, POOL_W), BF16),
        scratch_shapes=[pltpu.VMEM((HALO, POOL_W), F32)],
        compiler_params=_params("arbitrary"),
    )(z, z, wpool, scale)


def _pool_bwd(dyc, z, wpool, scale, tb=256):
    t = z.shape[0]
    nb = t // tb
    rev = lambda i: nb - 1 - i
    per_halo = tb // HALO

    def body(dyc_ref, xc_ref, xp_ref, gc_ref, wp_ref, sc_ref, dz_ref, dwp_ref, dsc_ref, ehalo):
        i = pl.program_id(0)

        @pl.when(i == 0)
        def _():
            ehalo[...] = jnp.zeros_like(ehalo)
            dwp_ref[...] = jnp.zeros_like(dwp_ref)
            dsc_ref[...] = jnp.zeros_like(dsc_ref)

        xc = xc_ref[...]
        prev = jnp.where(i < nb - 1, xp_ref[...], 0.0)
        sums = _window_sums(jnp.concatenate([prev, xc], axis=0), lookahead=False)[HALO:, :]
        cnt = _pool_counts(rev(i) * tb, tb)
        pb = (sums / cnt - xc).astype(BF16)
        q = jnp.concatenate([_dot(pb[:, 128 * gi:128 * (gi + 1)], wp_ref[gi]) for gi in range(4)], axis=1)
        silu_gc, silu_gc_grad = _silu_and_grad(gc_ref[...])
        dyc = dyc_ref[...]
        dz_ref[:, POOL_W:] = (dyc * (q * sc_ref[...]) * silu_gc_grad).astype(BF16)
        dyc0 = dyc * silu_gc
        dsc_ref[...] += _rowsum(dyc0 * q)
        dqb = (dyc0 * sc_ref[...]).astype(BF16)
        dp_cols = []
        for gi in range(4):
            cols = slice(128 * gi, 128 * (gi + 1))
            dwp_ref[gi] += _dot(pb[:, cols], dqb[:, cols], TN)
            dp_cols.append(_dot(dqb[:, cols], wp_ref[gi], NT))
        dp = jnp.concatenate(dp_cols, axis=1)
        e = dp / cnt
        fut = _window_sums(jnp.concatenate([e, ehalo[...]], axis=0), lookahead=True)[:tb, :]
        ehalo[...] = e[:HALO, :]
        dz_ref[:, 0:POOL_W] = (fut - dp).astype(BF16)

    const = lambda shape: pl.BlockSpec(shape, lambda i: (0,) * len(shape))
    return pl.pallas_call(
        body,
        name="pool_bwd",
        grid=(nb,),
        in_specs=[pl.BlockSpec((tb, POOL_W), lambda i: (rev(i), 0)),
                  pl.BlockSpec((tb, CB), lambda i: (rev(i), CB_XC)),
                  pl.BlockSpec((HALO, CB), lambda i: (jnp.maximum(rev(i) * per_halo - 1, 0), CB_XC)),
                  pl.BlockSpec((tb, CB), lambda i: (rev(i), CB_GC)),
                  const((4, 128, 128)), const((1, POOL_W))],
        out_specs=[pl.BlockSpec((tb, 2 * POOL_W), lambda i: (rev(i), 0)), const((4, 128, 128)), const((1, POOL_W))],
        out_shape=[jax.ShapeDtypeStruct((t, 2 * POOL_W), BF16),
                   jax.ShapeDtypeStruct((4, 128, 128), F32), jax.ShapeDtypeStruct((1, POOL_W), F32)],
        scratch_shapes=[pltpu.VMEM((HALO, POOL_W), F32)],
        compiler_params=_params("arbitrary"),
    )(dyc, z, z, z, wpool, scale)


def _adamw(w, m, v, parts, tr, name):
    r, c = w.shape
    n_slab = len(parts)
    per_slab = r // n_slab // tr

    def body(w_ref, m_ref, v_ref, *refs):
        p_refs, (g_ref, d_ref, nm_ref, nv_ref) = refs[:n_slab], refs[n_slab:]
        for s, p_ref in enumerate(p_refs):
            @pl.when(pl.program_id(0) // per_slab == s)
            def _(p_ref=p_ref):
                g = p_ref[0].astype(F32)
                for k in range(1, N_DEV):
                    g = g + p_ref[k].astype(F32)
                g_ref[...] = g

        g = g_ref[...]
        nm = ADAM_B1 * m_ref[...] + (1.0 - ADAM_B1) * g
        nv = ADAM_B2 * v_ref[...] + (1.0 - ADAM_B2) * (g * g)
        m_hat = nm / (1.0 - ADAM_B1 ** ADAM_STEP)
        v_hat = nv / (1.0 - ADAM_B2 ** ADAM_STEP)
        nm_ref[...] = nm
        nv_ref[...] = nv
        d_ref[...] = -ADAM_LR * (m_hat / (jnp.sqrt(v_hat) + ADAM_EPS) + ADAM_WD * w_ref[...])

    blk = pl.BlockSpec((tr, c), lambda i: (i, 0))
    slab = lambda s: pl.BlockSpec((N_DEV, tr, c), lambda i, s=s: (0, jnp.clip(i - s * per_slab, 0, per_slab - 1), 0))
    return pl.pallas_call(
        body,
        name=name,
        grid=(r // tr,),
        in_specs=[blk, blk, blk] + [slab(s) for s in range(n_slab)],
        out_specs=[blk, blk, blk, blk],
        out_shape=[jax.ShapeDtypeStruct((r, c), F32)] * 4,
        compiler_params=_params("arbitrary"),
    )(w, m, v, *parts)


MESH = pl.DeviceIdType.MESH
ANY = pl.BlockSpec(memory_space=pl.ANY)


def _dev_index(dev):
    return 4 * dev[0] + 2 * dev[1] + dev[2]


WHOLE_SHAPES = ((D_MODEL, IN_COLS), (D_MODEL, D_MODEL), (S5_W, S5_W))
SHARD_SHAPES = ((D_MODEL, SH_IN), (SH_OUT, D_MODEL), (SH_GLU, S5_W))


def _shard_of(ref, ti, idx):
    if ti == 0:
        return ref.at[:, pl.ds(pl.multiple_of(idx * SH_IN, 128), SH_IN)]
    rows = SHARD_SHAPES[ti][0]
    return ref.at[pl.ds(pl.multiple_of(idx * rows, rows), rows), :]


def _peer(mask, x, y, c):
    return (1 - x if mask & 4 else x, 1 - y if mask & 2 else y, 1 - c if mask & 1 else c)


def _allgather_weights(wi, wo, wg):
    n_t = 3

    def body(wi_ref, wo_ref, wg_ref, gi_ref, go_ref, gg_ref, send_sems, recv_sems, local_sems):
        x, y, c = lax.axis_index("x"), lax.axis_index("y"), lax.axis_index("c")
        me, sibling = (x, y, c), (x, y, 1 - c)
        chips = [(1 - x, y), (x, 1 - y), (1 - x, 1 - y)]
        shards = (wi_ref, wo_ref, wg_ref)
        wholes = (gi_ref, go_ref, gg_ref)

        def slot(ti, dev):
            return _shard_of(wholes[ti], ti, _dev_index(dev))

        def copy(k, ti, block, to, own=False):
            return pltpu.make_async_remote_copy(
                src_ref=shards[ti] if own else slot(ti, block), dst_ref=slot(ti, block),
                send_sem=send_sems.at[n_t * k + ti], recv_sem=recv_sems.at[n_t * k + ti],
                device_id=to, device_id_type=MESH)

        mine = [pltpu.make_async_copy(shards[ti], slot(ti, me), local_sems.at[ti]) for ti in range(n_t)]
        for cp in mine:
            cp.start()
        first = [copy(0, ti, me, sibling, own=True) for ti in range(n_t)]
        first += [copy(1 + j, ti, me, (*chip, c), own=True) for j, chip in enumerate(chips) for ti in range(n_t)]
        for cp in first:
            cp.start()
        passed = []
        for j, chip in enumerate(chips):
            for ti in range(n_t):
                copy(1 + j, ti, (*chip, c), me).wait_recv()
            onward = [copy(4 + j, ti, (*chip, c), sibling) for ti in range(n_t)]
            for cp in onward:
                cp.start()
            passed += onward
        for ti in range(n_t):
            copy(0, ti, sibling, me).wait_recv()
        for j, chip in enumerate(chips):
            for ti in range(n_t):
                copy(4 + j, ti, (*chip, 1 - c), me).wait_recv()
        for cp in first + passed:
            cp.wait_send()
        for cp in mine:
            cp.wait()

    return pl.pallas_call(
        body,
        name="allgather_weights",
        in_specs=[ANY, ANY, ANY],
        out_specs=[ANY, ANY, ANY],
        out_shape=[jax.ShapeDtypeStruct(s, BF16) for s in WHOLE_SHAPES],
        scratch_shapes=[pltpu.SemaphoreType.DMA((7 * n_t,)), pltpu.SemaphoreType.DMA((7 * n_t,)),
                        pltpu.SemaphoreType.DMA((n_t,))],
    )(wi, wo, wg)


HBM = pl.BlockSpec(memory_space=pltpu.HBM)
SEM = pl.BlockSpec(memory_space=pltpu.SEMAPHORE)
GATHER, SCATTER, SHARE = "gather", "scatter", "share"


def _split_route(kind, ti, sending, me_idx, p_idx, src_ref, land_ref):
    owner = me_idx if sending else p_idx
    if kind == GATHER:
        return src_ref, _shard_of(land_ref, ti, owner)
    if kind == SCATTER:
        return _shard_of(src_ref, ti, p_idx), land_ref.at[owner]
    return src_ref, land_ref.at[owner]


def _split_start(name, srcs, lands, kinds, after=None):
    n = len(srcs)
    arrays = list(srcs) + list(lands) + ([] if after is None else [after])

    def body(*refs):
        src_refs, land_refs = refs[0:n], refs[n:2 * n]
        send_sems, recv_sems, token = refs[len(arrays)], refs[len(arrays) + 1], refs[-1]
        x, y, c = lax.axis_index("x"), lax.axis_index("y"), lax.axis_index("c")
        me_idx = _dev_index((x, y, c))
        for mask in range(1, N_DEV):
            p = _peer(mask, x, y, c)
            for i, (kind, ti) in enumerate(kinds):
                k = (mask - 1) * n + i
                src, dst = _split_route(kind, ti, True, me_idx, _dev_index(p), src_refs[i], land_refs[i])
                pltpu.make_async_remote_copy(src_ref=src, dst_ref=dst, send_sem=send_sems.at[k], recv_sem=recv_sems.at[k],
                                             device_id=p, device_id_type=MESH).start()
        token[...] = jnp.zeros_like(token)

    n_copies = (N_DEV - 1) * n
    return pl.pallas_call(
        body,
        name=name,
        in_specs=[HBM] * len(arrays),
        out_specs=(SEM, SEM) + (HBM,) * (2 * n) + (pl.BlockSpec(memory_space=pltpu.VMEM),),
        out_shape=(pltpu.SemaphoreType.DMA((n_copies,)), pltpu.SemaphoreType.DMA((n_copies,)))
        + tuple(pltpu.HBM(a.shape, a.dtype) for a in arrays[:2 * n]) + (jax.ShapeDtypeStruct((8, 128), F32),),
        input_output_aliases={i: 2 + i for i in range(2 * n)},
        compiler_params=pltpu.CompilerParams(has_side_effects=pltpu.SideEffectType.DATAFLOW_SIDE_EFFECTING),
    )(*[pltpu.with_memory_space_constraint(a, pltpu.HBM) for a in arrays])


def _split_wait(name, started, kinds, after):
    n = len(kinds)
    send_sems, recv_sems, thru = started[0], started[1], started[2:2 + 2 * n]

    def body(*refs):
        src_refs, land_refs = refs[0:n], refs[n:2 * n]
        send_sems, recv_sems = refs[2 * n], refs[2 * n + 1]
        x, y, c = lax.axis_index("x"), lax.axis_index("y"), lax.axis_index("c")
        me_idx = _dev_index((x, y, c))
        for mask in range(1, N_DEV):
            p = _peer(mask, x, y, c)
            for i, (kind, ti) in enumerate(kinds):
                k = (mask - 1) * n + i
                src, dst = _split_route(kind, ti, False, me_idx, _dev_index(p), src_refs[i], land_refs[i])
                cp = pltpu.make_async_remote_copy(src_ref=src, dst_ref=dst, send_sem=send_sems.at[k],
                                                  recv_sem=recv_sems.at[k], device_id=p, device_id_type=MESH)
                cp.wait_send()
                cp.wait_recv()

    res = pl.pallas_call(
        body,
        name=name,
        in_specs=[HBM] * (2 * n) + [SEM, SEM, pl.BlockSpec(memory_space=pl.ANY)],
        out_specs=(HBM,) * (2 * n),
        out_shape=tuple(pltpu.HBM(a.shape, a.dtype) for a in thru),
        input_output_aliases={i: i for i in range(2 * n)},
        compiler_params=pltpu.CompilerParams(has_side_effects=pltpu.SideEffectType.DATAFLOW_SIDE_EFFECTING),
    )(*thru, send_sems, recv_sems, after)
    return res[n:2 * n]


def _landing_zones(srcs, kinds):
    me = _own_index()
    lands = []
    for src, (kind, ti) in zip(srcs, kinds):
        start = None if kind == SHARE else ((0, me * SH_IN) if ti == 0 else (me * SHARD_SHAPES[ti][0], 0))
        if kind == GATHER:
            lands.append(lax.dynamic_update_slice(lax.empty(WHOLE_SHAPES[ti], src.dtype), src, start))
            continue
        own = src if kind == SHARE else lax.dynamic_slice(src, start, SHARD_SHAPES[ti])
        lands.append(lax.dynamic_update_slice(lax.empty((N_DEV,) + own.shape, src.dtype), own[None], (me, 0, 0)))
    return lands


def _share_small(buf):
    def body(b_ref, o_ref, send_sems, recv_sems, local_sem):
        x, y, c = lax.axis_index("x"), lax.axis_index("y"), lax.axis_index("c")
        me_idx = _dev_index((x, y, c))
        local = pltpu.make_async_copy(b_ref, o_ref.at[me_idx], local_sem)
        local.start()

        def copy(mask, owner):
            return pltpu.make_async_remote_copy(
                src_ref=b_ref, dst_ref=o_ref.at[owner], send_sem=send_sems.at[mask - 1], recv_sem=recv_sems.at[mask - 1],
                device_id=_peer(mask, x, y, c), device_id_type=MESH)

        sends = [copy(mask, me_idx) for mask in range(1, N_DEV)]
        for cp in sends:
            cp.start()
        for mask in range(1, N_DEV):
            copy(mask, _dev_index(_peer(mask, x, y, c))).wait_recv()
        for cp in sends:
            cp.wait_send()
        local.wait()

    return pl.pallas_call(
        body,
        name="share_small",
        in_specs=[ANY],
        out_specs=ANY,
        out_shape=jax.ShapeDtypeStruct((N_DEV,) + buf.shape, buf.dtype),
        scratch_shapes=[pltpu.SemaphoreType.DMA((N_DEV - 1,)), pltpu.SemaphoreType.DMA((N_DEV - 1,)),
                        pltpu.SemaphoreType.DMA],
    )(buf)


def _s5_prep(lam_re, lam_im, b_re, b_im, c_re, c_im, d_skip, log_dt):
    dt = jnp.exp(log_dt)[:, None]
    mag = jnp.exp(lam_re * dt)
    a_re, a_im = mag * jnp.cos(lam_im * dt), mag * jnp.sin(lam_im * dt)
    den = lam_re * lam_re + lam_im * lam_im
    f_re = ((a_re - 1.0) * lam_re + a_im * lam_im) / den
    f_im = (a_im * lam_re - (a_re - 1.0) * lam_im) / den
    bb_re = f_re[..., None] * b_re - f_im[..., None] * b_im
    bb_im = f_re[..., None] * b_im + f_im[..., None] * b_re
    eye = jnp.eye(8, dtype=F32)

    def in_map(bb):
        return jnp.einsum("jgph,gk->ghjkp", bb.reshape(4, 8, S5_STATE, S5_CH), eye).reshape(128, N_STATE)

    def out_map(cm):
        return jnp.einsum("jghp,gk->ghjkp", cm.reshape(4, 8, S5_CH, S5_STATE), eye).reshape(128, N_STATE)

    a = jnp.concatenate([a_re.reshape(N_PAIR, 128), a_im.reshape(N_PAIR, 128)])
    bc = jnp.concatenate([in_map(bb_re), in_map(bb_im)], axis=1)
    cct = jnp.concatenate([out_map(c_re), -out_map(c_im)], axis=1)
    return a, bc, cct, d_skip.reshape(1, S5_W)


WEIGHTS = ["norm_g", "w_in", "lam_re", "lam_im", "b_re", "b_im", "c_re", "c_im", "d_skip", "log_dt", "w_glu", "b_glu",
           "ln_g", "ln_b", "w_s", "b_s", "w_pool", "pool_scale", "w_out", "final_g"]
SHARDED = ("w_in", "w_glu", "w_out")
SMALL = [n for n in WEIGHTS if n not in SHARDED]
INPUTS = ["x"] + WEIGHTS + ["loss_target"] + ["m_" + n for n in WEIGHTS] + ["v_" + n for n in WEIGHTS]
SMALL_B = ["norm_g", "final_g"]
SMALL_A = [n for n in SMALL if n not in SMALL_B]
SMALL_TILE = 16 * 128
LOSS_AT = (DEPTH + 1) * D_MODEL


def _pack_small(arrays, dtype):
    flat = jnp.concatenate([a.reshape(-1) for a in arrays])
    pad = (-flat.shape[0]) % SMALL_TILE
    return jnp.pad(flat, (0, pad)).astype(dtype).reshape(-1, 128)


def _unpack_small(packed, like):
    flat = packed.reshape(-1)
    out, off = [], 0
    for a in like:
        out.append(flat[off:off + a.size].reshape(a.shape))
        off += a.size
    return out


def _layer_fwd(p, l, x, wi, wo, wg, token=None):
    row = lambda v: v.reshape(1, -1)
    causal = jnp.tril(jnp.ones((CHUNK, CHUNK), dtype=bool))
    (a, bc, cct, dvec), prep_vjp = jax.vjp(
        _s5_prep, p["lam_re"][l], p["lam_im"][l], p["b_re"][l], p["b_im"][l], p["c_re"][l], p["c_im"][l],
        p["d_skip"][l], p["log_dt"][l])
    ws_f32 = jnp.where(causal[None], p["w_s"][l], 0.0)
    pm, pt = _step_major(S5_TB)
    c = dict(
        x=x, wi=wi, wo=wo, wg=wg, a=a, bc=bc.astype(BF16), cc=cct.T.astype(BF16), dvec=dvec, prep_vjp=prep_vjp,
        pm=pm, pt=pt,
        ws=ws_f32.astype(BF16), wst=jnp.swapaxes(ws_f32, 1, 2).astype(BF16),
        bsx=jnp.broadcast_to(p["b_s"][l][:, :, None], (SGU_HEADS, CHUNK, 128)),
        wpool=p["w_pool"][l].astype(BF16), scale=row(p["pool_scale"][l]),
        lng=row(p["ln_g"][l]), lnb=row(p["ln_b"][l]), bglu=row(p["b_glu"][l]), norm_g=row(p["norm_g"][l]))
    c["z"], c["h"] = _rms_inproj(x, c["norm_g"], wi, token)
    c["ya"], c["s"], c["ys"] = _s5_fwd(c["z"], c["bc"], c["cc"], a, pm, pt, dvec, wg, c["bglu"], S5_TB)
    c["yb"] = _sgu_fwd(c["z"], c["lng"], c["lnb"], c["ws"], c["bsx"])
    c["yc"] = _pool_fwd(c["z"], c["wpool"], c["scale"])
    return _outproj(x, c["ya"], c["yb"], c["yc"], wo), c


def _mixers_bwd(c, dx, token=None):
    dya, dyb, dyc = _outproj_bwd(dx, c["wo"], token)
    dwo = _wgrad_out(c["ya"], c["yb"], c["yc"], dx, token)
    dz_s5, dbc, dcct, da, dd, dwg, dbglu = _s5_bwd(
        dya, c["ys"], c["z"], c["s"], c["bc"], c["cc"], c["a"], c["pm"], c["pt"], c["dvec"], c["wg"], c["bglu"], S5_TB)
    dz_sgu, dlng, dlnb, dws, dbsx = _sgu_bwd(dyb, c["z"], c["lng"], c["lnb"], c["ws"], c["wst"], c["bsx"])
    dz_pool, dwp, dsc = _pool_bwd(dyc, c["z"], c["wpool"], c["scale"])
    g_lam_re, g_lam_im, g_b_re, g_b_im, g_c_re, g_c_im, g_d, g_dt = c["prep_vjp"]((da, dbc, dcct, dd))
    small = dict(lam_re=g_lam_re, lam_im=g_lam_im, b_re=g_b_re, b_im=g_b_im, c_re=g_c_re,
                 c_im=g_c_im, d_skip=g_d, log_dt=g_dt, b_glu=dbglu.reshape(-1), ln_g=dlng.reshape(-1),
                 ln_b=dlnb.reshape(-1), w_s=dws, b_s=jnp.sum(dbsx, axis=-1), w_pool=dwp, pool_scale=dsc.reshape(-1))
    return (dz_s5, dz_sgu, dz_pool), dwo, dwg, small


def _inproj_grads(c, dz, dx, token_w=None, token_x=None):
    dwi = _wgrad_in(c["h"], *dz, token_w)
    dx, dnorm = _inproj_bwd(*dz, c["wi"], c["x"], c["norm_g"], dx, token_x)
    return dwi, dx, dnorm.reshape(-1)


def _own_index():
    return _dev_index((lax.axis_index("x"), lax.axis_index("y"), lax.axis_index("c")))


def kernel(x, norm_g, w_in, lam_re, lam_im, b_re, b_im, c_re, c_im, d_skip, log_dt, w_glu, b_glu, ln_g, ln_b, w_s, b_s, w_pool, pool_scale, w_out, final_g, loss_target, m_norm_g, m_w_in, m_lam_re, m_lam_im, m_b_re, m_b_im, m_c_re, m_c_im, m_d_skip, m_log_dt, m_w_glu, m_b_glu, m_ln_g, m_ln_b, m_w_s, m_b_s, m_w_pool, m_pool_scale, m_w_out, m_final_g, v_norm_g, v_w_in, v_lam_re, v_lam_im, v_b_re, v_b_im, v_c_re, v_c_im, v_d_skip, v_log_dt, v_w_glu, v_b_glu, v_ln_g, v_ln_b, v_w_s, v_b_s, v_w_pool, v_pool_scale, v_w_out, v_final_g):
    p = dict(zip(INPUTS, (x, norm_g, w_in, lam_re, lam_im, b_re, b_im, c_re, c_im, d_skip, log_dt, w_glu, b_glu, ln_g, ln_b, w_s, b_s, w_pool, pool_scale, w_out, final_g, loss_target, m_norm_g, m_w_in, m_lam_re, m_lam_im, m_b_re, m_b_im, m_c_re, m_c_im, m_d_skip, m_log_dt, m_w_glu, m_b_glu, m_ln_g, m_ln_b, m_w_s, m_b_s, m_w_pool, m_pool_scale, m_w_out, m_final_g, v_norm_g, v_w_in, v_lam_re, v_lam_im, v_b_re, v_b_im, v_c_re, v_c_im, v_d_skip, v_log_dt, v_w_glu, v_b_glu, v_ln_g, v_ln_b, v_w_s, v_b_s, v_w_pool, v_pool_scale, v_w_out, v_final_g)))

    shards = [[w[l].astype(BF16) for w in (w_in, w_out, w_glu)] for l in range(DEPTH)]
    gather3 = [(GATHER, ti) for ti in range(3)]
    scatter3 = [(SCATTER, ti) for ti in range(3)]

    whole0 = _allgather_weights(*shards[0])
    lands = _landing_zones(shards[1], gather3)
    gather1 = _split_start("gather1_start", shards[1], lands, gather3, after=whole0[2])
    x1, c0 = _layer_fwd(p, 0, x[0], *whole0, token=gather1[-1])
    whole1 = _split_wait("gather1_wait", gather1, gather3, x1)
    x2, c1 = _layer_fwd(p, 1, x1, *whole1)

    dx, loss_tile, dfinal = _final_loss(x2, final_g.reshape(1, -1), loss_target[0])

    dz1, dwo1, dwg1, small1 = _mixers_bwd(c1, dx)
    dwi1, dx, dnorm1 = _inproj_grads(c1, dz1, dx)
    partials1 = [dwi1, dwo1, dwg1.astype(BF16)]
    grads1 = _split_start("grads1_start", partials1, _landing_zones(partials1, scatter3), scatter3)
    dz0, dwo0, dwg0, small0 = _mixers_bwd(c0, dx, token=grads1[-1])
    small_a = _pack_small([jnp.stack([small0[n], small1[n]]) for n in SMALL_A], BF16)
    srcs_a, kinds_a = [dwo0, dwg0.astype(BF16), small_a], [(SCATTER, 1), (SCATTER, 2), (SHARE, None)]
    grads0a = _split_start("grads0a_start", srcs_a, _landing_zones(srcs_a, kinds_a), kinds_a)
    dwi0 = _wgrad_in(c0["h"], *dz0, grads0a[-1])
    kinds_b = [(SCATTER, 0)]
    grads0b = _split_start("grads0b_start", [dwi0], _landing_zones([dwi0], kinds_b), kinds_b)
    dx, dnorm0 = _inproj_bwd(*dz0, c0["wi"], c0["x"], c0["norm_g"], dx, grads0b[-1])
    parts1 = _split_wait("grads1_wait", grads1, scatter3, dx)
    r_out0, r_glu0, r_a = _split_wait("grads0a_wait", grads0a, kinds_a, dx)
    (r_in0,) = _split_wait("grads0b_wait", grads0b, kinds_b, dx)
    parts0 = [r_in0, r_out0, r_glu0]
    r_b = _share_small(_pack_small(
        [jnp.stack([dnorm0.reshape(-1), dnorm1]), dfinal.reshape(-1), loss_tile[0, 0:1]], F32))
    loss = jnp.sum(r_b.reshape(N_DEV, -1)[:, LOSS_AT])

    out = {}

    def adam(name, ti, tr):
        shape2d = (DEPTH * SHARD_SHAPES[ti][0], SHARD_SHAPES[ti][1])
        res = _adamw(p[name].reshape(shape2d), p["m_" + name].reshape(shape2d), p["v_" + name].reshape(shape2d),
                     [parts0[ti], parts1[ti]], tr, "adamw_" + name)
        out[name] = [r.reshape(p[name].shape) for r in res]

    adam("w_in", 0, 256)
    adam("w_out", 1, 128)
    adam("w_glu", 2, 64)
    for names, parts, steps, tag in ((SMALL_A, r_a, 7, "a"), (SMALL_B, r_b, 1, "b")):
        like = [p[n] for n in names]
        pad = [jnp.zeros((1,), F32)] if tag == "b" else []
        packed = _adamw(*[_pack_small([p[pre + n] for n in names] + pad, F32) for pre in ("", "m_", "v_")],
                        [parts], parts.shape[1] // steps, "adamw_small_" + tag)
        for n, vals in zip(names, zip(*[_unpack_small(r, like) for r in packed])):
            out[n] = list(vals)

    return (loss, dx[None], *[out[n][0] for n in WEIGHTS], *[out[n][1] for n in WEIGHTS],
            *[out[n][2] for n in WEIGHTS], *[out[n][3] for n in WEIGHTS])
```

```python
import functools
import math

import jax
import jax.numpy as jnp
import numpy as np
from jax import lax
from jax.experimental import pallas as pl
from jax.experimental.pallas import tpu as pltpu

F32 = jnp.float32
BF16 = jnp.bfloat16

D_MODEL = 2048
DEPTH = 2
S5_W, SGU_W, POOL_W = 512, 1024, 512
S5_GROUPS, S5_STATE, S5_CH = 32, 64, 16
N_STATE = S5_GROUPS * S5_STATE
CHUNK = 128
SGU_HEADS = 8
POOL_WINDOWS = (2, 4, 8, 16)
IN_COLS = 5120
RMS_EPS = 1e-6
LN_EPS = 1e-5
ADAM_LR, ADAM_B1, ADAM_B2, ADAM_EPS, ADAM_WD, ADAM_STEP = 0.001, 0.9, 0.999, 1e-08, 0.01, 10

CB = 512
N_CB = IN_COLS // CB
CB_XA, CB_U, CB_V, CB_XC, CB_GA, CB_GB, CB_GC = 0, 1, 3, 5, 6, 7, 9

N_DEV = 8
SH_IN = IN_COLS // N_DEV
SH_OUT = D_MODEL // N_DEV
SH_GLU = S5_W // N_DEV

VMEM_LIMIT = 52 * 1024 * 1024
VMEM_LIMIT_WHOLE = 56 * 1024 * 1024
HALO = 16
LANE_CH = 512

TN = (((0,), (0,)), ((), ()))
NT = (((1,), (1,)), ((), ()))


def _params(*sem):
    return pltpu.CompilerParams(dimension_semantics=sem if sem else None, vmem_limit_bytes=VMEM_LIMIT)


def _dot(a, b, dims=None):
    if dims is None:
        return jnp.dot(a, b, preferred_element_type=F32)
    return lax.dot_general(a, b, dims, preferred_element_type=F32)


_GELU_C = math.sqrt(2.0 / math.pi)


def _gelu(x):
    return 0.5 * x * (1.0 + jnp.tanh(_GELU_C * (x + 0.044715 * x * x * x)))


def _gelu_and_grad(x):
    t = jnp.tanh(_GELU_C * (x + 0.044715 * x * x * x))
    half = 0.5 * (1.0 + t)
    return x * half, half + 0.5 * x * (1.0 - t * t) * _GELU_C * (1.0 + 3.0 * 0.044715 * x * x)


def _gelu_grad(x):
    return _gelu_and_grad(x)[1]


def _silu_and_grad(x):
    s = jax.nn.sigmoid(x)
    return x * s, s * (1.0 + x * (1.0 - s))


def _rowsum(x):
    return jnp.sum(x, axis=0, keepdims=True)


def _after(token):
    if token is None:
        return [], []
    return [pl.BlockSpec((8, 128), lambda *_: (0, 0))], [token]


def _rms_inproj(x, g, w, token=None, tm=1024, tn=1024):
    t = x.shape[0]
    tm = min(tm, t)
    after_specs, after = _after(token)

    def body(x_ref, g_ref, w_ref, *rest):
        z_ref, h_ref = rest[-2:]

        @pl.when(pl.program_id(1) == 0)
        def _():
            xv = x_ref[...]
            r = lax.rsqrt(jnp.mean(xv * xv, axis=-1, keepdims=True) + RMS_EPS)
            h_ref[...] = (xv * r * g_ref[...]).astype(BF16)

        z_ref[...] = _dot(h_ref[...], w_ref[...])

    return pl.pallas_call(
        body,
        name="rms_inproj",
        grid=(t // tm, IN_COLS // tn),
        in_specs=[
            pl.BlockSpec((tm, D_MODEL), lambda m, n: (m, 0)),
            pl.BlockSpec((1, D_MODEL), lambda m, n: (0, 0)),
            pl.BlockSpec((D_MODEL, tn), lambda m, n: (0, n)),
        ] + after_specs,
        out_specs=[
            pl.BlockSpec((tm, tn), lambda m, n: (m, n)),
            pl.BlockSpec((tm, D_MODEL), lambda m, n: (m, 0)),
        ],
        out_shape=[jax.ShapeDtypeStruct((t, IN_COLS), F32), jax.ShapeDtypeStruct((t, D_MODEL), BF16)],
        compiler_params=_params("arbitrary", "arbitrary"),
    )(x, g, w, *after)


def _outproj(x, ya, yb, yc, w, tm=1024, tn=1024):
    t = x.shape[0]
    tm = min(tm, t)

    def body(x_ref, ya_ref, yb_ref, yc_ref, w_ref, o_ref):
        acc = _dot(ya_ref[...], w_ref[0:S5_W, :])
        acc += _dot(yb_ref[...], w_ref[S5_W:S5_W + SGU_W, :])
        acc += _dot(yc_ref[...], w_ref[S5_W + SGU_W:, :])
        o_ref[...] = x_ref[...] + acc

    return pl.pallas_call(
        body,
        name="outproj",
        grid=(t // tm, D_MODEL // tn),
        in_specs=[
            pl.BlockSpec((tm, tn), lambda m, n: (m, n)),
            pl.BlockSpec((tm, S5_W), lambda m, n: (m, 0)),
            pl.BlockSpec((tm, SGU_W), lambda m, n: (m, 0)),
            pl.BlockSpec((tm, POOL_W), lambda m, n: (m, 0)),
            pl.BlockSpec((D_MODEL, tn), lambda m, n: (0, n)),
        ],
        out_specs=pl.BlockSpec((tm, tn), lambda m, n: (m, n)),
        out_shape=jax.ShapeDtypeStruct((t, D_MODEL), F32),
        compiler_params=_params("arbitrary", "arbitrary"),
    )(x, ya, yb, yc, w)


def _outproj_bwd(dx, w, token=None, tm=1024):
    t = dx.shape[0]
    tm = min(tm, t)
    after_specs, after = _after(token)

    def body(dx_ref, w_ref, *rest):
        dya_ref, dyb_ref, dyc_ref = rest[-3:]
        dy = _dot(dx_ref[...].astype(BF16), w_ref[...], NT)
        dya_ref[...] = dy[:, 0:S5_W]
        dyb_ref[...] = dy[:, S5_W:S5_W + SGU_W]
        dyc_ref[...] = dy[:, S5_W + SGU_W:]

    return pl.pallas_call(
        body,
        name="outproj_bwd",
        grid=(t // tm,),
        in_specs=[
            pl.BlockSpec((tm, D_MODEL), lambda m: (m, 0)),
            pl.BlockSpec((D_MODEL, D_MODEL), lambda m: (0, 0)),
        ] + after_specs,
        out_specs=[
            pl.BlockSpec((tm, S5_W), lambda m: (m, 0)),
            pl.BlockSpec((tm, SGU_W), lambda m: (m, 0)),
            pl.BlockSpec((tm, POOL_W), lambda m: (m, 0)),
        ],
        out_shape=[
            jax.ShapeDtypeStruct((t, S5_W), F32),
            jax.ShapeDtypeStruct((t, SGU_W), F32),
            jax.ShapeDtypeStruct((t, POOL_W), F32),
        ],
        compiler_params=_params("arbitrary"),
    )(dx, w, *after)


def _dz_piece_maps():
    s5_map = lambda j: jnp.where(j >= CB_GA, 1, 0)
    sgu_map = lambda j: jnp.clip(jnp.where(j <= 4, j - 1, j - 3), 0, 5)
    pool_map = lambda j: jnp.where(j >= CB_GC, 1, 0)
    return s5_map, sgu_map, pool_map


def _pick_piece(j):
    is_s5 = jnp.logical_or(j == CB_XA, j == CB_GA)
    is_pool = jnp.logical_or(j == CB_XC, j == CB_GC)
    return is_s5, is_pool, jnp.logical_not(jnp.logical_or(is_s5, is_pool))


def _inproj_bwd(dz_s5, dz_sgu, dz_pool, w, x, g, dxo, token=None, tm=512):
    t = x.shape[0]
    s5_map, sgu_map, pool_map = _dz_piece_maps()
    after_specs, after = _after(token)

    def body(s5_ref, sgu_ref, pool_ref, w_ref, x_ref, g_ref, dxo_ref, *rest):
        dx_ref, dg_ref, acc = rest[-3:]
        m, j = pl.program_id(0), pl.program_id(1)

        @pl.when(jnp.logical_and(m == 0, j == 0))
        def _():
            dg_ref[...] = jnp.zeros_like(dg_ref)

        @pl.when(j == 0)
        def _():
            acc[...] = jnp.zeros_like(acc)

        is_s5, is_pool, is_sgu = _pick_piece(j)

        @pl.when(is_s5)
        def _():
            acc[...] += _dot(w_ref[...], s5_ref[...], NT)

        @pl.when(is_sgu)
        def _():
            acc[...] += _dot(w_ref[...], sgu_ref[...], NT)

        @pl.when(is_pool)
        def _():
            acc[...] += _dot(w_ref[...], pool_ref[...], NT)

        @pl.when(j == N_CB - 1)
        def _():
            xv = x_ref[...]
            r = lax.rsqrt(jnp.mean(xv * xv, axis=-1, keepdims=True) + RMS_EPS)
            n = xv * r
            dh = acc[...].T
            dg_ref[...] += _rowsum(dh * n)
            dn = dh * g_ref[...]
            dx_ref[...] = dxo_ref[...] + r * (dn - n * jnp.mean(dn * n, axis=-1, keepdims=True))

    return pl.pallas_call(
        body,
        name="inproj_bwd",
        grid=(t // tm, N_CB),
        in_specs=[
            pl.BlockSpec((tm, CB), lambda m, j: (m, s5_map(j))),
            pl.BlockSpec((tm, CB), lambda m, j: (m, sgu_map(j))),
            pl.BlockSpec((tm, CB), lambda m, j: (m, pool_map(j))),
            pl.BlockSpec((D_MODEL, CB), lambda m, j: (0, j)),
            pl.BlockSpec((tm, D_MODEL), lambda m, j: (m, 0)),
            pl.BlockSpec((1, D_MODEL), lambda m, j: (0, 0)),
            pl.BlockSpec((tm, D_MODEL), lambda m, j: (m, 0)),
        ] + after_specs,
        out_specs=[
            pl.BlockSpec((tm, D_MODEL), lambda m, j: (m, 0)),
            pl.BlockSpec((1, D_MODEL), lambda m, j: (0, 0)),
        ],
        out_shape=[jax.ShapeDtypeStruct((t, D_MODEL), F32), jax.ShapeDtypeStruct((1, D_MODEL), F32)],
        scratch_shapes=[pltpu.VMEM((D_MODEL, tm), F32)],
        compiler_params=_params("arbitrary", "arbitrary"),
    )(dz_s5, dz_sgu, dz_pool, w, x, g, dxo, *after)


def _wgrad_in(h, dz_s5, dz_sgu, dz_pool, token=None, tm=2048):
    t = h.shape[0]
    tm = min(tm, t)
    s5_map, sgu_map, pool_map = _dz_piece_maps()
    after_specs, after = _after(token)

    def body(h_ref, s5_ref, sgu_ref, pool_ref, *rest):
        o_ref, acc = rest[-2:]
        j, m = pl.program_id(0), pl.program_id(1)

        @pl.when(m == 0)
        def _():
            acc[...] = jnp.zeros_like(acc)

        is_s5, is_pool, is_sgu = _pick_piece(j)

        @pl.when(is_s5)
        def _():
            acc[...] += _dot(s5_ref[...], h_ref[...], TN)

        @pl.when(is_sgu)
        def _():
            acc[...] += _dot(sgu_ref[...], h_ref[...], TN)

        @pl.when(is_pool)
        def _():
            acc[...] += _dot(pool_ref[...], h_ref[...], TN)

        @pl.when(m == pl.num_programs(1) - 1)
        def _():
            o_ref[...] = acc[...].T.astype(BF16)

    return pl.pallas_call(
        body,
        name="wgrad_in",
        grid=(N_CB, t // tm),
        in_specs=[
            pl.BlockSpec((tm, D_MODEL), lambda j, m: (m, 0)),
            pl.BlockSpec((tm, CB), lambda j, m: (m, s5_map(j))),
            pl.BlockSpec((tm, CB), lambda j, m: (m, sgu_map(j))),
            pl.BlockSpec((tm, CB), lambda j, m: (m, pool_map(j))),
        ] + after_specs,
        out_specs=pl.BlockSpec((D_MODEL, CB), lambda j, m: (0, j)),
        out_shape=jax.ShapeDtypeStruct((D_MODEL, IN_COLS), BF16),
        scratch_shapes=[pltpu.VMEM((CB, D_MODEL), F32)],
        compiler_params=_params("arbitrary", "arbitrary"),
    )(h, dz_s5, dz_sgu, dz_pool, *after)


def _wgrad_out(ya, yb, yc, dx, token=None, tm=2048, tn=512):
    t = dx.shape[0]
    tm = min(tm, t)
    after_specs, after = _after(token)

    def body(ya_ref, yb_ref, yc_ref, dx_ref, *rest):
        o_ref, acc = rest[-2:]
        m = pl.program_id(1)

        @pl.when(m == 0)
        def _():
            acc[...] = jnp.zeros_like(acc)

        dxb = dx_ref[...].astype(BF16)
        acc[:, 0:S5_W] += _dot(dxb, ya_ref[...], TN)
        acc[:, S5_W:S5_W + SGU_W] += _dot(dxb, yb_ref[...], TN)
        acc[:, S5_W + SGU_W:] += _dot(dxb, yc_ref[...], TN)

        @pl.when(m == pl.num_programs(1) - 1)
        def _():
            o_ref[...] = acc[...].T.astype(BF16)

    return pl.pallas_call(
        body,
        name="wgrad_out",
        grid=(D_MODEL // tn, t // tm),
        in_specs=[
            pl.BlockSpec((tm, S5_W), lambda n, m: (m, 0)),
            pl.BlockSpec((tm, SGU_W), lambda n, m: (m, 0)),
            pl.BlockSpec((tm, POOL_W), lambda n, m: (m, 0)),
            pl.BlockSpec((tm, tn), lambda n, m: (m, n)),
        ] + after_specs,
        out_specs=pl.BlockSpec((D_MODEL, tn), lambda n, m: (0, n)),
        out_shape=jax.ShapeDtypeStruct((D_MODEL, D_MODEL), BF16),
        scratch_shapes=[pltpu.VMEM((tn, D_MODEL), F32)],
        compiler_params=_params("arbitrary", "arbitrary"),
    )(ya, yb, yc, dx, *after)


def _final_loss(x, g, target, tm=512):
    t = x.shape[0]

    def body(x_ref, g_ref, t_ref, dx_ref, loss_ref, dg_ref):
        @pl.when(pl.program_id(0) == 0)
        def _():
            loss_ref[...] = jnp.zeros_like(loss_ref)
            dg_ref[...] = jnp.zeros_like(dg_ref)

        xv = x_ref[...]
        gv = g_ref[...]
        r = lax.rsqrt(jnp.mean(xv * xv, axis=-1, keepdims=True) + RMS_EPS)
        n = xv * r
        err = n * gv - t_ref[...]
        loss_ref[...] += 0.5 * jnp.sum(jnp.mean(err * err, axis=-1, keepdims=True))
        dy = err * (1.0 / D_MODEL)
        dg_ref[...] += _rowsum(dy * n)
        dn = dy * gv
        dx_ref[...] = r * (dn - n * jnp.mean(dn * n, axis=-1, keepdims=True))

    return pl.pallas_call(
        body,
        name="final_loss",
        grid=(t // tm,),
        in_specs=[
            pl.BlockSpec((tm, D_MODEL), lambda m: (m, 0)),
            pl.BlockSpec((1, D_MODEL), lambda m: (0, 0)),
            pl.BlockSpec((tm, D_MODEL), lambda m: (m, 0)),
        ],
        out_specs=[
            pl.BlockSpec((tm, D_MODEL), lambda m: (m, 0)),
            pl.BlockSpec((8, 128), lambda m: (0, 0)),
            pl.BlockSpec((1, D_MODEL), lambda m: (0, 0)),
        ],
        out_shape=[
            jax.ShapeDtypeStruct((t, D_MODEL), F32),
            jax.ShapeDtypeStruct((8, 128), F32),
            jax.ShapeDtypeStruct((1, D_MODEL), F32),
        ],
        compiler_params=_params("arbitrary"),
    )(x, g, target)


N_Q = 2 * N_STATE // LANE_CH
N_LT = 2 * N_STATE // 128
N_PAIR = N_LT // 2
SEG = 8
PAIR_GROUP = 8
S5_TB = 512


def _cmul_add(b_re, b_im, a_re, a_im, s_re, s_im):
    return b_re + (a_re * s_re - a_im * s_im), b_im + (a_re * s_im + a_im * s_re)


def _s5_fill_powers(pw, a_ref, tb, reverse):
    seg_len = tb // SEG
    sign = -1.0 if reverse else 1.0
    for p in range(N_PAIR):
        a_re = jnp.broadcast_to(a_ref[p:p + 1, :], (SEG, 128))
        a_im = sign * jnp.broadcast_to(a_ref[N_PAIR + p:N_PAIR + p + 1, :], (SEG, 128))

        def step(k, c, p=p, a_re=a_re, a_im=a_im):
            rows = pl.ds(pl.multiple_of(((seg_len - 1 - k) if reverse else k) * SEG, SEG), SEG)
            pw[p, rows, :] = c[0]
            pw[N_PAIR + p, rows, :] = c[1]
            return c[0] * a_re - c[1] * a_im, c[0] * a_im + c[1] * a_re

        lax.fori_loop(0, seg_len, step, (a_re, a_im))


def _s5_scan(st, carry, a_ref, pw_ref, tb, reverse):
    seg_len = tb // SEG
    sign = -1.0 if reverse else 1.0
    sub = lax.broadcasted_iota(jnp.int32, (SEG, 128), 0)
    chain = (0 if reverse else seg_len - 1) * SEG
    full = lambda row: jnp.broadcast_to(row, (SEG, 128))
    for p0 in range(0, N_PAIR, PAIR_GROUP):
        pairs = list(range(p0, p0 + PAIR_GROUP))
        a_re = [full(a_ref[p:p + 1, :]) for p in pairs]
        a_im = [sign * full(a_ref[N_PAIR + p:N_PAIR + p + 1, :]) for p in pairs]

        def step(k, c, pairs=pairs, a_re=a_re, a_im=a_im):
            rows = pl.ds(pl.multiple_of(((seg_len - 1 - k) if reverse else k) * SEG, SEG), SEG)
            out = []
            for i, p in enumerate(pairs):
                n_re, n_im = _cmul_add(st[p, rows, :], st[N_PAIR + p, rows, :], a_re[i], a_im[i], c[2 * i], c[2 * i + 1])
                st[p, rows, :] = n_re
                st[N_PAIR + p, rows, :] = n_im
                out += [n_re, n_im]
            return tuple(out)

        ends = lax.fori_loop(0, seg_len, step, tuple(jnp.zeros((SEG, 128), F32) for _ in range(2 * PAIR_GROUP)))
        for i, p in enumerate(pairs):
            e_re, e_im = ends[2 * i], ends[2 * i + 1]
            w_re, w_im = pw_ref[p, chain:chain + SEG, :], pw_ref[N_PAIR + p, chain:chain + SEG, :]
            c_re, c_im = full(carry[p:p + 1, :]), full(carry[N_PAIR + p:N_PAIR + p + 1, :])
            for hop in range(SEG - 1):
                n_re, n_im = _cmul_add(e_re, e_im, w_re, w_im, c_re, c_im)
                target = SEG - 2 - hop if reverse else hop + 1
                shift = SEG - 1 if reverse else 1
                c_re = jnp.where(sub == target, pltpu.roll(n_re, shift, 0), c_re)
                c_im = jnp.where(sub == target, pltpu.roll(n_im, shift, 0), c_im)
            n_re, n_im = _cmul_add(e_re, e_im, w_re, w_im, c_re, c_im)
            last = 0 if reverse else SEG - 1
            carry[p:p + 1, :] = n_re[last:last + 1, :]
            carry[N_PAIR + p:N_PAIR + p + 1, :] = n_im[last:last + 1, :]
            in_re, in_im = jnp.tile(c_re, (seg_len, 1)), jnp.tile(c_im, (seg_len, 1))
            st[p], st[N_PAIR + p] = _cmul_add(st[p], st[N_PAIR + p], pw_ref[p], pw_ref[N_PAIR + p], in_re, in_im)


def _lane_chunk(ref, q):
    return jnp.concatenate([ref[4 * q + i] for i in range(4)], axis=1)


def _put_lane_chunk(ref, q, value):
    for i in range(4):
        ref[4 * q + i] = value[:, 128 * i:128 * (i + 1)]


def _step_major(tb):
    r = np.arange(tb)
    pm = np.zeros((tb, tb), np.float32)
    pm[r, (r % SEG) * (tb // SEG) + r // SEG] = 1.0
    return jnp.asarray(pm, BF16), jnp.asarray(pm.T, BF16)


def _unpermute(pt_ref, v):
    hi = v.astype(BF16)
    rest = v - hi.astype(F32)
    mid = rest.astype(BF16)
    lo = (rest - mid.astype(F32)).astype(BF16)
    return _dot(pt_ref[...], hi) + _dot(pt_ref[...], mid) + _dot(pt_ref[...], lo)


def _s5_fwd(z, bc, cc, a, pm, pt, dvec, wglu, bglu, tb=256):
    t = z.shape[0]

    def body(xa_ref, ga_ref, bc_ref, cc_ref, a_ref, pm_ref, pt_ref, d_ref, wglu_ref, bglu_ref,
             ya_ref, s_ref, ys_ref, st, carry, pw_ref):
        @pl.when(pl.program_id(0) == 0)
        def _():
            carry[...] = jnp.zeros_like(carry)
            _s5_fill_powers(pw_ref, a_ref, tb, reverse=False)

        xa = xa_ref[...]
        xab = _dot(pm_ref[...], xa.astype(BF16)).astype(BF16)
        for q in range(N_Q):
            _put_lane_chunk(st, q, _dot(xab[:, 128 * (q % 4):128 * (q % 4) + 128], bc_ref[:, pl.ds(LANE_CH * q, LANE_CH)]))
        _s5_scan(st, carry, a_ref, pw_ref, tb, reverse=False)
        s_ref[...] = st[...].astype(BF16)
        cols = []
        for j in range(4):
            lo, hi = LANE_CH * j, N_STATE + LANE_CH * j
            cols.append(_dot(_lane_chunk(s_ref, j), cc_ref[lo:lo + LANE_CH, :])
                        + _dot(_lane_chunk(s_ref, 4 + j), cc_ref[hi:hi + LANE_CH, :]))
        ys = _unpermute(pt_ref, jnp.concatenate(cols, axis=1)) + d_ref[...] * xa
        ys_ref[...] = ys
        ya1 = _gelu(ys)
        pre = _dot(ya1.astype(BF16), wglu_ref[...]) + bglu_ref[...]
        silu_ga, _ = _silu_and_grad(ga_ref[...])
        ya_ref[...] = (ya1 * jax.nn.sigmoid(pre) * silu_ga).astype(BF16)

    const = lambda shape: pl.BlockSpec(shape, lambda i: (0,) * len(shape))
    return pl.pallas_call(
        body,
        name="s5_fwd",
        grid=(t // tb,),
        in_specs=[
            pl.BlockSpec((tb, CB), lambda i: (i, CB_XA)),
            pl.BlockSpec((tb, CB), lambda i: (i, CB_GA)),
            const((128, 2 * N_STATE)),
            const((2 * N_STATE, 128)),
            const((N_LT, 128)),
            const((tb, tb)),
            const((tb, tb)),
            const((1, S5_W)),
            const((S5_W, S5_W)),
            const((1, S5_W)),
        ],
        out_specs=[
            pl.BlockSpec((tb, S5_W), lambda i: (i, 0)),
            pl.BlockSpec((N_LT, tb, 128), lambda i: (0, i, 0)),
            pl.BlockSpec((tb, S5_W), lambda i: (i, 0)),
        ],
        out_shape=[
            jax.ShapeDtypeStruct((t, S5_W), BF16),
            jax.ShapeDtypeStruct((N_LT, t, 128), BF16),
            jax.ShapeDtypeStruct((t, S5_W), F32),
        ],
        scratch_shapes=[pltpu.VMEM((N_LT, tb, 128), F32), pltpu.VMEM((N_LT, 128), F32), pltpu.VMEM((N_LT, tb, 128), F32)],
        compiler_params=_params("arbitrary"),
    )(z, z, bc, cc, a, pm, pt, dvec, wglu, bglu)


def _s5_bwd(dya, ys, z, s, bc, cc, a, pm, pt, dvec, wglu, bglu, tb=256):
    t = z.shape[0]
    nb = t // tb
    rev = lambda i: nb - 1 - i

    def body(dya_ref, ys_ref, xa_ref, ga_ref, s_ref, sp_ref, bc_ref, cc_ref, a_ref, pm_ref, pt_ref, d_ref,
             wglu_ref, bglu_ref, dz_ref, dbc_ref, dcct_ref, da_ref, dd_ref, dwglu_ref, dbglu_ref, g, carry, pw_ref):
        i = pl.program_id(0)

        @pl.when(i == 0)
        def _():
            carry[...] = jnp.zeros_like(carry)
            _s5_fill_powers(pw_ref, a_ref, tb, reverse=True)
            for r in (dbc_ref, dcct_ref, da_ref, dd_ref, dwglu_ref, dbglu_ref):
                r[...] = jnp.zeros_like(r)

        ys = ys_ref[...]
        xa = xa_ref[...]
        ga = ga_ref[...]
        dya = dya_ref[...]
        ya1, ya1_grad = _gelu_and_grad(ys)
        ya1b = ya1.astype(BF16)
        sg = jax.nn.sigmoid(_dot(ya1b, wglu_ref[...]) + bglu_ref[...])
        silu_ga, silu_ga_grad = _silu_and_grad(ga)
        dz_ref[:, S5_W:] = (dya * (ya1 * sg) * silu_ga_grad).astype(BF16)
        dya2 = dya * silu_ga
        dpre = dya2 * ya1 * sg * (1.0 - sg)
        dbglu_ref[...] += _rowsum(dpre)
        dpreb = dpre.astype(BF16)
        dwglu_ref[...] += _dot(ya1b, dpreb, TN)
        dys = (dya2 * sg + _dot(dpreb, wglu_ref[...], NT)) * ya1_grad
        dd_ref[...] += _rowsum(dys * xa)
        dysb = _dot(pm_ref[...], dys.astype(BF16)).astype(BF16)
        xab = _dot(pm_ref[...], xa.astype(BF16)).astype(BF16)

        for q in range(N_Q):
            cq = pl.ds(LANE_CH * q, LANE_CH)
            x0 = 128 * (q % 4)
            dcct_ref[:, cq] += _dot(dysb[:, x0:x0 + 128], _lane_chunk(s_ref, q), TN)
            _put_lane_chunk(g, q, _dot(dysb[:, x0:x0 + 128], cc_ref[cq, :], NT))
        _s5_scan(g, carry, a_ref, pw_ref, tb, reverse=True)

        seg0 = (lax.broadcasted_iota(jnp.int32, (SEG, 128), 0) == 0)
        have_prev = i < nb - 1

        def before(tile, halo):
            tile = tile.astype(F32)
            prev_last = jnp.where(have_prev, halo.astype(F32)[HALO - 1:HALO, :], 0.0)
            step0 = jnp.where(seg0, prev_last, pltpu.roll(tile[tb - SEG:, :], 1, 0))
            return jnp.concatenate([step0, tile[:tb - SEG, :]], axis=0)

        for p in range(N_PAIR):
            g_re, g_im = g[p], g[N_PAIR + p]
            sp_re, sp_im = before(s_ref[p], sp_ref[p]), before(s_ref[N_PAIR + p], sp_ref[N_PAIR + p])
            da_ref[p:p + 1, :] += _rowsum(sp_re * g_re + sp_im * g_im)
            da_ref[N_PAIR + p:N_PAIR + p + 1, :] += _rowsum(sp_re * g_im - sp_im * g_re)
        dxa_cols = []
        for j in range(4):
            re = pl.ds(LANE_CH * j, LANE_CH)
            im = pl.ds(N_STATE + LANE_CH * j, LANE_CH)
            x0 = 128 * j
            gb_re, gb_im = _lane_chunk(g, j).astype(BF16), _lane_chunk(g, 4 + j).astype(BF16)
            dbc_ref[:, re] += _dot(xab[:, x0:x0 + 128], gb_re, TN)
            dbc_ref[:, im] += _dot(xab[:, x0:x0 + 128], gb_im, TN)
            dxa_cols.append(_dot(gb_re, bc_ref[:, re], NT) + _dot(gb_im, bc_ref[:, im], NT))
        dz_ref[:, 0:S5_W] = (dys * d_ref[...] + _unpermute(pt_ref, jnp.concatenate(dxa_cols, axis=1))).astype(BF16)

    const = lambda shape: pl.BlockSpec(shape, lambda i: (0,) * len(shape))
    per_halo = tb // HALO
    return pl.pallas_call(
        body,
        name="s5_bwd",
        grid=(nb,),
        in_specs=[
            pl.BlockSpec((tb, S5_W), lambda i: (rev(i), 0)),
            pl.BlockSpec((tb, S5_W), lambda i: (rev(i), 0)),
            pl.BlockSpec((tb, CB), lambda i: (rev(i), CB_XA)),
            pl.BlockSpec((tb, CB), lambda i: (rev(i), CB_GA)),
            pl.BlockSpec((N_LT, tb, 128), lambda i: (0, rev(i), 0)),
            pl.BlockSpec((N_LT, HALO, 128), lambda i: (0, jnp.maximum(rev(i) * per_halo - 1, 0), 0)),
            const((128, 2 * N_STATE)),
            const((2 * N_STATE, 128)),
            const((N_LT, 128)),
            const((tb, tb)),
            const((tb, tb)),
            const((1, S5_W)),
            const((S5_W, S5_W)),
            const((1, S5_W)),
        ],
        out_specs=[
            pl.BlockSpec((tb, 2 * CB), lambda i: (rev(i), 0)),
            const((128, 2 * N_STATE)),
            const((128, 2 * N_STATE)),
            const((N_LT, 128)),
            const((1, S5_W)),
            const((S5_W, S5_W)),
            const((1, S5_W)),
        ],
        out_shape=[
            jax.ShapeDtypeStruct((t, 2 * CB), BF16),
            jax.ShapeDtypeStruct((128, 2 * N_STATE), F32),
            jax.ShapeDtypeStruct((128, 2 * N_STATE), F32),
            jax.ShapeDtypeStruct((N_LT, 128), F32),
            jax.ShapeDtypeStruct((1, S5_W), F32),
            jax.ShapeDtypeStruct((S5_W, S5_W), F32),
            jax.ShapeDtypeStruct((1, S5_W), F32),
        ],
        scratch_shapes=[pltpu.VMEM((N_LT, tb, 128), F32), pltpu.VMEM((N_LT, 128), F32), pltpu.VMEM((N_LT, tb, 128), F32)],
        compiler_params=_params("arbitrary"),
    )(dya, ys, z, z, s, s, bc, cc, a, pm, pt, dvec, wglu, bglu)


def _sgu_norm(v0, v1, lng_ref, lnb_ref):
    g0, g1 = _gelu(v0), _gelu(v1)
    mu = (jnp.sum(g0, axis=-1, keepdims=True) + jnp.sum(g1, axis=-1, keepdims=True)) * (1.0 / SGU_W)
    c0, c1 = g0 - mu, g1 - mu
    var = (jnp.sum(c0 * c0, axis=-1, keepdims=True) + jnp.sum(c1 * c1, axis=-1, keepdims=True)) * (1.0 / SGU_W)
    rstd = lax.rsqrt(var + LN_EPS)
    vh0, vh1 = c0 * rstd, c1 * rstd
    vn0 = vh0 * lng_ref[:, 0:CB] + lnb_ref[:, 0:CB]
    vn1 = vh1 * lng_ref[:, CB:] + lnb_ref[:, CB:]
    return (vh0, vh1), (vn0, vn1), rstd


def _sgu_fwd(z, lng, lnb, ws, bsx, tb=256):
    t = z.shape[0]

    def body(u0_ref, u1_ref, v0_ref, v1_ref, gb0_ref, gb1_ref, lng_ref, lnb_ref, ws_ref, bsx_ref, yb_ref):
        _, (vn0, vn1), _ = _sgu_norm(v0_ref[...], v1_ref[...], lng_ref, lnb_ref)
        for half, (vn, u_ref, gb_ref) in enumerate(((vn0, u0_ref, gb0_ref), (vn1, u1_ref, gb1_ref))):
            vnb = vn.astype(BF16)
            silu_gb, _ = _silu_and_grad(gb_ref[...])
            gate = _gelu(u_ref[...]) * silu_gb
            for hh in range(4):
                h = 4 * half + hh
                for c in range(tb // CHUNK):
                    rows, cols = slice(CHUNK * c, CHUNK * (c + 1)), slice(128 * hh, 128 * (hh + 1))
                    sp = _dot(ws_ref[h], vnb[rows, cols]) + bsx_ref[h]
                    yb_ref[rows, CB * half + 128 * hh:CB * half + 128 * (hh + 1)] = (gate[rows, cols] * sp).astype(BF16)

    zb = lambda j: pl.BlockSpec((tb, CB), lambda i, j=j: (i, j))
    const = lambda shape: pl.BlockSpec(shape, lambda i: (0,) * len(shape))
    return pl.pallas_call(
        body,
        name="sgu_fwd",
        grid=(t // tb,),
        in_specs=[zb(CB_U), zb(CB_U + 1), zb(CB_V), zb(CB_V + 1), zb(CB_GB), zb(CB_GB + 1),
                  const((1, SGU_W)), const((1, SGU_W)), const((SGU_HEADS, CHUNK, CHUNK)), const((SGU_HEADS, CHUNK, 128))],
        out_specs=pl.BlockSpec((tb, SGU_W), lambda i: (i, 0)),
        out_shape=jax.ShapeDtypeStruct((t, SGU_W), BF16),
        compiler_params=_params("arbitrary"),
    )(z, z, z, z, z, z, lng, lnb, ws, bsx)


def _sgu_bwd(dyb, z, lng, lnb, ws, wst, bsx, tb=256):
    t = z.shape[0]

    def body(dyb_ref, u0_ref, u1_ref, v0_ref, v1_ref, gb0_ref, gb1_ref, lng_ref, lnb_ref, ws_ref, wst_ref, bsx_ref,
             dz_ref, dlng_ref, dlnb_ref, dws_ref, dbs_ref, dvn):
        @pl.when(pl.program_id(0) == 0)
        def _():
            for r in (dlng_ref, dlnb_ref, dws_ref, dbs_ref):
                r[...] = jnp.zeros_like(r)

        v0, v1 = v0_ref[...], v1_ref[...]
        (vh0, vh1), (vn0, vn1), rstd = _sgu_norm(v0, v1, lng_ref, lnb_ref)
        causal = (lax.broadcasted_iota(jnp.int32, (CHUNK, CHUNK), 0) >= lax.broadcasted_iota(jnp.int32, (CHUNK, CHUNK), 1))
        for half, (vn, u_ref, gb_ref) in enumerate(((vn0, u0_ref, gb0_ref), (vn1, u1_ref, gb1_ref))):
            vnb = vn.astype(BF16)
            u = u_ref[...]
            ug, ug_grad = _gelu_and_grad(u)
            silu_gb, silu_gb_grad = _silu_and_grad(gb_ref[...])
            dyb = dyb_ref[:, CB * half:CB * (half + 1)]
            dyb0 = dyb * silu_gb
            ds = dyb0 * ug
            sp_cols = []
            for hh in range(4):
                h = 4 * half + hh
                cols = slice(128 * hh, 128 * (hh + 1))
                sp_rows = []
                for c in range(tb // CHUNK):
                    rows = slice(CHUNK * c, CHUNK * (c + 1))
                    vt = vnb[rows, cols]
                    sp_rows.append(_dot(ws_ref[h], vt) + bsx_ref[h])
                    dst = ds[rows, cols]
                    dstb = dst.astype(BF16)
                    dbs_ref[h] += dst
                    dws_ref[h] += jnp.where(causal, _dot(dstb, vt, NT), 0.0)
                    dvn[rows, CB * half + 128 * hh:CB * half + 128 * (hh + 1)] = _dot(wst_ref[h], dstb)
                sp_cols.append(jnp.concatenate(sp_rows, axis=0))
            sp = jnp.concatenate(sp_cols, axis=1)
            dz_ref[:, CB * half:CB * (half + 1)] = (dyb0 * sp * ug_grad).astype(BF16)
            dz_ref[:, 2 * SGU_W + CB * half:2 * SGU_W + CB * (half + 1)] = (dyb * (ug * sp) * silu_gb_grad).astype(BF16)

        dvn0, dvn1 = dvn[:, 0:CB], dvn[:, CB:]
        dlng_ref[:, 0:CB] += _rowsum(dvn0 * vh0)
        dlng_ref[:, CB:] += _rowsum(dvn1 * vh1)
        dlnb_ref[:, 0:CB] += _rowsum(dvn0)
        dlnb_ref[:, CB:] += _rowsum(dvn1)
        dh0, dh1 = dvn0 * lng_ref[:, 0:CB], dvn1 * lng_ref[:, CB:]
        m1 = (jnp.sum(dh0, axis=-1, keepdims=True) + jnp.sum(dh1, axis=-1, keepdims=True)) * (1.0 / SGU_W)
        m2 = (jnp.sum(dh0 * vh0, axis=-1, keepdims=True) + jnp.sum(dh1 * vh1, axis=-1, keepdims=True)) * (1.0 / SGU_W)
        dz_ref[:, SGU_W:SGU_W + CB] = (rstd * (dh0 - m1 - vh0 * m2) * _gelu_grad(v0)).astype(BF16)
        dz_ref[:, SGU_W + CB:2 * SGU_W] = (rstd * (dh1 - m1 - vh1 * m2) * _gelu_grad(v1)).astype(BF16)

    zb = lambda j: pl.BlockSpec((tb, CB), lambda i, j=j: (i, j))
    const = lambda shape: pl.BlockSpec(shape, lambda i: (0,) * len(shape))
    hmat = (SGU_HEADS, CHUNK, CHUNK)
    return pl.pallas_call(
        body,
        name="sgu_bwd",
        grid=(t // tb,),
        in_specs=[pl.BlockSpec((tb, SGU_W), lambda i: (i, 0)),
                  zb(CB_U), zb(CB_U + 1), zb(CB_V), zb(CB_V + 1), zb(CB_GB), zb(CB_GB + 1),
                  const((1, SGU_W)), const((1, SGU_W)), const(hmat), const(hmat), const(hmat)],
        out_specs=[pl.BlockSpec((tb, 3 * SGU_W), lambda i: (i, 0)),
                   const((1, SGU_W)), const((1, SGU_W)), const(hmat), const(hmat)],
        out_shape=[jax.ShapeDtypeStruct((t, 3 * SGU_W), BF16),
                   jax.ShapeDtypeStruct((1, SGU_W), F32), jax.ShapeDtypeStruct((1, SGU_W), F32),
                   jax.ShapeDtypeStruct(hmat, F32), jax.ShapeDtypeStruct(hmat, F32)],
        scratch_shapes=[pltpu.VMEM((tb, SGU_W), F32)],
        compiler_params=_params("arbitrary"),
    )(dyb, z, z, z, z, z, z, lng, lnb, ws, wst, bsx)


def _window_sums(ext, lookahead):
    n = ext.shape[0]
    out = []
    for gi, w in enumerate(POOL_WINDOWS):
        acc = ext[:, 128 * gi:128 * (gi + 1)]
        k = 1
        while k < w:
            acc = acc + pltpu.roll(acc, (n - k) if lookahead else k, 0)
            k *= 2
        out.append(acc)
    return jnp.concatenate(out, axis=1)


def _pool_counts(row0, tb):
    pos = (row0 + 1 + lax.broadcasted_iota(jnp.int32, (tb, POOL_W), 0)).astype(F32)
    lane = lax.broadcasted_iota(jnp.int32, (tb, POOL_W), 1)
    win = jnp.where(lane < 128, 2.0, jnp.where(lane < 256, 4.0, jnp.where(lane < 384, 8.0, 16.0)))
    return jnp.minimum(pos, win)


def _pool_fwd(z, wpool, scale, tb=256):
    t = z.shape[0]

    def body(xc_ref, gc_ref, wp_ref, sc_ref, yc_ref, halo):
        i = pl.program_id(0)

        @pl.when(i == 0)
        def _():
            halo[...] = jnp.zeros_like(halo)

        xc = xc_ref[...]
        sums = _window_sums(jnp.concatenate([halo[...], xc], axis=0), lookahead=False)[HALO:, :]
        halo[...] = xc[tb - HALO:, :]
        pb = (sums / _pool_counts(i * tb, tb) - xc).astype(BF16)
        q = jnp.concatenate([_dot(pb[:, 128 * gi:128 * (gi + 1)], wp_ref[gi]) for gi in range(4)], axis=1)
        silu_gc, _ = _silu_and_grad(gc_ref[...])
        yc_ref[...] = (q * sc_ref[...] * silu_gc).astype(BF16)

    const = lambda shape: pl.BlockSpec(shape, lambda i: (0,) * len(shape))
    return pl.pallas_call(
        body,
        name="pool_fwd",
        grid=(t // tb,),
        in_specs=[pl.BlockSpec((tb, CB), lambda i: (i, CB_XC)), pl.BlockSpec((tb, CB), lambda i: (i, CB_GC)),
                  const((4, 128, 128)), const((1, POOL_W))],
        out_specs=pl.BlockSpec((tb, POOL_W), lambda i: (i, 0)),
        out_shape=jax.ShapeDtypeStruct((t, POOL_W), BF16),
        scratch_shapes=[pltpu.VMEM((HALO, POOL_W), F32)],
        compiler_params=_params("arbitrary"),
    )(z, z, wpool, scale)


def _pool_bwd(dyc, z, wpool, scale, tb=256):
    t = z.shape[0]
    nb = t // tb
    rev = lambda i: nb - 1 - i
    per_halo = tb // HALO

    def body(dyc_ref, xc_ref, xp_ref, gc_ref, wp_ref, sc_ref, dz_ref, dwp_ref, dsc_ref, ehalo):
        i = pl.program_id(0)

        @pl.when(i == 0)
        def _():
            ehalo[...] = jnp.zeros_like(ehalo)
            dwp_ref[...] = jnp.zeros_like(dwp_ref)
            dsc_ref[...] = jnp.zeros_like(dsc_ref)

        xc = xc_ref[...]
        prev = jnp.where(i < nb - 1, xp_ref[...], 0.0)
        sums = _window_sums(jnp.concatenate([prev, xc], axis=0), lookahead=False)[HALO:, :]
        cnt = _pool_counts(rev(i) * tb, tb)
        pb = (sums / cnt - xc).astype(BF16)
        q = jnp.concatenate([_dot(pb[:, 128 * gi:128 * (gi + 1)], wp_ref[gi]) for gi in range(4)], axis=1)
        silu_gc, silu_gc_grad = _silu_and_grad(gc_ref[...])
        dyc = dyc_ref[...]
        dz_ref[:, POOL_W:] = (dyc * (q * sc_ref[...]) * silu_gc_grad).astype(BF16)
        dyc0 = dyc * silu_gc
        dsc_ref[...] += _rowsum(dyc0 * q)
        dqb = (dyc0 * sc_ref[...]).astype(BF16)
        dp_cols = []
        for gi in range(4):
            cols = slice(128 * gi, 128 * (gi + 1))
            dwp_ref[gi] += _dot(pb[:, cols], dqb[:, cols], TN)
            dp_cols.append(_dot(dqb[:, cols], wp_ref[gi], NT))
        dp = jnp.concatenate(dp_cols, axis=1)
        e = dp / cnt
        fut = _window_sums(jnp.concatenate([e, ehalo[...]], axis=0), lookahead=True)[:tb, :]
        ehalo[...] = e[:HALO, :]
        dz_ref[:, 0:POOL_W] = (fut - dp).astype(BF16)

    const = lambda shape: pl.BlockSpec(shape, lambda i: (0,) * len(shape))
    return pl.pallas_call(
        body,
        name="pool_bwd",
        grid=(nb,),
        in_specs=[pl.BlockSpec((tb, POOL_W), lambda i: (rev(i), 0)),
                  pl.BlockSpec((tb, CB), lambda i: (rev(i), CB_XC)),
                  pl.BlockSpec((HALO, CB), lambda i: (jnp.maximum(rev(i) * per_halo - 1, 0), CB_XC)),
                  pl.BlockSpec((tb, CB), lambda i: (rev(i), CB_GC)),
                  const((4, 128, 128)), const((1, POOL_W))],
        out_specs=[pl.BlockSpec((tb, 2 * POOL_W), lambda i: (rev(i), 0)), const((4, 128, 128)), const((1, POOL_W))],
        out_shape=[jax.ShapeDtypeStruct((t, 2 * POOL_W), BF16),
                   jax.ShapeDtypeStruct((4, 128, 128), F32), jax.ShapeDtypeStruct((1, POOL_W), F32)],
        scratch_shapes=[pltpu.VMEM((HALO, POOL_W), F32)],
        compiler_params=_params("arbitrary"),
    )(dyc, z, z, z, wpool, scale)


def _adam_math(w, m, v, g):
    nm = ADAM_B1 * m + (1.0 - ADAM_B1) * g
    nv = ADAM_B2 * v + (1.0 - ADAM_B2) * (g * g)
    m_hat = nm / (1.0 - ADAM_B1 ** ADAM_STEP)
    v_hat = nv / (1.0 - ADAM_B2 ** ADAM_STEP)
    return -ADAM_LR * (m_hat / (jnp.sqrt(v_hat) + ADAM_EPS) + ADAM_WD * w), nm, nv


def _sum_small(r_a, r_b, steps=7):
    rows_a, rows_b = r_a.shape[1], r_b.shape[1]
    tr = rows_a // steps

    def body(a_ref, b_ref, ga_ref, gb_ref):
        for src, dst in ((a_ref, ga_ref), (b_ref, gb_ref)):
            g = src[0].astype(F32)
            for k in range(1, N_DEV):
                g = g + src[k].astype(F32)
            dst[...] = g

    return pl.pallas_call(
        body,
        name="sum_small",
        grid=(steps,),
        in_specs=[pl.BlockSpec((N_DEV, tr, 128), lambda i: (0, i, 0)), pl.BlockSpec((N_DEV, rows_b, 128), lambda i: (0, 0, 0))],
        out_specs=[pl.BlockSpec((tr, 128), lambda i: (i, 0)), pl.BlockSpec((rows_b, 128), lambda i: (0, 0))],
        out_shape=[jax.ShapeDtypeStruct((rows_a, 128), F32), jax.ShapeDtypeStruct((rows_b, 128), F32)],
        compiler_params=_params("arbitrary"),
    )(r_a, r_b)


def _adamw_small(ws, ms, vs, gs):
    n = len(ws)
    whole = pl.BlockSpec(memory_space=pltpu.VMEM)

    def body(*refs):
        ins, outs = refs[:4 * n], refs[4 * n:]
        for i in range(n):
            w, m, v, g = (ins[4 * i + j][...] for j in range(4))
            outs[3 * i][...], outs[3 * i + 1][...], outs[3 * i + 2][...] = _adam_math(w, m, v, g)

    return pl.pallas_call(
        body,
        name="adamw_small",
        in_specs=[whole] * (4 * n),
        out_specs=[whole] * (3 * n),
        out_shape=[jax.ShapeDtypeStruct(w.shape, F32) for w in ws for _ in range(3)],
        compiler_params=pltpu.CompilerParams(vmem_limit_bytes=VMEM_LIMIT_WHOLE),
    )(*[a for group in zip(ws, ms, vs, gs) for a in group])


def _adamw(w, m, v, parts, tr, name):
    r, c = w.shape
    n_slab = len(parts)
    per_slab = r // n_slab // tr

    def body(w_ref, m_ref, v_ref, *refs):
        p_refs, (g_ref, d_ref, nm_ref, nv_ref) = refs[:n_slab], refs[n_slab:]
        for s, p_ref in enumerate(p_refs):
            @pl.when(pl.program_id(0) // per_slab == s)
            def _(p_ref=p_ref):
                g = p_ref[0].astype(F32)
                for k in range(1, N_DEV):
                    g = g + p_ref[k].astype(F32)
                g_ref[...] = g

        d_ref[...], nm_ref[...], nv_ref[...] = _adam_math(w_ref[...], m_ref[...], v_ref[...], g_ref[...])

    blk = pl.BlockSpec((tr, c), lambda i: (i, 0))
    slab = lambda s: pl.BlockSpec((N_DEV, tr, c), lambda i, s=s: (0, jnp.clip(i - s * per_slab, 0, per_slab - 1), 0))
    return pl.pallas_call(
        body,
        name=name,
        grid=(r // tr,),
        in_specs=[blk, blk, blk] + [slab(s) for s in range(n_slab)],
        out_specs=[blk, blk, blk, blk],
        out_shape=[jax.ShapeDtypeStruct((r, c), F32)] * 4,
        compiler_params=_params("arbitrary"),
    )(w, m, v, *parts)


MESH = pl.DeviceIdType.MESH
ANY = pl.BlockSpec(memory_space=pl.ANY)


def _dev_index(dev):
    return 4 * dev[0] + 2 * dev[1] + dev[2]


WHOLE_SHAPES = ((D_MODEL, IN_COLS), (D_MODEL, D_MODEL), (S5_W, S5_W))
SHARD_SHAPES =((D_MODEL, SH_IN), (SH_OUT, D_MODEL), (SH_GLU, S5_W))


def _shard_of(ref, ti, idx):
    if ti == 0:
        return ref.at[:, pl.ds(pl.multiple_of(idx * SH_IN, 128), SH_IN)]
    rows = SHARD_SHAPES[ti][0]
    return ref.at[pl.ds(pl.multiple_of(idx * rows, rows), rows), :]


def _peer(mask, x, y, c):
    return (1 - x if mask & 4 else x, 1 - y if mask & 2 else y, 1 - c if mask & 1 else c)


def _allgather_weights(shards, next_shards):
    n_t = 3

    def body(wi_ref, wo_ref, wg_ref, ni_ref, no_ref, ng_ref, gi_ref, go_ref, gg_ref, li_ref, lo_ref, lg_ref,
             send_sems, recv_sems, local_sems):
        x, y, c = lax.axis_index("x"), lax.axis_index("y"), lax.axis_index("c")
        me, sibling = (x, y, c), (x, y, 1 - c)
        chips = [(1 - x, y), (x, 1 - y), (1 - x, 1 - y)]
        shards = (wi_ref, wo_ref, wg_ref)
        wholes = (gi_ref, go_ref, gg_ref)
        nexts, lands = (ni_ref, no_ref, ng_ref), (li_ref, lo_ref, lg_ref)

        def slot(ti, dev):
            return _shard_of(wholes[ti], ti, _dev_index(dev))

        def copy(k, ti, block, to, own=False):
            return pltpu.make_async_remote_copy(
                src_ref=shards[ti] if own else slot(ti, block), dst_ref=slot(ti, block),
                send_sem=send_sems.at[n_t * k + ti], recv_sem=recv_sems.at[n_t * k + ti],
                device_id=to, device_id_type=MESH)

        mine = [pltpu.make_async_copy(shards[ti], slot(ti, me), local_sems.at[ti]) for ti in range(n_t)]
        mine += [pltpu.make_async_copy(nexts[ti], _shard_of(lands[ti], ti, _dev_index(me)), local_sems.at[n_t + ti])
                 for ti in range(n_t)]
        for cp in mine:
            cp.start()
        first = [copy(0, ti, me, sibling, own=True) for ti in range(n_t)]
        first += [copy(1 + j, ti, me, (*chip, c), own=True) for j, chip in enumerate(chips) for ti in range(n_t)]
        for cp in first:
            cp.start()
        passed = []
        for j, chip in enumerate(chips):
            for ti in range(n_t):
                copy(1 + j, ti, (*chip, c), me).wait_recv()
            onward = [copy(4 + j, ti, (*chip, c), sibling) for ti in range(n_t)]
            for cp in onward:
                cp.start()
            passed += onward
        for ti in range(n_t):
            copy(0, ti, sibling, me).wait_recv()
        for j, chip in enumerate(chips):
            for ti in range(n_t):
                copy(4 + j, ti, (*chip, 1 - c), me).wait_recv()
        for cp in first + passed:
            cp.wait_send()
        for cp in mine:
            cp.wait()

    res = pl.pallas_call(
        body,
        name="allgather_weights",
        in_specs=[ANY] * (2 * n_t),
        out_specs=[ANY] * (2 * n_t),
        out_shape=[jax.ShapeDtypeStruct(s, BF16) for s in WHOLE_SHAPES] * 2,
        scratch_shapes=[pltpu.SemaphoreType.DMA((7 * n_t,)), pltpu.SemaphoreType.DMA((7 * n_t,)),
                        pltpu.SemaphoreType.DMA((2 * n_t,))],
    )(*shards, *next_shards)
    return res[:n_t], res[n_t:]


HBM = pl.BlockSpec(memory_space=pltpu.HBM)
SEM = pl.BlockSpec(memory_space=pltpu.SEMAPHORE)
GATHER, SCATTER, SHARE = "gather", "scatter", "share"


def _split_route(kind, ti, sending, me_idx, p_idx, src_ref, land_ref):
    owner = me_idx if sending else p_idx
    if kind == GATHER:
        return src_ref, _shard_of(land_ref, ti, owner)
    if kind == SCATTER:
        return _shard_of(src_ref, ti, p_idx), land_ref.at[owner]
    return src_ref, land_ref.at[owner]


def _split_start(name, srcs, lands, kinds, after=None):
    n = len(srcs)
    arrays = list(srcs) + list(lands) + ([] if after is None else [after])

    def body(*refs):
        src_refs, land_refs = refs[0:n], refs[n:2 * n]
        send_sems, recv_sems, token = refs[len(arrays)], refs[len(arrays) + 1], refs[-1]
        x, y, c = lax.axis_index("x"), lax.axis_index("y"), lax.axis_index("c")
        me_idx = _dev_index((x, y, c))
        for mask in range(1, N_DEV):
            p = _peer(mask, x, y, c)
            for i, (kind, ti) in enumerate(kinds):
                k = (mask - 1) * n + i
                src, dst = _split_route(kind, ti, True, me_idx, _dev_index(p), src_refs[i], land_refs[i])
                pltpu.make_async_remote_copy(src_ref=src, dst_ref=dst, send_sem=send_sems.at[k], recv_sem=recv_sems.at[k],
                                             device_id=p, device_id_type=MESH).start()
        token[...] = jnp.zeros_like(token)

    n_copies = (N_DEV - 1) * n
    return pl.pallas_call(
        body,
        name=name,
        in_specs=[HBM] * len(arrays),
        out_specs=(SEM, SEM) + (HBM,) * (2 * n) + (pl.BlockSpec(memory_space=pltpu.VMEM),),
        out_shape=(pltpu.SemaphoreType.DMA((n_copies,)), pltpu.SemaphoreType.DMA((n_copies,)))
        + tuple(pltpu.HBM(a.shape, a.dtype) for a in arrays[:2 * n]) + (jax.ShapeDtypeStruct((8, 128), F32),),
        input_output_aliases={i: 2 + i for i in range(2 * n)},
        compiler_params=pltpu.CompilerParams(has_side_effects=pltpu.SideEffectType.DATAFLOW_SIDE_EFFECTING),
    )(*[pltpu.with_memory_space_constraint(a, pltpu.HBM) for a in arrays])


def _split_wait(name, started, kinds, after):
    n = len(kinds)
    send_sems, recv_sems, thru = started[0], started[1], started[2:2 + 2 * n]

    def body(*refs):
        src_refs, land_refs = refs[0:n], refs[n:2 * n]
        send_sems, recv_sems = refs[2 * n], refs[2 * n + 1]
        x, y, c = lax.axis_index("x"), lax.axis_index("y"), lax.axis_index("c")
        me_idx = _dev_index((x, y, c))
        for mask in range(1, N_DEV):
            p = _peer(mask, x, y, c)
            for i, (kind, ti) in enumerate(kinds):
                k = (mask - 1) * n + i
                src, dst = _split_route(kind, ti, False, me_idx, _dev_index(p), src_refs[i], land_refs[i])
                cp = pltpu.make_async_remote_copy(src_ref=src, dst_ref=dst, send_sem=send_sems.at[k],
                                                  recv_sem=recv_sems.at[k], device_id=p, device_id_type=MESH)
                cp.wait_send()
                cp.wait_recv()

    res = pl.pallas_call(
        body,
        name=name,
        in_specs=[HBM] * (2 * n) + [SEM, SEM, pl.BlockSpec(memory_space=pl.ANY)],
        out_specs=(HBM,) * (2 * n),
        out_shape=tuple(pltpu.HBM(a.shape, a.dtype) for a in thru),
        input_output_aliases={i: i for i in range(2 * n)},
        compiler_params=pltpu.CompilerParams(has_side_effects=pltpu.SideEffectType.DATAFLOW_SIDE_EFFECTING),
    )(*thru, send_sems, recv_sems, after)
    return res[n:2 * n]


def _landing_zones(srcs, kinds):
    me = _own_index()
    lands = []
    for src, (kind, ti) in zip(srcs, kinds):
        start = None if kind == SHARE else ((0, me * SH_IN) if ti == 0 else (me * SHARD_SHAPES[ti][0], 0))
        if kind == GATHER:
            lands.append(lax.dynamic_update_slice(lax.empty(WHOLE_SHAPES[ti], src.dtype), src, start))
            continue
        own = src if kind == SHARE else lax.dynamic_slice(src, start, SHARD_SHAPES[ti])
        lands.append(lax.dynamic_update_slice(lax.empty((N_DEV,) + own.shape, src.dtype), own[None], (me, 0, 0)))
    return lands


def _share_small(buf):
    def body(b_ref, o_ref, send_sems, recv_sems, local_sem):
        x, y, c = lax.axis_index("x"), lax.axis_index("y"), lax.axis_index("c")
        me_idx = _dev_index((x, y, c))
        local = pltpu.make_async_copy(b_ref, o_ref.at[me_idx], local_sem)
        local.start()

        def copy(mask, owner):
            return pltpu.make_async_remote_copy(
                src_ref=b_ref, dst_ref=o_ref.at[owner], send_sem=send_sems.at[mask - 1], recv_sem=recv_sems.at[mask - 1],
                device_id=_peer(mask, x, y, c), device_id_type=MESH)

        sends = [copy(mask, me_idx) for mask in range(1, N_DEV)]
        for cp in sends:
            cp.start()
        for mask in range(1, N_DEV):
            copy(mask, _dev_index(_peer(mask, x, y, c))).wait_recv()
        for cp in sends:
            cp.wait_send()
        local.wait()

    return pl.pallas_call(
        body,
        name="share_small",
        in_specs=[ANY],
        out_specs=ANY,
        out_shape=jax.ShapeDtypeStruct((N_DEV,) + buf.shape, buf.dtype),
        scratch_shapes=[pltpu.SemaphoreType.DMA((N_DEV - 1,)), pltpu.SemaphoreType.DMA((N_DEV - 1,)),
                        pltpu.SemaphoreType.DMA],
    )(buf)


def _s5_prep(lam_re, lam_im, b_re, b_im, c_re, c_im, d_skip, log_dt):
    dt = jnp.exp(log_dt)[:, None]
    mag = jnp.exp(lam_re * dt)
    a_re, a_im = mag * jnp.cos(lam_im * dt), mag * jnp.sin(lam_im * dt)
    den = lam_re * lam_re + lam_im * lam_im
    f_re = ((a_re - 1.0) * lam_re + a_im * lam_im) / den
    f_im = (a_im * lam_re - (a_re - 1.0) * lam_im) / den
    bb_re = f_re[..., None] * b_re - f_im[..., None] * b_im
    bb_im = f_re[..., None] * b_im + f_im[..., None] * b_re
    eye = jnp.eye(8, dtype=F32)

    def in_map(bb):
        return jnp.einsum("jgph,gk->ghjkp", bb.reshape(4, 8, S5_STATE, S5_CH), eye).reshape(128, N_STATE)

    def out_map(cm):
        return jnp.einsum("jghp,gk->ghjkp", cm.reshape(4, 8, S5_CH, S5_STATE), eye).reshape(128, N_STATE)

    a = jnp.concatenate([a_re.reshape(N_PAIR, 128), a_im.reshape(N_PAIR, 128)])
    bc = jnp.concatenate([in_map(bb_re), in_map(bb_im)], axis=1)
    cct = jnp.concatenate([out_map(c_re), out_map(-c_im)], axis=1)
    return a, bc, cct, d_skip.reshape(1, S5_W)


WEIGHTS = ["norm_g", "w_in", "lam_re", "lam_im", "b_re", "b_im", "c_re", "c_im", "d_skip", "log_dt", "w_glu", "b_glu",
           "ln_g", "ln_b", "w_s", "b_s", "w_pool", "pool_scale", "w_out", "final_g"]
SHARDED = ("w_in", "w_glu", "w_out")
SMALL = [n for n in WEIGHTS if n not in SHARDED]
INPUTS = ["x"] + WEIGHTS + ["loss_target"] + ["m_" + n for n in WEIGHTS] + ["v_" + n for n in WEIGHTS]
SMALL_B = ["norm_g", "final_g"]
SMALL_A = [n for n in SMALL if n not in SMALL_B]
SMALL_TILE = 16 * 128
LOSS_AT = (DEPTH + 1) * D_MODEL


def _pack_small(arrays, dtype):
    flat = jnp.concatenate([a.reshape(-1) for a in arrays])
    pad = (-flat.shape[0]) % SMALL_TILE
    return jnp.pad(flat, (0, pad)).astype(dtype).reshape(-1, 128)


def _unpack_small(packed, like):
    flat = packed.reshape(-1)
    out, off = [], 0
    for a in like:
        out.append(flat[off:off + a.size].reshape(a.shape))
        off += a.size
    return out


def _layer_fwd(p, l, x, wi, wo, wg, token=None):
    row = lambda v: v.reshape(1, -1)
    causal = jnp.tril(jnp.ones((CHUNK, CHUNK), dtype=bool))
    (a, bc, cct, dvec), prep_vjp = jax.vjp(
        _s5_prep, p["lam_re"][l], p["lam_im"][l], p["b_re"][l], p["b_im"][l], p["c_re"][l], p["c_im"][l],
        p["d_skip"][l], p["log_dt"][l])
    ws_f32 = jnp.where(causal[None], p["w_s"][l], 0.0)
    pm, pt = _step_major(S5_TB)
    c = dict(
        x=x, wi=wi, wo=wo, wg=wg, a=a, bc=bc.astype(BF16), cc=cct.T.astype(BF16), dvec=dvec, prep_vjp=prep_vjp,
        pm=pm, pt=pt,
        ws=ws_f32.astype(BF16), wst=jnp.swapaxes(ws_f32, 1, 2).astype(BF16),
        bsx=jnp.broadcast_to(p["b_s"][l][:, :, None], (SGU_HEADS, CHUNK, 128)),
        wpool=p["w_pool"][l].astype(BF16), scale=row(p["pool_scale"][l]),
        lng=row(p["ln_g"][l]), lnb=row(p["ln_b"][l]), bglu=row(p["b_glu"][l]), norm_g=row(p["norm_g"][l]))
    c["z"], c["h"] = _rms_inproj(x, c["norm_g"], wi, token)
    c["ya"], c["s"], c["ys"] = _s5_fwd(c["z"], c["bc"], c["cc"], a, pm, pt, dvec, wg, c["bglu"], S5_TB)
    c["yb"] = _sgu_fwd(c["z"], c["lng"], c["lnb"], c["ws"], c["bsx"])
    c["yc"] = _pool_fwd(c["z"], c["wpool"], c["scale"])
    return _outproj(x, c["ya"], c["yb"], c["yc"], wo), c


def _mixers_bwd(c, dx, token=None):
    dya, dyb, dyc = _outproj_bwd(dx, c["wo"], token)
    dwo = _wgrad_out(c["ya"], c["yb"], c["yc"], dx, token)
    dz_s5, dbc, dcct, da, dd, dwg, dbglu = _s5_bwd(
        dya, c["ys"], c["z"], c["s"], c["bc"], c["cc"], c["a"], c["pm"], c["pt"], c["dvec"], c["wg"], c["bglu"], S5_TB)
    dz_sgu, dlng, dlnb, dws, dbsx = _sgu_bwd(dyb, c["z"], c["lng"], c["lnb"], c["ws"], c["wst"], c["bsx"])
    dz_pool, dwp, dsc = _pool_bwd(dyc, c["z"], c["wpool"], c["scale"])
    g_lam_re, g_lam_im, g_b_re, g_b_im, g_c_re, g_c_im, g_d, g_dt = c["prep_vjp"]((da, dbc, dcct, dd))
    small = dict(lam_re=g_lam_re, lam_im=g_lam_im, b_re=g_b_re, b_im=g_b_im, c_re=g_c_re,
                 c_im=g_c_im, d_skip=g_d, log_dt=g_dt, b_glu=dbglu.reshape(-1), ln_g=dlng.reshape(-1),
                 ln_b=dlnb.reshape(-1), w_s=dws, b_s=jnp.sum(dbsx, axis=-1), w_pool=dwp, pool_scale=dsc.reshape(-1))
    return (dz_s5, dz_sgu, dz_pool), dwo, dwg, small


def _inproj_grads(c, dz, dx, token_w=None, token_x=None):
    dwi = _wgrad_in(c["h"], *dz, token_w)
    dx, dnorm = _inproj_bwd(*dz, c["wi"], c["x"], c["norm_g"], dx, token_x)
    return dwi, dx, dnorm.reshape(-1)


def _own_index():
    return _dev_index((lax.axis_index("x"), lax.axis_index("y"), lax.axis_index("c")))


def kernel(x, norm_g, w_in, lam_re, lam_im, b_re, b_im, c_re, c_im, d_skip, log_dt, w_glu, b_glu, ln_g, ln_b, w_s, b_s, w_pool, pool_scale, w_out, final_g, loss_target, m_norm_g, m_w_in, m_lam_re, m_lam_im, m_b_re, m_b_im, m_c_re, m_c_im, m_d_skip, m_log_dt, m_w_glu, m_b_glu, m_ln_g, m_ln_b, m_w_s, m_b_s, m_w_pool, m_pool_scale, m_w_out, m_final_g, v_norm_g, v_w_in, v_lam_re, v_lam_im, v_b_re, v_b_im, v_c_re, v_c_im, v_d_skip, v_log_dt, v_w_glu, v_b_glu, v_ln_g, v_ln_b, v_w_s, v_b_s, v_w_pool, v_pool_scale, v_w_out, v_final_g):
    p = dict(zip(INPUTS, (x, norm_g, w_in, lam_re, lam_im, b_re, b_im, c_re, c_im, d_skip, log_dt, w_glu, b_glu, ln_g, ln_b, w_s, b_s, w_pool, pool_scale, w_out, final_g, loss_target, m_norm_g, m_w_in, m_lam_re, m_lam_im, m_b_re, m_b_im, m_c_re, m_c_im, m_d_skip, m_log_dt, m_w_glu, m_b_glu, m_ln_g, m_ln_b, m_w_s, m_b_s, m_w_pool, m_pool_scale, m_w_out, m_final_g, v_norm_g, v_w_in, v_lam_re, v_lam_im, v_b_re, v_b_im, v_c_re, v_c_im, v_d_skip, v_log_dt, v_w_glu, v_b_glu, v_ln_g, v_ln_b, v_w_s, v_b_s, v_w_pool, v_pool_scale, v_w_out, v_final_g)))

    shards = [[w[l].astype(BF16) for w in (w_in, w_out, w_glu)] for l in range(DEPTH)]
    gather3 = [(GATHER, ti) for ti in range(3)]
    scatter3 = [(SCATTER, ti) for ti in range(3)]

    whole0, lands = _allgather_weights(shards[0], shards[1])
    gather1 = _split_start("gather1_start", shards[1], lands, gather3)
    x1, c0 = _layer_fwd(p, 0, x[0], *whole0, token=gather1[-1])
    whole1 = _split_wait("gather1_wait", gather1, gather3, x1)
    x2, c1 = _layer_fwd(p, 1, x1, *whole1)

    dx, loss_tile, dfinal = _final_loss(x2, final_g.reshape(1, -1), loss_target[0])

    dz1, dwo1, dwg1, small1 = _mixers_bwd(c1, dx)
    dwi1, dx, dnorm1 = _inproj_grads(c1, dz1, dx)
    partials1 = [dwi1, dwo1, dwg1.astype(BF16)]
    grads1 = _split_start("grads1_start", partials1, _landing_zones(partials1, scatter3), scatter3)
    dz0, dwo0, dwg0, small0 = _mixers_bwd(c0, dx, token=grads1[-1])
    small_a = _pack_small([jnp.stack([small0[n], small1[n]]) for n in SMALL_A], BF16)
    srcs_a, kinds_a = [dwo0, dwg0.astype(BF16), small_a], [(SCATTER, 1), (SCATTER, 2), (SHARE, None)]
    grads0a = _split_start("grads0a_start", srcs_a, _landing_zones(srcs_a, kinds_a), kinds_a)
    dwi0 = _wgrad_in(c0["h"], *dz0, grads0a[-1])
    kinds_b = [(SCATTER, 0)]
    grads0b = _split_start("grads0b_start", [dwi0], _landing_zones([dwi0], kinds_b), kinds_b)
    dx, dnorm0 = _inproj_bwd(*dz0, c0["wi"], c0["x"], c0["norm_g"], dx, grads0b[-1])
    parts1 = _split_wait("grads1_wait", grads1, scatter3, dx)
    r_out0, r_glu0, r_a = _split_wait("grads0a_wait", grads0a, kinds_a, dx)
    (r_in0,) = _split_wait("grads0b_wait", grads0b, kinds_b, dx)
    parts0 = [r_in0, r_out0, r_glu0]
    r_b = _share_small(_pack_small(
        [jnp.stack([dnorm0.reshape(-1), dnorm1]), dfinal.reshape(-1), loss_tile[0, 0:1]], F32))
    loss = jnp.sum(r_b.reshape(N_DEV, -1)[:, LOSS_AT])

    out = {}

    def adam(name, ti, tr):
        shape2d = (DEPTH * SHARD_SHAPES[ti][0], SHARD_SHAPES[ti][1])
        res = _adamw(p[name].reshape(shape2d), p["m_" + name].reshape(shape2d), p["v_" + name].reshape(shape2d),
                     [parts0[ti], parts1[ti]], tr, "adamw_" + name)
        out[name] = [r.reshape(p[name].shape) for r in res]

    adam("w_in", 0, 256)
    adam("w_out", 1, 128)
    adam("w_glu", 2, 64)
    g_a, g_b = _sum_small(r_a, r_b)
    grads = dict(zip(SMALL_A, _unpack_small(g_a, [p[n] for n in SMALL_A])))
    grads.update(zip(SMALL_B, _unpack_small(g_b, [p[n] for n in SMALL_B])))
    rank2 = lambda a: a.reshape(1, -1) if a.ndim == 1 else a
    res = _adamw_small(*[[rank2(src[pre + n]) for n in SMALL] for src, pre in ((p, ""), (p, "m_"), (p, "v_"), (grads, ""))])
    for i, n in enumerate(SMALL):
        out[n] = [grads[n]] + [r.reshape(p[n].shape) for r in res[3 * i:3 * i + 3]]

    return (loss, dx[None], *[out[n][0] for n in WEIGHTS], *[out[n][1] for n in WEIGHTS],
            *[out[n][2] for n in WEIGHTS], *[out[n][3] for n in WEIGHTS])
```

```python
import functools
import math

import jax
import jax.numpy as jnp
import numpy as np
from jax import lax
from jax.experimental import pallas as pl
from jax.experimental.pallas import tpu as pltpu

F32 = jnp.float32
BF16 = jnp.bfloat16

D_MODEL = 2048
DEPTH = 2
S5_W, SGU_W, POOL_W = 512, 1024, 512
S5_GROUPS, S5_STATE, S5_CH = 32, 64, 16
N_STATE = S5_GROUPS * S5_STATE
CHUNK = 128
SGU_HEADS = 8
POOL_WINDOWS = (2, 4, 8, 16)
IN_COLS = 5120
RMS_EPS = 1e-6
LN_EPS = 1e-5
ADAM_LR, ADAM_B1, ADAM_B2, ADAM_EPS, ADAM_WD, ADAM_STEP = 0.001, 0.9, 0.999, 1e-08, 0.01, 10

CB = 512
N_CB = IN_COLS // CB
CB_XA, CB_U, CB_V, CB_XC, CB_GA, CB_GB, CB_GC = 0, 1, 3, 5, 6, 7, 9

N_DEV = 8
SH_IN = IN_COLS // N_DEV
SH_OUT = D_MODEL // N_DEV
SH_GLU = S5_W // N_DEV

VMEM_LIMIT = 52 * 1024 * 1024
VMEM_LIMIT_WHOLE = 56 * 1024 * 1024
HALO = 16
LANE_CH = 512

TN = (((0,), (0,)), ((), ()))
NT = (((1,), (1,)), ((), ()))


def _params(*sem):
    return pltpu.CompilerParams(dimension_semantics=sem if sem else None, vmem_limit_bytes=VMEM_LIMIT)


def _dot(a, b, dims=None):
    if dims is None:
        return jnp.dot(a, b, preferred_element_type=F32)
    return lax.dot_general(a, b, dims, preferred_element_type=F32)


_GELU_C = math.sqrt(2.0 / math.pi)


def _gelu(x):
    return 0.5 * x * (1.0 + jnp.tanh(_GELU_C * (x + 0.044715 * x * x * x)))


def _gelu_and_grad(x):
    t = jnp.tanh(_GELU_C * (x + 0.044715 * x * x * x))
    half = 0.5 * (1.0 + t)
    return x * half, half + 0.5 * x * (1.0 - t * t) * _GELU_C * (1.0 + 3.0 * 0.044715 * x * x)


def _gelu_grad(x):
    return _gelu_and_grad(x)[1]


def _silu_and_grad(x):
    s = jax.nn.sigmoid(x)
    return x * s, s * (1.0 + x * (1.0 - s))


def _rowsum(x):
    return jnp.sum(x, axis=0, keepdims=True)


def _after(token):
    if token is None:
        return [], []
    return [pl.BlockSpec((8, 128), lambda *_: (0, 0))], [token]


def _rms_inproj(x, g, w, token=None, tm=1024, tn=1024):
    t = x.shape[0]
    tm = min(tm, t)
    after_specs, after = _after(token)

    def body(x_ref, g_ref, w_ref, *rest):
        z_ref, h_ref = rest[-2:]

        @pl.when(pl.program_id(1) == 0)
        def _():
            xv = x_ref[...]
            r = lax.rsqrt(jnp.mean(xv * xv, axis=-1, keepdims=True) + RMS_EPS)
            h_ref[...] = (xv * r * g_ref[...]).astype(BF16)

        z_ref[...] = _dot(h_ref[...], w_ref[...])

    return pl.pallas_call(
        body,
        name="rms_inproj",
        grid=(t // tm, IN_COLS // tn),
        in_specs=[
            pl.BlockSpec((tm, D_MODEL), lambda m, n: (m, 0)),
            pl.BlockSpec((1, D_MODEL), lambda m, n: (0, 0)),
            pl.BlockSpec((D_MODEL, tn), lambda m, n: (0, n)),
        ] + after_specs,
        out_specs=[
            pl.BlockSpec((tm, tn), lambda m, n: (m, n)),
            pl.BlockSpec((tm, D_MODEL), lambda m, n: (m, 0)),
        ],
        out_shape=[jax.ShapeDtypeStruct((t, IN_COLS), F32), jax.ShapeDtypeStruct((t, D_MODEL), BF16)],
        compiler_params=_params("arbitrary", "arbitrary"),
    )(x, g, w, *after)


def _outproj(x, ya, yb, yc, w, tm=1024, tn=1024):
    t = x.shape[0]
    tm = min(tm, t)

    def body(x_ref, ya_ref, yb_ref, yc_ref, w_ref, o_ref):
        acc = _dot(ya_ref[...], w_ref[0:S5_W, :])
        acc += _dot(yb_ref[...], w_ref[S5_W:S5_W + SGU_W, :])
        acc += _dot(yc_ref[...], w_ref[S5_W + SGU_W:, :])
        o_ref[...] = x_ref[...] + acc

    return pl.pallas_call(
        body,
        name="outproj",
        grid=(t // tm, D_MODEL // tn),
        in_specs=[
            pl.BlockSpec((tm, tn), lambda m, n: (m, n)),
            pl.BlockSpec((tm, S5_W), lambda m, n: (m, 0)),
            pl.BlockSpec((tm, SGU_W), lambda m, n: (m, 0)),
            pl.BlockSpec((tm, POOL_W), lambda m, n: (m, 0)),
            pl.BlockSpec((D_MODEL, tn), lambda m, n: (0, n)),
        ],
        out_specs=pl.BlockSpec((tm, tn), lambda m, n: (m, n)),
        out_shape=jax.ShapeDtypeStruct((t, D_MODEL), F32),
        compiler_params=_params("arbitrary", "arbitrary"),
    )(x, ya, yb, yc, w)


def _outproj_bwd(dx, w, token=None, tm=1024):
    t = dx.shape[0]
    tm = min(tm, t)
    after_specs, after = _after(token)

    def body(dx_ref, w_ref, *rest):
        dya_ref, dyb_ref, dyc_ref = rest[-3:]
        dy = _dot(dx_ref[...].astype(BF16), w_ref[...], NT)
        dya_ref[...] = dy[:, 0:S5_W]
        dyb_ref[...] = dy[:, S5_W:S5_W + SGU_W]
        dyc_ref[...] = dy[:, S5_W + SGU_W:]

    return pl.pallas_call(
        body,
        name="outproj_bwd",
        grid=(t // tm,),
        in_specs=[
            pl.BlockSpec((tm, D_MODEL), lambda m: (m, 0)),
            pl.BlockSpec((D_MODEL, D_MODEL), lambda m: (0, 0)),
        ] + after_specs,
        out_specs=[
            pl.BlockSpec((tm, S5_W), lambda m: (m, 0)),
            pl.BlockSpec((tm, SGU_W), lambda m: (m, 0)),
            pl.BlockSpec((tm, POOL_W), lambda m: (m, 0)),
        ],
        out_shape=[
            jax.ShapeDtypeStruct((t, S5_W), F32),
            jax.ShapeDtypeStruct((t, SGU_W), F32),
            jax.ShapeDtypeStruct((t, POOL_W), F32),
        ],
        compiler_params=_params("arbitrary"),
    )(dx, w, *after)


def _dz_piece_maps():
    s5_map = lambda j: jnp.where(j >= CB_GA, 1, 0)
    sgu_map = lambda j: jnp.clip(jnp.where(j <= 4, j - 1, j - 3), 0, 5)
    pool_map = lambda j: jnp.where(j >= CB_GC, 1, 0)
    return s5_map, sgu_map, pool_map


def _pick_piece(j):
    is_s5 = jnp.logical_or(j == CB_XA, j == CB_GA)
    is_pool = jnp.logical_or(j == CB_XC, j == CB_GC)
    return is_s5, is_pool, jnp.logical_not(jnp.logical_or(is_s5, is_pool))


def _inproj_bwd(dz_s5, dz_sgu, dz_pool, w, x, g, dxo, token=None, tm=512):
    t = x.shape[0]
    s5_map, sgu_map, pool_map = _dz_piece_maps()
    after_specs, after = _after(token)

    def body(s5_ref, sgu_ref, pool_ref, w_ref, x_ref, g_ref, dxo_ref, *rest):
        dx_ref, dg_ref, acc = rest[-3:]
        m, j = pl.program_id(0), pl.program_id(1)

        @pl.when(jnp.logical_and(m == 0, j == 0))
        def _():
            dg_ref[...] = jnp.zeros_like(dg_ref)

        @pl.when(j == 0)
        def _():
            acc[...] = jnp.zeros_like(acc)

        is_s5, is_pool, is_sgu = _pick_piece(j)

        @pl.when(is_s5)
        def _():
            acc[...] += _dot(w_ref[...], s5_ref[...], NT)

        @pl.when(is_sgu)
        def _():
            acc[...] += _dot(w_ref[...], sgu_ref[...], NT)

        @pl.when(is_pool)
        def _():
            acc[...] += _dot(w_ref[...], pool_ref[...], NT)

        @pl.when(j == N_CB - 1)
        def _():
            xv = x_ref[...]
            r = lax.rsqrt(jnp.mean(xv * xv, axis=-1, keepdims=True) + RMS_EPS)
            n = xv * r
            dh = acc[...].T
            dg_ref[...] += _rowsum(dh * n)
            dn = dh * g_ref[...]
            dx_ref[...] = dxo_ref[...] + r * (dn - n * jnp.mean(dn * n, axis=-1, keepdims=True))

    return pl.pallas_call(
        body,
        name="inproj_bwd",
        grid=(t // tm, N_CB),
        in_specs=[
            pl.BlockSpec((tm, CB), lambda m, j: (m, s5_map(j))),
            pl.BlockSpec((tm, CB), lambda m, j: (m, sgu_map(j))),
            pl.BlockSpec((tm, CB), lambda m, j: (m, pool_map(j))),
            pl.BlockSpec((D_MODEL, CB), lambda m, j: (0, j)),
            pl.BlockSpec((tm, D_MODEL), lambda m, j: (m, 0)),
            pl.BlockSpec((1, D_MODEL), lambda m, j: (0, 0)),
            pl.BlockSpec((tm, D_MODEL), lambda m, j: (m, 0)),
        ] + after_specs,
        out_specs=[
            pl.BlockSpec((tm, D_MODEL), lambda m, j: (m, 0)),
            pl.BlockSpec((1, D_MODEL), lambda m, j: (0, 0)),
        ],
        out_shape=[jax.ShapeDtypeStruct((t, D_MODEL), F32), jax.ShapeDtypeStruct((1, D_MODEL), F32)],
        scratch_shapes=[pltpu.VMEM((D_MODEL, tm), F32)],
        compiler_params=_params("arbitrary", "arbitrary"),
    )(dz_s5, dz_sgu, dz_pool, w, x, g, dxo, *after)


def _wgrad_in(h, dz_s5, dz_sgu, dz_pool, token=None, tm=2048):
    t = h.shape[0]
    tm = min(tm, t)
    s5_map, sgu_map, pool_map = _dz_piece_maps()
    after_specs, after = _after(token)

    def body(h_ref, s5_ref, sgu_ref, pool_ref, *rest):
        o_ref, acc = rest[-2:]
        j, m = pl.program_id(0), pl.program_id(1)

        @pl.when(m == 0)
        def _():
            acc[...] = jnp.zeros_like(acc)

        is_s5, is_pool, is_sgu = _pick_piece(j)

        @pl.when(is_s5)
        def _():
            acc[...] += _dot(s5_ref[...], h_ref[...], TN)

        @pl.when(is_sgu)
        def _():
            acc[...] += _dot(sgu_ref[...], h_ref[...], TN)

        @pl.when(is_pool)
        def _():
            acc[...] += _dot(pool_ref[...], h_ref[...], TN)

        @pl.when(m == pl.num_programs(1) - 1)
        def _():
            o_ref[...] = acc[...].T.astype(BF16)

    return pl.pallas_call(
        body,
        name="wgrad_in",
        grid=(N_CB, t // tm),
        in_specs=[
            pl.BlockSpec((tm, D_MODEL), lambda j, m: (m, 0)),
            pl.BlockSpec((tm, CB), lambda j, m: (m, s5_map(j))),
            pl.BlockSpec((tm, CB), lambda j, m: (m, sgu_map(j))),
            pl.BlockSpec((tm, CB), lambda j, m: (m, pool_map(j))),
        ] + after_specs,
        out_specs=pl.BlockSpec((D_MODEL, CB), lambda j, m: (0, j)),
        out_shape=jax.ShapeDtypeStruct((D_MODEL, IN_COLS), BF16),
        scratch_shapes=[pltpu.VMEM((CB, D_MODEL), F32)],
        compiler_params=_params("arbitrary", "arbitrary"),
    )(h, dz_s5, dz_sgu, dz_pool, *after)


def _wgrad_out(ya, yb, yc, dx, token=None, tm=2048, tn=512):
    t = dx.shape[0]
    tm = min(tm, t)
    after_specs, after = _after(token)

    def body(ya_ref, yb_ref, yc_ref, dx_ref, *rest):
        o_ref, acc = rest[-2:]
        m = pl.program_id(1)

        @pl.when(m == 0)
        def _():
            acc[...] = jnp.zeros_like(acc)

        dxb = dx_ref[...].astype(BF16)
        acc[:, 0:S5_W] += _dot(dxb, ya_ref[...], TN)
        acc[:, S5_W:S5_W + SGU_W] += _dot(dxb, yb_ref[...], TN)
        acc[:, S5_W + SGU_W:] += _dot(dxb, yc_ref[...], TN)

        @pl.when(m == pl.num_programs(1) - 1)
        def _():
            o_ref[...] = acc[...].T.astype(BF16)

    return pl.pallas_call(
        body,
        name="wgrad_out",
        grid=(D_MODEL // tn, t // tm),
        in_specs=[
            pl.BlockSpec((tm, S5_W), lambda n, m: (m, 0)),
            pl.BlockSpec((tm, SGU_W), lambda n, m: (m, 0)),
            pl.BlockSpec((tm, POOL_W), lambda n, m: (m, 0)),
            pl.BlockSpec((tm, tn), lambda n, m: (m, n)),
        ] + after_specs,
        out_specs=pl.BlockSpec((D_MODEL, tn), lambda n, m: (0, n)),
        out_shape=jax.ShapeDtypeStruct((D_MODEL, D_MODEL), BF16),
        scratch_shapes=[pltpu.VMEM((tn, D_MODEL), F32)],
        compiler_params=_params("arbitrary", "arbitrary"),
    )(ya, yb, yc, dx, *after)


def _final_loss(x, g, target, tm=512):
    t = x.shape[0]

    def body(x_ref, g_ref, t_ref, dx_ref, loss_ref, dg_ref):
        @pl.when(pl.program_id(0) == 0)
        def _():
            loss_ref[...] = jnp.zeros_like(loss_ref)
            dg_ref[...] = jnp.zeros_like(dg_ref)

        xv = x_ref[...]
        gv = g_ref[...]
        r = lax.rsqrt(jnp.mean(xv * xv, axis=-1, keepdims=True) + RMS_EPS)
        n = xv * r
        err = n * gv - t_ref[...]
        loss_ref[...] += 0.5 * jnp.sum(jnp.mean(err * err, axis=-1, keepdims=True))
        dy = err * (1.0 / D_MODEL)
        dg_ref[...] += _rowsum(dy * n)
        dn = dy * gv
        dx_ref[...] = r * (dn - n * jnp.mean(dn * n, axis=-1, keepdims=True))

    return pl.pallas_call(
        body,
        name="final_loss",
        grid=(t // tm,),
        in_specs=[
            pl.BlockSpec((tm, D_MODEL), lambda m: (m, 0)),
            pl.BlockSpec((1, D_MODEL), lambda m: (0, 0)),
            pl.BlockSpec((tm, D_MODEL), lambda m: (m, 0)),
        ],
        out_specs=[
            pl.BlockSpec((tm, D_MODEL), lambda m: (m, 0)),
            pl.BlockSpec((8, 128), lambda m: (0, 0)),
            pl.BlockSpec((1, D_MODEL), lambda m: (0, 0)),
        ],
        out_shape=[
            jax.ShapeDtypeStruct((t, D_MODEL), F32),
            jax.ShapeDtypeStruct((8, 128), F32),
            jax.ShapeDtypeStruct((1, D_MODEL), F32),
        ],
        compiler_params=_params("arbitrary"),
    )(x, g, target)


N_Q = 2 * N_STATE // LANE_CH
N_LT = 2 * N_STATE // 128
N_PAIR = N_LT // 2
SEG = 8
PAIR_GROUP = 8
S5_TB = 512


def _cmul_add(b_re, b_im, a_re, a_im, s_re, s_im):
    return b_re + (a_re * s_re - a_im * s_im), b_im + (a_re * s_im + a_im * s_re)


def _s5_fill_powers(pw, a_ref, tb, reverse):
    seg_len = tb // SEG
    sign = -1.0 if reverse else 1.0
    for p in range(N_PAIR):
        a_re = jnp.broadcast_to(a_ref[p:p + 1, :], (SEG, 128))
        a_im = sign * jnp.broadcast_to(a_ref[N_PAIR + p:N_PAIR + p + 1, :], (SEG, 128))

        def step(k, c, p=p, a_re=a_re, a_im=a_im):
            rows = pl.ds(pl.multiple_of(((seg_len - 1 - k) if reverse else k) * SEG, SEG), SEG)
            pw[p, rows, :] = c[0]
            pw[N_PAIR + p, rows, :] = c[1]
            return c[0] * a_re - c[1] * a_im, c[0] * a_im + c[1] * a_re

        lax.fori_loop(0, seg_len, step, (a_re, a_im))


def _s5_scan(st, carry, a_ref, pw_ref, tb, reverse):
    seg_len = tb // SEG
    sign = -1.0 if reverse else 1.0
    sub = lax.broadcasted_iota(jnp.int32, (SEG, 128), 0)
    chain = (0 if reverse else seg_len - 1) * SEG
    full = lambda row: jnp.broadcast_to(row, (SEG, 128))
    for p0 in range(0, N_PAIR, PAIR_GROUP):
        pairs = list(range(p0, p0 + PAIR_GROUP))
        a_re = [full(a_ref[p:p + 1, :]) for p in pairs]
        a_im = [sign * full(a_ref[N_PAIR + p:N_PAIR + p + 1, :]) for p in pairs]

        def step(k, c, pairs=pairs, a_re=a_re, a_im=a_im):
            rows = pl.ds(pl.multiple_of(((seg_len - 1 - k) if reverse else k) * SEG, SEG), SEG)
            out = []
            for i, p in enumerate(pairs):
                n_re, n_im = _cmul_add(st[p, rows, :], st[N_PAIR + p, rows, :], a_re[i], a_im[i], c[2 * i], c[2 * i + 1])
                st[p, rows, :] = n_re
                st[N_PAIR + p, rows, :] = n_im
                out += [n_re, n_im]
            return tuple(out)

        ends = lax.fori_loop(0, seg_len, step, tuple(jnp.zeros((SEG, 128), F32) for _ in range(2 * PAIR_GROUP)))
        for i, p in enumerate(pairs):
            e_re, e_im = ends[2 * i], ends[2 * i + 1]
            w_re, w_im = pw_ref[p, chain:chain + SEG, :], pw_ref[N_PAIR + p, chain:chain + SEG, :]
            c_re, c_im = full(carry[p:p + 1, :]), full(carry[N_PAIR + p:N_PAIR + p + 1, :])
            for hop in range(SEG - 1):
                n_re, n_im = _cmul_add(e_re, e_im, w_re, w_im, c_re, c_im)
                target = SEG - 2 - hop if reverse else hop + 1
                shift = SEG - 1 if reverse else 1
                c_re = jnp.where(sub == target, pltpu.roll(n_re, shift, 0), c_re)
                c_im = jnp.where(sub == target, pltpu.roll(n_im, shift, 0), c_im)
            n_re, n_im = _cmul_add(e_re, e_im, w_re, w_im, c_re, c_im)
            last = 0 if reverse else SEG - 1
            carry[p:p + 1, :] = n_re[last:last + 1, :]
            carry[N_PAIR + p:N_PAIR + p + 1, :] = n_im[last:last + 1, :]
            in_re, in_im = jnp.tile(c_re, (seg_len, 1)), jnp.tile(c_im, (seg_len, 1))
            st[p], st[N_PAIR + p] = _cmul_add(st[p], st[N_PAIR + p], pw_ref[p], pw_ref[N_PAIR + p], in_re, in_im)


def _lane_chunk(ref, q):
    return jnp.concatenate([ref[4 * q + i] for i in range(4)], axis=1)


def _put_lane_chunk(ref, q, value):
    for i in range(4):
        ref[4 * q + i] = value[:, 128 * i:128 * (i + 1)]


def _step_major(tb):
    r = np.arange(tb)
    pm = np.zeros((tb, tb), np.float32)
    pm[r, (r % SEG) * (tb // SEG) + r // SEG] = 1.0
    return jnp.asarray(pm, BF16), jnp.asarray(pm.T, BF16)


def _unpermute(pt_ref, v):
    hi = v.astype(BF16)
    rest = v - hi.astype(F32)
    mid = rest.astype(BF16)
    lo = (rest - mid.astype(F32)).astype(BF16)
    return _dot(pt_ref[...], hi) + _dot(pt_ref[...], mid) + _dot(pt_ref[...], lo)


def _s5_fwd(z, bc, cc, a, pm, pt, dvec, wglu, bglu, tb=256):
    t = z.shape[0]

    def body(xa_ref, ga_ref, bc_ref, cc_ref, a_ref, pm_ref, pt_ref, d_ref, wglu_ref, bglu_ref,
             ya_ref, s_ref, ys_ref, st, carry, pw_ref):
        @pl.when(pl.program_id(0) == 0)
        def _():
            carry[...] = jnp.zeros_like(carry)
            _s5_fill_powers(pw_ref, a_ref, tb, reverse=False)

        xa = xa_ref[...]
        xab = _dot(pm_ref[...], xa.astype(BF16)).astype(BF16)
        for q in range(N_Q):
            _put_lane_chunk(st, q, _dot(xab[:, 128 * (q % 4):128 * (q % 4) + 128], bc_ref[:, pl.ds(LANE_CH * q, LANE_CH)]))
        _s5_scan(st, carry, a_ref, pw_ref, tb, reverse=False)
        s_ref[...] = st[...].astype(BF16)
        cols = []
        for j in range(4):
            lo, hi = LANE_CH * j, N_STATE + LANE_CH * j
            cols.append(_dot(_lane_chunk(s_ref, j), cc_ref[lo:lo + LANE_CH, :])
                        + _dot(_lane_chunk(s_ref, 4 + j), cc_ref[hi:hi + LANE_CH, :]))
        ys = _unpermute(pt_ref, jnp.concatenate(cols, axis=1)) + d_ref[...] * xa
        ys_ref[...] = ys
        ya1 = _gelu(ys)
        pre = _dot(ya1.astype(BF16), wglu_ref[...]) + bglu_ref[...]
        silu_ga, _ = _silu_and_grad(ga_ref[...])
        ya_ref[...] = (ya1 * jax.nn.sigmoid(pre) * silu_ga).astype(BF16)

    const = lambda shape: pl.BlockSpec(shape, lambda i: (0,) * len(shape))
    return pl.pallas_call(
        body,
        name="s5_fwd",
        grid=(t // tb,),
        in_specs=[
            pl.BlockSpec((tb, CB), lambda i: (i, CB_XA)),
            pl.BlockSpec((tb, CB), lambda i: (i, CB_GA)),
            const((128, 2 * N_STATE)),
            const((2 * N_STATE, 128)),
            const((N_LT, 128)),
            const((tb, tb)),
            const((tb, tb)),
            const((1, S5_W)),
            const((S5_W, S5_W)),
            const((1, S5_W)),
        ],
        out_specs=[
            pl.BlockSpec((tb, S5_W), lambda i: (i, 0)),
            pl.BlockSpec((N_LT, tb, 128), lambda i: (0, i, 0)),
            pl.BlockSpec((tb, S5_W), lambda i: (i, 0)),
        ],
        out_shape=[
            jax.ShapeDtypeStruct((t, S5_W), BF16),
            jax.ShapeDtypeStruct((N_LT, t, 128), BF16),
            jax.ShapeDtypeStruct((t, S5_W), F32),
        ],
        scratch_shapes=[pltpu.VMEM((N_LT, tb, 128), F32), pltpu.VMEM((N_LT, 128), F32), pltpu.VMEM((N_LT, tb, 128), F32)],
        compiler_params=_params("arbitrary"),
    )(z, z, bc, cc, a, pm, pt, dvec, wglu, bglu)


def _s5_bwd(dya, ys, z, s, bc, cc, a, pm, pt, dvec, wglu, bglu, tb=256):
    t = z.shape[0]
    nb = t // tb
    rev = lambda i: nb - 1 - i

    def body(dya_ref, ys_ref, xa_ref, ga_ref, s_ref, sp_ref, bc_ref, cc_ref, a_ref, pm_ref, pt_ref, d_ref,
             wglu_ref, bglu_ref, dz_ref, dbc_ref, dcct_ref, da_ref, dd_ref, dwglu_ref, dbglu_ref, g, carry, pw_ref):
        i = pl.program_id(0)

        @pl.when(i == 0)
        def _():
            carry[...] = jnp.zeros_like(carry)
            _s5_fill_powers(pw_ref, a_ref, tb, reverse=True)
            for r in (dbc_ref, dcct_ref, da_ref, dd_ref, dwglu_ref, dbglu_ref):
                r[...] = jnp.zeros_like(r)

        ys = ys_ref[...]
        xa = xa_ref[...]
        ga = ga_ref[...]
        dya = dya_ref[...]
        ya1, ya1_grad = _gelu_and_grad(ys)
        ya1b = ya1.astype(BF16)
        sg = jax.nn.sigmoid(_dot(ya1b, wglu_ref[...]) + bglu_ref[...])
        silu_ga, silu_ga_grad = _silu_and_grad(ga)
        dz_ref[:, S5_W:] = (dya * (ya1 * sg) * silu_ga_grad).astype(BF16)
        dya2 = dya * silu_ga
        dpre = dya2 * ya1 * sg * (1.0 - sg)
        dbglu_ref[...] += _rowsum(dpre)
        dpreb = dpre.astype(BF16)
        dwglu_ref[...] += _dot(ya1b, dpreb, TN)
        dys = (dya2 * sg + _dot(dpreb, wglu_ref[...], NT)) * ya1_grad
        dd_ref[...] += _rowsum(dys * xa)
        dysb = _dot(pm_ref[...], dys.astype(BF16)).astype(BF16)
        xab = _dot(pm_ref[...], xa.astype(BF16)).astype(BF16)

        for q in range(N_Q):
            cq = pl.ds(LANE_CH * q, LANE_CH)
            x0 = 128 * (q % 4)
            dcct_ref[:, cq] += _dot(dysb[:, x0:x0 + 128], _lane_chunk(s_ref, q), TN)
            _put_lane_chunk(g, q, _dot(dysb[:, x0:x0 + 128], cc_ref[cq, :], NT))
        _s5_scan(g, carry, a_ref, pw_ref, tb, reverse=True)

        seg0 = (lax.broadcasted_iota(jnp.int32, (SEG, 128), 0) == 0)
        have_prev = i < nb - 1

        def before(tile, halo):
            tile = tile.astype(F32)
            prev_last = jnp.where(have_prev, halo.astype(F32)[HALO - 1:HALO, :], 0.0)
            step0 = jnp.where(seg0, prev_last, pltpu.roll(tile[tb - SEG:, :], 1, 0))
            return jnp.concatenate([step0, tile[:tb - SEG, :]], axis=0)

        for p in range(N_PAIR):
            g_re, g_im = g[p], g[N_PAIR + p]
            sp_re, sp_im = before(s_ref[p], sp_ref[p]), before(s_ref[N_PAIR + p], sp_ref[N_PAIR + p])
            da_ref[p:p + 1, :] += _rowsum(sp_re * g_re + sp_im * g_im)
            da_ref[N_PAIR + p:N_PAIR + p + 1, :] += _rowsum(sp_re * g_im - sp_im * g_re)
        dxa_cols = []
        for j in range(4):
            re = pl.ds(LANE_CH * j, LANE_CH)
            im = pl.ds(N_STATE + LANE_CH * j, LANE_CH)
            x0 = 128 * j
            gb_re, gb_im = _lane_chunk(g, j).astype(BF16), _lane_chunk(g, 4 + j).astype(BF16)
            dbc_ref[:, re] += _dot(xab[:, x0:x0 + 128], gb_re, TN)
            dbc_ref[:, im] += _dot(xab[:, x0:x0 + 128], gb_im, TN)
            dxa_cols.append(_dot(gb_re, bc_ref[:, re], NT) + _dot(gb_im, bc_ref[:, im], NT))
        dz_ref[:, 0:S5_W] = (dys * d_ref[...] + _unpermute(pt_ref, jnp.concatenate(dxa_cols, axis=1))).astype(BF16)

    const = lambda shape: pl.BlockSpec(shape, lambda i: (0,) * len(shape))
    per_halo = tb // HALO
    return pl.pallas_call(
        body,
        name="s5_bwd",
        grid=(nb,),
        in_specs=[
            pl.BlockSpec((tb, S5_W), lambda i: (rev(i), 0)),
            pl.BlockSpec((tb, S5_W), lambda i: (rev(i), 0)),
            pl.BlockSpec((tb, CB), lambda i: (rev(i), CB_XA)),
            pl.BlockSpec((tb, CB), lambda i: (rev(i), CB_GA)),
            pl.BlockSpec((N_LT, tb, 128), lambda i: (0, rev(i), 0)),
            pl.BlockSpec((N_LT, HALO, 128), lambda i: (0, jnp.maximum(rev(i) * per_halo - 1, 0), 0)),
            const((128, 2 * N_STATE)),
            const((2 * N_STATE, 128)),
            const((N_LT, 128)),
            const((tb, tb)),
            const((tb, tb)),
            const((1, S5_W)),
            const((S5_W, S5_W)),
            const((1, S5_W)),
        ],
        out_specs=[
            pl.BlockSpec((tb, 2 * CB), lambda i: (rev(i), 0)),
            const((128, 2 * N_STATE)),
            const((128, 2 * N_STATE)),
            const((N_LT, 128)),
            const((1, S5_W)),
            const((S5_W, S5_W)),
            const((1, S5_W)),
        ],
        out_shape=[
            jax.ShapeDtypeStruct((t, 2 * CB), BF16),
            jax.ShapeDtypeStruct((128, 2 * N_STATE), F32),
            jax.ShapeDtypeStruct((128, 2 * N_STATE), F32),
            jax.ShapeDtypeStruct((N_LT, 128), F32),
            jax.ShapeDtypeStruct((1, S5_W), F32),
            jax.ShapeDtypeStruct((S5_W, S5_W), F32),
            jax.ShapeDtypeStruct((1, S5_W), F32),
        ],
        scratch_shapes=[pltpu.VMEM((N_LT, tb, 128), F32), pltpu.VMEM((N_LT, 128), F32), pltpu.VMEM((N_LT, tb, 128), F32)],
        compiler_params=_params("arbitrary"),
    )(dya, ys, z, z, s, s, bc, cc, a, pm, pt, dvec, wglu, bglu)


def _sgu_norm(v0, v1, lng_ref, lnb_ref):
    g0, g1 = _gelu(v0), _gelu(v1)
    mu = (jnp.sum(g0, axis=-1, keepdims=True) + jnp.sum(g1, axis=-1, keepdims=True)) * (1.0 / SGU_W)
    c0, c1 = g0 - mu, g1 - mu
    var = (jnp.sum(c0 * c0, axis=-1, keepdims=True) + jnp.sum(c1 * c1, axis=-1, keepdims=True)) * (1.0 / SGU_W)
    rstd = lax.rsqrt(var + LN_EPS)
    vh0, vh1 = c0 * rstd, c1 * rstd
    vn0 = vh0 * lng_ref[:, 0:CB] + lnb_ref[:, 0:CB]
    vn1 = vh1 * lng_ref[:, CB:] + lnb_ref[:, CB:]
    return (vh0, vh1), (vn0, vn1), rstd


def _sgu_fwd(z, lng, lnb, ws, bsx, tb=256):
    t = z.shape[0]

    def body(u0_ref, u1_ref, v0_ref, v1_ref, gb0_ref, gb1_ref, lng_ref, lnb_ref, ws_ref, bsx_ref, yb_ref):
        _, (vn0, vn1), _ = _sgu_norm(v0_ref[...], v1_ref[...], lng_ref, lnb_ref)
        for half, (vn, u_ref, gb_ref) in enumerate(((vn0, u0_ref, gb0_ref), (vn1, u1_ref, gb1_ref))):
            vnb = vn.astype(BF16)
            silu_gb, _ = _silu_and_grad(gb_ref[...])
            gate = _gelu(u_ref[...]) * silu_gb
            for hh in range(4):
                h = 4 * half + hh
                for c in range(tb // CHUNK):
                    rows, cols = slice(CHUNK * c, CHUNK * (c + 1)), slice(128 * hh, 128 * (hh + 1))
                    sp = _dot(ws_ref[h], vnb[rows, cols]) + bsx_ref[h]
                    yb_ref[rows, CB * half + 128 * hh:CB * half + 128 * (hh + 1)] = (gate[rows, cols] * sp).astype(BF16)

    zb = lambda j: pl.BlockSpec((tb, CB), lambda i, j=j: (i, j))
    const = lambda shape: pl.BlockSpec(shape, lambda i: (0,) * len(shape))
    return pl.pallas_call(
        body,
        name="sgu_fwd",
        grid=(t // tb,),
        in_specs=[zb(CB_U), zb(CB_U + 1), zb(CB_V), zb(CB_V + 1), zb(CB_GB), zb(CB_GB + 1),
                  const((1, SGU_W)), const((1, SGU_W)), const((SGU_HEADS, CHUNK, CHUNK)), const((SGU_HEADS, CHUNK, 128))],
        out_specs=pl.BlockSpec((tb, SGU_W), lambda i: (i, 0)),
        out_shape=jax.ShapeDtypeStruct((t, SGU_W), BF16),
        compiler_params=_params("arbitrary"),
    )(z, z, z, z, z, z, lng, lnb, ws, bsx)


def _sgu_bwd(dyb, z, lng, lnb, ws, wst, bsx, tb=256):
    t = z.shape[0]

    def body(dyb_ref, u0_ref, u1_ref, v0_ref, v1_ref, gb0_ref, gb1_ref, lng_ref, lnb_ref, ws_ref, wst_ref, bsx_ref,
             dz_ref, dlng_ref, dlnb_ref, dws_ref, dbs_ref, dvn):
        @pl.when(pl.program_id(0) == 0)
        def _():
            for r in (dlng_ref, dlnb_ref, dws_ref, dbs_ref):
                r[...] = jnp.zeros_like(r)

        v0, v1 = v0_ref[...], v1_ref[...]
        (vh0, vh1), (vn0, vn1), rstd = _sgu_norm(v0, v1, lng_ref, lnb_ref)
        causal = (lax.broadcasted_iota(jnp.int32, (CHUNK, CHUNK), 0) >= lax.broadcasted_iota(jnp.int32, (CHUNK, CHUNK), 1))
        for half, (vn, u_ref, gb_ref) in enumerate(((vn0, u0_ref, gb0_ref), (vn1, u1_ref, gb1_ref))):
            vnb = vn.astype(BF16)
            u = u_ref[...]
            ug, ug_grad = _gelu_and_grad(u)
            silu_gb, silu_gb_grad = _silu_and_grad(gb_ref[...])
            dyb = dyb_ref[:, CB * half:CB * (half + 1)]
            dyb0 = dyb * silu_gb
            ds = dyb0 * ug
            sp_cols = []
            for hh in range(4):
                h = 4 * half + hh
                cols = slice(128 * hh, 128 * (hh + 1))
                sp_rows = []
                for c in range(tb // CHUNK):
                    rows = slice(CHUNK * c, CHUNK * (c + 1))
                    vt = vnb[rows, cols]
                    sp_rows.append(_dot(ws_ref[h], vt) + bsx_ref[h])
                    dst = ds[rows, cols]
                    dstb = dst.astype(BF16)
                    dbs_ref[h] += dst
                    dws_ref[h] += jnp.where(causal, _dot(dstb, vt, NT), 0.0)
                    dvn[rows, CB * half + 128 * hh:CB * half + 128 * (hh + 1)] = _dot(wst_ref[h], dstb)
                sp_cols.append(jnp.concatenate(sp_rows, axis=0))
            sp = jnp.concatenate(sp_cols, axis=1)
            dz_ref[:, CB * half:CB * (half + 1)] = (dyb0 * sp * ug_grad).astype(BF16)
            dz_ref[:, 2 * SGU_W + CB * half:2 * SGU_W + CB * (half + 1)] = (dyb * (ug * sp) * silu_gb_grad).astype(BF16)

        dvn0, dvn1 = dvn[:, 0:CB], dvn[:, CB:]
        dlng_ref[:, 0:CB] += _rowsum(dvn0 * vh0)
        dlng_ref[:, CB:] += _rowsum(dvn1 * vh1)
        dlnb_ref[:, 0:CB] += _rowsum(dvn0)
        dlnb_ref[:, CB:] += _rowsum(dvn1)
        dh0, dh1 = dvn0 * lng_ref[:, 0:CB], dvn1 * lng_ref[:, CB:]
        m1 = (jnp.sum(dh0, axis=-1, keepdims=True) + jnp.sum(dh1, axis=-1, keepdims=True)) * (1.0 / SGU_W)
        m2 = (jnp.sum(dh0 * vh0, axis=-1, keepdims=True) + jnp.sum(dh1 * vh1, axis=-1, keepdims=True)) * (1.0 / SGU_W)
        dz_ref[:, SGU_W:SGU_W + CB] = (rstd * (dh0 - m1 - vh0 * m2) * _gelu_grad(v0)).astype(BF16)
        dz_ref[:, SGU_W + CB:2 * SGU_W] = (rstd * (dh1 - m1 - vh1 * m2) * _gelu_grad(v1)).astype(BF16)

    zb = lambda j: pl.BlockSpec((tb, CB), lambda i, j=j: (i, j))
    const = lambda shape: pl.BlockSpec(shape, lambda i: (0,) * len(shape))
    hmat = (SGU_HEADS, CHUNK, CHUNK)
    return pl.pallas_call(
        body,
        name="sgu_bwd",
        grid=(t // tb,),
        in_specs=[pl.BlockSpec((tb, SGU_W), lambda i: (i, 0)),
                  zb(CB_U), zb(CB_U + 1), zb(CB_V), zb(CB_V + 1), zb(CB_GB), zb(CB_GB + 1),
                  const((1, SGU_W)), const((1, SGU_W)), const(hmat), const(hmat), const(hmat)],
        out_specs=[pl.BlockSpec((tb, 3 * SGU_W), lambda i: (i, 0)),
                   const((1, SGU_W)), const((1, SGU_W)), const(hmat), const(hmat)],
        out_shape=[jax.ShapeDtypeStruct((t, 3 * SGU_W), BF16),
                   jax.ShapeDtypeStruct((1, SGU_W), F32), jax.ShapeDtypeStruct((1, SGU_W), F32),
                   jax.ShapeDtypeStruct(hmat, F32), jax.ShapeDtypeStruct(hmat, F32)],
        scratch_shapes=[pltpu.VMEM((tb, SGU_W), F32)],
        compiler_params=_params("arbitrary"),
    )(dyb, z, z, z, z, z, z, lng, lnb, ws, wst, bsx)


def _window_sums(ext, lookahead):
    n = ext.shape[0]
    out = []
    for gi, w in enumerate(POOL_WINDOWS):
        acc = ext[:, 128 * gi:128 * (gi + 1)]
        k = 1
        while k < w:
            acc = acc + pltpu.roll(acc, (n - k) if lookahead else k, 0)
            k *= 2
        out.append(acc)
    return jnp.concatenate(out, axis=1)


def _pool_counts(row0, tb):
    pos = (row0 + 1 + lax.broadcasted_iota(jnp.int32, (tb, POOL_W), 0)).astype(F32)
    lane = lax.broadcasted_iota(jnp.int32, (tb, POOL_W), 1)
    win = jnp.where(lane < 128, 2.0, jnp.where(lane < 256, 4.0, jnp.where(lane < 384, 8.0, 16.0)))
    return jnp.minimum(pos, win)


def _pool_fwd(z, wpool, scale, tb=256):
    t = z.shape[0]

    def body(xc_ref, gc_ref, wp_ref, sc_ref, yc_ref, halo):
        i = pl.program_id(0)

        @pl.when(i == 0)
        def _():
            halo[...] = jnp.zeros_like(halo)

        xc = xc_ref[...]
        sums = _window_sums(jnp.concatenate([halo[...], xc], axis=0), lookahead=False)[HALO:, :]
        halo[...] = xc[tb - HALO:, :]
        pb = (sums / _pool_counts(i * tb, tb) - xc).astype(BF16)
        q = jnp.concatenate([_dot(pb[:, 128 * gi:128 * (gi + 1)], wp_ref[gi]) for gi in range(4)], axis=1)
        silu_gc, _ = _silu_and_grad(gc_ref[...])
        yc_ref[...] = (q * sc_ref[...] * silu_gc).astype(BF16)

    const = lambda shape: pl.BlockSpec(shape, lambda i: (0,) * len(shape))
    return pl.pallas_call(
        body,
        name="pool_fwd",
        grid=(t // tb,),
        in_specs=[pl.BlockSpec((tb, CB), lambda i: (i, CB_XC)), pl.BlockSpec((tb, CB), lambda i: (i, CB_GC)),
                  const((4, 128, 128)), const((1, POOL_W))],
        out_specs=pl.BlockSpec((tb, POOL_W), lambda i: (i, 0)),
        out_shape=jax.ShapeDtypeStruct((t, POOL_W), BF16),
        scratch_shapes=[pltpu.VMEM((HALO, POOL_W), F32)],
        compiler_params=_params("arbitrary"),
    )(z, z, wpool, scale)


def _pool_bwd(dyc, z, wpool, scale, tb=256):
    t = z.shape[0]
    nb = t // tb
    rev = lambda i: nb - 1 - i
    per_halo = tb // HALO

    def body(dyc_ref, xc_ref, xp_ref, gc_ref, wp_ref, sc_ref, dz_ref, dwp_ref, dsc_ref, ehalo):
        i = pl.program_id(0)

        @pl.when(i == 0)
        def _():
            ehalo[...] = jnp.zeros_like(ehalo)
            dwp_ref[...] = jnp.zeros_like(dwp_ref)
            dsc_ref[...] = jnp.zeros_like(dsc_ref)

        xc = xc_ref[...]
        prev = jnp.where(i < nb - 1, xp_ref[...], 0.0)
        sums = _window_sums(jnp.concatenate([prev, xc], axis=0), lookahead=False)[HALO:, :]
        cnt = _pool_counts(rev(i) * tb, tb)
        pb = (sums / cnt - xc).astype(BF16)
        q = jnp.concatenate([_dot(pb[:, 128 * gi:128 * (gi + 1)], wp_ref[gi]) for gi in range(4)], axis=1)
        silu_gc, silu_gc_grad = _silu_and_grad(gc_ref[...])
        dyc = dyc_ref[...]
        dz_ref[:, POOL_W:] = (dyc * (q * sc_ref[...]) * silu_gc_grad).astype(BF16)
        dyc0 = dyc * silu_gc
        dsc_ref[...] += _rowsum(dyc0 * q)
        dqb = (dyc0 * sc_ref[...]).astype(BF16)
        dp_cols = []
        for gi in range(4):
            cols = slice(128 * gi, 128 * (gi + 1))
            dwp_ref[gi] += _dot(pb[:, cols], dqb[:, cols], TN)
            dp_cols.append(_dot(dqb[:, cols], wp_ref[gi], NT))
        dp = jnp.concatenate(dp_cols, axis=1)
        e = dp / cnt
        fut = _window_sums(jnp.concatenate([e, ehalo[...]], axis=0), lookahead=True)[:tb, :]
        ehalo[...] = e[:HALO, :]
        dz_ref[:, 0:POOL_W] = (fut - dp).astype(BF16)

    const = lambda shape: pl.BlockSpec(shape, lambda i: (0,) * len(shape))
    return pl.pallas_call(
        body,
        name="pool_bwd",
        grid=(nb,),
        in_specs=[pl.BlockSpec((tb, POOL_W), lambda i: (rev(i), 0)),
                  pl.BlockSpec((tb, CB), lambda i: (rev(i), CB_XC)),
                  pl.BlockSpec((HALO, CB), lambda i: (jnp.maximum(rev(i) * per_halo - 1, 0), CB_XC)),
                  pl.BlockSpec((tb, CB), lambda i: (rev(i), CB_GC)),
                  const((4, 128, 128)), const((1, POOL_W))],
        out_specs=[pl.BlockSpec((tb, 2 * POOL_W), lambda i: (rev(i), 0)), const((4, 128, 128)), const((1, POOL_W))],
        out_shape=[jax.ShapeDtypeStruct((t, 2 * POOL_W), BF16),
                   jax.ShapeDtypeStruct((4, 128, 128), F32), jax.ShapeDtypeStruct((1, POOL_W), F32)],
        scratch_shapes=[pltpu.VMEM((HALO, POOL_W), F32)],
        compiler_params=_params("arbitrary"),
    )(dyc, z, z, z, wpool, scale)


def _adam_math(w, m, v, g):
    nm = ADAM_B1 * m + (1.0 - ADAM_B1) * g
    nv = ADAM_B2 * v + (1.0 - ADAM_B2) * (g * g)
    m_hat = nm / (1.0 - ADAM_B1 ** ADAM_STEP)
    v_hat = nv / (1.0 - ADAM_B2 ** ADAM_STEP)
    return -ADAM_LR * (m_hat / (jnp.sqrt(v_hat) + ADAM_EPS) + ADAM_WD * w), nm, nv


def _sum_small(r_a, r_b, steps=7):
    rows_a, rows_b = r_a.shape[1], r_b.shape[1]
    tr = rows_a // steps

    def body(a_ref, b_ref, ga_ref, gb_ref):
        for src, dst in ((a_ref, ga_ref), (b_ref, gb_ref)):
            g = src[0].astype(F32)
            for k in range(1, N_DEV):
                g = g + src[k].astype(F32)
            dst[...] = g

    return pl.pallas_call(
        body,
        name="sum_small",
        grid=(steps,),
        in_specs=[pl.BlockSpec((N_DEV, tr, 128), lambda i: (0, i, 0)), pl.BlockSpec((N_DEV, rows_b, 128), lambda i: (0, 0, 0))],
        out_specs=[pl.BlockSpec((tr, 128), lambda i: (i, 0)), pl.BlockSpec((rows_b, 128), lambda i: (0, 0))],
        out_shape=[jax.ShapeDtypeStruct((rows_a, 128), F32), jax.ShapeDtypeStruct((rows_b, 128), F32)],
        compiler_params=_params("arbitrary"),
    )(r_a, r_b)


def _adamw_small(ws, ms, vs, gs):
    n = len(ws)
    whole = pl.BlockSpec(memory_space=pltpu.VMEM)

    def body(*refs):
        ins, outs = refs[:4 * n], refs[4 * n:]
        for i in range(n):
            w, m, v, g = (ins[4 * i + j][...] for j in range(4))
            outs[3 * i][...], outs[3 * i + 1][...], outs[3 * i + 2][...] = _adam_math(w, m, v, g)

    return pl.pallas_call(
        body,
        name="adamw_small",
        in_specs=[whole] * (4 * n),
        out_specs=[whole] * (3 * n),
        out_shape=[jax.ShapeDtypeStruct(w.shape, F32) for w in ws for _ in range(3)],
        compiler_params=pltpu.CompilerParams(vmem_limit_bytes=VMEM_LIMIT_WHOLE),
    )(*[a for group in zip(ws, ms, vs, gs) for a in group])


def _adamw(w, m, v, parts, tr, name):
    r, c = w.shape
    n_slab = len(parts)
    per_slab = r // n_slab // tr

    def body(w_ref, m_ref, v_ref, *refs):
        p_refs, (g_ref, d_ref, nm_ref, nv_ref) = refs[:n_slab], refs[n_slab:]
        for s, p_ref in enumerate(p_refs):
            @pl.when(pl.program_id(0) // per_slab == s)
            def _(p_ref=p_ref):
                g = p_ref[0].astype(F32)
                for k in range(1, N_DEV):
                    g = g + p_ref[k].astype(F32)
                g_ref[...] = g

        d_ref[...], nm_ref[...], nv_ref[...] = _adam_math(w_ref[...], m_ref[...], v_ref[...], g_ref[...])

    blk = pl.BlockSpec((tr, c), lambda i: (i, 0))
    slab = lambda s: pl.BlockSpec((N_DEV, tr, c), lambda i, s=s: (0, jnp.clip(i - s * per_slab, 0, per_slab - 1), 0))
    return pl.pallas_call(
        body,
        name=name,
        grid=(r // tr,),
        in_specs=[blk, blk, blk] + [slab(s) for s in range(n_slab)],
        out_specs=[blk, blk, blk, blk],
        out_shape=[jax.ShapeDtypeStruct((r, c), F32)] * 4,
        compiler_params=_params("arbitrary"),
    )(w, m, v, *parts)


MESH = pl.DeviceIdType.MESH
ANY = pl.BlockSpec(memory_space=pl.ANY)


def _dev_index(dev):
    return 4 * dev[0] + 2 * dev[1] + dev[2]


WHOLE_SHAPES = ((D_MODEL, IN_COLS), (D_MODEL, D_MODEL), (S5_W, S5_W))
SHARD_SHAPES =((D_MODEL, SH_IN), (SH_OUT, D_MODEL), (SH_GLU, S5_W))


def _shard_of(ref, ti, idx):
    if ti == 0:
        return ref.at[:, pl.ds(pl.multiple_of(idx * SH_IN, 128), SH_IN)]
    rows = SHARD_SHAPES[ti][0]
    return ref.at[pl.ds(pl.multiple_of(idx * rows, rows), rows), :]


def _peer(mask, x, y, c):
    return (1 - x if mask & 4 else x, 1 - y if mask & 2 else y, 1 - c if mask & 1 else c)


def _allgather_weights(wi, wo, wg):
    n_t = 3

    def body(wi_ref, wo_ref, wg_ref, gi_ref, go_ref, gg_ref, send_sems, recv_sems, local_sems):
        x, y, c = lax.axis_index("x"), lax.axis_index("y"), lax.axis_index("c")
        me, sibling = (x, y, c), (x, y, 1 - c)
        chips = [(1 - x, y), (x, 1 - y), (1 - x, 1 - y)]
        shards = (wi_ref, wo_ref, wg_ref)
        wholes = (gi_ref, go_ref, gg_ref)

        def slot(ti, dev):
            return _shard_of(wholes[ti], ti, _dev_index(dev))

        def copy(k, ti, block, to, own=False):
            return pltpu.make_async_remote_copy(
                src_ref=shards[ti] if own else slot(ti, block), dst_ref=slot(ti, block),
                send_sem=send_sems.at[n_t * k + ti], recv_sem=recv_sems.at[n_t * k + ti],
                device_id=to, device_id_type=MESH)

        mine = [pltpu.make_async_copy(shards[ti], slot(ti, me), local_sems.at[ti]) for ti in range(n_t)]
        for cp in mine:
            cp.start()
        first = [copy(0, ti, me, sibling, own=True) for ti in range(n_t)]
        first += [copy(1 + j, ti, me, (*chip, c), own=True) for j, chip in enumerate(chips) for ti in range(n_t)]
        for cp in first:
            cp.start()
        passed = []
        for j, chip in enumerate(chips):
            for ti in range(n_t):
                copy(1 + j, ti, (*chip, c), me).wait_recv()
            onward = [copy(4 + j, ti, (*chip, c), sibling) for ti in range(n_t)]
            for cp in onward:
                cp.start()
            passed += onward
        for ti in range(n_t):
            copy(0, ti, sibling, me).wait_recv()
        for j, chip in enumerate(chips):
            for ti in range(n_t):
                copy(4 + j, ti, (*chip, 1 - c), me).wait_recv()
        for cp in first + passed:
            cp.wait_send()
        for cp in mine:
            cp.wait()

    return pl.pallas_call(
        body,
        name="allgather_weights",
        in_specs=[ANY] * n_t,
        out_specs=[ANY] * n_t,
        out_shape=[jax.ShapeDtypeStruct(s, BF16) for s in WHOLE_SHAPES],
        scratch_shapes=[pltpu.SemaphoreType.DMA((7 * n_t,)), pltpu.SemaphoreType.DMA((7 * n_t,)),
                        pltpu.SemaphoreType.DMA((n_t,))],
    )(wi, wo, wg)


HBM = pl.BlockSpec(memory_space=pltpu.HBM)
SEM = pl.BlockSpec(memory_space=pltpu.SEMAPHORE)
GATHER, SCATTER, SHARE = "gather", "scatter", "share"


def _split_route(kind, ti, sending, me_idx, p_idx, src_ref, land_ref):
    owner = me_idx if sending else p_idx
    if kind == GATHER:
        return src_ref, _shard_of(land_ref, ti, owner)
    if kind == SCATTER:
        return _shard_of(src_ref, ti, p_idx), land_ref.at[owner]
    return src_ref, land_ref.at[owner]


def _split_start(name, srcs, lands, kinds, after=None):
    n = len(srcs)
    arrays = list(srcs) + list(lands) + ([] if after is None else [after])

    def body(*refs):
        src_refs, land_refs = refs[0:n], refs[n:2 * n]
        send_sems, recv_sems, token = refs[len(arrays)], refs[len(arrays) + 1], refs[-1]
        x, y, c = lax.axis_index("x"), lax.axis_index("y"), lax.axis_index("c")
        me_idx = _dev_index((x, y, c))
        for mask in range(N_DEV):
            p = _peer(mask, x, y, c)
            for i, (kind, ti) in enumerate(kinds):
                k = mask * n + i
                src, dst = _split_route(kind, ti, True, me_idx, _dev_index(p), src_refs[i], land_refs[i])
                pltpu.make_async_remote_copy(src_ref=src, dst_ref=dst, send_sem=send_sems.at[k], recv_sem=recv_sems.at[k],
                                             device_id=p, device_id_type=MESH).start()
        token[...] = jnp.zeros_like(token)

    n_copies = N_DEV * n
    return pl.pallas_call(
        body,
        name=name,
        in_specs=[HBM] * len(arrays),
        out_specs=(SEM, SEM) + (HBM,) * (2 * n) + (pl.BlockSpec(memory_space=pltpu.VMEM),),
        out_shape=(pltpu.SemaphoreType.DMA((n_copies,)), pltpu.SemaphoreType.DMA((n_copies,)))
        + tuple(pltpu.HBM(a.shape, a.dtype) for a in arrays[:2 * n]) + (jax.ShapeDtypeStruct((8, 128), F32),),
        input_output_aliases={i: 2 + i for i in range(2 * n)},
        compiler_params=pltpu.CompilerParams(has_side_effects=pltpu.SideEffectType.DATAFLOW_SIDE_EFFECTING),
    )(*[pltpu.with_memory_space_constraint(a, pltpu.HBM) for a in arrays])


def _split_wait(name, started, kinds, after):
    n = len(kinds)
    send_sems, recv_sems, thru = started[0], started[1], started[2:2 + 2 * n]

    def body(*refs):
        src_refs, land_refs = refs[0:n], refs[n:2 * n]
        send_sems, recv_sems = refs[2 * n], refs[2 * n + 1]
        x, y, c = lax.axis_index("x"), lax.axis_index("y"), lax.axis_index("c")
        me_idx = _dev_index((x, y, c))
        for mask in range(N_DEV):
            p = _peer(mask, x, y, c)
            for i, (kind, ti) in enumerate(kinds):
                k = mask * n + i
                src, dst = _split_route(kind, ti, False, me_idx, _dev_index(p), src_refs[i], land_refs[i])
                cp = pltpu.make_async_remote_copy(src_ref=src, dst_ref=dst, send_sem=send_sems.at[k],
                                                  recv_sem=recv_sems.at[k], device_id=p, device_id_type=MESH)
                cp.wait_send()
                cp.wait_recv()

    res = pl.pallas_call(
        body,
        name=name,
        in_specs=[HBM] * (2 * n) + [SEM, SEM, pl.BlockSpec(memory_space=pl.ANY)],
        out_specs=(HBM,) * (2 * n),
        out_shape=tuple(pltpu.HBM(a.shape, a.dtype) for a in thru),
        input_output_aliases={i: i for i in range(2 * n)},
        compiler_params=pltpu.CompilerParams(has_side_effects=pltpu.SideEffectType.DATAFLOW_SIDE_EFFECTING),
    )(*thru, send_sems, recv_sems, after)
    return res[n:2 * n]


def _empty_zones(srcs, kinds):
    def shape(src, kind, ti):
        if kind == GATHER:
            return WHOLE_SHAPES[ti]
        return (N_DEV,) + (SHARD_SHAPES[ti] if kind == SCATTER else src.shape)

    return [lax.empty(shape(src, *k), src.dtype) for src, k in zip(srcs, kinds)]


def _share_small(buf):
    def body(b_ref, o_ref, send_sems, recv_sems, local_sem):
        x, y, c = lax.axis_index("x"), lax.axis_index("y"), lax.axis_index("c")
        me_idx = _dev_index((x, y, c))
        local = pltpu.make_async_copy(b_ref, o_ref.at[me_idx], local_sem)
        local.start()

        def copy(mask, owner):
            return pltpu.make_async_remote_copy(
                src_ref=b_ref, dst_ref=o_ref.at[owner], send_sem=send_sems.at[mask - 1], recv_sem=recv_sems.at[mask - 1],
                device_id=_peer(mask, x, y, c), device_id_type=MESH)

        sends = [copy(mask, me_idx) for mask in range(1, N_DEV)]
        for cp in sends:
            cp.start()
        for mask in range(1, N_DEV):
            copy(mask, _dev_index(_peer(mask, x, y, c))).wait_recv()
        for cp in sends:
            cp.wait_send()
        local.wait()

    return pl.pallas_call(
        body,
        name="share_small",
        in_specs=[ANY],
        out_specs=ANY,
        out_shape=jax.ShapeDtypeStruct((N_DEV,) + buf.shape, buf.dtype),
        scratch_shapes=[pltpu.SemaphoreType.DMA((N_DEV - 1,)), pltpu.SemaphoreType.DMA((N_DEV - 1,)),
                        pltpu.SemaphoreType.DMA],
    )(buf)


def _s5_prep(lam_re, lam_im, b_re, b_im, c_re, c_im, d_skip, log_dt):
    dt = jnp.exp(log_dt)[:, None]
    mag = jnp.exp(lam_re * dt)
    a_re, a_im = mag * jnp.cos(lam_im * dt), mag * jnp.sin(lam_im * dt)
    den = lam_re * lam_re + lam_im * lam_im
    f_re = ((a_re - 1.0) * lam_re + a_im * lam_im) / den
    f_im = (a_im * lam_re - (a_re - 1.0) * lam_im) / den
    bb_re = f_re[..., None] * b_re - f_im[..., None] * b_im
    bb_im = f_re[..., None] * b_im + f_im[..., None] * b_re
    eye = jnp.eye(8, dtype=F32)

    def in_map(bb):
        return jnp.einsum("jgph,gk->ghjkp", bb.reshape(4, 8, S5_STATE, S5_CH), eye).reshape(128, N_STATE)

    def out_map(cm):
        return jnp.einsum("jghp,gk->ghjkp", cm.reshape(4, 8, S5_CH, S5_STATE), eye).reshape(128, N_STATE)

    a = jnp.concatenate([a_re.reshape(N_PAIR, 128), a_im.reshape(N_PAIR, 128)])
    bc = jnp.concatenate([in_map(bb_re), in_map(bb_im)], axis=1)
    cct = jnp.concatenate([out_map(c_re), out_map(-c_im)], axis=1)
    return a, bc, cct, d_skip.reshape(1, S5_W)


WEIGHTS = ["norm_g", "w_in", "lam_re", "lam_im", "b_re", "b_im", "c_re", "c_im", "d_skip", "log_dt", "w_glu", "b_glu",
           "ln_g", "ln_b", "w_s", "b_s", "w_pool", "pool_scale", "w_out", "final_g"]
SHARDED = ("w_in", "w_glu", "w_out")
SMALL = [n for n in WEIGHTS if n not in SHARDED]
INPUTS = ["x"] + WEIGHTS + ["loss_target"] + ["m_" + n for n in WEIGHTS] + ["v_" + n for n in WEIGHTS]
SMALL_B = ["norm_g", "final_g"]
SMALL_A = [n for n in SMALL if n not in SMALL_B]
SMALL_TILE = 16 * 128
LOSS_AT = (DEPTH + 1) * D_MODEL


def _pack_small(arrays, dtype):
    flat = jnp.concatenate([a.reshape(-1) for a in arrays])
    pad = (-flat.shape[0]) % SMALL_TILE
    return jnp.pad(flat, (0, pad)).astype(dtype).reshape(-1, 128)


def _unpack_small(packed, like):
    flat = packed.reshape(-1)
    out, off = [], 0
    for a in like:
        out.append(flat[off:off + a.size].reshape(a.shape))
        off += a.size
    return out


def _layer_fwd(p, l, x, wi, wo, wg, token=None):
    row = lambda v: v.reshape(1, -1)
    causal = jnp.tril(jnp.ones((CHUNK, CHUNK), dtype=bool))
    (a, bc, cct, dvec), prep_vjp = jax.vjp(
        _s5_prep, p["lam_re"][l], p["lam_im"][l], p["b_re"][l], p["b_im"][l], p["c_re"][l], p["c_im"][l],
        p["d_skip"][l], p["log_dt"][l])
    ws_f32 = jnp.where(causal[None], p["w_s"][l], 0.0)
    pm, pt = _step_major(S5_TB)
    c = dict(
        x=x, wi=wi, wo=wo, wg=wg, a=a, bc=bc.astype(BF16), cc=cct.T.astype(BF16), dvec=dvec, prep_vjp=prep_vjp,
        pm=pm, pt=pt,
        ws=ws_f32.astype(BF16), wst=jnp.swapaxes(ws_f32, 1, 2).astype(BF16),
        bsx=jnp.broadcast_to(p["b_s"][l][:, :, None], (SGU_HEADS, CHUNK, 128)),
        wpool=p["w_pool"][l].astype(BF16), scale=row(p["pool_scale"][l]),
        lng=row(p["ln_g"][l]), lnb=row(p["ln_b"][l]), bglu=row(p["b_glu"][l]), norm_g=row(p["norm_g"][l]))
    c["z"], c["h"] = _rms_inproj(x, c["norm_g"], wi, token)
    c["ya"], c["s"], c["ys"] = _s5_fwd(c["z"], c["bc"], c["cc"], a, pm, pt, dvec, wg, c["bglu"], S5_TB)
    c["yb"] = _sgu_fwd(c["z"], c["lng"], c["lnb"], c["ws"], c["bsx"])
    c["yc"] = _pool_fwd(c["z"], c["wpool"], c["scale"])
    return _outproj(x, c["ya"], c["yb"], c["yc"], wo), c


def _mixers_bwd(c, dx, token=None):
    dya, dyb, dyc = _outproj_bwd(dx, c["wo"], token)
    dwo = _wgrad_out(c["ya"], c["yb"], c["yc"], dx, token)
    dz_s5, dbc, dcct, da, dd, dwg, dbglu = _s5_bwd(
        dya, c["ys"], c["z"], c["s"], c["bc"], c["cc"], c["a"], c["pm"], c["pt"], c["dvec"], c["wg"], c["bglu"], S5_TB)
    dz_sgu, dlng, dlnb, dws, dbsx = _sgu_bwd(dyb, c["z"], c["lng"], c["lnb"], c["ws"], c["wst"], c["bsx"])
    dz_pool, dwp, dsc = _pool_bwd(dyc, c["z"], c["wpool"], c["scale"])
    g_lam_re, g_lam_im, g_b_re, g_b_im, g_c_re, g_c_im, g_d, g_dt = c["prep_vjp"]((da, dbc, dcct, dd))
    small = dict(lam_re=g_lam_re, lam_im=g_lam_im, b_re=g_b_re, b_im=g_b_im, c_re=g_c_re,
                 c_im=g_c_im, d_skip=g_d, log_dt=g_dt, b_glu=dbglu.reshape(-1), ln_g=dlng.reshape(-1),
                 ln_b=dlnb.reshape(-1), w_s=dws, b_s=jnp.sum(dbsx, axis=-1), w_pool=dwp, pool_scale=dsc.reshape(-1))
    return (dz_s5, dz_sgu, dz_pool), dwo, dwg, small


def _inproj_grads(c, dz, dx, token_w=None, token_x=None):
    dwi = _wgrad_in(c["h"], *dz, token_w)
    dx, dnorm = _inproj_bwd(*dz, c["wi"], c["x"], c["norm_g"], dx, token_x)
    return dwi, dx, dnorm.reshape(-1)


def _own_index():
    return _dev_index((lax.axis_index("x"), lax.axis_index("y"), lax.axis_index("c")))


def kernel(x, norm_g, w_in, lam_re, lam_im, b_re, b_im, c_re, c_im, d_skip, log_dt, w_glu, b_glu, ln_g, ln_b, w_s, b_s, w_pool, pool_scale, w_out, final_g, loss_target, m_norm_g, m_w_in, m_lam_re, m_lam_im, m_b_re, m_b_im, m_c_re, m_c_im, m_d_skip, m_log_dt, m_w_glu, m_b_glu, m_ln_g, m_ln_b, m_w_s, m_b_s, m_w_pool, m_pool_scale, m_w_out, m_final_g, v_norm_g, v_w_in, v_lam_re, v_lam_im, v_b_re, v_b_im, v_c_re, v_c_im, v_d_skip, v_log_dt, v_w_glu, v_b_glu, v_ln_g, v_ln_b, v_w_s, v_b_s, v_w_pool, v_pool_scale, v_w_out, v_final_g):
    p = dict(zip(INPUTS, (x, norm_g, w_in, lam_re, lam_im, b_re, b_im, c_re, c_im, d_skip, log_dt, w_glu, b_glu, ln_g, ln_b, w_s, b_s, w_pool, pool_scale, w_out, final_g, loss_target, m_norm_g, m_w_in, m_lam_re, m_lam_im, m_b_re, m_b_im, m_c_re, m_c_im, m_d_skip, m_log_dt, m_w_glu, m_b_glu, m_ln_g, m_ln_b, m_w_s, m_b_s, m_w_pool, m_pool_scale, m_w_out, m_final_g, v_norm_g, v_w_in, v_lam_re, v_lam_im, v_b_re, v_b_im, v_c_re, v_c_im, v_d_skip, v_log_dt, v_w_glu, v_b_glu, v_ln_g, v_ln_b, v_w_s, v_b_s, v_w_pool, v_pool_scale, v_w_out, v_final_g)))

    shards = [[w[l].astype(BF16) for w in (w_in, w_out, w_glu)] for l in range(DEPTH)]
    gather3 = [(GATHER, ti) for ti in range(3)]
    scatter3 = [(SCATTER, ti) for ti in range(3)]

    whole0 = _allgather_weights(*shards[0])
    gather1 = _split_start("gather1_start", shards[1], _empty_zones(shards[1], gather3), gather3, after=whole0[2])
    x1, c0 = _layer_fwd(p, 0, x[0], *whole0, token=gather1[-1])
    whole1 = _split_wait("gather1_wait", gather1, gather3, x1)
    x2, c1 = _layer_fwd(p, 1, x1, *whole1)

    dx, loss_tile, dfinal = _final_loss(x2, final_g.reshape(1, -1), loss_target[0])

    dz1, dwo1, dwg1, small1 = _mixers_bwd(c1, dx)
    dwi1, dx, dnorm1 = _inproj_grads(c1, dz1, dx)
    partials1 = [dwi1, dwo1, dwg1.astype(BF16)]
    grads1 = _split_start("grads1_start", partials1, _empty_zones(partials1, scatter3), scatter3)
    dz0, dwo0, dwg0, small0 = _mixers_bwd(c0, dx, token=grads1[-1])
    small_a = _pack_small([jnp.stack([small0[n], small1[n]]) for n in SMALL_A], BF16)
    srcs_a, kinds_a = [dwo0, dwg0.astype(BF16), small_a], [(SCATTER, 1), (SCATTER, 2), (SHARE, None)]
    grads0a = _split_start("grads0a_start", srcs_a, _empty_zones(srcs_a, kinds_a), kinds_a)
    dwi0 = _wgrad_in(c0["h"], *dz0, grads0a[-1])
    kinds_b = [(SCATTER, 0)]
    grads0b = _split_start("grads0b_start", [dwi0], _empty_zones([dwi0], kinds_b), kinds_b)
    dx, dnorm0 = _inproj_bwd(*dz0, c0["wi"], c0["x"], c0["norm_g"], dx, grads0b[-1])
    parts1 = _split_wait("grads1_wait", grads1, scatter3, dx)
    r_out0, r_glu0, r_a = _split_wait("grads0a_wait", grads0a, kinds_a, dx)
    (r_in0,) = _split_wait("grads0b_wait", grads0b, kinds_b, dx)
    parts0 = [r_in0, r_out0, r_glu0]
    r_b = _share_small(_pack_small(
        [jnp.stack([dnorm0.reshape(-1), dnorm1]), dfinal.reshape(-1), loss_tile[0, 0:1]], F32))
    loss = jnp.sum(r_b.reshape(N_DEV, -1)[:, LOSS_AT])

    out = {}

    def adam(name, ti, tr):
        shape2d = (DEPTH * SHARD_SHAPES[ti][0], SHARD_SHAPES[ti][1])
        res = _adamw(p[name].reshape(shape2d), p["m_" + name].reshape(shape2d), p["v_" + name].reshape(shape2d),
                     [parts0[ti], parts1[ti]], tr, "adamw_" + name)
        out[name] = [r.reshape(p[name].shape) for r in res]

    adam("w_in", 0, 512)
    adam("w_out", 1, 128)
    adam("w_glu", 2, 64)
    g_a, g_b = _sum_small(r_a, r_b)
    grads = dict(zip(SMALL_A, _unpack_small(g_a, [p[n] for n in SMALL_A])))
    grads.update(zip(SMALL_B, _unpack_small(g_b, [p[n] for n in SMALL_B])))
    rank2 = lambda a: a.reshape(1, -1) if a.ndim == 1 else a
    res = _adamw_small(*[[rank2(src[pre + n]) for n in SMALL] for src, pre in ((p, ""), (p, "m_"), (p, "v_"), (grads, ""))])
    for i, n in enumerate(SMALL):
        out[n] = [grads[n]] + [r.reshape(p[n].shape) for r in res[3 * i:3 * i + 3]]

    return (loss, dx[None], *[out[n][0] for n in WEIGHTS], *[out[n][1] for n in WEIGHTS],
            *[out[n][2] for n in WEIGHTS], *[out[n][3] for n in WEIGHTS])
```

```python
import functools
import math

import jax
import jax.numpy as jnp
import numpy as np
from jax import lax
from jax.experimental import pallas as pl
from jax.experimental.pallas import tpu as pltpu

F32 = jnp.float32
BF16 = jnp.bfloat16

D_MODEL = 2048
DEPTH = 2
S5_W, SGU_W, POOL_W = 512, 1024, 512
S5_GROUPS, S5_STATE, S5_CH = 32, 64, 16
N_STATE = S5_GROUPS * S5_STATE
CHUNK = 128
SGU_HEADS = 8
POOL_WINDOWS = (2, 4, 8, 16)
IN_COLS = 5120
RMS_EPS = 1e-6
LN_EPS = 1e-5
ADAM_LR, ADAM_B1, ADAM_B2, ADAM_EPS, ADAM_WD, ADAM_STEP = 0.001, 0.9, 0.999, 1e-08, 0.01, 10

CB = 512
N_CB = IN_COLS // CB
CB_XA, CB_U, CB_V, CB_XC, CB_GA, CB_GB, CB_GC = 0, 1, 3, 5, 6, 7, 9

N_DEV = 8
SH_IN = IN_COLS // N_DEV
SH_OUT = D_MODEL // N_DEV
SH_GLU = S5_W // N_DEV

VMEM_LIMIT = 52 * 1024 * 1024
VMEM_LIMIT_WHOLE = 56 * 1024 * 1024
HALO = 16
LANE_CH = 512

TN = (((0,), (0,)), ((), ()))
NT = (((1,), (1,)), ((), ()))


def _params(*sem):
    return pltpu.CompilerParams(dimension_semantics=sem if sem else None, vmem_limit_bytes=VMEM_LIMIT)


def _dot(a, b, dims=None):
    if dims is None:
        return jnp.dot(a, b, preferred_element_type=F32)
    return lax.dot_general(a, b, dims, preferred_element_type=F32)


_GELU_C = math.sqrt(2.0 / math.pi)


def _gelu(x):
    return 0.5 * x * (1.0 + jnp.tanh(_GELU_C * (x + 0.044715 * x * x * x)))


def _gelu_and_grad(x):
    t = jnp.tanh(_GELU_C * (x + 0.044715 * x * x * x))
    half = 0.5 * (1.0 + t)
    return x * half, half + 0.5 * x * (1.0 - t * t) * _GELU_C * (1.0 + 3.0 * 0.044715 * x * x)


def _gelu_grad(x):
    return _gelu_and_grad(x)[1]


def _silu_and_grad(x):
    s = jax.nn.sigmoid(x)
    return x * s, s * (1.0 + x * (1.0 - s))


def _rowsum(x):
    return jnp.sum(x, axis=0, keepdims=True)


def _after(token):
    if token is None:
        return [], []
    return [pl.BlockSpec((8, 128), lambda *_: (0, 0))], [token]


def _rms_inproj(x, g, w, token=None, tm=1024, tn=1024):
    t = x.shape[0]
    tm = min(tm, t)
    after_specs, after = _after(token)

    def body(x_ref, g_ref, w_ref, *rest):
        z_ref, h_ref = rest[-2:]

        @pl.when(pl.program_id(1) == 0)
        def _():
            xv = x_ref[...]
            r = lax.rsqrt(jnp.mean(xv * xv, axis=-1, keepdims=True) + RMS_EPS)
            h_ref[...] = (xv * r * g_ref[...]).astype(BF16)

        z_ref[...] = _dot(h_ref[...], w_ref[...])

    return pl.pallas_call(
        body,
        name="rms_inproj",
        grid=(t // tm, IN_COLS // tn),
        in_specs=[
            pl.BlockSpec((tm, D_MODEL), lambda m, n: (m, 0)),
            pl.BlockSpec((1, D_MODEL), lambda m, n: (0, 0)),
            pl.BlockSpec((D_MODEL, tn), lambda m, n: (0, n)),
        ] + after_specs,
        out_specs=[
            pl.BlockSpec((tm, tn), lambda m, n: (m, n)),
            pl.BlockSpec((tm, D_MODEL), lambda m, n: (m, 0)),
        ],
        out_shape=[jax.ShapeDtypeStruct((t, IN_COLS), F32), jax.ShapeDtypeStruct((t, D_MODEL), BF16)],
        compiler_params=_params("arbitrary", "arbitrary"),
    )(x, g, w, *after)


def _outproj(x, ya, yb, yc, w, tm=1024, tn=1024):
    t = x.shape[0]
    tm = min(tm, t)

    def body(x_ref, ya_ref, yb_ref, yc_ref, w_ref, o_ref):
        acc = _dot(ya_ref[...], w_ref[0:S5_W, :])
        acc += _dot(yb_ref[...], w_ref[S5_W:S5_W + SGU_W, :])
        acc += _dot(yc_ref[...], w_ref[S5_W + SGU_W:, :])
        o_ref[...] = x_ref[...] + acc

    return pl.pallas_call(
        body,
        name="outproj",
        grid=(t // tm, D_MODEL // tn),
        in_specs=[
            pl.BlockSpec((tm, tn), lambda m, n: (m, n)),
            pl.BlockSpec((tm, S5_W), lambda m, n: (m, 0)),
            pl.BlockSpec((tm, SGU_W), lambda m, n: (m, 0)),
            pl.BlockSpec((tm, POOL_W), lambda m, n: (m, 0)),
            pl.BlockSpec((D_MODEL, tn), lambda m, n: (0, n)),
        ],
        out_specs=pl.BlockSpec((tm, tn), lambda m, n: (m, n)),
        out_shape=jax.ShapeDtypeStruct((t, D_MODEL), F32),
        compiler_params=_params("arbitrary", "arbitrary"),
    )(x, ya, yb, yc, w)


def _outproj_bwd(dx, w, token=None, tm=1024):
    t = dx.shape[0]
    tm = min(tm, t)
    after_specs, after = _after(token)

    def body(dx_ref, w_ref, *rest):
        dya_ref, dyb_ref, dyc_ref = rest[-3:]
        dy = _dot(dx_ref[...].astype(BF16), w_ref[...], NT)
        dya_ref[...] = dy[:, 0:S5_W]
        dyb_ref[...] = dy[:, S5_W:S5_W + SGU_W]
        dyc_ref[...] = dy[:, S5_W + SGU_W:]

    return pl.pallas_call(
        body,
        name="outproj_bwd",
        grid=(t // tm,),
        in_specs=[
            pl.BlockSpec((tm, D_MODEL), lambda m: (m, 0)),
            pl.BlockSpec((D_MODEL, D_MODEL), lambda m: (0, 0)),
        ] + after_specs,
        out_specs=[
            pl.BlockSpec((tm, S5_W), lambda m: (m, 0)),
            pl.BlockSpec((tm, SGU_W), lambda m: (m, 0)),
            pl.BlockSpec((tm, POOL_W), lambda m: (m, 0)),
        ],
        out_shape=[
            jax.ShapeDtypeStruct((t, S5_W), F32),
            jax.ShapeDtypeStruct((t, SGU_W), F32),
            jax.ShapeDtypeStruct((t, POOL_W), F32),
        ],
        compiler_params=_params("arbitrary"),
    )(dx, w, *after)


def _dz_piece_maps():
    s5_map = lambda j: jnp.where(j >= CB_GA, 1, 0)
    sgu_map = lambda j: jnp.clip(jnp.where(j <= 4, j - 1, j - 3), 0, 5)
    pool_map = lambda j: jnp.where(j >= CB_GC, 1, 0)
    return s5_map, sgu_map, pool_map


def _pick_piece(j):
    is_s5 = jnp.logical_or(j == CB_XA, j == CB_GA)
    is_pool = jnp.logical_or(j == CB_XC, j == CB_GC)
    return is_s5, is_pool, jnp.logical_not(jnp.logical_or(is_s5, is_pool))


def _inproj_bwd(dz_s5, dz_sgu, dz_pool, w, x, g, dxo, token=None, tm=512):
    t = x.shape[0]
    s5_map, sgu_map, pool_map = _dz_piece_maps()
    after_specs, after = _after(token)

    def body(s5_ref, sgu_ref, pool_ref, w_ref, x_ref, g_ref, dxo_ref, *rest):
        dx_ref, dg_ref, acc = rest[-3:]
        m, j = pl.program_id(0), pl.program_id(1)

        @pl.when(jnp.logical_and(m == 0, j == 0))
        def _():
            dg_ref[...] = jnp.zeros_like(dg_ref)

        @pl.when(j == 0)
        def _():
            acc[...] = jnp.zeros_like(acc)

        is_s5, is_pool, is_sgu = _pick_piece(j)

        @pl.when(is_s5)
        def _():
            acc[...] += _dot(w_ref[...], s5_ref[...], NT)

        @pl.when(is_sgu)
        def _():
            acc[...] += _dot(w_ref[...], sgu_ref[...], NT)

        @pl.when(is_pool)
        def _():
            acc[...] += _dot(w_ref[...], pool_ref[...], NT)

        @pl.when(j == N_CB - 1)
        def _():
            xv = x_ref[...]
            r = lax.rsqrt(jnp.mean(xv * xv, axis=-1, keepdims=True) + RMS_EPS)
            n = xv * r
            dh = acc[...].T
            dg_ref[...] += _rowsum(dh * n)
            dn = dh * g_ref[...]
            dx_ref[...] = dxo_ref[...] + r * (dn - n * jnp.mean(dn * n, axis=-1, keepdims=True))

    return pl.pallas_call(
        body,
        name="inproj_bwd",
        grid=(t // tm, N_CB),
        in_specs=[
            pl.BlockSpec((tm, CB), lambda m, j: (m, s5_map(j))),
            pl.BlockSpec((tm, CB), lambda m, j: (m, sgu_map(j))),
            pl.BlockSpec((tm, CB), lambda m, j: (m, pool_map(j))),
            pl.BlockSpec((D_MODEL, CB), lambda m, j: (0, j)),
            pl.BlockSpec((tm, D_MODEL), lambda m, j: (m, 0)),
            pl.BlockSpec((1, D_MODEL), lambda m, j: (0, 0)),
            pl.BlockSpec((tm, D_MODEL), lambda m, j: (m, 0)),
        ] + after_specs,
        out_specs=[
            pl.BlockSpec((tm, D_MODEL), lambda m, j: (m, 0)),
            pl.BlockSpec((1, D_MODEL), lambda m, j: (0, 0)),
        ],
        out_shape=[jax.ShapeDtypeStruct((t, D_MODEL), F32), jax.ShapeDtypeStruct((1, D_MODEL), F32)],
        scratch_shapes=[pltpu.VMEM((D_MODEL, tm), F32)],
        compiler_params=_params("arbitrary", "arbitrary"),
    )(dz_s5, dz_sgu, dz_pool, w, x, g, dxo, *after)


def _wgrad_in(h, dz_s5, dz_sgu, dz_pool, token=None, tm=2048):
    t = h.shape[0]
    tm = min(tm, t)
    s5_map, sgu_map, pool_map = _dz_piece_maps()
    after_specs, after = _after(token)

    def body(h_ref, s5_ref, sgu_ref, pool_ref, *rest):
        o_ref, acc = rest[-2:]
        j, m = pl.program_id(0), pl.program_id(1)

        @pl.when(m == 0)
        def _():
            acc[...] = jnp.zeros_like(acc)

        is_s5, is_pool, is_sgu = _pick_piece(j)

        @pl.when(is_s5)
        def _():
            acc[...] += _dot(s5_ref[...], h_ref[...], TN)

        @pl.when(is_sgu)
        def _():
            acc[...] += _dot(sgu_ref[...], h_ref[...], TN)

        @pl.when(is_pool)
        def _():
            acc[...] += _dot(pool_ref[...], h_ref[...], TN)

        @pl.when(m == pl.num_programs(1) - 1)
        def _():
            o_ref[...] = acc[...].T.astype(BF16)

    return pl.pallas_call(
        body,
        name="wgrad_in",
        grid=(N_CB, t // tm),
        in_specs=[
            pl.BlockSpec((tm, D_MODEL), lambda j, m: (m, 0)),
            pl.BlockSpec((tm, CB), lambda j, m: (m, s5_map(j))),
            pl.BlockSpec((tm, CB), lambda j, m: (m, sgu_map(j))),
            pl.BlockSpec((tm, CB), lambda j, m: (m, pool_map(j))),
        ] + after_specs,
        out_specs=pl.BlockSpec((D_MODEL, CB), lambda j, m: (0, j)),
        out_shape=jax.ShapeDtypeStruct((D_MODEL, IN_COLS), BF16),
        scratch_shapes=[pltpu.VMEM((CB, D_MODEL), F32)],
        compiler_params=_params("arbitrary", "arbitrary"),
    )(h, dz_s5, dz_sgu, dz_pool, *after)


def _wgrad_out(ya, yb, yc, dx, token=None, tm=2048, tn=512):
    t = dx.shape[0]
    tm = min(tm, t)
    after_specs, after = _after(token)

    def body(ya_ref, yb_ref, yc_ref, dx_ref, *rest):
        o_ref, acc = rest[-2:]
        m = pl.program_id(1)

        @pl.when(m == 0)
        def _():
            acc[...] = jnp.zeros_like(acc)

        dxb = dx_ref[...].astype(BF16)
        acc[:, 0:S5_W] += _dot(dxb, ya_ref[...], TN)
        acc[:, S5_W:S5_W + SGU_W] += _dot(dxb, yb_ref[...], TN)
        acc[:, S5_W + SGU_W:] += _dot(dxb, yc_ref[...], TN)

        @pl.when(m == pl.num_programs(1) - 1)
        def _():
            o_ref[...] = acc[...].T.astype(BF16)

    return pl.pallas_call(
        body,
        name="wgrad_out",
        grid=(D_MODEL // tn, t // tm),
        in_specs=[
            pl.BlockSpec((tm, S5_W), lambda n, m: (m, 0)),
            pl.BlockSpec((tm, SGU_W), lambda n, m: (m, 0)),
            pl.BlockSpec((tm, POOL_W), lambda n, m: (m, 0)),
            pl.BlockSpec((tm, tn), lambda n, m: (m, n)),
        ] + after_specs,
        out_specs=pl.BlockSpec((D_MODEL, tn), lambda n, m: (0, n)),
        out_shape=jax.ShapeDtypeStruct((D_MODEL, D_MODEL), BF16),
        scratch_shapes=[pltpu.VMEM((tn, D_MODEL), F32)],
        compiler_params=_params("arbitrary", "arbitrary"),
    )(ya, yb, yc, dx, *after)


def _final_loss(x, g, target, tm=512):
    t = x.shape[0]

    def body(x_ref, g_ref, t_ref, dx_ref, loss_ref, dg_ref):
        @pl.when(pl.program_id(0) == 0)
        def _():
            loss_ref[...] = jnp.zeros_like(loss_ref)
            dg_ref[...] = jnp.zeros_like(dg_ref)

        xv = x_ref[...]
        gv = g_ref[...]
        r = lax.rsqrt(jnp.mean(xv * xv, axis=-1, keepdims=True) + RMS_EPS)
        n = xv * r
        err = n * gv - t_ref[...]
        loss_ref[...] += 0.5 * jnp.sum(jnp.mean(err * err, axis=-1, keepdims=True))
        dy = err * (1.0 / D_MODEL)
        dg_ref[...] += _rowsum(dy * n)
        dn = dy * gv
        dx_ref[...] = r * (dn - n * jnp.mean(dn * n, axis=-1, keepdims=True))

    return pl.pallas_call(
        body,
        name="final_loss",
        grid=(t // tm,),
        in_specs=[
            pl.BlockSpec((tm, D_MODEL), lambda m: (m, 0)),
            pl.BlockSpec((1, D_MODEL), lambda m: (0, 0)),
            pl.BlockSpec((tm, D_MODEL), lambda m: (m, 0)),
        ],
        out_specs=[
            pl.BlockSpec((tm, D_MODEL), lambda m: (m, 0)),
            pl.BlockSpec((8, 128), lambda m: (0, 0)),
            pl.BlockSpec((1, D_MODEL), lambda m: (0, 0)),
        ],
        out_shape=[
            jax.ShapeDtypeStruct((t, D_MODEL), F32),
            jax.ShapeDtypeStruct((8, 128), F32),
            jax.ShapeDtypeStruct((1, D_MODEL), F32),
        ],
        compiler_params=_params("arbitrary"),
    )(x, g, target)


N_Q = 2 * N_STATE // LANE_CH
N_LT = 2 * N_STATE // 128
N_PAIR = N_LT // 2
SEG = 8
PAIR_GROUP = 8
S5_TB = 512


def _cmul_add(b_re, b_im, a_re, a_im, s_re, s_im):
    return b_re + (a_re * s_re - a_im * s_im), b_im + (a_re * s_im + a_im * s_re)


def _s5_fill_powers(pw, a_ref, tb, reverse):
    seg_len = tb // SEG
    sign = -1.0 if reverse else 1.0
    for p in range(N_PAIR):
        a_re = jnp.broadcast_to(a_ref[p:p + 1, :], (SEG, 128))
        a_im = sign * jnp.broadcast_to(a_ref[N_PAIR + p:N_PAIR + p + 1, :], (SEG, 128))

        def step(k, c, p=p, a_re=a_re, a_im=a_im):
            rows = pl.ds(pl.multiple_of(((seg_len - 1 - k) if reverse else k) * SEG, SEG), SEG)
            pw[p, rows, :] = c[0]
            pw[N_PAIR + p, rows, :] = c[1]
            return c[0] * a_re - c[1] * a_im, c[0] * a_im + c[1] * a_re

        lax.fori_loop(0, seg_len, step, (a_re, a_im))


def _s5_scan(st, carry, a_ref, pw_ref, tb, reverse):
    seg_len = tb // SEG
    sign = -1.0 if reverse else 1.0
    sub = lax.broadcasted_iota(jnp.int32, (SEG, 128), 0)
    chain = (0 if reverse else seg_len - 1) * SEG
    full = lambda row: jnp.broadcast_to(row, (SEG, 128))
    for p0 in range(0, N_PAIR, PAIR_GROUP):
        pairs = list(range(p0, p0 + PAIR_GROUP))
        a_re = [full(a_ref[p:p + 1, :]) for p in pairs]
        a_im = [sign * full(a_ref[N_PAIR + p:N_PAIR + p + 1, :]) for p in pairs]

        def step(k, c, pairs=pairs, a_re=a_re, a_im=a_im):
            rows = pl.ds(pl.multiple_of(((seg_len - 1 - k) if reverse else k) * SEG, SEG), SEG)
            out = []
            for i, p in enumerate(pairs):
                n_re, n_im = _cmul_add(st[p, rows, :], st[N_PAIR + p, rows, :], a_re[i], a_im[i], c[2 * i], c[2 * i + 1])
                st[p, rows, :] = n_re
                st[N_PAIR + p, rows, :] = n_im
                out += [n_re, n_im]
            return tuple(out)

        ends = lax.fori_loop(0, seg_len, step, tuple(jnp.zeros((SEG, 128), F32) for _ in range(2 * PAIR_GROUP)))
        for i, p in enumerate(pairs):
            e_re, e_im = ends[2 * i], ends[2 * i + 1]
            w_re, w_im = pw_ref[p, chain:chain + SEG, :], pw_ref[N_PAIR + p, chain:chain + SEG, :]
            c_re, c_im = full(carry[p:p + 1, :]), full(carry[N_PAIR + p:N_PAIR + p + 1, :])
            for hop in range(SEG - 1):
                n_re, n_im = _cmul_add(e_re, e_im, w_re, w_im, c_re, c_im)
                target = SEG - 2 - hop if reverse else hop + 1
                shift = SEG - 1 if reverse else 1
                c_re = jnp.where(sub == target, pltpu.roll(n_re, shift, 0), c_re)
                c_im = jnp.where(sub == target, pltpu.roll(n_im, shift, 0), c_im)
            n_re, n_im = _cmul_add(e_re, e_im, w_re, w_im, c_re, c_im)
            last = 0 if reverse else SEG - 1
            carry[p:p + 1, :] = n_re[last:last + 1, :]
            carry[N_PAIR + p:N_PAIR + p + 1, :] = n_im[last:last + 1, :]
            in_re, in_im = jnp.tile(c_re, (seg_len, 1)), jnp.tile(c_im, (seg_len, 1))
            st[p], st[N_PAIR + p] = _cmul_add(st[p], st[N_PAIR + p], pw_ref[p], pw_ref[N_PAIR + p], in_re, in_im)


def _lane_chunk(ref, q):
    return jnp.concatenate([ref[4 * q + i] for i in range(4)], axis=1)


def _put_lane_chunk(ref, q, value):
    for i in range(4):
        ref[4 * q + i] = value[:, 128 * i:128 * (i + 1)]


def _step_major(tb):
    r = np.arange(tb)
    pm = np.zeros((tb, tb), np.float32)
    pm[r, (r % SEG) * (tb // SEG) + r // SEG] = 1.0
    return jnp.asarray(pm, BF16), jnp.asarray(pm.T, BF16)


def _unpermute(pt_ref, v):
    hi = v.astype(BF16)
    lo = (v - hi.astype(F32)).astype(BF16)
    return _dot(pt_ref[...], hi) + _dot(pt_ref[...], lo)


def _s5_fwd(z, bc, cc, a, pm, pt, dvec, wglu, bglu, tb=256):
    t = z.shape[0]

    def body(xa_ref, ga_ref, bc_ref, cc_ref, a_ref, pm_ref, pt_ref, d_ref, wglu_ref, bglu_ref,
             ya_ref, s_ref, ys_ref, st, carry, pw_ref):
        @pl.when(pl.program_id(0) == 0)
        def _():
            carry[...] = jnp.zeros_like(carry)
            _s5_fill_powers(pw_ref, a_ref, tb, reverse=False)

        xa = xa_ref[...]
        xab = _dot(pm_ref[...], xa.astype(BF16)).astype(BF16)
        for q in range(N_Q):
            _put_lane_chunk(st, q, _dot(xab[:, 128 * (q % 4):128 * (q % 4) + 128], bc_ref[:, pl.ds(LANE_CH * q, LANE_CH)]))
        _s5_scan(st, carry, a_ref, pw_ref, tb, reverse=False)
        s_ref[...] = st[...].astype(BF16)
        cols = []
        for j in range(4):
            lo, hi = LANE_CH * j, N_STATE + LANE_CH * j
            cols.append(_dot(_lane_chunk(s_ref, j), cc_ref[lo:lo + LANE_CH, :])
                        + _dot(_lane_chunk(s_ref, 4 + j), cc_ref[hi:hi + LANE_CH, :]))
        ys = _unpermute(pt_ref, jnp.concatenate(cols, axis=1)) + d_ref[...] * xa
        ys_ref[...] = ys
        ya1 = _gelu(ys)
        pre = _dot(ya1.astype(BF16), wglu_ref[...]) + bglu_ref[...]
        silu_ga, _ = _silu_and_grad(ga_ref[...])
        ya_ref[...] = (ya1 * jax.nn.sigmoid(pre) * silu_ga).astype(BF16)

    const = lambda shape: pl.BlockSpec(shape, lambda i: (0,) * len(shape))
    return pl.pallas_call(
        body,
        name="s5_fwd",
        grid=(t // tb,),
        in_specs=[
            pl.BlockSpec((tb, CB), lambda i: (i, CB_XA)),
            pl.BlockSpec((tb, CB), lambda i: (i, CB_GA)),
            const((128, 2 * N_STATE)),
            const((2 * N_STATE, 128)),
            const((N_LT, 128)),
            const((tb, tb)),
            const((tb, tb)),
            const((1, S5_W)),
            const((S5_W, S5_W)),
            const((1, S5_W)),
        ],
        out_specs=[
            pl.BlockSpec((tb, S5_W), lambda i: (i, 0)),
            pl.BlockSpec((N_LT, tb, 128), lambda i: (0, i, 0)),
            pl.BlockSpec((tb, S5_W), lambda i: (i, 0)),
        ],
        out_shape=[
            jax.ShapeDtypeStruct((t, S5_W), BF16),
            jax.ShapeDtypeStruct((N_LT, t, 128), BF16),
            jax.ShapeDtypeStruct((t, S5_W), F32),
        ],
        scratch_shapes=[pltpu.VMEM((N_LT, tb, 128), F32), pltpu.VMEM((N_LT, 128), F32), pltpu.VMEM((N_LT, tb, 128), F32)],
        compiler_params=_params("arbitrary"),
    )(z, z, bc, cc, a, pm, pt, dvec, wglu, bglu)


def _s5_bwd(dya, ys, z, s, bc, cc, a, pm, pt, dvec, wglu, bglu, tb=256):
    t = z.shape[0]
    nb = t // tb
    rev = lambda i: nb - 1 - i

    def body(dya_ref, ys_ref, xa_ref, ga_ref, s_ref, sp_ref, bc_ref, cc_ref, a_ref, pm_ref, pt_ref, d_ref,
             wglu_ref, bglu_ref, dz_ref, dbc_ref, dcct_ref, da_ref, dd_ref, dwglu_ref, dbglu_ref, g, carry, pw_ref):
        i = pl.program_id(0)

        @pl.when(i == 0)
        def _():
            carry[...] = jnp.zeros_like(carry)
            _s5_fill_powers(pw_ref, a_ref, tb, reverse=True)
            for r in (dbc_ref, dcct_ref, da_ref, dd_ref, dwglu_ref, dbglu_ref):
                r[...] = jnp.zeros_like(r)

        ys = ys_ref[...]
        xa = xa_ref[...]
        ga = ga_ref[...]
        dya = dya_ref[...]
        ya1, ya1_grad = _gelu_and_grad(ys)
        ya1b = ya1.astype(BF16)
        sg = jax.nn.sigmoid(_dot(ya1b, wglu_ref[...]) + bglu_ref[...])
        silu_ga, silu_ga_grad = _silu_and_grad(ga)
        dz_ref[:, S5_W:] = (dya * (ya1 * sg) * silu_ga_grad).astype(BF16)
        dya2 = dya * silu_ga
        dpre = dya2 * ya1 * sg * (1.0 - sg)
        dbglu_ref[...] += _rowsum(dpre)
        dpreb = dpre.astype(BF16)
        dwglu_ref[...] += _dot(ya1b, dpreb, TN)
        dys = (dya2 * sg + _dot(dpreb, wglu_ref[...], NT)) * ya1_grad
        dd_ref[...] += _rowsum(dys * xa)
        dysb = _dot(pm_ref[...], dys.astype(BF16)).astype(BF16)
        xab = _dot(pm_ref[...], xa.astype(BF16)).astype(BF16)

        for q in range(N_Q):
            cq = pl.ds(LANE_CH * q, LANE_CH)
            x0 = 128 * (q % 4)
            dcct_ref[:, cq] += _dot(dysb[:, x0:x0 + 128], _lane_chunk(s_ref, q), TN)
            _put_lane_chunk(g, q, _dot(dysb[:, x0:x0 + 128], cc_ref[cq, :], NT))
        _s5_scan(g, carry, a_ref, pw_ref, tb, reverse=True)

        seg0 = (lax.broadcasted_iota(jnp.int32, (SEG, 128), 0) == 0)
        have_prev = i < nb - 1

        def before_step0(tile, halo):
            prev_last = jnp.where(have_prev, halo.astype(F32)[HALO - 1:HALO, :], 0.0)
            return jnp.where(seg0, prev_last, pltpu.roll(tile[tb - SEG:, :], 1, 0))

        for p in range(N_PAIR):
            g_re, g_im = g[p], g[N_PAIR + p]
            s_re, s_im = s_ref[p].astype(F32), s_ref[N_PAIR + p].astype(F32)
            f_re, f_im = before_step0(s_re, sp_ref[p]), before_step0(s_im, sp_ref[N_PAIR + p])
            b_re, b_im, h_re, h_im = s_re[:tb - SEG], s_im[:tb - SEG], g_re[SEG:], g_im[SEG:]
            da_ref[p:p + 1, :] += (_rowsum(b_re * h_re + b_im * h_im)
                                   + _rowsum(f_re * g_re[:SEG] + f_im * g_im[:SEG]))
            da_ref[N_PAIR + p:N_PAIR + p + 1, :] += (_rowsum(b_re * h_im - b_im * h_re)
                                                     + _rowsum(f_re * g_im[:SEG] - f_im * g_re[:SEG]))
        dxa_cols = []
        for j in range(4):
            re = pl.ds(LANE_CH * j, LANE_CH)
            im = pl.ds(N_STATE + LANE_CH * j, LANE_CH)
            x0 = 128 * j
            gb_re, gb_im = _lane_chunk(g, j).astype(BF16), _lane_chunk(g, 4 + j).astype(BF16)
            dbc_ref[:, re] += _dot(xab[:, x0:x0 + 128], gb_re, TN)
            dbc_ref[:, im] += _dot(xab[:, x0:x0 + 128], gb_im, TN)
            dxa_cols.append(_dot(gb_re, bc_ref[:, re], NT) + _dot(gb_im, bc_ref[:, im], NT))
        dz_ref[:, 0:S5_W] = (dys * d_ref[...] + _unpermute(pt_ref, jnp.concatenate(dxa_cols, axis=1))).astype(BF16)

    const = lambda shape: pl.BlockSpec(shape, lambda i: (0,) * len(shape))
    per_halo = tb // HALO
    return pl.pallas_call(
        body,
        name="s5_bwd",
        grid=(nb,),
        in_specs=[
            pl.BlockSpec((tb, S5_W), lambda i: (rev(i), 0)),
            pl.BlockSpec((tb, S5_W), lambda i: (rev(i), 0)),
            pl.BlockSpec((tb, CB), lambda i: (rev(i), CB_XA)),
            pl.BlockSpec((tb, CB), lambda i: (rev(i), CB_GA)),
            pl.BlockSpec((N_LT, tb, 128), lambda i: (0, rev(i), 0)),
            pl.BlockSpec((N_LT, HALO, 128), lambda i: (0, jnp.maximum(rev(i) * per_halo - 1, 0), 0)),
            const((128, 2 * N_STATE)),
            const((2 * N_STATE, 128)),
            const((N_LT, 128)),
            const((tb, tb)),
            const((tb, tb)),
            const((1, S5_W)),
            const((S5_W, S5_W)),
            const((1, S5_W)),
        ],
        out_specs=[
            pl.BlockSpec((tb, 2 * CB), lambda i: (rev(i), 0)),
            const((128, 2 * N_STATE)),
            const((128, 2 * N_STATE)),
            const((N_LT, 128)),
            const((1, S5_W)),
            const((S5_W, S5_W)),
            const((1, S5_W)),
        ],
        out_shape=[
            jax.ShapeDtypeStruct((t, 2 * CB), BF16),
            jax.ShapeDtypeStruct((128, 2 * N_STATE), F32),
            jax.ShapeDtypeStruct((128, 2 * N_STATE), F32),
            jax.ShapeDtypeStruct((N_LT, 128), F32),
            jax.ShapeDtypeStruct((1, S5_W), F32),
            jax.ShapeDtypeStruct((S5_W, S5_W), F32),
            jax.ShapeDtypeStruct((1, S5_W), F32),
        ],
        scratch_shapes=[pltpu.VMEM((N_LT, tb, 128), F32), pltpu.VMEM((N_LT, 128), F32), pltpu.VMEM((N_LT, tb, 128), F32)],
        compiler_params=_params("arbitrary"),
    )(dya, ys, z, z, s, s, bc, cc, a, pm, pt, dvec, wglu, bglu)


def _sgu_norm(v0, v1, lng_ref, lnb_ref):
    g0, g1 = _gelu(v0), _gelu(v1)
    mu = (jnp.sum(g0, axis=-1, keepdims=True) + jnp.sum(g1, axis=-1, keepdims=True)) * (1.0 / SGU_W)
    c0, c1 = g0 - mu, g1 - mu
    var = (jnp.sum(c0 * c0, axis=-1, keepdims=True) + jnp.sum(c1 * c1, axis=-1, keepdims=True)) * (1.0 / SGU_W)
    rstd = lax.rsqrt(var + LN_EPS)
    vh0, vh1 = c0 * rstd, c1 * rstd
    vn0 = vh0 * lng_ref[:, 0:CB] + lnb_ref[:, 0:CB]
    vn1 = vh1 * lng_ref[:, CB:] + lnb_ref[:, CB:]
    return (vh0, vh1), (vn0, vn1), rstd


def _sgu_fwd(z, lng, lnb, ws, bsx, tb=256):
    t = z.shape[0]

    def body(u0_ref, u1_ref, v0_ref, v1_ref, gb0_ref, gb1_ref, lng_ref, lnb_ref, ws_ref, bsx_ref, yb_ref):
        _, (vn0, vn1), _ = _sgu_norm(v0_ref[...], v1_ref[...], lng_ref, lnb_ref)
        for half, (vn, u_ref, gb_ref) in enumerate(((vn0, u0_ref, gb0_ref), (vn1, u1_ref, gb1_ref))):
            vnb = vn.astype(BF16)
            silu_gb, _ = _silu_and_grad(gb_ref[...])
            gate = _gelu(u_ref[...]) * silu_gb
            for hh in range(4):
                h = 4 * half + hh
                for c in range(tb // CHUNK):
                    rows, cols = slice(CHUNK * c, CHUNK * (c + 1)), slice(128 * hh, 128 * (hh + 1))
                    sp = _dot(ws_ref[h], vnb[rows, cols]) + bsx_ref[h]
                    yb_ref[rows, CB * half + 128 * hh:CB * half + 128 * (hh + 1)] = (gate[rows, cols] * sp).astype(BF16)

    zb = lambda j: pl.BlockSpec((tb, CB), lambda i, j=j: (i, j))
    const = lambda shape: pl.BlockSpec(shape, lambda i: (0,) * len(shape))
    return pl.pallas_call(
        body,
        name="sgu_fwd",
        grid=(t // tb,),
        in_specs=[zb(CB_U), zb(CB_U + 1), zb(CB_V), zb(CB_V + 1), zb(CB_GB), zb(CB_GB + 1),
                  const((1, SGU_W)), const((1, SGU_W)), const((SGU_HEADS, CHUNK, CHUNK)), const((SGU_HEADS, CHUNK, 128))],
        out_specs=pl.BlockSpec((tb, SGU_W), lambda i: (i, 0)),
        out_shape=jax.ShapeDtypeStruct((t, SGU_W), BF16),
        compiler_params=_params("arbitrary"),
    )(z, z, z, z, z, z, lng, lnb, ws, bsx)


def _sgu_bwd(dyb, z, lng, lnb, ws, wst, bsx, tb=256):
    t = z.shape[0]

    def body(dyb_ref, u0_ref, u1_ref, v0_ref, v1_ref, gb0_ref, gb1_ref, lng_ref, lnb_ref, ws_ref, wst_ref, bsx_ref,
             dz_ref, dlng_ref, dlnb_ref, dws_ref, dbs_ref, dvn):
        @pl.when(pl.program_id(0) == 0)
        def _():
            for r in (dlng_ref, dlnb_ref, dws_ref, dbs_ref):
                r[...] = jnp.zeros_like(r)

        v0, v1 = v0_ref[...], v1_ref[...]
        (vh0, vh1), (vn0, vn1), rstd = _sgu_norm(v0, v1, lng_ref, lnb_ref)
        causal = (lax.broadcasted_iota(jnp.int32, (CHUNK, CHUNK), 0) >= lax.broadcasted_iota(jnp.int32, (CHUNK, CHUNK), 1))
        for half, (vn, u_ref, gb_ref) in enumerate(((vn0, u0_ref, gb0_ref), (vn1, u1_ref, gb1_ref))):
            vnb = vn.astype(BF16)
            u = u_ref[...]
            ug, ug_grad = _gelu_and_grad(u)
            silu_gb, silu_gb_grad = _silu_and_grad(gb_ref[...])
            dyb = dyb_ref[:, CB * half:CB * (half + 1)]
            dyb0 = dyb * silu_gb
            ds = dyb0 * ug
            sp_cols = []
            for hh in range(4):
                h = 4 * half + hh
                cols = slice(128 * hh, 128 * (hh + 1))
                sp_rows = []
                for c in range(tb // CHUNK):
                    rows = slice(CHUNK * c, CHUNK * (c + 1))
                    vt = vnb[rows, cols]
                    sp_rows.append(_dot(ws_ref[h], vt) + bsx_ref[h])
                    dst = ds[rows, cols]
                    dstb = dst.astype(BF16)
                    dbs_ref[h] += dst
                    dws_ref[h] += jnp.where(causal, _dot(dstb, vt, NT), 0.0)
                    dvn[rows, CB * half + 128 * hh:CB * half + 128 * (hh + 1)] = _dot(wst_ref[h], dstb)
                sp_cols.append(jnp.concatenate(sp_rows, axis=0))
            sp = jnp.concatenate(sp_cols, axis=1)
            dz_ref[:, CB * half:CB * (half + 1)] = (dyb0 * sp * ug_grad).astype(BF16)
            dz_ref[:, 2 * SGU_W + CB * half:2 * SGU_W + CB * (half + 1)] = (dyb * (ug * sp) * silu_gb_grad).astype(BF16)

        dvn0, dvn1 = dvn[:, 0:CB], dvn[:, CB:]
        dlng_ref[:, 0:CB] += _rowsum(dvn0 * vh0)
        dlng_ref[:, CB:] += _rowsum(dvn1 * vh1)
        dlnb_ref[:, 0:CB] += _rowsum(dvn0)
        dlnb_ref[:, CB:] += _rowsum(dvn1)
        dh0, dh1 = dvn0 * lng_ref[:, 0:CB], dvn1 * lng_ref[:, CB:]
        m1 = (jnp.sum(dh0, axis=-1, keepdims=True) + jnp.sum(dh1, axis=-1, keepdims=True)) * (1.0 / SGU_W)
        m2 = (jnp.sum(dh0 * vh0, axis=-1, keepdims=True) + jnp.sum(dh1 * vh1, axis=-1, keepdims=True)) * (1.0 / SGU_W)
        dz_ref[:, SGU_W:SGU_W + CB] = (rstd * (dh0 - m1 - vh0 * m2) * _gelu_grad(v0)).astype(BF16)
        dz_ref[:, SGU_W + CB:2 * SGU_W] = (rstd * (dh1 - m1 - vh1 * m2) * _gelu_grad(v1)).astype(BF16)

    zb = lambda j: pl.BlockSpec((tb, CB), lambda i, j=j: (i, j))
    const = lambda shape: pl.BlockSpec(shape, lambda i: (0,) * len(shape))
    hmat = (SGU_HEADS, CHUNK, CHUNK)
    return pl.pallas_call(
        body,
        name="sgu_bwd",
        grid=(t // tb,),
        in_specs=[pl.BlockSpec((tb, SGU_W), lambda i: (i, 0)),
                  zb(CB_U), zb(CB_U + 1), zb(CB_V), zb(CB_V + 1), zb(CB_GB), zb(CB_GB + 1),
                  const((1, SGU_W)), const((1, SGU_W)), const(hmat), const(hmat), const(hmat)],
        out_specs=[pl.BlockSpec((tb, 3 * SGU_W), lambda i: (i, 0)),
                   const((1, SGU_W)), const((1, SGU_W)), const(hmat), const(hmat)],
        out_shape=[jax.ShapeDtypeStruct((t, 3 * SGU_W), BF16),
                   jax.ShapeDtypeStruct((1, SGU_W), F32), jax.ShapeDtypeStruct((1, SGU_W), F32),
                   jax.ShapeDtypeStruct(hmat, F32), jax.ShapeDtypeStruct(hmat, F32)],
        scratch_shapes=[pltpu.VMEM((tb, SGU_W), F32)],
        compiler_params=_params("arbitrary"),
    )(dyb, z, z, z, z, z, z, lng, lnb, ws, wst, bsx)


def _window_sums(ext, lookahead):
    n = ext.shape[0]
    out = []
    for gi, w in enumerate(POOL_WINDOWS):
        acc = ext[:, 128 * gi:128 * (gi + 1)]
        k = 1
        while k < w:
            acc = acc + pltpu.roll(acc, (n - k) if lookahead else k, 0)
            k *= 2
        out.append(acc)
    return jnp.concatenate(out, axis=1)


def _pool_counts(row0, tb):
    pos = (row0 + 1 + lax.broadcasted_iota(jnp.int32, (tb, POOL_W), 0)).astype(F32)
    lane = lax.broadcasted_iota(jnp.int32, (tb, POOL_W), 1)
    win = jnp.where(lane < 128, 2.0, jnp.where(lane < 256, 4.0, jnp.where(lane < 384, 8.0, 16.0)))
    return jnp.minimum(pos, win)


def _pool_fwd(z, wpool, scale, tb=256):
    t = z.shape[0]

    def body(xc_ref, gc_ref, wp_ref, sc_ref, yc_ref, halo):
        i = pl.program_id(0)

        @pl.when(i == 0)
        def _():
            halo[...] = jnp.zeros_like(halo)

        xc = xc_ref[...]
        sums = _window_sums(jnp.concatenate([halo[...], xc], axis=0), lookahead=False)[HALO:, :]
        halo[...] = xc[tb - HALO:, :]
        pb = (sums / _pool_counts(i * tb, tb) - xc).astype(BF16)
        q = jnp.concatenate([_dot(pb[:, 128 * gi:128 * (gi + 1)], wp_ref[gi]) for gi in range(4)], axis=1)
        silu_gc, _ = _silu_and_grad(gc_ref[...])
        yc_ref[...] = (q * sc_ref[...] * silu_gc).astype(BF16)

    const = lambda shape: pl.BlockSpec(shape, lambda i: (0,) * len(shape))
    return pl.pallas_call(
        body,
        name="pool_fwd",
        grid=(t // tb,),
        in_specs=[pl.BlockSpec((tb, CB), lambda i: (i, CB_XC)), pl.BlockSpec((tb, CB), lambda i: (i, CB_GC)),
                  const((4, 128, 128)), const((1, POOL_W))],
        out_specs=pl.BlockSpec((tb, POOL_W), lambda i: (i, 0)),
        out_shape=jax.ShapeDtypeStruct((t, POOL_W), BF16),
        scratch_shapes=[pltpu.VMEM((HALO, POOL_W), F32)],
        compiler_params=_params("arbitrary"),
    )(z, z, wpool, scale)


def _pool_bwd(dyc, z, wpool, scale, tb=256):
    t = z.shape[0]
    nb = t // tb
    rev = lambda i: nb - 1 - i
    per_halo = tb // HALO

    def body(dyc_ref, xc_ref, xp_ref, gc_ref, wp_ref, sc_ref, dz_ref, dwp_ref, dsc_ref, ehalo):
        i = pl.program_id(0)

        @pl.when(i == 0)
        def _():
            ehalo[...] = jnp.zeros_like(ehalo)
            dwp_ref[...] = jnp.zeros_like(dwp_ref)
            dsc_ref[...] = jnp.zeros_like(dsc_ref)

        xc = xc_ref[...]
        prev = jnp.where(i < nb - 1, xp_ref[...], 0.0)
        sums = _window_sums(jnp.concatenate([prev, xc], axis=0), lookahead=False)[HALO:, :]
        cnt = _pool_counts(rev(i) * tb, tb)
        pb = (sums / cnt - xc).astype(BF16)
        q = jnp.concatenate([_dot(pb[:, 128 * gi:128 * (gi + 1)], wp_ref[gi]) for gi in range(4)], axis=1)
        silu_gc, silu_gc_grad = _silu_and_grad(gc_ref[...])
        dyc = dyc_ref[...]
        dz_ref[:, POOL_W:] = (dyc * (q * sc_ref[...]) * silu_gc_grad).astype(BF16)
        dyc0 = dyc * silu_gc
        dsc_ref[...] += _rowsum(dyc0 * q)
        dqb = (dyc0 * sc_ref[...]).astype(BF16)
        dp_cols = []
        for gi in range(4):
            cols = slice(128 * gi, 128 * (gi + 1))
            dwp_ref[gi] += _dot(pb[:, cols], dqb[:, cols], TN)
            dp_cols.append(_dot(dqb[:, cols], wp_ref[gi], NT))
        dp = jnp.concatenate(dp_cols, axis=1)
        e = dp / cnt
        fut = _window_sums(jnp.concatenate([e, ehalo[...]], axis=0), lookahead=True)[:tb, :]
        ehalo[...] = e[:HALO, :]
        dz_ref[:, 0:POOL_W] = (fut - dp).astype(BF16)

    const = lambda shape: pl.BlockSpec(shape, lambda i: (0,) * len(shape))
    return pl.pallas_call(
        body,
        name="pool_bwd",
        grid=(nb,),
        in_specs=[pl.BlockSpec((tb, POOL_W), lambda i: (rev(i), 0)),
                  pl.BlockSpec((tb, CB), lambda i: (rev(i), CB_XC)),
                  pl.BlockSpec((HALO, CB), lambda i: (jnp.maximum(rev(i) * per_halo - 1, 0), CB_XC)),
                  pl.BlockSpec((tb, CB), lambda i: (rev(i), CB_GC)),
                  const((4, 128, 128)), const((1, POOL_W))],
        out_specs=[pl.BlockSpec((tb, 2 * POOL_W), lambda i: (rev(i), 0)), const((4, 128, 128)), const((1, POOL_W))],
        out_shape=[jax.ShapeDtypeStruct((t, 2 * POOL_W), BF16),
                   jax.ShapeDtypeStruct((4, 128, 128), F32), jax.ShapeDtypeStruct((1, POOL_W), F32)],
        scratch_shapes=[pltpu.VMEM((HALO, POOL_W), F32)],
        compiler_params=_params("arbitrary"),
    )(dyc, z, z, z, wpool, scale)


def _adam_math(w, m, v, g):
    nm = ADAM_B1 * m + (1.0 - ADAM_B1) * g
    nv = ADAM_B2 * v + (1.0 - ADAM_B2) * (g * g)
    m_hat = nm / (1.0 - ADAM_B1 ** ADAM_STEP)
    v_hat = nv / (1.0 - ADAM_B2 ** ADAM_STEP)
    return -ADAM_LR * (m_hat / (jnp.sqrt(v_hat) + ADAM_EPS) + ADAM_WD * w), nm, nv


def _sum_small(r_a, r_b, steps=7):
    rows_a, rows_b = r_a.shape[1], r_b.shape[1]
    tr = rows_a // steps

    def body(a_ref, b_ref, ga_ref, gb_ref):
        for src, dst in ((a_ref, ga_ref), (b_ref, gb_ref)):
            g = src[0].astype(F32)
            for k in range(1, N_DEV):
                g = g + src[k].astype(F32)
            dst[...] = g

    return pl.pallas_call(
        body,
        name="sum_small",
        grid=(steps,),
        in_specs=[pl.BlockSpec((N_DEV, tr, 128), lambda i: (0, i, 0)), pl.BlockSpec((N_DEV, rows_b, 128), lambda i: (0, 0, 0))],
        out_specs=[pl.BlockSpec((tr, 128), lambda i: (i, 0)), pl.BlockSpec((rows_b, 128), lambda i: (0, 0))],
        out_shape=[jax.ShapeDtypeStruct((rows_a, 128), F32), jax.ShapeDtypeStruct((rows_b, 128), F32)],
        compiler_params=_params("arbitrary"),
    )(r_a, r_b)


def _adamw_small(ws, ms, vs, gs):
    n = len(ws)
    whole = pl.BlockSpec(memory_space=pltpu.VMEM)

    def body(*refs):
        ins, outs = refs[:4 * n], refs[4 * n:]
        for i in range(n):
            w, m, v, g = (ins[4 * i + j][...] for j in range(4))
            outs[3 * i][...], outs[3 * i + 1][...], outs[3 * i + 2][...] = _adam_math(w, m, v, g)

    return pl.pallas_call(
        body,
        name="adamw_small",
        in_specs=[whole] * (4 * n),
        out_specs=[whole] * (3 * n),
        out_shape=[jax.ShapeDtypeStruct(w.shape, F32) for w in ws for _ in range(3)],
        compiler_params=pltpu.CompilerParams(vmem_limit_bytes=VMEM_LIMIT_WHOLE),
    )(*[a for group in zip(ws, ms, vs, gs) for a in group])


def _adamw(w, m, v, parts, tr, name):
    r, c = w.shape
    n_slab = len(parts)
    per_slab = r // n_slab // tr

    def body(w_ref, m_ref, v_ref, *refs):
        p_refs, (g_ref, d_ref, nm_ref, nv_ref) = refs[:n_slab], refs[n_slab:]
        for s, p_ref in enumerate(p_refs):
            @pl.when(pl.program_id(0) // per_slab == s)
            def _(p_ref=p_ref):
                g = p_ref[0].astype(F32)
                for k in range(1, N_DEV):
                    g = g + p_ref[k].astype(F32)
                g_ref[...] = g

        d_ref[...], nm_ref[...], nv_ref[...] = _adam_math(w_ref[...], m_ref[...], v_ref[...], g_ref[...])

    blk = pl.BlockSpec((tr, c), lambda i: (i, 0))
    slab = lambda s: pl.BlockSpec((N_DEV, tr, c), lambda i, s=s: (0, jnp.clip(i - s * per_slab, 0, per_slab - 1), 0))
    return pl.pallas_call(
        body,
        name=name,
        grid=(r // tr,),
        in_specs=[blk, blk, blk] + [slab(s) for s in range(n_slab)],
        out_specs=[blk, blk, blk, blk],
        out_shape=[jax.ShapeDtypeStruct((r, c), F32)] * 4,
        compiler_params=_params("arbitrary"),
    )(w, m, v, *parts)


MESH = pl.DeviceIdType.MESH
ANY = pl.BlockSpec(memory_space=pl.ANY)


def _dev_index(dev):
    return 4 * dev[0] + 2 * dev[1] + dev[2]


WHOLE_SHAPES = ((D_MODEL, IN_COLS), (D_MODEL, D_MODEL), (S5_W, S5_W))
SHARD_SHAPES =((D_MODEL, SH_IN), (SH_OUT, D_MODEL), (SH_GLU, S5_W))


def _shard_of(ref, ti, idx):
    if ti == 0:
        return ref.at[:, pl.ds(pl.multiple_of(idx * SH_IN, 128), SH_IN)]
    rows = SHARD_SHAPES[ti][0]
    return ref.at[pl.ds(pl.multiple_of(idx * rows, rows), rows), :]


def _peer(mask, x, y, c):
    return (1 - x if mask & 4 else x, 1 - y if mask & 2 else y, 1 - c if mask & 1 else c)


def _allgather_weights(wi, wo, wg):
    n_t = 3

    def body(wi_ref, wo_ref, wg_ref, gi_ref, go_ref, gg_ref, send_sems, recv_sems, local_sems):
        x, y, c = lax.axis_index("x"), lax.axis_index("y"), lax.axis_index("c")
        me, sibling = (x, y, c), (x, y, 1 - c)
        chips = [(1 - x, y), (x, 1 - y), (1 - x, 1 - y)]
        shards = (wi_ref, wo_ref, wg_ref)
        wholes = (gi_ref, go_ref, gg_ref)

        def slot(ti, dev):
            return _shard_of(wholes[ti], ti, _dev_index(dev))

        def copy(k, ti, block, to, own=False):
            return pltpu.make_async_remote_copy(
                src_ref=shards[ti] if own else slot(ti, block), dst_ref=slot(ti, block),
                send_sem=send_sems.at[n_t * k + ti], recv_sem=recv_sems.at[n_t * k + ti],
                device_id=to, device_id_type=MESH)

        mine = [pltpu.make_async_copy(shards[ti], slot(ti, me), local_sems.at[ti]) for ti in range(n_t)]
        for cp in mine:
            cp.start()
        first = [copy(0, ti, me, sibling, own=True) for ti in range(n_t)]
        first += [copy(1 + j, ti, me, (*chip, c), own=True) for j, chip in enumerate(chips) for ti in range(n_t)]
        for cp in first:
            cp.start()
        passed = []
        for j, chip in enumerate(chips):
            for ti in range(n_t):
                copy(1 + j, ti, (*chip, c), me).wait_recv()
            onward = [copy(4 + j, ti, (*chip, c), sibling) for ti in range(n_t)]
            for cp in onward:
                cp.start()
            passed += onward
        for ti in range(n_t):
            copy(0, ti, sibling, me).wait_recv()
        for j, chip in enumerate(chips):
            for ti in range(n_t):
                copy(4 + j, ti, (*chip, 1 - c), me).wait_recv()
        for cp in first + passed:
            cp.wait_send()
        for cp in mine:
            cp.wait()

    return pl.pallas_call(
        body,
        name="allgather_weights",
        in_specs=[ANY] * n_t,
        out_specs=[ANY] * n_t,
        out_shape=[jax.ShapeDtypeStruct(s, BF16) for s in WHOLE_SHAPES],
        scratch_shapes=[pltpu.SemaphoreType.DMA((7 * n_t,)), pltpu.SemaphoreType.DMA((7 * n_t,)),
                        pltpu.SemaphoreType.DMA((n_t,))],
    )(wi, wo, wg)


HBM = pl.BlockSpec(memory_space=pltpu.HBM)
SEM = pl.BlockSpec(memory_space=pltpu.SEMAPHORE)
GATHER, SCATTER, SHARE = "gather", "scatter", "share"


def _split_route(kind, ti, sending, me_idx, p_idx, src_ref, land_ref):
    owner = me_idx if sending else p_idx
    if kind == GATHER:
        return src_ref, _shard_of(land_ref, ti, owner)
    if kind == SCATTER:
        return _shard_of(src_ref, ti, p_idx), land_ref.at[owner]
    return src_ref, land_ref.at[owner]


def _split_start(name, srcs, lands, kinds, after=None):
    n = len(srcs)
    arrays = list(srcs) + list(lands) + ([] if after is None else [after])

    def body(*refs):
        src_refs, land_refs = refs[0:n], refs[n:2 * n]
        send_sems, recv_sems, token = refs[len(arrays)], refs[len(arrays) + 1], refs[-1]
        x, y, c = lax.axis_index("x"), lax.axis_index("y"), lax.axis_index("c")
        me_idx = _dev_index((x, y, c))
        for mask in range(N_DEV):
            p = _peer(mask, x, y, c)
            for i, (kind, ti) in enumerate(kinds):
                k = mask * n + i
                src, dst = _split_route(kind, ti, True, me_idx, _dev_index(p), src_refs[i], land_refs[i])
                pltpu.make_async_remote_copy(src_ref=src, dst_ref=dst, send_sem=send_sems.at[k], recv_sem=recv_sems.at[k],
                                             device_id=p, device_id_type=MESH).start()
        token[...] = jnp.zeros_like(token)

    n_copies = N_DEV * n
    return pl.pallas_call(
        body,
        name=name,
        in_specs=[HBM] * len(arrays),
        out_specs=(SEM, SEM) + (HBM,) * (2 * n) + (pl.BlockSpec(memory_space=pltpu.VMEM),),
        out_shape=(pltpu.SemaphoreType.DMA((n_copies,)), pltpu.SemaphoreType.DMA((n_copies,)))
        + tuple(pltpu.HBM(a.shape, a.dtype) for a in arrays[:2 * n]) + (jax.ShapeDtypeStruct((8, 128), F32),),
        input_output_aliases={i: 2 + i for i in range(2 * n)},
        compiler_params=pltpu.CompilerParams(has_side_effects=pltpu.SideEffectType.DATAFLOW_SIDE_EFFECTING),
    )(*[pltpu.with_memory_space_constraint(a, pltpu.HBM) for a in arrays])


def _split_wait(name, started, kinds, after):
    n = len(kinds)
    send_sems, recv_sems, thru = started[0], started[1], started[2:2 + 2 * n]

    def body(*refs):
        src_refs, land_refs = refs[0:n], refs[n:2 * n]
        send_sems, recv_sems = refs[2 * n], refs[2 * n + 1]
        x, y, c = lax.axis_index("x"), lax.axis_index("y"), lax.axis_index("c")
        me_idx = _dev_index((x, y, c))
        for mask in range(N_DEV):
            p = _peer(mask, x, y, c)
            for i, (kind, ti) in enumerate(kinds):
                k = mask * n + i
                src, dst = _split_route(kind, ti, False, me_idx, _dev_index(p), src_refs[i], land_refs[i])
                cp = pltpu.make_async_remote_copy(src_ref=src, dst_ref=dst, send_sem=send_sems.at[k],
                                                  recv_sem=recv_sems.at[k], device_id=p, device_id_type=MESH)
                cp.wait_send()
                cp.wait_recv()

    res = pl.pallas_call(
        body,
        name=name,
        in_specs=[HBM] * (2 * n) + [SEM, SEM, pl.BlockSpec(memory_space=pl.ANY)],
        out_specs=(HBM,) * (2 * n),
        out_shape=tuple(pltpu.HBM(a.shape, a.dtype) for a in thru),
        input_output_aliases={i: i for i in range(2 * n)},
        compiler_params=pltpu.CompilerParams(has_side_effects=pltpu.SideEffectType.DATAFLOW_SIDE_EFFECTING),
    )(*thru, send_sems, recv_sems, after)
    return res[n:2 * n]


def _empty_zones(srcs, kinds):
    def shape(src, kind, ti):
        if kind == GATHER:
            return WHOLE_SHAPES[ti]
        return (N_DEV,) + (SHARD_SHAPES[ti] if kind == SCATTER else src.shape)

    return [lax.empty(shape(src, *k), src.dtype) for src, k in zip(srcs, kinds)]


def _share_small(buf):
    def body(b_ref, o_ref, send_sems, recv_sems, local_sem):
        x, y, c = lax.axis_index("x"), lax.axis_index("y"), lax.axis_index("c")
        me_idx = _dev_index((x, y, c))
        local = pltpu.make_async_copy(b_ref, o_ref.at[me_idx], local_sem)
        local.start()

        def copy(mask, owner):
            return pltpu.make_async_remote_copy(
                src_ref=b_ref, dst_ref=o_ref.at[owner], send_sem=send_sems.at[mask - 1], recv_sem=recv_sems.at[mask - 1],
                device_id=_peer(mask, x, y, c), device_id_type=MESH)

        sends = [copy(mask, me_idx) for mask in range(1, N_DEV)]
        for cp in sends:
            cp.start()
        for mask in range(1, N_DEV):
            copy(mask, _dev_index(_peer(mask, x, y, c))).wait_recv()
        for cp in sends:
            cp.wait_send()
        local.wait()

    return pl.pallas_call(
        body,
        name="share_small",
        in_specs=[ANY],
        out_specs=ANY,
        out_shape=jax.ShapeDtypeStruct((N_DEV,) + buf.shape, buf.dtype),
        scratch_shapes=[pltpu.SemaphoreType.DMA((N_DEV - 1,)), pltpu.SemaphoreType.DMA((N_DEV - 1,)),
                        pltpu.SemaphoreType.DMA],
    )(buf)


def _s5_prep(lam_re, lam_im, b_re, b_im, c_re, c_im, d_skip, log_dt):
    dt = jnp.exp(log_dt)[:, None]
    mag = jnp.exp(lam_re * dt)
    a_re, a_im = mag * jnp.cos(lam_im * dt), mag * jnp.sin(lam_im * dt)
    den = lam_re * lam_re + lam_im * lam_im
    f_re = ((a_re - 1.0) * lam_re + a_im * lam_im) / den
    f_im = (a_im * lam_re - (a_re - 1.0) * lam_im) / den
    bb_re = f_re[..., None] * b_re - f_im[..., None] * b_im
    bb_im = f_re[..., None] * b_im + f_im[..., None] * b_re
    eye = jnp.eye(8, dtype=F32)

    def in_map(bb):
        return jnp.einsum("jgph,gk->ghjkp", bb.reshape(4, 8, S5_STATE, S5_CH), eye).reshape(128, N_STATE)

    def out_map(cm):
        return jnp.einsum("jghp,gk->ghjkp", cm.reshape(4, 8, S5_CH, S5_STATE), eye).reshape(128, N_STATE)

    a = jnp.concatenate([a_re.reshape(N_PAIR, 128), a_im.reshape(N_PAIR, 128)])
    bc = jnp.concatenate([in_map(bb_re), in_map(bb_im)], axis=1)
    cct = jnp.concatenate([out_map(c_re), out_map(-c_im)], axis=1)
    return a, bc, cct, d_skip.reshape(1, S5_W)


WEIGHTS = ["norm_g", "w_in", "lam_re", "lam_im", "b_re", "b_im", "c_re", "c_im", "d_skip", "log_dt", "w_glu", "b_glu",
           "ln_g", "ln_b", "w_s", "b_s", "w_pool", "pool_scale", "w_out", "final_g"]
SHARDED = ("w_in", "w_glu", "w_out")
SMALL = [n for n in WEIGHTS if n not in SHARDED]
INPUTS = ["x"] + WEIGHTS + ["loss_target"] + ["m_" + n for n in WEIGHTS] + ["v_" + n for n in WEIGHTS]
SMALL_B = ["norm_g", "final_g"]
SMALL_A = [n for n in SMALL if n not in SMALL_B]
SMALL_TILE = 16 * 128
LOSS_AT = (DEPTH + 1) * D_MODEL


def _pack_small(arrays, dtype):
    flat = jnp.concatenate([a.reshape(-1) for a in arrays])
    pad = (-flat.shape[0]) % SMALL_TILE
    return jnp.pad(flat, (0, pad)).astype(dtype).reshape(-1, 128)


def _unpack_small(packed, like):
    flat = packed.reshape(-1)
    out, off = [], 0
    for a in like:
        out.append(flat[off:off + a.size].reshape(a.shape))
        off += a.size
    return out


def _layer_fwd(p, l, x, wi, wo, wg, token=None):
    row = lambda v: v.reshape(1, -1)
    causal = jnp.tril(jnp.ones((CHUNK, CHUNK), dtype=bool))
    (a, bc, cct, dvec), prep_vjp = jax.vjp(
        _s5_prep, p["lam_re"][l], p["lam_im"][l], p["b_re"][l], p["b_im"][l], p["c_re"][l], p["c_im"][l],
        p["d_skip"][l], p["log_dt"][l])
    ws_f32 = jnp.where(causal[None], p["w_s"][l], 0.0)
    pm, pt = _step_major(S5_TB)
    c = dict(
        x=x, wi=wi, wo=wo, wg=wg, a=a, bc=bc.astype(BF16), cc=cct.T.astype(BF16), dvec=dvec, prep_vjp=prep_vjp,
        pm=pm, pt=pt,
        ws=ws_f32.astype(BF16), wst=jnp.swapaxes(ws_f32, 1, 2).astype(BF16),
        bsx=jnp.broadcast_to(p["b_s"][l][:, :, None], (SGU_HEADS, CHUNK, 128)),
        wpool=p["w_pool"][l].astype(BF16), scale=row(p["pool_scale"][l]),
        lng=row(p["ln_g"][l]), lnb=row(p["ln_b"][l]), bglu=row(p["b_glu"][l]), norm_g=row(p["norm_g"][l]))
    c["z"], c["h"] = _rms_inproj(x, c["norm_g"], wi, token)
    c["ya"], c["s"], c["ys"] = _s5_fwd(c["z"], c["bc"], c["cc"], a, pm, pt, dvec, wg, c["bglu"], S5_TB)
    c["yb"] = _sgu_fwd(c["z"], c["lng"], c["lnb"], c["ws"], c["bsx"])
    c["yc"] = _pool_fwd(c["z"], c["wpool"], c["scale"])
    return _outproj(x, c["ya"], c["yb"], c["yc"], wo), c


def _mixers_bwd(c, dx, token=None):
    dya, dyb, dyc = _outproj_bwd(dx, c["wo"], token)
    dwo = _wgrad_out(c["ya"], c["yb"], c["yc"], dx, token)
    dz_s5, dbc, dcct, da, dd, dwg, dbglu = _s5_bwd(
        dya, c["ys"], c["z"], c["s"], c["bc"], c["cc"], c["a"], c["pm"], c["pt"], c["dvec"], c["wg"], c["bglu"], S5_TB)
    dz_sgu, dlng, dlnb, dws, dbsx = _sgu_bwd(dyb, c["z"], c["lng"], c["lnb"], c["ws"], c["wst"], c["bsx"])
    dz_pool, dwp, dsc = _pool_bwd(dyc, c["z"], c["wpool"], c["scale"])
    g_lam_re, g_lam_im, g_b_re, g_b_im, g_c_re, g_c_im, g_d, g_dt = c["prep_vjp"]((da, dbc, dcct, dd))
    small = dict(lam_re=g_lam_re, lam_im=g_lam_im, b_re=g_b_re, b_im=g_b_im, c_re=g_c_re,
                 c_im=g_c_im, d_skip=g_d, log_dt=g_dt, b_glu=dbglu.reshape(-1), ln_g=dlng.reshape(-1),
                 ln_b=dlnb.reshape(-1), w_s=dws, b_s=jnp.sum(dbsx, axis=-1), w_pool=dwp, pool_scale=dsc.reshape(-1))
    return (dz_s5, dz_sgu, dz_pool), dwo, dwg, small


def _inproj_grads(c, dz, dx, token_w=None, token_x=None):
    dwi = _wgrad_in(c["h"], *dz, token_w)
    dx, dnorm = _inproj_bwd(*dz, c["wi"], c["x"], c["norm_g"], dx, token_x)
    return dwi, dx, dnorm.reshape(-1)


def _own_index():
    return _dev_index((lax.axis_index("x"), lax.axis_index("y"), lax.axis_index("c")))


def kernel(x, norm_g, w_in, lam_re, lam_im, b_re, b_im, c_re, c_im, d_skip, log_dt, w_glu, b_glu, ln_g, ln_b, w_s, b_s, w_pool, pool_scale, w_out, final_g, loss_target, m_norm_g, m_w_in, m_lam_re, m_lam_im, m_b_re, m_b_im, m_c_re, m_c_im, m_d_skip, m_log_dt, m_w_glu, m_b_glu, m_ln_g, m_ln_b, m_w_s, m_b_s, m_w_pool, m_pool_scale, m_w_out, m_final_g, v_norm_g, v_w_in, v_lam_re, v_lam_im, v_b_re, v_b_im, v_c_re, v_c_im, v_d_skip, v_log_dt, v_w_glu, v_b_glu, v_ln_g, v_ln_b, v_w_s, v_b_s, v_w_pool, v_pool_scale, v_w_out, v_final_g):
    p = dict(zip(INPUTS, (x, norm_g, w_in, lam_re, lam_im, b_re, b_im, c_re, c_im, d_skip, log_dt, w_glu, b_glu, ln_g, ln_b, w_s, b_s, w_pool, pool_scale, w_out, final_g, loss_target, m_norm_g, m_w_in, m_lam_re, m_lam_im, m_b_re, m_b_im, m_c_re, m_c_im, m_d_skip, m_log_dt, m_w_glu, m_b_glu, m_ln_g, m_ln_b, m_w_s, m_b_s, m_w_pool, m_pool_scale, m_w_out, m_final_g, v_norm_g, v_w_in, v_lam_re, v_lam_im, v_b_re, v_b_im, v_c_re, v_c_im, v_d_skip, v_log_dt, v_w_glu, v_b_glu, v_ln_g, v_ln_b, v_w_s, v_b_s, v_w_pool, v_pool_scale, v_w_out, v_final_g)))

    shards = [[w[l].astype(BF16) for w in (w_in, w_out, w_glu)] for l in range(DEPTH)]
    gather3 = [(GATHER, ti) for ti in range(3)]
    scatter3 = [(SCATTER, ti) for ti in range(3)]

    whole0 = _allgather_weights(*shards[0])
    gather1 = _split_start("gather1_start", shards[1], _empty_zones(shards[1], gather3), gather3, after=whole0[2])
    x1, c0 = _layer_fwd(p, 0, x[0], *whole0, token=gather1[-1])
    whole1 = _split_wait("gather1_wait", gather1, gather3, x1)
    x2, c1 = _layer_fwd(p, 1, x1, *whole1)

    dx, loss_tile, dfinal = _final_loss(x2, final_g.reshape(1, -1), loss_target[0])

    dz1, dwo1, dwg1, small1 = _mixers_bwd(c1, dx)
    dwi1, dx, dnorm1 = _inproj_grads(c1, dz1, dx)
    partials1 = [dwi1, dwo1, dwg1.astype(BF16)]
    grads1 = _split_start("grads1_start", partials1, _empty_zones(partials1, scatter3), scatter3)
    dz0, dwo0, dwg0, small0 = _mixers_bwd(c0, dx, token=grads1[-1])
    small_a = _pack_small([jnp.stack([small0[n], small1[n]]) for n in SMALL_A], BF16)
    srcs_a, kinds_a = [dwo0, dwg0.astype(BF16), small_a], [(SCATTER, 1), (SCATTER, 2), (SHARE, None)]
    grads0a = _split_start("grads0a_start", srcs_a, _empty_zones(srcs_a, kinds_a), kinds_a)
    dwi0 = _wgrad_in(c0["h"], *dz0, grads0a[-1])
    kinds_b = [(SCATTER, 0)]
    grads0b = _split_start("grads0b_start", [dwi0], _empty_zones([dwi0], kinds_b), kinds_b)
    dx, dnorm0 = _inproj_bwd(*dz0, c0["wi"], c0["x"], c0["norm_g"], dx, grads0b[-1])
    parts1 = _split_wait("grads1_wait", grads1, scatter3, dx)
    r_out0, r_glu0, r_a = _split_wait("grads0a_wait", grads0a, kinds_a, dx)
    (r_in0,) = _split_wait("grads0b_wait", grads0b, kinds_b, dx)
    parts0 = [r_in0, r_out0, r_glu0]
    r_b = _share_small(_pack_small(
        [jnp.stack([dnorm0.reshape(-1), dnorm1]), dfinal.reshape(-1), loss_tile[0, 0:1]], F32))
    loss = jnp.sum(r_b.reshape(N_DEV, -1)[:, LOSS_AT])

    out = {}

    def adam(name, ti, tr):
        shape2d = (DEPTH * SHARD_SHAPES[ti][0], SHARD_SHAPES[ti][1])
        res = _adamw(p[name].reshape(shape2d), p["m_" + name].reshape(shape2d), p["v_" + name].reshape(shape2d),
                     [parts0[ti], parts1[ti]], tr, "adamw_" + name)
        out[name] = [r.reshape(p[name].shape) for r in res]

    adam("w_in", 0, 512)
    adam("w_out", 1, 128)
    adam("w_glu", 2, 64)
    g_a, g_b = _sum_small(r_a, r_b)
    grads = dict(zip(SMALL_A, _unpack_small(g_a, [p[n] for n in SMALL_A])))
    grads.update(zip(SMALL_B, _unpack_small(g_b, [p[n] for n in SMALL_B])))
    rank2 = lambda a: a.reshape(1, -1) if a.ndim == 1 else a
    res = _adamw_small(*[[rank2(src[pre + n]) for n in SMALL] for src, pre in ((p, ""), (p, "m_"), (p, "v_"), (grads, ""))])
    for i, n in enumerate(SMALL):
        out[n] = [grads[n]] + [r.reshape(p[n].shape) for r in res[3 * i:3 * i + 3]]

    return (loss, dx[None], *[out[n][0] for n in WEIGHTS], *[out[n][1] for n in WEIGHTS],
            *[out[n][2] for n in WEIGHTS], *[out[n][3] for n in WEIGHTS])
```

```python
import math

import jax
import jax.numpy as jnp
import numpy as np
from jax import lax
from jax.experimental import pallas as pl
from jax.experimental.pallas import tpu as pltpu

F32 = jnp.float32
BF16 = jnp.bfloat16

D_MODEL = 2048
DEPTH = 2
S5_W, SGU_W, POOL_W = 512, 1024, 512
S5_GROUPS, S5_STATE, S5_CH = 32, 64, 16
N_STATE = S5_GROUPS * S5_STATE
CHUNK = 128
SGU_HEADS = 8
POOL_WINDOWS = (2, 4, 8, 16)
IN_COLS = 5120
RMS_EPS = 1e-6
LN_EPS = 1e-5
ADAM_LR, ADAM_B1, ADAM_B2, ADAM_EPS, ADAM_WD, ADAM_STEP = 0.001, 0.9, 0.999, 1e-08, 0.01, 10

CB = 512
N_CB = IN_COLS // CB
CB_XA, CB_U, CB_V, CB_XC, CB_GA, CB_GB, CB_GC = 0, 1, 3, 5, 6, 7, 9

N_DEV = 8
SH_IN = IN_COLS // N_DEV
SH_OUT = D_MODEL // N_DEV
SH_GLU = S5_W // N_DEV

VMEM_LIMIT = 52 * 1024 * 1024
VMEM_LIMIT_WHOLE = 56 * 1024 * 1024
HALO = 16
LANE_CH = 512

TN = (((0,), (0,)), ((), ()))
NT = (((1,), (1,)), ((), ()))


def _params(*sem):
    return pltpu.CompilerParams(dimension_semantics=sem if sem else None, vmem_limit_bytes=VMEM_LIMIT)


def _dot(a, b, dims=None):
    if dims is None:
        return jnp.dot(a, b, preferred_element_type=F32)
    return lax.dot_general(a, b, dims, preferred_element_type=F32)


_GELU_C = math.sqrt(2.0 / math.pi)


def _gelu(x):
    return 0.5 * x * (1.0 + jnp.tanh(_GELU_C * (x + 0.044715 * x * x * x)))


def _gelu_and_grad(x):
    t = jnp.tanh(_GELU_C * (x + 0.044715 * x * x * x))
    half = 0.5 * (1.0 + t)
    return x * half, half + 0.5 * x * (1.0 - t * t) * _GELU_C * (1.0 + 3.0 * 0.044715 * x * x)


def _gelu_grad(x):
    return _gelu_and_grad(x)[1]


def _silu_and_grad(x):
    s = jax.nn.sigmoid(x)
    return x * s, s * (1.0 + x * (1.0 - s))


def _rowsum(x):
    return jnp.sum(x, axis=0, keepdims=True)


def _after(token):
    if token is None:
        return [], []
    return [pl.BlockSpec((8, 128), lambda *_: (0, 0))], [token]


def _rms_inproj(x, g, w, token=None, tm=1024, tn=1024):
    t = x.shape[0]
    tm = min(tm, t)
    after_specs, after = _after(token)

    def body(x_ref, g_ref, w_ref, *rest):
        z_ref, h_ref = rest[-2:]

        @pl.when(pl.program_id(1) == 0)
        def _():
            xv = x_ref[...]
            r = lax.rsqrt(jnp.mean(xv * xv, axis=-1, keepdims=True) + RMS_EPS)
            h_ref[...] = (xv * r * g_ref[...]).astype(BF16)

        z_ref[...] = _dot(h_ref[...], w_ref[...])

    return pl.pallas_call(
        body,
        name="rms_inproj",
        grid=(t // tm, IN_COLS // tn),
        in_specs=[
            pl.BlockSpec((tm, D_MODEL), lambda m, n: (m, 0)),
            pl.BlockSpec((1, D_MODEL), lambda m, n: (0, 0)),
            pl.BlockSpec((D_MODEL, tn), lambda m, n: (0, n)),
        ] + after_specs,
        out_specs=[
            pl.BlockSpec((tm, tn), lambda m, n: (m, n)),
            pl.BlockSpec((tm, D_MODEL), lambda m, n: (m, 0)),
        ],
        out_shape=[jax.ShapeDtypeStruct((t, IN_COLS), F32), jax.ShapeDtypeStruct((t, D_MODEL), BF16)],
        compiler_params=_params("arbitrary", "arbitrary"),
    )(x, g, w, *after)


def _outproj(x, ya, yb, yc, w, tm=1024, tn=1024):
    t = x.shape[0]
    tm = min(tm, t)

    def body(x_ref, ya_ref, yb_ref, yc_ref, w_ref, o_ref):
        acc = _dot(ya_ref[...], w_ref[0:S5_W, :])
        acc += _dot(yb_ref[...], w_ref[S5_W:S5_W + SGU_W, :])
        acc += _dot(yc_ref[...], w_ref[S5_W + SGU_W:, :])
        o_ref[...] = x_ref[...] + acc

    return pl.pallas_call(
        body,
        name="outproj",
        grid=(t // tm, D_MODEL // tn),
        in_specs=[
            pl.BlockSpec((tm, tn), lambda m, n: (m, n)),
            pl.BlockSpec((tm, S5_W), lambda m, n: (m, 0)),
            pl.BlockSpec((tm, SGU_W), lambda m, n: (m, 0)),
            pl.BlockSpec((tm, POOL_W), lambda m, n: (m, 0)),
            pl.BlockSpec((D_MODEL, tn), lambda m, n: (0, n)),
        ],
        out_specs=pl.BlockSpec((tm, tn), lambda m, n: (m, n)),
        out_shape=jax.ShapeDtypeStruct((t, D_MODEL), F32),
        compiler_params=_params("arbitrary", "arbitrary"),
    )(x, ya, yb, yc, w)


def _outproj_bwd(dx, w, token=None, tm=1024):
    t = dx.shape[0]
    tm = min(tm, t)
    after_specs, after = _after(token)

    def body(dx_ref, w_ref, *rest):
        dya_ref, dyb_ref, dyc_ref = rest[-3:]
        dy = _dot(dx_ref[...].astype(BF16), w_ref[...], NT)
        dya_ref[...] = dy[:, 0:S5_W]
        dyb_ref[...] = dy[:, S5_W:S5_W + SGU_W]
        dyc_ref[...] = dy[:, S5_W + SGU_W:]

    return pl.pallas_call(
        body,
        name="outproj_bwd",
        grid=(t // tm,),
        in_specs=[
            pl.BlockSpec((tm, D_MODEL), lambda m: (m, 0)),
            pl.BlockSpec((D_MODEL, D_MODEL), lambda m: (0, 0)),
        ] + after_specs,
        out_specs=[
            pl.BlockSpec((tm, S5_W), lambda m: (m, 0)),
            pl.BlockSpec((tm, SGU_W), lambda m: (m, 0)),
            pl.BlockSpec((tm, POOL_W), lambda m: (m, 0)),
        ],
        out_shape=[
            jax.ShapeDtypeStruct((t, S5_W), F32),
            jax.ShapeDtypeStruct((t, SGU_W), F32),
            jax.ShapeDtypeStruct((t, POOL_W), F32),
        ],
        compiler_params=_params("arbitrary"),
    )(dx, w, *after)


def _dz_piece_maps():
    s5_map = lambda j: jnp.where(j >= CB_GA, 1, 0)
    sgu_map = lambda j: jnp.clip(jnp.where(j <= 4, j - 1, j - 3), 0, 5)
    pool_map = lambda j: jnp.where(j >= CB_GC, 1, 0)
    return s5_map, sgu_map, pool_map


def _pick_piece(j):
    is_s5 = jnp.logical_or(j == CB_XA, j == CB_GA)
    is_pool = jnp.logical_or(j == CB_XC, j == CB_GC)
    return is_s5, is_pool, jnp.logical_not(jnp.logical_or(is_s5, is_pool))


def _inproj_bwd(dz_s5, dz_sgu, dz_pool, w, x, g, dxo, token=None, tm=512):
    t = x.shape[0]
    s5_map, sgu_map, pool_map = _dz_piece_maps()
    after_specs, after = _after(token)

    def body(s5_ref, sgu_ref, pool_ref, w_ref, x_ref, g_ref, dxo_ref, *rest):
        dx_ref, dg_ref, acc = rest[-3:]
        m, j = pl.program_id(0), pl.program_id(1)

        @pl.when(jnp.logical_and(m == 0, j == 0))
        def _():
            dg_ref[...] = jnp.zeros_like(dg_ref)

        @pl.when(j == 0)
        def _():
            acc[...] = jnp.zeros_like(acc)

        is_s5, is_pool, is_sgu = _pick_piece(j)

        @pl.when(is_s5)
        def _():
            acc[...] += _dot(w_ref[...], s5_ref[...], NT)

        @pl.when(is_sgu)
        def _():
            acc[...] += _dot(w_ref[...], sgu_ref[...], NT)

        @pl.when(is_pool)
        def _():
            acc[...] += _dot(w_ref[...], pool_ref[...], NT)

        @pl.when(j == N_CB - 1)
        def _():
            xv = x_ref[...]
            r = lax.rsqrt(jnp.mean(xv * xv, axis=-1, keepdims=True) + RMS_EPS)
            n = xv * r
            dh = acc[...].T
            dg_ref[...] += _rowsum(dh * n)
            dn = dh * g_ref[...]
            dx_ref[...] = dxo_ref[...] + r * (dn - n * jnp.mean(dn * n, axis=-1, keepdims=True))

    return pl.pallas_call(
        body,
        name="inproj_bwd",
        grid=(t // tm, N_CB),
        in_specs=[
            pl.BlockSpec((tm, CB), lambda m, j: (m, s5_map(j))),
            pl.BlockSpec((tm, CB), lambda m, j: (m, sgu_map(j))),
            pl.BlockSpec((tm, CB), lambda m, j: (m, pool_map(j))),
            pl.BlockSpec((D_MODEL, CB), lambda m, j: (0, j)),
            pl.BlockSpec((tm, D_MODEL), lambda m, j: (m, 0)),
            pl.BlockSpec((1, D_MODEL), lambda m, j: (0, 0)),
            pl.BlockSpec((tm, D_MODEL), lambda m, j: (m, 0)),
        ] + after_specs,
        out_specs=[
            pl.BlockSpec((tm, D_MODEL), lambda m, j: (m, 0)),
            pl.BlockSpec((1, D_MODEL), lambda m, j: (0, 0)),
        ],
        out_shape=[jax.ShapeDtypeStruct((t, D_MODEL), F32), jax.ShapeDtypeStruct((1, D_MODEL), F32)],
        scratch_shapes=[pltpu.VMEM((D_MODEL, tm), F32)],
        compiler_params=_params("arbitrary", "arbitrary"),
    )(dz_s5, dz_sgu, dz_pool, w, x, g, dxo, *after)


def _wgrad_in(h, dz_s5, dz_sgu, dz_pool, token=None, tm=2048):
    t = h.shape[0]
    tm = min(tm, t)
    s5_map, sgu_map, pool_map = _dz_piece_maps()
    after_specs, after = _after(token)

    def body(h_ref, s5_ref, sgu_ref, pool_ref, *rest):
        o_ref, acc = rest[-2:]
        j, m = pl.program_id(0), pl.program_id(1)

        @pl.when(m == 0)
        def _():
            acc[...] = jnp.zeros_like(acc)

        is_s5, is_pool, is_sgu = _pick_piece(j)

        @pl.when(is_s5)
        def _():
            acc[...] += _dot(s5_ref[...], h_ref[...], TN)

        @pl.when(is_sgu)
        def _():
            acc[...] += _dot(sgu_ref[...], h_ref[...], TN)

        @pl.when(is_pool)
        def _():
            acc[...] += _dot(pool_ref[...], h_ref[...], TN)

        @pl.when(m == pl.num_programs(1) - 1)
        def _():
            o_ref[...] = acc[...].T.astype(BF16)

    return pl.pallas_call(
        body,
        name="wgrad_in",
        grid=(N_CB, t // tm),
        in_specs=[
            pl.BlockSpec((tm, D_MODEL), lambda j, m: (m, 0)),
            pl.BlockSpec((tm, CB), lambda j, m: (m, s5_map(j))),
            pl.BlockSpec((tm, CB), lambda j, m: (m, sgu_map(j))),
            pl.BlockSpec((tm, CB), lambda j, m: (m, pool_map(j))),
        ] + after_specs,
        out_specs=pl.BlockSpec((D_MODEL, CB), lambda j, m: (0, j)),
        out_shape=jax.ShapeDtypeStruct((D_MODEL, IN_COLS), BF16),
        scratch_shapes=[pltpu.VMEM((CB, D_MODEL), F32)],
        compiler_params=_params("arbitrary", "arbitrary"),
    )(h, dz_s5, dz_sgu, dz_pool, *after)


def _wgrad_out(ya, yb, yc, dx, token=None, tm=2048, tn=512):
    t = dx.shape[0]
    tm = min(tm, t)
    after_specs, after = _after(token)

    def body(ya_ref, yb_ref, yc_ref, dx_ref, *rest):
        o_ref, acc = rest[-2:]
        m = pl.program_id(1)

        @pl.when(m == 0)
        def _():
            acc[...] = jnp.zeros_like(acc)

        dxb = dx_ref[...].astype(BF16)
        acc[:, 0:S5_W] += _dot(dxb, ya_ref[...], TN)
        acc[:, S5_W:S5_W + SGU_W] += _dot(dxb, yb_ref[...], TN)
        acc[:, S5_W + SGU_W:] += _dot(dxb, yc_ref[...], TN)

        @pl.when(m == pl.num_programs(1) - 1)
        def _():
            o_ref[...] = acc[...].T.astype(BF16)

    return pl.pallas_call(
        body,
        name="wgrad_out",
        grid=(D_MODEL // tn, t // tm),
        in_specs=[
            pl.BlockSpec((tm, S5_W), lambda n, m: (m, 0)),
            pl.BlockSpec((tm, SGU_W), lambda n, m: (m, 0)),
            pl.BlockSpec((tm, POOL_W), lambda n, m: (m, 0)),
            pl.BlockSpec((tm, tn), lambda n, m: (m, n)),
        ] + after_specs,
        out_specs=pl.BlockSpec((D_MODEL, tn), lambda n, m: (0, n)),
        out_shape=jax.ShapeDtypeStruct((D_MODEL, D_MODEL), BF16),
        scratch_shapes=[pltpu.VMEM((tn, D_MODEL), F32)],
        compiler_params=_params("arbitrary", "arbitrary"),
    )(ya, yb, yc, dx, *after)


def _final_loss(x, g, target, tm=512):
    t = x.shape[0]

    def body(x_ref, g_ref, t_ref, dx_ref, loss_ref, dg_ref):
        @pl.when(pl.program_id(0) == 0)
        def _():
            loss_ref[...] = jnp.zeros_like(loss_ref)
            dg_ref[...] = jnp.zeros_like(dg_ref)

        xv = x_ref[...]
        gv = g_ref[...]
        r = lax.rsqrt(jnp.mean(xv * xv, axis=-1, keepdims=True) + RMS_EPS)
        n = xv * r
        err = n * gv - t_ref[...]
        loss_ref[...] += 0.5 * jnp.sum(jnp.mean(err * err, axis=-1, keepdims=True))
        dy = err * (1.0 / D_MODEL)
        dg_ref[...] += _rowsum(dy * n)
        dn = dy * gv
        dx_ref[...] = r * (dn - n * jnp.mean(dn * n, axis=-1, keepdims=True))

    return pl.pallas_call(
        body,
        name="final_loss",
        grid=(t // tm,),
        in_specs=[
            pl.BlockSpec((tm, D_MODEL), lambda m: (m, 0)),
            pl.BlockSpec((1, D_MODEL), lambda m: (0, 0)),
            pl.BlockSpec((tm, D_MODEL), lambda m: (m, 0)),
        ],
        out_specs=[
            pl.BlockSpec((tm, D_MODEL), lambda m: (m, 0)),
            pl.BlockSpec((8, 128), lambda m: (0, 0)),
            pl.BlockSpec((1, D_MODEL), lambda m: (0, 0)),
        ],
        out_shape=[
            jax.ShapeDtypeStruct((t, D_MODEL), F32),
            jax.ShapeDtypeStruct((8, 128), F32),
            jax.ShapeDtypeStruct((1, D_MODEL), F32),
        ],
        compiler_params=_params("arbitrary"),
    )(x, g, target)


N_Q = 2 * N_STATE // LANE_CH
N_LT = 2 * N_STATE // 128
N_PAIR = N_LT // 2
SEG = 8
PAIR_GROUP = 8
S5_TB = 512


def _cmul_add(b_re, b_im, a_re, a_im, s_re, s_im):
    return b_re + (a_re * s_re - a_im * s_im), b_im + (a_re * s_im + a_im * s_re)


def _s5_fill_powers(pw, a_ref, tb, reverse):
    seg_len = tb // SEG
    sign = -1.0 if reverse else 1.0
    for p in range(N_PAIR):
        a_re = jnp.broadcast_to(a_ref[p:p + 1, :], (SEG, 128))
        a_im = sign * jnp.broadcast_to(a_ref[N_PAIR + p:N_PAIR + p + 1, :], (SEG, 128))

        def step(k, c, p=p, a_re=a_re, a_im=a_im):
            rows = pl.ds(pl.multiple_of(((seg_len - 1 - k) if reverse else k) * SEG, SEG), SEG)
            pw[p, rows, :] = c[0]
            pw[N_PAIR + p, rows, :] = c[1]
            return c[0] * a_re - c[1] * a_im, c[0] * a_im + c[1] * a_re

        lax.fori_loop(0, seg_len, step, (a_re, a_im))


def _s5_scan(st, carry, a_ref, pw_ref, tb, reverse):
    seg_len = tb // SEG
    sign = -1.0 if reverse else 1.0
    sub = lax.broadcasted_iota(jnp.int32, (SEG, 128), 0)
    chain = (0 if reverse else seg_len - 1) * SEG
    full = lambda row: jnp.broadcast_to(row, (SEG, 128))
    for p0 in range(0, N_PAIR, PAIR_GROUP):
        pairs = list(range(p0, p0 + PAIR_GROUP))
        a_re = [full(a_ref[p:p + 1, :]) for p in pairs]
        a_im = [sign * full(a_ref[N_PAIR + p:N_PAIR + p + 1, :]) for p in pairs]

        def step(k, c, pairs=pairs, a_re=a_re, a_im=a_im):
            rows = pl.ds(pl.multiple_of(((seg_len - 1 - k) if reverse else k) * SEG, SEG), SEG)
            out = []
            for i, p in enumerate(pairs):
                n_re, n_im = _cmul_add(st[p, rows, :], st[N_PAIR + p, rows, :], a_re[i], a_im[i], c[2 * i], c[2 * i + 1])
                st[p, rows, :] = n_re
                st[N_PAIR + p, rows, :] = n_im
                out += [n_re, n_im]
            return tuple(out)

        ends = lax.fori_loop(0, seg_len, step, tuple(jnp.zeros((SEG, 128), F32) for _ in range(2 * PAIR_GROUP)))
        for i, p in enumerate(pairs):
            e_re, e_im = ends[2 * i], ends[2 * i + 1]
            w_re, w_im = pw_ref[p, chain:chain + SEG, :], pw_ref[N_PAIR + p, chain:chain + SEG, :]
            c_re, c_im = full(carry[p:p + 1, :]), full(carry[N_PAIR + p:N_PAIR + p + 1, :])
            for hop in range(SEG - 1):
                n_re, n_im = _cmul_add(e_re, e_im, w_re, w_im, c_re, c_im)
                target = SEG - 2 - hop if reverse else hop + 1
                shift = SEG - 1 if reverse else 1
                c_re = jnp.where(sub == target, pltpu.roll(n_re, shift, 0), c_re)
                c_im = jnp.where(sub == target, pltpu.roll(n_im, shift, 0), c_im)
            n_re, n_im = _cmul_add(e_re, e_im, w_re, w_im, c_re, c_im)
            last = 0 if reverse else SEG - 1
            carry[p:p + 1, :] = n_re[last:last + 1, :]
            carry[N_PAIR + p:N_PAIR + p + 1, :] = n_im[last:last + 1, :]
            in_re, in_im = jnp.tile(c_re, (seg_len, 1)), jnp.tile(c_im, (seg_len, 1))
            st[p], st[N_PAIR + p] = _cmul_add(st[p], st[N_PAIR + p], pw_ref[p], pw_ref[N_PAIR + p], in_re, in_im)


def _lane_chunk(ref, q):
    return jnp.concatenate([ref[4 * q + i] for i in range(4)], axis=1)


def _put_lane_chunk(ref, q, value):
    for i in range(4):
        ref[4 * q + i] = value[:, 128 * i:128 * (i + 1)]


def _step_major(tb):
    r = np.arange(tb)
    pm = np.zeros((tb, tb), np.float32)
    pm[r, (r % SEG) * (tb // SEG) + r // SEG] = 1.0
    return jnp.asarray(pm, BF16), jnp.asarray(pm.T, BF16)


def _unpermute(pt_ref, v):
    hi = v.astype(BF16)
    lo = (v - hi.astype(F32)).astype(BF16)
    return _dot(pt_ref[...], hi) + _dot(pt_ref[...], lo)


def _s5_fwd(z, bc, cc, a, pm, pt, dvec, wglu, bglu, tb=256):
    t = z.shape[0]

    def body(xa_ref, ga_ref, bc_ref, cc_ref, a_ref, pm_ref, pt_ref, d_ref, wglu_ref, bglu_ref,
             ya_ref, s_ref, ys_ref, st, carry, pw_ref):
        @pl.when(pl.program_id(0) == 0)
        def _():
            carry[...] = jnp.zeros_like(carry)
            _s5_fill_powers(pw_ref, a_ref, tb, reverse=False)

        xa = xa_ref[...]
        xab = _dot(pm_ref[...], xa.astype(BF16)).astype(BF16)
        for q in range(N_Q):
            _put_lane_chunk(st, q, _dot(xab[:, 128 * (q % 4):128 * (q % 4) + 128], bc_ref[:, pl.ds(LANE_CH * q, LANE_CH)]))
        _s5_scan(st, carry, a_ref, pw_ref, tb, reverse=False)
        s_ref[...] = st[...].astype(BF16)
        cols = []
        for j in range(4):
            lo, hi = LANE_CH * j, N_STATE + LANE_CH * j
            cols.append(_dot(_lane_chunk(s_ref, j), cc_ref[lo:lo + LANE_CH, :])
                        + _dot(_lane_chunk(s_ref, 4 + j), cc_ref[hi:hi + LANE_CH, :]))
        ys = _unpermute(pt_ref, jnp.concatenate(cols, axis=1)) + d_ref[...] * xa
        ys_ref[...] = ys
        ya1 = _gelu(ys)
        pre = _dot(ya1.astype(BF16), wglu_ref[...]) + bglu_ref[...]
        silu_ga, _ = _silu_and_grad(ga_ref[...])
        ya_ref[...] = (ya1 * jax.nn.sigmoid(pre) * silu_ga).astype(BF16)

    const = lambda shape: pl.BlockSpec(shape, lambda i: (0,) * len(shape))
    return pl.pallas_call(
        body,
        name="s5_fwd",
        grid=(t // tb,),
        in_specs=[
            pl.BlockSpec((tb, CB), lambda i: (i, CB_XA)),
            pl.BlockSpec((tb, CB), lambda i: (i, CB_GA)),
            const((128, 2 * N_STATE)),
            const((2 * N_STATE, 128)),
            const((N_LT, 128)),
            const((tb, tb)),
            const((tb, tb)),
            const((1, S5_W)),
            const((S5_W, S5_W)),
            const((1, S5_W)),
        ],
        out_specs=[
            pl.BlockSpec((tb, S5_W), lambda i: (i, 0)),
            pl.BlockSpec((N_LT, tb, 128), lambda i: (0, i, 0)),
            pl.BlockSpec((tb, S5_W), lambda i: (i, 0)),
        ],
        out_shape=[
            jax.ShapeDtypeStruct((t, S5_W), BF16),
            jax.ShapeDtypeStruct((N_LT, t, 128), BF16),
            jax.ShapeDtypeStruct((t, S5_W), F32),
        ],
        scratch_shapes=[pltpu.VMEM((N_LT, tb, 128), F32), pltpu.VMEM((N_LT, 128), F32), pltpu.VMEM((N_LT, tb, 128), F32)],
        compiler_params=_params("arbitrary"),
    )(z, z, bc, cc, a, pm, pt, dvec, wglu, bglu)


def _s5_bwd(dya, ys, z, s, bc, cc, a, pm, pt, dvec, wglu, bglu, tb=256):
    t = z.shape[0]
    nb = t // tb
    rev = lambda i: nb - 1 - i

    def body(dya_ref, ys_ref, xa_ref, ga_ref, s_ref, sp_ref, bc_ref, cc_ref, a_ref, pm_ref, pt_ref, d_ref,
             wglu_ref, bglu_ref, dz_ref, dbc_ref, dcct_ref, da_ref, dd_ref, dwglu_ref, dbglu_ref, g, carry, pw_ref):
        i = pl.program_id(0)

        @pl.when(i == 0)
        def _():
            carry[...] = jnp.zeros_like(carry)
            _s5_fill_powers(pw_ref, a_ref, tb, reverse=True)
            for r in (dbc_ref, dcct_ref, da_ref, dd_ref, dwglu_ref, dbglu_ref):
                r[...] = jnp.zeros_like(r)

        ys = ys_ref[...]
        xa = xa_ref[...]
        ga = ga_ref[...]
        dya = dya_ref[...]
        ya1, ya1_grad = _gelu_and_grad(ys)
        ya1b = ya1.astype(BF16)
        sg = jax.nn.sigmoid(_dot(ya1b, wglu_ref[...]) + bglu_ref[...])
        silu_ga, silu_ga_grad = _silu_and_grad(ga)
        dz_ref[:, S5_W:] = (dya * (ya1 * sg) * silu_ga_grad).astype(BF16)
        dya2 = dya * silu_ga
        dpre = dya2 * ya1 * sg * (1.0 - sg)
        dbglu_ref[...] += _rowsum(dpre)
        dpreb = dpre.astype(BF16)
        dwglu_ref[...] += _dot(ya1b, dpreb, TN)
        dys = (dya2 * sg + _dot(dpreb, wglu_ref[...], NT)) * ya1_grad
        dd_ref[...] += _rowsum(dys * xa)
        dysb = _dot(pm_ref[...], dys.astype(BF16)).astype(BF16)
        xab = _dot(pm_ref[...], xa.astype(BF16)).astype(BF16)

        for q in range(N_Q):
            cq = pl.ds(LANE_CH * q, LANE_CH)
            x0 = 128 * (q % 4)
            dcct_ref[:, cq] += _dot(dysb[:, x0:x0 + 128], _lane_chunk(s_ref, q), TN)
            _put_lane_chunk(g, q, _dot(dysb[:, x0:x0 + 128], cc_ref[cq, :], NT))
        _s5_scan(g, carry, a_ref, pw_ref, tb, reverse=True)

        seg0 = (lax.broadcasted_iota(jnp.int32, (SEG, 128), 0) == 0)
        have_prev = i < nb - 1

        def before_step0(tile, halo):
            prev_last = jnp.where(have_prev, halo.astype(F32)[HALO - 1:HALO, :], 0.0)
            return jnp.where(seg0, prev_last, pltpu.roll(tile[tb - SEG:, :], 1, 0))

        for p in range(N_PAIR):
            g_re, g_im = g[p], g[N_PAIR + p]
            s_re, s_im = s_ref[p].astype(F32), s_ref[N_PAIR + p].astype(F32)
            f_re, f_im = before_step0(s_re, sp_ref[p]), before_step0(s_im, sp_ref[N_PAIR + p])
            b_re, b_im, h_re, h_im = s_re[:tb - SEG], s_im[:tb - SEG], g_re[SEG:], g_im[SEG:]
            da_ref[p:p + 1, :] += (_rowsum(b_re * h_re + b_im * h_im)
                                   + _rowsum(f_re * g_re[:SEG] + f_im * g_im[:SEG]))
            da_ref[N_PAIR + p:N_PAIR + p + 1, :] += (_rowsum(b_re * h_im - b_im * h_re)
                                                     + _rowsum(f_re * g_im[:SEG] - f_im * g_re[:SEG]))
        dxa_cols = []
        for j in range(4):
            re = pl.ds(LANE_CH * j, LANE_CH)
            im = pl.ds(N_STATE + LANE_CH * j, LANE_CH)
            x0 = 128 * j
            gb_re, gb_im = _lane_chunk(g, j).astype(BF16), _lane_chunk(g, 4 + j).astype(BF16)
            dbc_ref[:, re] += _dot(xab[:, x0:x0 + 128], gb_re, TN)
            dbc_ref[:, im] += _dot(xab[:, x0:x0 + 128], gb_im, TN)
            dxa_cols.append(_dot(gb_re, bc_ref[:, re], NT) + _dot(gb_im, bc_ref[:, im], NT))
        dz_ref[:, 0:S5_W] = (dys * d_ref[...] + _unpermute(pt_ref, jnp.concatenate(dxa_cols, axis=1))).astype(BF16)

    const = lambda shape: pl.BlockSpec(shape, lambda i: (0,) * len(shape))
    per_halo = tb // HALO
    return pl.pallas_call(
        body,
        name="s5_bwd",
        grid=(nb,),
        in_specs=[
            pl.BlockSpec((tb, S5_W), lambda i: (rev(i), 0)),
            pl.BlockSpec((tb, S5_W), lambda i: (rev(i), 0)),
            pl.BlockSpec((tb, CB), lambda i: (rev(i), CB_XA)),
            pl.BlockSpec((tb, CB), lambda i: (rev(i), CB_GA)),
            pl.BlockSpec((N_LT, tb, 128), lambda i: (0, rev(i), 0)),
            pl.BlockSpec((N_LT, HALO, 128), lambda i: (0, jnp.maximum(rev(i) * per_halo - 1, 0), 0)),
            const((128, 2 * N_STATE)),
            const((2 * N_STATE, 128)),
            const((N_LT, 128)),
            const((tb, tb)),
            const((tb, tb)),
            const((1, S5_W)),
            const((S5_W, S5_W)),
            const((1, S5_W)),
        ],
        out_specs=[
            pl.BlockSpec((tb, 2 * CB), lambda i: (rev(i), 0)),
            const((128, 2 * N_STATE)),
            const((128, 2 * N_STATE)),
            const((N_LT, 128)),
            const((1, S5_W)),
            const((S5_W, S5_W)),
            const((1, S5_W)),
        ],
        out_shape=[
            jax.ShapeDtypeStruct((t, 2 * CB), BF16),
            jax.ShapeDtypeStruct((128, 2 * N_STATE), F32),
            jax.ShapeDtypeStruct((128, 2 * N_STATE), F32),
            jax.ShapeDtypeStruct((N_LT, 128), F32),
            jax.ShapeDtypeStruct((1, S5_W), F32),
            jax.ShapeDtypeStruct((S5_W, S5_W), F32),
            jax.ShapeDtypeStruct((1, S5_W), F32),
        ],
        scratch_shapes=[pltpu.VMEM((N_LT, tb, 128), F32), pltpu.VMEM((N_LT, 128), F32), pltpu.VMEM((N_LT, tb, 128), F32)],
        compiler_params=_params("arbitrary"),
    )(dya, ys, z, z, s, s, bc, cc, a, pm, pt, dvec, wglu, bglu)


def _sgu_norm(v0, v1, lng_ref, lnb_ref):
    g0, g1 = _gelu(v0), _gelu(v1)
    mu = (jnp.sum(g0, axis=-1, keepdims=True) + jnp.sum(g1, axis=-1, keepdims=True)) * (1.0 / SGU_W)
    c0, c1 = g0 - mu, g1 - mu
    var = (jnp.sum(c0 * c0, axis=-1, keepdims=True) + jnp.sum(c1 * c1, axis=-1, keepdims=True)) * (1.0 / SGU_W)
    rstd = lax.rsqrt(var + LN_EPS)
    vh0, vh1 = c0 * rstd, c1 * rstd
    vn0 = vh0 * lng_ref[:, 0:CB] + lnb_ref[:, 0:CB]
    vn1 = vh1 * lng_ref[:, CB:] + lnb_ref[:, CB:]
    return (vh0, vh1), (vn0, vn1), rstd


def _sgu_fwd(z, lng, lnb, ws, bsx, tb=512):
    t = z.shape[0]
    tb = min(tb, t)

    def body(u0_ref, u1_ref, v0_ref, v1_ref, gb0_ref, gb1_ref, lng_ref, lnb_ref, ws_ref, bsx_ref, yb_ref):
        _, (vn0, vn1), _ = _sgu_norm(v0_ref[...], v1_ref[...], lng_ref, lnb_ref)
        for half, (vn, u_ref, gb_ref) in enumerate(((vn0, u0_ref, gb0_ref), (vn1, u1_ref, gb1_ref))):
            vnb = vn.astype(BF16)
            silu_gb, _ = _silu_and_grad(gb_ref[...])
            gate = _gelu(u_ref[...]) * silu_gb
            for hh in range(4):
                h = 4 * half + hh
                for c in range(tb // CHUNK):
                    rows, cols = slice(CHUNK * c, CHUNK * (c + 1)), slice(128 * hh, 128 * (hh + 1))
                    sp = _dot(ws_ref[h], vnb[rows, cols]) + bsx_ref[h]
                    yb_ref[rows, CB * half + 128 * hh:CB * half + 128 * (hh + 1)] = (gate[rows, cols] * sp).astype(BF16)

    zb = lambda j: pl.BlockSpec((tb, CB), lambda i, j=j: (i, j))
    const = lambda shape: pl.BlockSpec(shape, lambda i: (0,) * len(shape))
    return pl.pallas_call(
        body,
        name="sgu_fwd",
        grid=(t // tb,),
        in_specs=[zb(CB_U), zb(CB_U + 1), zb(CB_V), zb(CB_V + 1), zb(CB_GB), zb(CB_GB + 1),
                  const((1, SGU_W)), const((1, SGU_W)), const((SGU_HEADS, CHUNK, CHUNK)), const((SGU_HEADS, CHUNK, 128))],
        out_specs=pl.BlockSpec((tb, SGU_W), lambda i: (i, 0)),
        out_shape=jax.ShapeDtypeStruct((t, SGU_W), BF16),
        compiler_params=_params("arbitrary"),
    )(z, z, z, z, z, z, lng, lnb, ws, bsx)


def _sgu_bwd(dyb, z, lng, lnb, ws, wst, bsx, tb=512):
    t = z.shape[0]
    tb = min(tb, t)

    def body(dyb_ref, u0_ref, u1_ref, v0_ref, v1_ref, gb0_ref, gb1_ref, lng_ref, lnb_ref, ws_ref, wst_ref, bsx_ref,
             dz_ref, dlng_ref, dlnb_ref, dws_ref, dbs_ref, dvn):
        @pl.when(pl.program_id(0) == 0)
        def _():
            for r in (dlng_ref, dlnb_ref, dws_ref, dbs_ref):
                r[...] = jnp.zeros_like(r)

        v0, v1 = v0_ref[...], v1_ref[...]
        (vh0, vh1), (vn0, vn1), rstd = _sgu_norm(v0, v1, lng_ref, lnb_ref)
        causal = (lax.broadcasted_iota(jnp.int32, (CHUNK, CHUNK), 0) >= lax.broadcasted_iota(jnp.int32, (CHUNK, CHUNK), 1))
        for half, (vn, u_ref, gb_ref) in enumerate(((vn0, u0_ref, gb0_ref), (vn1, u1_ref, gb1_ref))):
            vnb = vn.astype(BF16)
            u = u_ref[...]
            ug, ug_grad = _gelu_and_grad(u)
            silu_gb, silu_gb_grad = _silu_and_grad(gb_ref[...])
            dyb = dyb_ref[:, CB * half:CB * (half + 1)]
            dyb0 = dyb * silu_gb
            ds = dyb0 * ug
            sp_cols = []
            for hh in range(4):
                h = 4 * half + hh
                cols = slice(128 * hh, 128 * (hh + 1))
                sp_rows = []
                for c in range(tb // CHUNK):
                    rows = slice(CHUNK * c, CHUNK * (c + 1))
                    vt = vnb[rows, cols]
                    sp_rows.append(_dot(ws_ref[h], vt) + bsx_ref[h])
                    dst = ds[rows, cols]
                    dstb = dst.astype(BF16)
                    dbs_ref[h] += dst
                    dws_ref[h] += jnp.where(causal, _dot(dstb, vt, NT), 0.0)
                    dvn[rows, CB * half + 128 * hh:CB * half + 128 * (hh + 1)] = _dot(wst_ref[h], dstb)
                sp_cols.append(jnp.concatenate(sp_rows, axis=0))
            sp = jnp.concatenate(sp_cols, axis=1)
            dz_ref[:, CB * half:CB * (half + 1)] = (dyb0 * sp * ug_grad).astype(BF16)
            dz_ref[:, 2 * SGU_W + CB * half:2 * SGU_W + CB * (half + 1)] = (dyb * (ug * sp) * silu_gb_grad).astype(BF16)

        dvn0, dvn1 = dvn[:, 0:CB], dvn[:, CB:]
        dlng_ref[:, 0:CB] += _rowsum(dvn0 * vh0)
        dlng_ref[:, CB:] += _rowsum(dvn1 * vh1)
        dlnb_ref[:, 0:CB] += _rowsum(dvn0)
        dlnb_ref[:, CB:] += _rowsum(dvn1)
        dh0, dh1 = dvn0 * lng_ref[:, 0:CB], dvn1 * lng_ref[:, CB:]
        m1 = (jnp.sum(dh0, axis=-1, keepdims=True) + jnp.sum(dh1, axis=-1, keepdims=True)) * (1.0 / SGU_W)
        m2 = (jnp.sum(dh0 * vh0, axis=-1, keepdims=True) + jnp.sum(dh1 * vh1, axis=-1, keepdims=True)) * (1.0 / SGU_W)
        dz_ref[:, SGU_W:SGU_W + CB] = (rstd * (dh0 - m1 - vh0 * m2) * _gelu_grad(v0)).astype(BF16)
        dz_ref[:, SGU_W + CB:2 * SGU_W] = (rstd * (dh1 - m1 - vh1 * m2) * _gelu_grad(v1)).astype(BF16)

    zb = lambda j: pl.BlockSpec((tb, CB), lambda i, j=j: (i, j))
    const = lambda shape: pl.BlockSpec(shape, lambda i: (0,) * len(shape))
    hmat = (SGU_HEADS, CHUNK, CHUNK)
    return pl.pallas_call(
        body,
        name="sgu_bwd",
        grid=(t // tb,),
        in_specs=[pl.BlockSpec((tb, SGU_W), lambda i: (i, 0)),
                  zb(CB_U), zb(CB_U + 1), zb(CB_V), zb(CB_V + 1), zb(CB_GB), zb(CB_GB + 1),
                  const((1, SGU_W)), const((1, SGU_W)), const(hmat), const(hmat), const(hmat)],
        out_specs=[pl.BlockSpec((tb, 3 * SGU_W), lambda i: (i, 0)),
                   const((1, SGU_W)), const((1, SGU_W)), const(hmat), const(hmat)],
        out_shape=[jax.ShapeDtypeStruct((t, 3 * SGU_W), BF16),
                   jax.ShapeDtypeStruct((1, SGU_W), F32), jax.ShapeDtypeStruct((1, SGU_W), F32),
                   jax.ShapeDtypeStruct(hmat, F32), jax.ShapeDtypeStruct(hmat, F32)],
        scratch_shapes=[pltpu.VMEM((tb, SGU_W), F32)],
        compiler_params=_params("arbitrary"),
    )(dyb, z, z, z, z, z, z, lng, lnb, ws, wst, bsx)


def _window_sums(ext, lookahead):
    n = ext.shape[0]
    out = []
    for gi, w in enumerate(POOL_WINDOWS):
        acc = ext[:, 128 * gi:128 * (gi + 1)]
        k = 1
        while k < w:
            acc = acc + pltpu.roll(acc, (n - k) if lookahead else k, 0)
            k *= 2
        out.append(acc)
    return jnp.concatenate(out, axis=1)


def _pool_counts(row0, tb):
    pos = (row0 + 1 + lax.broadcasted_iota(jnp.int32, (tb, POOL_W), 0)).astype(F32)
    lane = lax.broadcasted_iota(jnp.int32, (tb, POOL_W), 1)
    win = jnp.where(lane < 128, 2.0, jnp.where(lane < 256, 4.0, jnp.where(lane < 384, 8.0, 16.0)))
    return jnp.minimum(pos, win)


def _pool_fwd(z, wpool, scale, tb=1024):
    t = z.shape[0]
    tb = min(tb, t)

    def body(xc_ref, gc_ref, wp_ref, sc_ref, yc_ref, halo):
        i = pl.program_id(0)

        @pl.when(i == 0)
        def _():
            halo[...] = jnp.zeros_like(halo)

        xc = xc_ref[...]
        sums = _window_sums(jnp.concatenate([halo[...], xc], axis=0), lookahead=False)[HALO:, :]
        halo[...] = xc[tb - HALO:, :]
        pb = (sums / _pool_counts(i * tb, tb) - xc).astype(BF16)
        q = jnp.concatenate([_dot(pb[:, 128 * gi:128 * (gi + 1)], wp_ref[gi]) for gi in range(4)], axis=1)
        silu_gc, _ = _silu_and_grad(gc_ref[...])
        yc_ref[...] = (q * sc_ref[...] * silu_gc).astype(BF16)

    const = lambda shape: pl.BlockSpec(shape, lambda i: (0,) * len(shape))
    return pl.pallas_call(
        body,
        name="pool_fwd",
        grid=(t // tb,),
        in_specs=[pl.BlockSpec((tb, CB), lambda i: (i, CB_XC)), pl.BlockSpec((tb, CB), lambda i: (i, CB_GC)),
                  const((4, 128, 128)), const((1, POOL_W))],
        out_specs=pl.BlockSpec((tb, POOL_W), lambda i: (i, 0)),
        out_shape=jax.ShapeDtypeStruct((t, POOL_W), BF16),
        scratch_shapes=[pltpu.VMEM((HALO, POOL_W), F32)],
        compiler_params=_params("arbitrary"),
    )(z, z, wpool, scale)


def _pool_bwd(dyc, z, wpool, scale, tb=1024):
    t = z.shape[0]
    tb = min(tb, t)
    nb = t // tb
    rev = lambda i: nb - 1 - i
    per_halo = tb // HALO

    def body(dyc_ref, xc_ref, xp_ref, gc_ref, wp_ref, sc_ref, dz_ref, dwp_ref, dsc_ref, ehalo):
        i = pl.program_id(0)

        @pl.when(i == 0)
        def _():
            ehalo[...] = jnp.zeros_like(ehalo)
            dwp_ref[...] = jnp.zeros_like(dwp_ref)
            dsc_ref[...] = jnp.zeros_like(dsc_ref)

        xc = xc_ref[...]
        prev = jnp.where(i < nb - 1, xp_ref[...], 0.0)
        sums = _window_sums(jnp.concatenate([prev, xc], axis=0), lookahead=False)[HALO:, :]
        cnt = _pool_counts(rev(i) * tb, tb)
        pb = (sums / cnt - xc).astype(BF16)
        q = jnp.concatenate([_dot(pb[:, 128 * gi:128 * (gi + 1)], wp_ref[gi]) for gi in range(4)], axis=1)
        silu_gc, silu_gc_grad = _silu_and_grad(gc_ref[...])
        dyc = dyc_ref[...]
        dz_ref[:, POOL_W:] = (dyc * (q * sc_ref[...]) * silu_gc_grad).astype(BF16)
        dyc0 = dyc * silu_gc
        dsc_ref[...] += _rowsum(dyc0 * q)
        dqb = (dyc0 * sc_ref[...]).astype(BF16)
        dp_cols = []
        for gi in range(4):
            cols = slice(128 * gi, 128 * (gi + 1))
            dwp_ref[gi] += _dot(pb[:, cols], dqb[:, cols], TN)
            dp_cols.append(_dot(dqb[:, cols], wp_ref[gi], NT))
        dp = jnp.concatenate(dp_cols, axis=1)
        e = dp / cnt
        fut = _window_sums(jnp.concatenate([e, ehalo[...]], axis=0), lookahead=True)[:tb, :]
        ehalo[...] = e[:HALO, :]
        dz_ref[:, 0:POOL_W] = (fut - dp).astype(BF16)

    const = lambda shape: pl.BlockSpec(shape, lambda i: (0,) * len(shape))
    return pl.pallas_call(
        body,
        name="pool_bwd",
        grid=(nb,),
        in_specs=[pl.BlockSpec((tb, POOL_W), lambda i: (rev(i), 0)),
                  pl.BlockSpec((tb, CB), lambda i: (rev(i), CB_XC)),
                  pl.BlockSpec((HALO, CB), lambda i: (jnp.maximum(rev(i) * per_halo - 1, 0), CB_XC)),
                  pl.BlockSpec((tb, CB), lambda i: (rev(i), CB_GC)),
                  const((4, 128, 128)), const((1, POOL_W))],
        out_specs=[pl.BlockSpec((tb, 2 * POOL_W), lambda i: (rev(i), 0)), const((4, 128, 128)), const((1, POOL_W))],
        out_shape=[jax.ShapeDtypeStruct((t, 2 * POOL_W), BF16),
                   jax.ShapeDtypeStruct((4, 128, 128), F32), jax.ShapeDtypeStruct((1, POOL_W), F32)],
        scratch_shapes=[pltpu.VMEM((HALO, POOL_W), F32)],
        compiler_params=_params("arbitrary"),
    )(dyc, z, z, z, wpool, scale)


def _adam_math(w, m, v, g):
    nm = ADAM_B1 * m + (1.0 - ADAM_B1) * g
    nv = ADAM_B2 * v + (1.0 - ADAM_B2) * (g * g)
    m_hat = nm / (1.0 - ADAM_B1 ** ADAM_STEP)
    v_hat = nv / (1.0 - ADAM_B2 ** ADAM_STEP)
    return -ADAM_LR * (m_hat / (jnp.sqrt(v_hat) + ADAM_EPS) + ADAM_WD * w), nm, nv


def _sum_small(r_a, r_b, steps=7):
    rows_a, rows_b = r_a.shape[1], r_b.shape[1]
    tr = rows_a // steps

    def body(a_ref, b_ref, ga_ref, gb_ref):
        for src, dst in ((a_ref, ga_ref), (b_ref, gb_ref)):
            g = src[0].astype(F32)
            for k in range(1, N_DEV):
                g = g + src[k].astype(F32)
            dst[...] = g

    return pl.pallas_call(
        body,
        name="sum_small",
        grid=(steps,),
        in_specs=[pl.BlockSpec((N_DEV, tr, 128), lambda i: (0, i, 0)), pl.BlockSpec((N_DEV, rows_b, 128), lambda i: (0, 0, 0))],
        out_specs=[pl.BlockSpec((tr, 128), lambda i: (i, 0)), pl.BlockSpec((rows_b, 128), lambda i: (0, 0))],
        out_shape=[jax.ShapeDtypeStruct((rows_a, 128), F32), jax.ShapeDtypeStruct((rows_b, 128), F32)],
        compiler_params=_params("arbitrary"),
    )(r_a, r_b)


def _adamw_small(ws, ms, vs, gs):
    n = len(ws)
    whole = pl.BlockSpec(memory_space=pltpu.VMEM)

    def body(*refs):
        ins, outs = refs[:4 * n], refs[4 * n:]
        for i in range(n):
            w, m, v, g = (ins[4 * i + j][...] for j in range(4))
            outs[3 * i][...], outs[3 * i + 1][...], outs[3 * i + 2][...] = _adam_math(w, m, v, g)

    return pl.pallas_call(
        body,
        name="adamw_small",
        in_specs=[whole] * (4 * n),
        out_specs=[whole] * (3 * n),
        out_shape=[jax.ShapeDtypeStruct(w.shape, F32) for w in ws for _ in range(3)],
        compiler_params=pltpu.CompilerParams(vmem_limit_bytes=VMEM_LIMIT_WHOLE),
    )(*[a for group in zip(ws, ms, vs, gs) for a in group])


def _adamw(w, m, v, parts, tr, name):
    r, c = w.shape
    n_slab = len(parts)
    per_slab = r // n_slab // tr

    def body(w_ref, m_ref, v_ref, *refs):
        p_refs, (g_ref, d_ref, nm_ref, nv_ref) = refs[:n_slab], refs[n_slab:]
        for s, p_ref in enumerate(p_refs):
            @pl.when(pl.program_id(0) // per_slab == s)
            def _(p_ref=p_ref):
                g = p_ref[0].astype(F32)
                for k in range(1, N_DEV):
                    g = g + p_ref[k].astype(F32)
                g_ref[...] = g

        d_ref[...], nm_ref[...], nv_ref[...] = _adam_math(w_ref[...], m_ref[...], v_ref[...], g_ref[...])

    blk = pl.BlockSpec((tr, c), lambda i: (i, 0))
    slab = lambda s: pl.BlockSpec((N_DEV, tr, c), lambda i, s=s: (0, jnp.clip(i - s * per_slab, 0, per_slab - 1), 0))
    return pl.pallas_call(
        body,
        name=name,
        grid=(r // tr,),
        in_specs=[blk, blk, blk] + [slab(s) for s in range(n_slab)],
        out_specs=[blk, blk, blk, blk],
        out_shape=[jax.ShapeDtypeStruct((r, c), F32)] * 4,
        compiler_params=_params("arbitrary"),
    )(w, m, v, *parts)


MESH = pl.DeviceIdType.MESH
ANY = pl.BlockSpec(memory_space=pl.ANY)


def _dev_index(dev):
    return 4 * dev[0] + 2 * dev[1] + dev[2]


WHOLE_SHAPES = ((D_MODEL, IN_COLS), (D_MODEL, D_MODEL), (S5_W, S5_W))
SHARD_SHAPES =((D_MODEL, SH_IN), (SH_OUT, D_MODEL), (SH_GLU, S5_W))


def _shard_of(ref, ti, idx):
    if ti == 0:
        return ref.at[:, pl.ds(pl.multiple_of(idx * SH_IN, 128), SH_IN)]
    rows = SHARD_SHAPES[ti][0]
    return ref.at[pl.ds(pl.multiple_of(idx * rows, rows), rows), :]


def _peer(mask, x, y, c):
    return (1 - x if mask & 4 else x, 1 - y if mask & 2 else y, 1 - c if mask & 1 else c)


def _allgather_weights(wi, wo, wg):
    n_t = 3

    def body(wi_ref, wo_ref, wg_ref, gi_ref, go_ref, gg_ref, send_sems, recv_sems, local_sems):
        x, y, c = lax.axis_index("x"), lax.axis_index("y"), lax.axis_index("c")
        me, sibling = (x, y, c), (x, y, 1 - c)
        chips = [(1 - x, y), (x, 1 - y), (1 - x, 1 - y)]
        shards = (wi_ref, wo_ref, wg_ref)
        wholes = (gi_ref, go_ref, gg_ref)

        def slot(ti, dev):
            return _shard_of(wholes[ti], ti, _dev_index(dev))

        def copy(k, ti, block, to, own=False):
            return pltpu.make_async_remote_copy(
                src_ref=shards[ti] if own else slot(ti, block), dst_ref=slot(ti, block),
                send_sem=send_sems.at[n_t * k + ti], recv_sem=recv_sems.at[n_t * k + ti],
                device_id=to, device_id_type=MESH)

        mine = [pltpu.make_async_copy(shards[ti], slot(ti, me), local_sems.at[ti]) for ti in range(n_t)]
        for cp in mine:
            cp.start()
        first = [copy(0, ti, me, sibling, own=True) for ti in range(n_t)]
        first += [copy(1 + j, ti, me, (*chip, c), own=True) for j, chip in enumerate(chips) for ti in range(n_t)]
        for cp in first:
            cp.start()
        passed = []
        for j, chip in enumerate(chips):
            for ti in range(n_t):
                copy(1 + j, ti, (*chip, c), me).wait_recv()
            onward = [copy(4 + j, ti, (*chip, c), sibling) for ti in range(n_t)]
            for cp in onward:
                cp.start()
            passed += onward
        for ti in range(n_t):
            copy(0, ti, sibling, me).wait_recv()
        for j, chip in enumerate(chips):
            for ti in range(n_t):
                copy(4 + j, ti, (*chip, 1 - c), me).wait_recv()
        for cp in first + passed:
            cp.wait_send()
        for cp in mine:
            cp.wait()

    return pl.pallas_call(
        body,
        name="allgather_weights",
        in_specs=[ANY] * n_t,
        out_specs=[ANY] * n_t,
        out_shape=[jax.ShapeDtypeStruct(s, BF16) for s in WHOLE_SHAPES],
        scratch_shapes=[pltpu.SemaphoreType.DMA((7 * n_t,)), pltpu.SemaphoreType.DMA((7 * n_t,)),
                        pltpu.SemaphoreType.DMA((n_t,))],
    )(wi, wo, wg)


HBM = pl.BlockSpec(memory_space=pltpu.HBM)
SEM = pl.BlockSpec(memory_space=pltpu.SEMAPHORE)
GATHER, SCATTER, SHARE = "gather", "scatter", "share"


def _split_route(kind, ti, sending, me_idx, p_idx, src_ref, land_ref):
    owner = me_idx if sending else p_idx
    if kind == GATHER:
        return src_ref, _shard_of(land_ref, ti, owner)
    if kind == SCATTER:
        return _shard_of(src_ref, ti, p_idx), land_ref.at[owner]
    return src_ref, land_ref.at[owner]


def _split_start(name, srcs, lands, kinds, after=None):
    n = len(srcs)
    arrays = list(srcs) + list(lands) + ([] if after is None else [after])

    def body(*refs):
        src_refs, land_refs = refs[0:n], refs[n:2 * n]
        send_sems, recv_sems, token = refs[len(arrays)], refs[len(arrays) + 1], refs[-1]
        x, y, c = lax.axis_index("x"), lax.axis_index("y"), lax.axis_index("c")
        me_idx = _dev_index((x, y, c))
        for mask in range(N_DEV):
            p = _peer(mask, x, y, c)
            for i, (kind, ti) in enumerate(kinds):
                k = mask * n + i
                src, dst = _split_route(kind, ti, True, me_idx, _dev_index(p), src_refs[i], land_refs[i])
                pltpu.make_async_remote_copy(src_ref=src, dst_ref=dst, send_sem=send_sems.at[k], recv_sem=recv_sems.at[k],
                                             device_id=p, device_id_type=MESH).start()
        token[...] = jnp.zeros_like(token)

    n_copies = N_DEV * n
    return pl.pallas_call(
        body,
        name=name,
        in_specs=[HBM] * len(arrays),
        out_specs=(SEM, SEM) + (HBM,) * (2 * n) + (pl.BlockSpec(memory_space=pltpu.VMEM),),
        out_shape=(pltpu.SemaphoreType.DMA((n_copies,)), pltpu.SemaphoreType.DMA((n_copies,)))
        + tuple(pltpu.HBM(a.shape, a.dtype) for a in arrays[:2 * n]) + (jax.ShapeDtypeStruct((8, 128), F32),),
        input_output_aliases={i: 2 + i for i in range(2 * n)},
        compiler_params=pltpu.CompilerParams(has_side_effects=pltpu.SideEffectType.DATAFLOW_SIDE_EFFECTING),
    )(*[pltpu.with_memory_space_constraint(a, pltpu.HBM) for a in arrays])


def _split_wait(name, started, kinds, after):
    n = len(kinds)
    send_sems, recv_sems, thru = started[0], started[1], started[2:2 + 2 * n]

    def body(*refs):
        src_refs, land_refs = refs[0:n], refs[n:2 * n]
        send_sems, recv_sems = refs[2 * n], refs[2 * n + 1]
        x, y, c = lax.axis_index("x"), lax.axis_index("y"), lax.axis_index("c")
        me_idx = _dev_index((x, y, c))
        for mask in range(N_DEV):
            p = _peer(mask, x, y, c)
            for i, (kind, ti) in enumerate(kinds):
                k = mask * n + i
                src, dst = _split_route(kind, ti, False, me_idx, _dev_index(p), src_refs[i], land_refs[i])
                cp = pltpu.make_async_remote_copy(src_ref=src, dst_ref=dst, send_sem=send_sems.at[k],
                                                  recv_sem=recv_sems.at[k], device_id=p, device_id_type=MESH)
                cp.wait_send()
                cp.wait_recv()

    res = pl.pallas_call(
        body,
        name=name,
        in_specs=[HBM] * (2 * n) + [SEM, SEM, pl.BlockSpec(memory_space=pl.ANY)],
        out_specs=(HBM,) * (2 * n),
        out_shape=tuple(pltpu.HBM(a.shape, a.dtype) for a in thru),
        input_output_aliases={i: i for i in range(2 * n)},
        compiler_params=pltpu.CompilerParams(has_side_effects=pltpu.SideEffectType.DATAFLOW_SIDE_EFFECTING),
    )(*thru, send_sems, recv_sems, after)
    return res[n:2 * n]


def _empty_zones(srcs, kinds):
    def shape(src, kind, ti):
        if kind == GATHER:
            return WHOLE_SHAPES[ti]
        return (N_DEV,) + (SHARD_SHAPES[ti] if kind == SCATTER else src.shape)

    return [lax.empty(shape(src, *k), src.dtype) for src, k in zip(srcs, kinds)]


def _share_small(buf):
    def body(b_ref, o_ref, send_sems, recv_sems, local_sem):
        x, y, c = lax.axis_index("x"), lax.axis_index("y"), lax.axis_index("c")
        me_idx = _dev_index((x, y, c))
        local = pltpu.make_async_copy(b_ref, o_ref.at[me_idx], local_sem)
        local.start()

        def copy(mask, owner):
            return pltpu.make_async_remote_copy(
                src_ref=b_ref, dst_ref=o_ref.at[owner], send_sem=send_sems.at[mask - 1], recv_sem=recv_sems.at[mask - 1],
                device_id=_peer(mask, x, y, c), device_id_type=MESH)

        sends = [copy(mask, me_idx) for mask in range(1, N_DEV)]
        for cp in sends:
            cp.start()
        for mask in range(1, N_DEV):
            copy(mask, _dev_index(_peer(mask, x, y, c))).wait_recv()
        for cp in sends:
            cp.wait_send()
        local.wait()

    return pl.pallas_call(
        body,
        name="share_small",
        in_specs=[ANY],
        out_specs=ANY,
        out_shape=jax.ShapeDtypeStruct((N_DEV,) + buf.shape, buf.dtype),
        scratch_shapes=[pltpu.SemaphoreType.DMA((N_DEV - 1,)), pltpu.SemaphoreType.DMA((N_DEV - 1,)),
                        pltpu.SemaphoreType.DMA],
    )(buf)


def _s5_prep(lam_re, lam_im, b_re, b_im, c_re, c_im, d_skip, log_dt):
    dt = jnp.exp(log_dt)[:, None]
    mag = jnp.exp(lam_re * dt)
    a_re, a_im = mag * jnp.cos(lam_im * dt), mag * jnp.sin(lam_im * dt)
    den = lam_re * lam_re + lam_im * lam_im
    f_re = ((a_re - 1.0) * lam_re + a_im * lam_im) / den
    f_im = (a_im * lam_re - (a_re - 1.0) * lam_im) / den
    bb_re = f_re[..., None] * b_re - f_im[..., None] * b_im
    bb_im = f_re[..., None] * b_im + f_im[..., None] * b_re
    eye = jnp.eye(8, dtype=F32)

    def in_map(bb):
        return jnp.einsum("jgph,gk->ghjkp", bb.reshape(4, 8, S5_STATE, S5_CH), eye).reshape(128, N_STATE)

    def out_map(cm):
        return jnp.einsum("jghp,gk->ghjkp", cm.reshape(4, 8, S5_CH, S5_STATE), eye).reshape(128, N_STATE)

    a = jnp.concatenate([a_re.reshape(N_PAIR, 128), a_im.reshape(N_PAIR, 128)])
    bc = jnp.concatenate([in_map(bb_re), in_map(bb_im)], axis=1)
    cct = jnp.concatenate([out_map(c_re), out_map(-c_im)], axis=1)
    return a, bc, cct, d_skip.reshape(1, S5_W)


WEIGHTS = ["norm_g", "w_in", "lam_re", "lam_im", "b_re", "b_im", "c_re", "c_im", "d_skip", "log_dt", "w_glu", "b_glu",
           "ln_g", "ln_b", "w_s", "b_s", "w_pool", "pool_scale", "w_out", "final_g"]
SHARDED = ("w_in", "w_glu", "w_out")
SMALL = [n for n in WEIGHTS if n not in SHARDED]
INPUTS = ["x"] + WEIGHTS + ["loss_target"] + ["m_" + n for n in WEIGHTS] + ["v_" + n for n in WEIGHTS]
SMALL_B = ["norm_g", "final_g"]
SMALL_A = [n for n in SMALL if n not in SMALL_B]
SMALL_TILE = 16 * 128
LOSS_AT = (DEPTH + 1) * D_MODEL


def _pack_small(arrays, dtype):
    flat = jnp.concatenate([a.reshape(-1) for a in arrays])
    pad = (-flat.shape[0]) % SMALL_TILE
    return jnp.pad(flat, (0, pad)).astype(dtype).reshape(-1, 128)


def _unpack_small(packed, like):
    flat = packed.reshape(-1)
    out, off = [], 0
    for a in like:
        out.append(flat[off:off + a.size].reshape(a.shape))
        off += a.size
    return out


def _layer_fwd(p, l, x, wi, wo, wg, token=None):
    row = lambda v: v.reshape(1, -1)
    causal = jnp.tril(jnp.ones((CHUNK, CHUNK), dtype=bool))
    (a, bc, cct, dvec), prep_vjp = jax.vjp(
        _s5_prep, p["lam_re"][l], p["lam_im"][l], p["b_re"][l], p["b_im"][l], p["c_re"][l], p["c_im"][l],
        p["d_skip"][l], p["log_dt"][l])
    ws_f32 = jnp.where(causal[None], p["w_s"][l], 0.0)
    pm, pt = _step_major(S5_TB)
    c = dict(
        x=x, wi=wi, wo=wo, wg=wg, a=a, bc=bc.astype(BF16), cc=cct.T.astype(BF16), dvec=dvec, prep_vjp=prep_vjp,
        pm=pm, pt=pt,
        ws=ws_f32.astype(BF16), wst=jnp.swapaxes(ws_f32, 1, 2).astype(BF16),
        bsx=jnp.broadcast_to(p["b_s"][l][:, :, None], (SGU_HEADS, CHUNK, 128)),
        wpool=p["w_pool"][l].astype(BF16), scale=row(p["pool_scale"][l]),
        lng=row(p["ln_g"][l]), lnb=row(p["ln_b"][l]), bglu=row(p["b_glu"][l]), norm_g=row(p["norm_g"][l]))
    c["z"], c["h"] = _rms_inproj(x, c["norm_g"], wi, token)
    c["ya"], c["s"], c["ys"] = _s5_fwd(c["z"], c["bc"], c["cc"], a, pm, pt, dvec, wg, c["bglu"], S5_TB)
    c["yb"] = _sgu_fwd(c["z"], c["lng"], c["lnb"], c["ws"], c["bsx"])
    c["yc"] = _pool_fwd(c["z"], c["wpool"], c["scale"])
    return _outproj(x, c["ya"], c["yb"], c["yc"], wo), c


def _mixers_bwd(c, dx, token=None):
    dya, dyb, dyc = _outproj_bwd(dx, c["wo"], token)
    dwo = _wgrad_out(c["ya"], c["yb"], c["yc"], dx, token)
    dz_s5, dbc, dcct, da, dd, dwg, dbglu = _s5_bwd(
        dya, c["ys"], c["z"], c["s"], c["bc"], c["cc"], c["a"], c["pm"], c["pt"], c["dvec"], c["wg"], c["bglu"], S5_TB)
    dz_sgu, dlng, dlnb, dws, dbsx = _sgu_bwd(dyb, c["z"], c["lng"], c["lnb"], c["ws"], c["wst"], c["bsx"])
    dz_pool, dwp, dsc = _pool_bwd(dyc, c["z"], c["wpool"], c["scale"])
    g_lam_re, g_lam_im, g_b_re, g_b_im, g_c_re, g_c_im, g_d, g_dt = c["prep_vjp"]((da, dbc, dcct, dd))
    small = dict(lam_re=g_lam_re, lam_im=g_lam_im, b_re=g_b_re, b_im=g_b_im, c_re=g_c_re,
                 c_im=g_c_im, d_skip=g_d, log_dt=g_dt, b_glu=dbglu.reshape(-1), ln_g=dlng.reshape(-1),
                 ln_b=dlnb.reshape(-1), w_s=dws, b_s=jnp.sum(dbsx, axis=-1), w_pool=dwp, pool_scale=dsc.reshape(-1))
    return (dz_s5, dz_sgu, dz_pool), dwo, dwg, small


def _inproj_grads(c, dz, dx, token_w=None, token_x=None):
    dwi = _wgrad_in(c["h"], *dz, token_w)
    dx, dnorm = _inproj_bwd(*dz, c["wi"], c["x"], c["norm_g"], dx, token_x)
    return dwi, dx, dnorm.reshape(-1)


def kernel(x, norm_g, w_in, lam_re, lam_im, b_re, b_im, c_re, c_im, d_skip, log_dt, w_glu, b_glu, ln_g, ln_b, w_s, b_s, w_pool, pool_scale, w_out, final_g, loss_target, m_norm_g, m_w_in, m_lam_re, m_lam_im, m_b_re, m_b_im, m_c_re, m_c_im, m_d_skip, m_log_dt, m_w_glu, m_b_glu, m_ln_g, m_ln_b, m_w_s, m_b_s, m_w_pool, m_pool_scale, m_w_out, m_final_g, v_norm_g, v_w_in, v_lam_re, v_lam_im, v_b_re, v_b_im, v_c_re, v_c_im, v_d_skip, v_log_dt, v_w_glu, v_b_glu, v_ln_g, v_ln_b, v_w_s, v_b_s, v_w_pool, v_pool_scale, v_w_out, v_final_g):
    p = dict(zip(INPUTS, (x, norm_g, w_in, lam_re, lam_im, b_re, b_im, c_re, c_im, d_skip, log_dt, w_glu, b_glu, ln_g, ln_b, w_s, b_s, w_pool, pool_scale, w_out, final_g, loss_target, m_norm_g, m_w_in, m_lam_re, m_lam_im, m_b_re, m_b_im, m_c_re, m_c_im, m_d_skip, m_log_dt, m_w_glu, m_b_glu, m_ln_g, m_ln_b, m_w_s, m_b_s, m_w_pool, m_pool_scale, m_w_out, m_final_g, v_norm_g, v_w_in, v_lam_re, v_lam_im, v_b_re, v_b_im, v_c_re, v_c_im, v_d_skip, v_log_dt, v_w_glu, v_b_glu, v_ln_g, v_ln_b, v_w_s, v_b_s, v_w_pool, v_pool_scale, v_w_out, v_final_g)))

    shards = [[w[l].astype(BF16) for w in (w_in, w_out, w_glu)] for l in range(DEPTH)]
    gather3 = [(GATHER, ti) for ti in range(3)]
    scatter3 = [(SCATTER, ti) for ti in range(3)]

    whole0 = _allgather_weights(*shards[0])
    gather1 = _split_start("gather1_start", shards[1], _empty_zones(shards[1], gather3), gather3, after=whole0[2])
    x1, c0 = _layer_fwd(p, 0, x[0], *whole0, token=gather1[-1])
    whole1 = _split_wait("gather1_wait", gather1, gather3, x1)
    x2, c1 = _layer_fwd(p, 1, x1, *whole1)

    dx, loss_tile, dfinal = _final_loss(x2, final_g.reshape(1, -1), loss_target[0])

    dz1, dwo1, dwg1, small1 = _mixers_bwd(c1, dx)
    dwi1, dx, dnorm1 = _inproj_grads(c1, dz1, dx)
    partials1 = [dwi1, dwo1, dwg1.astype(BF16)]
    grads1 = _split_start("grads1_start", partials1, _empty_zones(partials1, scatter3), scatter3)
    dz0, dwo0, dwg0, small0 = _mixers_bwd(c0, dx, token=grads1[-1])
    small_a = _pack_small([jnp.stack([small0[n], small1[n]]) for n in SMALL_A], BF16)
    srcs_a, kinds_a = [dwo0, dwg0.astype(BF16), small_a], [(SCATTER, 1), (SCATTER, 2), (SHARE, None)]
    grads0a = _split_start("grads0a_start", srcs_a, _empty_zones(srcs_a, kinds_a), kinds_a)
    dwi0 = _wgrad_in(c0["h"], *dz0, grads0a[-1])
    kinds_b = [(SCATTER, 0)]
    grads0b = _split_start("grads0b_start", [dwi0], _empty_zones([dwi0], kinds_b), kinds_b)
    dx, dnorm0 = _inproj_bwd(*dz0, c0["wi"], c0["x"], c0["norm_g"], dx, grads0b[-1])
    parts1 = _split_wait("grads1_wait", grads1, scatter3, dx)
    r_out0, r_glu0, r_a = _split_wait("grads0a_wait", grads0a, kinds_a, dx)
    (r_in0,) = _split_wait("grads0b_wait", grads0b, kinds_b, dx)
    parts0 = [r_in0, r_out0, r_glu0]
    r_b = _share_small(_pack_small(
        [jnp.stack([dnorm0.reshape(-1), dnorm1]), dfinal.reshape(-1), loss_tile[0, 0:1]], F32))
    loss = jnp.sum(r_b.reshape(N_DEV, -1)[:, LOSS_AT])

    out = {}

    def adam(name, ti, tr):
        shape2d = (DEPTH * SHARD_SHAPES[ti][0], SHARD_SHAPES[ti][1])
        res = _adamw(p[name].reshape(shape2d), p["m_" + name].reshape(shape2d), p["v_" + name].reshape(shape2d),
                     [parts0[ti], parts1[ti]], tr, "adamw_" + name)
        out[name] = [r.reshape(p[name].shape) for r in res]

    adam("w_in", 0, 512)
    adam("w_out", 1, 128)
    adam("w_glu", 2, 64)
    g_a, g_b = _sum_small(r_a, r_b)
    grads = dict(zip(SMALL_A, _unpack_small(g_a, [p[n] for n in SMALL_A])))
    grads.update(zip(SMALL_B, _unpack_small(g_b, [p[n] for n in SMALL_B])))
    rank2 = lambda a: a.reshape(1, -1) if a.ndim == 1 else a
    res = _adamw_small(*[[rank2(src[pre + n]) for n in SMALL] for src, pre in ((p, ""), (p, "m_"), (p, "v_"), (grads, ""))])
    for i, n in enumerate(SMALL):
        out[n] = [grads[n]] + [r.reshape(p[n].shape) for r in res[3 * i:3 * i + 3]]

    return (loss, dx[None], *[out[n][0] for n in WEIGHTS], *[out[n][1] for n in WEIGHTS],
            *[out[n][2] for n in WEIGHTS], *[out[n][3] for n in WEIGHTS])
```

```python
import math

import jax
import jax.numpy as jnp
import numpy as np
from jax import lax
from jax.experimental import pallas as pl
from jax.experimental.pallas import tpu as pltpu

F32 = jnp.float32
BF16 = jnp.bfloat16

D_MODEL = 2048
DEPTH = 2
S5_W, SGU_W, POOL_W = 512, 1024, 512
S5_GROUPS, S5_STATE, S5_CH = 32, 64, 16
N_STATE = S5_GROUPS * S5_STATE
CHUNK = 128
SGU_HEADS = 8
POOL_WINDOWS = (2, 4, 8, 16)
IN_COLS = 5120
RMS_EPS = 1e-6
LN_EPS = 1e-5
ADAM_LR, ADAM_B1, ADAM_B2, ADAM_EPS, ADAM_WD, ADAM_STEP = 0.001, 0.9, 0.999, 1e-08, 0.01, 10

CB = 512
N_CB = IN_COLS // CB
CB_XA, CB_U, CB_V, CB_XC, CB_GA, CB_GB, CB_GC = 0, 1, 3, 5, 6, 7, 9

N_DEV = 8
SH_IN = IN_COLS // N_DEV
SH_OUT = D_MODEL // N_DEV
SH_GLU = S5_W // N_DEV

VMEM_LIMIT = 52 * 1024 * 1024
VMEM_LIMIT_WHOLE = 56 * 1024 * 1024
HALO = 16
LANE_CH = 512

TN = (((0,), (0,)), ((), ()))
NT = (((1,), (1,)), ((), ()))


def _params(*sem):
    return pltpu.CompilerParams(dimension_semantics=sem if sem else None, vmem_limit_bytes=VMEM_LIMIT)


def _dot(a, b, dims=None):
    if dims is None:
        return jnp.dot(a, b, preferred_element_type=F32)
    return lax.dot_general(a, b, dims, preferred_element_type=F32)


_GELU_C = math.sqrt(2.0 / math.pi)


def _gelu(x):
    return 0.5 * x * (1.0 + jnp.tanh(_GELU_C * (x + 0.044715 * x * x * x)))


def _gelu_and_grad(x):
    t = jnp.tanh(_GELU_C * (x + 0.044715 * x * x * x))
    half = 0.5 * (1.0 + t)
    return x * half, half + 0.5 * x * (1.0 - t * t) * _GELU_C * (1.0 + 3.0 * 0.044715 * x * x)


def _gelu_grad(x):
    return _gelu_and_grad(x)[1]


def _silu_and_grad(x):
    s = jax.nn.sigmoid(x)
    return x * s, s * (1.0 + x * (1.0 - s))


def _rowsum(x):
    return jnp.sum(x, axis=0, keepdims=True)


def _after(token):
    if token is None:
        return [], []
    return [pl.BlockSpec((8, 128), lambda *_: (0, 0))], [token]


def _rms_inproj(x, g, w, token=None, tm=1024, tn=1024):
    t = x.shape[0]
    tm = min(tm, t)
    after_specs, after = _after(token)

    def body(x_ref, g_ref, w_ref, *rest):
        z_ref, h_ref = rest[-2:]

        @pl.when(pl.program_id(1) == 0)
        def _():
            xv = x_ref[...]
            r = lax.rsqrt(jnp.mean(xv * xv, axis=-1, keepdims=True) + RMS_EPS)
            h_ref[...] = (xv * r * g_ref[...]).astype(BF16)

        z_ref[...] = _dot(h_ref[...], w_ref[...])

    return pl.pallas_call(
        body,
        name="rms_inproj",
        grid=(t // tm, IN_COLS // tn),
        in_specs=[
            pl.BlockSpec((tm, D_MODEL), lambda m, n: (m, 0)),
            pl.BlockSpec((1, D_MODEL), lambda m, n: (0, 0)),
            pl.BlockSpec((D_MODEL, tn), lambda m, n: (0, n)),
        ] + after_specs,
        out_specs=[
            pl.BlockSpec((tm, tn), lambda m, n: (m, n)),
            pl.BlockSpec((tm, D_MODEL), lambda m, n: (m, 0)),
        ],
        out_shape=[jax.ShapeDtypeStruct((t, IN_COLS), F32), jax.ShapeDtypeStruct((t, D_MODEL), BF16)],
        compiler_params=_params("arbitrary", "arbitrary"),
    )(x, g, w, *after)


def _outproj(x, ya, yb, yc, w, tm=1024, tn=1024):
    t = x.shape[0]
    tm = min(tm, t)

    def body(x_ref, ya_ref, yb_ref, yc_ref, w_ref, o_ref):
        acc = _dot(ya_ref[...], w_ref[0:S5_W, :])
        acc += _dot(yb_ref[...], w_ref[S5_W:S5_W + SGU_W, :])
        acc += _dot(yc_ref[...], w_ref[S5_W + SGU_W:, :])
        o_ref[...] = x_ref[...] + acc

    return pl.pallas_call(
        body,
        name="outproj",
        grid=(t // tm, D_MODEL // tn),
        in_specs=[
            pl.BlockSpec((tm, tn), lambda m, n: (m, n)),
            pl.BlockSpec((tm, S5_W), lambda m, n: (m, 0)),
            pl.BlockSpec((tm, SGU_W), lambda m, n: (m, 0)),
            pl.BlockSpec((tm, POOL_W), lambda m, n: (m, 0)),
            pl.BlockSpec((D_MODEL, tn), lambda m, n: (0, n)),
        ],
        out_specs=pl.BlockSpec((tm, tn), lambda m, n: (m, n)),
        out_shape=jax.ShapeDtypeStruct((t, D_MODEL), F32),
        compiler_params=_params("arbitrary", "arbitrary"),
    )(x, ya, yb, yc, w)


def _outproj_bwd(dx, w, token=None, tm=1024):
    t = dx.shape[0]
    tm = min(tm, t)
    after_specs, after = _after(token)

    def body(dx_ref, w_ref, *rest):
        dya_ref, dyb_ref, dyc_ref = rest[-3:]
        dy = _dot(dx_ref[...].astype(BF16), w_ref[...], NT)
        dya_ref[...] = dy[:, 0:S5_W]
        dyb_ref[...] = dy[:, S5_W:S5_W + SGU_W]
        dyc_ref[...] = dy[:, S5_W + SGU_W:]

    return pl.pallas_call(
        body,
        name="outproj_bwd",
        grid=(t // tm,),
        in_specs=[
            pl.BlockSpec((tm, D_MODEL), lambda m: (m, 0)),
            pl.BlockSpec((D_MODEL, D_MODEL), lambda m: (0, 0)),
        ] + after_specs,
        out_specs=[
            pl.BlockSpec((tm, S5_W), lambda m: (m, 0)),
            pl.BlockSpec((tm, SGU_W), lambda m: (m, 0)),
            pl.BlockSpec((tm, POOL_W), lambda m: (m, 0)),
        ],
        out_shape=[
            jax.ShapeDtypeStruct((t, S5_W), F32),
            jax.ShapeDtypeStruct((t, SGU_W), F32),
            jax.ShapeDtypeStruct((t, POOL_W), F32),
        ],
        compiler_params=_params("arbitrary"),
    )(dx, w, *after)


def _dz_piece_maps():
    s5_map = lambda j: jnp.where(j >= CB_GA, 1, 0)
    sgu_map = lambda j: jnp.clip(jnp.where(j <= 4, j - 1, j - 3), 0, 5)
    pool_map = lambda j: jnp.where(j >= CB_GC, 1, 0)
    return s5_map, sgu_map, pool_map


def _pick_piece(j):
    is_s5 = jnp.logical_or(j == CB_XA, j == CB_GA)
    is_pool = jnp.logical_or(j == CB_XC, j == CB_GC)
    return is_s5, is_pool, jnp.logical_not(jnp.logical_or(is_s5, is_pool))


def _inproj_bwd(dz_s5, dz_sgu, dz_pool, w, x, g, dxo, token=None, tm=512):
    t = x.shape[0]
    s5_map, sgu_map, pool_map = _dz_piece_maps()
    after_specs, after = _after(token)

    def body(s5_ref, sgu_ref, pool_ref, w_ref, x_ref, g_ref, dxo_ref, *rest):
        dx_ref, dg_ref, acc = rest[-3:]
        m, j = pl.program_id(0), pl.program_id(1)

        @pl.when(jnp.logical_and(m == 0, j == 0))
        def _():
            dg_ref[...] = jnp.zeros_like(dg_ref)

        @pl.when(j == 0)
        def _():
            acc[...] = jnp.zeros_like(acc)

        is_s5, is_pool, is_sgu = _pick_piece(j)

        @pl.when(is_s5)
        def _():
            acc[...] += _dot(w_ref[...], s5_ref[...], NT)

        @pl.when(is_sgu)
        def _():
            acc[...] += _dot(w_ref[...], sgu_ref[...], NT)

        @pl.when(is_pool)
        def _():
            acc[...] += _dot(w_ref[...], pool_ref[...], NT)

        @pl.when(j == N_CB - 1)
        def _():
            xv = x_ref[...]
            r = lax.rsqrt(jnp.mean(xv * xv, axis=-1, keepdims=True) + RMS_EPS)
            n = xv * r
            dh = acc[...].T
            dg_ref[...] += _rowsum(dh * n)
            dn = dh * g_ref[...]
            dx_ref[...] = dxo_ref[...] + r * (dn - n * jnp.mean(dn * n, axis=-1, keepdims=True))

    return pl.pallas_call(
        body,
        name="inproj_bwd",
        grid=(t // tm, N_CB),
        in_specs=[
            pl.BlockSpec((tm, CB), lambda m, j: (m, s5_map(j))),
            pl.BlockSpec((tm, CB), lambda m, j: (m, sgu_map(j))),
            pl.BlockSpec((tm, CB), lambda m, j: (m, pool_map(j))),
            pl.BlockSpec((D_MODEL, CB), lambda m, j: (0, j)),
            pl.BlockSpec((tm, D_MODEL), lambda m, j: (m, 0)),
            pl.BlockSpec((1, D_MODEL), lambda m, j: (0, 0)),
            pl.BlockSpec((tm, D_MODEL), lambda m, j: (m, 0)),
        ] + after_specs,
        out_specs=[
            pl.BlockSpec((tm, D_MODEL), lambda m, j: (m, 0)),
            pl.BlockSpec((1, D_MODEL), lambda m, j: (0, 0)),
        ],
        out_shape=[jax.ShapeDtypeStruct((t, D_MODEL), F32), jax.ShapeDtypeStruct((1, D_MODEL), F32)],
        scratch_shapes=[pltpu.VMEM((D_MODEL, tm), F32)],
        compiler_params=_params("arbitrary", "arbitrary"),
    )(dz_s5, dz_sgu, dz_pool, w, x, g, dxo, *after)


def _wgrad_in(h, dz_s5, dz_sgu, dz_pool, token=None, tm=2048):
    t = h.shape[0]
    tm = min(tm, t)
    s5_map, sgu_map, pool_map = _dz_piece_maps()
    after_specs, after = _after(token)

    def body(h_ref, s5_ref, sgu_ref, pool_ref, *rest):
        o_ref, acc = rest[-2:]
        j, m = pl.program_id(0), pl.program_id(1)

        @pl.when(m == 0)
        def _():
            acc[...] = jnp.zeros_like(acc)

        is_s5, is_pool, is_sgu = _pick_piece(j)

        @pl.when(is_s5)
        def _():
            acc[...] += _dot(s5_ref[...], h_ref[...], TN)

        @pl.when(is_sgu)
        def _():
            acc[...] += _dot(sgu_ref[...], h_ref[...], TN)

        @pl.when(is_pool)
        def _():
            acc[...] += _dot(pool_ref[...], h_ref[...], TN)

        @pl.when(m == pl.num_programs(1) - 1)
        def _():
            o_ref[...] = acc[...].T.astype(BF16)

    return pl.pallas_call(
        body,
        name="wgrad_in",
        grid=(N_CB, t // tm),
        in_specs=[
            pl.BlockSpec((tm, D_MODEL), lambda j, m: (m, 0)),
            pl.BlockSpec((tm, CB), lambda j, m: (m, s5_map(j))),
            pl.BlockSpec((tm, CB), lambda j, m: (m, sgu_map(j))),
            pl.BlockSpec((tm, CB), lambda j, m: (m, pool_map(j))),
        ] + after_specs,
        out_specs=pl.BlockSpec((D_MODEL, CB), lambda j, m: (0, j)),
        out_shape=jax.ShapeDtypeStruct((D_MODEL, IN_COLS), BF16),
        scratch_shapes=[pltpu.VMEM((CB, D_MODEL), F32)],
        compiler_params=_params("arbitrary", "arbitrary"),
    )(h, dz_s5, dz_sgu, dz_pool, *after)


def _wgrad_out(ya, yb, yc, dx, token=None, tm=2048, tn=512):
    t = dx.shape[0]
    tm = min(tm, t)
    after_specs, after = _after(token)

    def body(ya_ref, yb_ref, yc_ref, dx_ref, *rest):
        o_ref, acc = rest[-2:]
        m = pl.program_id(1)

        @pl.when(m == 0)
        def _():
            acc[...] = jnp.zeros_like(acc)

        dxb = dx_ref[...].astype(BF16)
        acc[:, 0:S5_W] += _dot(dxb, ya_ref[...], TN)
        acc[:, S5_W:S5_W + SGU_W] += _dot(dxb, yb_ref[...], TN)
        acc[:, S5_W + SGU_W:] += _dot(dxb, yc_ref[...], TN)

        @pl.when(m == pl.num_programs(1) - 1)
        def _():
            o_ref[...] = acc[...].T.astype(BF16)

    return pl.pallas_call(
        body,
        name="wgrad_out",
        grid=(D_MODEL // tn, t // tm),
        in_specs=[
            pl.BlockSpec((tm, S5_W), lambda n, m: (m, 0)),
            pl.BlockSpec((tm, SGU_W), lambda n, m: (m, 0)),
            pl.BlockSpec((tm, POOL_W), lambda n, m: (m, 0)),
            pl.BlockSpec((tm, tn), lambda n, m: (m, n)),
        ] + after_specs,
        out_specs=pl.BlockSpec((D_MODEL, tn), lambda n, m: (0, n)),
        out_shape=jax.ShapeDtypeStruct((D_MODEL, D_MODEL), BF16),
        scratch_shapes=[pltpu.VMEM((tn, D_MODEL), F32)],
        compiler_params=_params("arbitrary", "arbitrary"),
    )(ya, yb, yc, dx, *after)


def _final_loss(x, g, target, tm=512):
    t = x.shape[0]

    def body(x_ref, g_ref, t_ref, dx_ref, loss_ref, dg_ref):
        @pl.when(pl.program_id(0) == 0)
        def _():
            loss_ref[...] = jnp.zeros_like(loss_ref)
            dg_ref[...] = jnp.zeros_like(dg_ref)

        xv = x_ref[...]
        gv = g_ref[...]
        r = lax.rsqrt(jnp.mean(xv * xv, axis=-1, keepdims=True) + RMS_EPS)
        n = xv * r
        err = n * gv - t_ref[...]
        loss_ref[...] += 0.5 * jnp.sum(jnp.mean(err * err, axis=-1, keepdims=True))
        dy = err * (1.0 / D_MODEL)
        dg_ref[...] += _rowsum(dy * n)
        dn = dy * gv
        dx_ref[...] = r * (dn - n * jnp.mean(dn * n, axis=-1, keepdims=True))

    return pl.pallas_call(
        body,
        name="final_loss",
        grid=(t // tm,),
        in_specs=[
            pl.BlockSpec((tm, D_MODEL), lambda m: (m, 0)),
            pl.BlockSpec((1, D_MODEL), lambda m: (0, 0)),
            pl.BlockSpec((tm, D_MODEL), lambda m: (m, 0)),
        ],
        out_specs=[
            pl.BlockSpec((tm, D_MODEL), lambda m: (m, 0)),
            pl.BlockSpec((8, 128), lambda m: (0, 0)),
            pl.BlockSpec((1, D_MODEL), lambda m: (0, 0)),
        ],
        out_shape=[
            jax.ShapeDtypeStruct((t, D_MODEL), F32),
            jax.ShapeDtypeStruct((8, 128), F32),
            jax.ShapeDtypeStruct((1, D_MODEL), F32),
        ],
        compiler_params=_params("arbitrary"),
    )(x, g, target)


N_Q = 2 * N_STATE // LANE_CH
N_LT = 2 * N_STATE // 128
N_PAIR = N_LT // 2
SEG = 8
PAIR_GROUP = 8
S5_TB = 512


def _cmul_add(b_re, b_im, a_re, a_im, s_re, s_im):
    return b_re + (a_re * s_re - a_im * s_im), b_im + (a_re * s_im + a_im * s_re)


def _s5_fill_powers(pw, a_ref, tb, reverse):
    seg_len = tb // SEG
    sign = -1.0 if reverse else 1.0
    for p in range(N_PAIR):
        a_re = jnp.broadcast_to(a_ref[p:p + 1, :], (SEG, 128))
        a_im = sign * jnp.broadcast_to(a_ref[N_PAIR + p:N_PAIR + p + 1, :], (SEG, 128))

        def step(k, c, p=p, a_re=a_re, a_im=a_im):
            rows = pl.ds(pl.multiple_of(((seg_len - 1 - k) if reverse else k) * SEG, SEG), SEG)
            pw[p, rows, :] = c[0]
            pw[N_PAIR + p, rows, :] = c[1]
            return c[0] * a_re - c[1] * a_im, c[0] * a_im + c[1] * a_re

        lax.fori_loop(0, seg_len, step, (a_re, a_im))


def _s5_scan(st, carry, a_ref, pw_ref, tb, reverse):
    seg_len = tb // SEG
    sign = -1.0 if reverse else 1.0
    sub = lax.broadcasted_iota(jnp.int32, (SEG, 128), 0)
    chain = (0 if reverse else seg_len - 1) * SEG
    full = lambda row: jnp.broadcast_to(row, (SEG, 128))
    for p0 in range(0, N_PAIR, PAIR_GROUP):
        pairs = list(range(p0, p0 + PAIR_GROUP))
        a_re = [full(a_ref[p:p + 1, :]) for p in pairs]
        a_im = [sign * full(a_ref[N_PAIR + p:N_PAIR + p + 1, :]) for p in pairs]

        def step(k, c, pairs=pairs, a_re=a_re, a_im=a_im):
            rows = pl.ds(pl.multiple_of(((seg_len - 1 - k) if reverse else k) * SEG, SEG), SEG)
            out = []
            for i, p in enumerate(pairs):
                n_re, n_im = _cmul_add(st[p, rows, :], st[N_PAIR + p, rows, :], a_re[i], a_im[i], c[2 * i], c[2 * i + 1])
                st[p, rows, :] = n_re
                st[N_PAIR + p, rows, :] = n_im
                out += [n_re, n_im]
            return tuple(out)

        ends = lax.fori_loop(0, seg_len, step, tuple(jnp.zeros((SEG, 128), F32) for _ in range(2 * PAIR_GROUP)))
        for i, p in enumerate(pairs):
            e_re, e_im = ends[2 * i], ends[2 * i + 1]
            w_re, w_im = pw_ref[p, chain:chain + SEG, :], pw_ref[N_PAIR + p, chain:chain + SEG, :]
            c_re, c_im = full(carry[p:p + 1, :]), full(carry[N_PAIR + p:N_PAIR + p + 1, :])
            for hop in range(SEG - 1):
                n_re, n_im = _cmul_add(e_re, e_im, w_re, w_im, c_re, c_im)
                target = SEG - 2 - hop if reverse else hop + 1
                shift = SEG - 1 if reverse else 1
                c_re = jnp.where(sub == target, pltpu.roll(n_re, shift, 0), c_re)
                c_im = jnp.where(sub == target, pltpu.roll(n_im, shift, 0), c_im)
            n_re, n_im = _cmul_add(e_re, e_im, w_re, w_im, c_re, c_im)
            last = 0 if reverse else SEG - 1
            carry[p:p + 1, :] = n_re[last:last + 1, :]
            carry[N_PAIR + p:N_PAIR + p + 1, :] = n_im[last:last + 1, :]
            in_re, in_im = jnp.tile(c_re, (seg_len, 1)), jnp.tile(c_im, (seg_len, 1))
            st[p], st[N_PAIR + p] = _cmul_add(st[p], st[N_PAIR + p], pw_ref[p], pw_ref[N_PAIR + p], in_re, in_im)


def _lane_chunk(ref, q):
    return jnp.concatenate([ref[4 * q + i] for i in range(4)], axis=1)


def _put_lane_chunk(ref, q, value):
    for i in range(4):
        ref[4 * q + i] = value[:, 128 * i:128 * (i + 1)]


def _step_major(tb):
    r = np.arange(tb)
    pm = np.zeros((tb, tb), np.float32)
    pm[r, (r % SEG) * (tb // SEG) + r // SEG] = 1.0
    return jnp.asarray(pm, BF16), jnp.asarray(pm.T, BF16)


def _unpermute(pt_ref, v):
    hi = v.astype(BF16)
    lo = (v - hi.astype(F32)).astype(BF16)
    return _dot(pt_ref[...], hi) + _dot(pt_ref[...], lo)


def _s5_fwd(z, bc, cc, a, pm, pt, dvec, wglu, bglu, tb=256):
    t = z.shape[0]

    def body(xa_ref, ga_ref, bc_ref, cc_ref, a_ref, pm_ref, pt_ref, d_ref, wglu_ref, bglu_ref,
             ya_ref, s_ref, ys_ref, st, carry, pw_ref):
        @pl.when(pl.program_id(0) == 0)
        def _():
            carry[...] = jnp.zeros_like(carry)
            _s5_fill_powers(pw_ref, a_ref, tb, reverse=False)

        xa = xa_ref[...]
        xab = _dot(pm_ref[...], xa.astype(BF16)).astype(BF16)
        for q in range(N_Q):
            _put_lane_chunk(st, q, _dot(xab[:, 128 * (q % 4):128 * (q % 4) + 128], bc_ref[:, pl.ds(LANE_CH * q, LANE_CH)]))
        _s5_scan(st, carry, a_ref, pw_ref, tb, reverse=False)
        s_ref[...] = st[...].astype(BF16)
        cols = []
        for j in range(4):
            lo, hi = LANE_CH * j, N_STATE + LANE_CH * j
            cols.append(_dot(_lane_chunk(s_ref, j), cc_ref[lo:lo + LANE_CH, :])
                        + _dot(_lane_chunk(s_ref, 4 + j), cc_ref[hi:hi + LANE_CH, :]))
        ys = _unpermute(pt_ref, jnp.concatenate(cols, axis=1)) + d_ref[...] * xa
        ys_ref[...] = ys
        ya1 = _gelu(ys)
        pre = _dot(ya1.astype(BF16), wglu_ref[...]) + bglu_ref[...]
        silu_ga, _ = _silu_and_grad(ga_ref[...])
        ya_ref[...] = (ya1 * jax.nn.sigmoid(pre) * silu_ga).astype(BF16)

    const = lambda shape: pl.BlockSpec(shape, lambda i: (0,) * len(shape))
    return pl.pallas_call(
        body,
        name="s5_fwd",
        grid=(t // tb,),
        in_specs=[
            pl.BlockSpec((tb, CB), lambda i: (i, CB_XA)),
            pl.BlockSpec((tb, CB), lambda i: (i, CB_GA)),
            const((128, 2 * N_STATE)),
            const((2 * N_STATE, 128)),
            const((N_LT, 128)),
            const((tb, tb)),
            const((tb, tb)),
            const((1, S5_W)),
            const((S5_W, S5_W)),
            const((1, S5_W)),
        ],
        out_specs=[
            pl.BlockSpec((tb, S5_W), lambda i: (i, 0)),
            pl.BlockSpec((N_LT, tb, 128), lambda i: (0, i, 0)),
            pl.BlockSpec((tb, S5_W), lambda i: (i, 0)),
        ],
        out_shape=[
            jax.ShapeDtypeStruct((t, S5_W), BF16),
            jax.ShapeDtypeStruct((N_LT, t, 128), BF16),
            jax.ShapeDtypeStruct((t, S5_W), F32),
        ],
        scratch_shapes=[pltpu.VMEM((N_LT, tb, 128), F32), pltpu.VMEM((N_LT, 128), F32), pltpu.VMEM((N_LT, tb, 128), F32)],
        compiler_params=_params("arbitrary"),
    )(z, z, bc, cc, a, pm, pt, dvec, wglu, bglu)


def _s5_bwd(dya, ys, z, s, bc, cc, a, pm, pt, dvec, wglu, bglu, tb=256):
    t = z.shape[0]
    nb = t // tb
    rev = lambda i: nb - 1 - i

    def body(dya_ref, ys_ref, xa_ref, ga_ref, s_ref, sp_ref, bc_ref, cc_ref, a_ref, pm_ref, pt_ref, d_ref,
             wglu_ref, bglu_ref, dz_ref, dbc_ref, dcct_ref, da_ref, dd_ref, dwglu_ref, dbglu_ref, g, carry, pw_ref):
        i = pl.program_id(0)

        @pl.when(i == 0)
        def _():
            carry[...] = jnp.zeros_like(carry)
            _s5_fill_powers(pw_ref, a_ref, tb, reverse=True)
            for r in (dbc_ref, dcct_ref, da_ref, dd_ref, dwglu_ref, dbglu_ref):
                r[...] = jnp.zeros_like(r)

        ys = ys_ref[...]
        xa = xa_ref[...]
        ga = ga_ref[...]
        dya = dya_ref[...]
        ya1, ya1_grad = _gelu_and_grad(ys)
        ya1b = ya1.astype(BF16)
        sg = jax.nn.sigmoid(_dot(ya1b, wglu_ref[...]) + bglu_ref[...])
        silu_ga, silu_ga_grad = _silu_and_grad(ga)
        dz_ref[:, S5_W:] = (dya * (ya1 * sg) * silu_ga_grad).astype(BF16)
        dya2 = dya * silu_ga
        dpre = dya2 * ya1 * sg * (1.0 - sg)
        dbglu_ref[...] += _rowsum(dpre)
        dpreb = dpre.astype(BF16)
        dwglu_ref[...] += _dot(ya1b, dpreb, TN)
        dys = (dya2 * sg + _dot(dpreb, wglu_ref[...], NT)) * ya1_grad
        dd_ref[...] += _rowsum(dys * xa)
        dysb = _dot(pm_ref[...], dys.astype(BF16)).astype(BF16)
        xab = _dot(pm_ref[...], xa.astype(BF16)).astype(BF16)

        for q in range(N_Q):
            cq = pl.ds(LANE_CH * q, LANE_CH)
            x0 = 128 * (q % 4)
            dcct_ref[:, cq] += _dot(dysb[:, x0:x0 + 128], _lane_chunk(s_ref, q), TN)
            _put_lane_chunk(g, q, _dot(dysb[:, x0:x0 + 128], cc_ref[cq, :], NT))
        _s5_scan(g, carry, a_ref, pw_ref, tb, reverse=True)

        seg0 = (lax.broadcasted_iota(jnp.int32, (SEG, 128), 0) == 0)
        have_prev = i < nb - 1

        def before_step0(tile, halo):
            prev_last = jnp.where(have_prev, halo.astype(F32)[HALO - 1:HALO, :], 0.0)
            return jnp.where(seg0, prev_last, pltpu.roll(tile[tb - SEG:, :], 1, 0))

        for p in range(N_PAIR):
            g_re, g_im = g[p], g[N_PAIR + p]
            s_re, s_im = s_ref[p].astype(F32), s_ref[N_PAIR + p].astype(F32)
            f_re, f_im = before_step0(s_re, sp_ref[p]), before_step0(s_im, sp_ref[N_PAIR + p])
            b_re, b_im, h_re, h_im = s_re[:tb - SEG], s_im[:tb - SEG], g_re[SEG:], g_im[SEG:]
            da_ref[p:p + 1, :] += (_rowsum(b_re * h_re + b_im * h_im)
                                   + _rowsum(f_re * g_re[:SEG] + f_im * g_im[:SEG]))
            da_ref[N_PAIR + p:N_PAIR + p + 1, :] += (_rowsum(b_re * h_im - b_im * h_re)
                                                     + _rowsum(f_re * g_im[:SEG] - f_im * g_re[:SEG]))
        dxa_cols = []
        for j in range(4):
            re = pl.ds(LANE_CH * j, LANE_CH)
            im = pl.ds(N_STATE + LANE_CH * j, LANE_CH)
            x0 = 128 * j
            gb_re, gb_im = _lane_chunk(g, j).astype(BF16), _lane_chunk(g, 4 + j).astype(BF16)
            dbc_ref[:, re] += _dot(xab[:, x0:x0 + 128], gb_re, TN)
            dbc_ref[:, im] += _dot(xab[:, x0:x0 + 128], gb_im, TN)
            dxa_cols.append(_dot(gb_re, bc_ref[:, re], NT) + _dot(gb_im, bc_ref[:, im], NT))
        dz_ref[:, 0:S5_W] = (dys * d_ref[...] + _unpermute(pt_ref, jnp.concatenate(dxa_cols, axis=1))).astype(BF16)

    const = lambda shape: pl.BlockSpec(shape, lambda i: (0,) * len(shape))
    per_halo = tb // HALO
    return pl.pallas_call(
        body,
        name="s5_bwd",
        grid=(nb,),
        in_specs=[
            pl.BlockSpec((tb, S5_W), lambda i: (rev(i), 0)),
            pl.BlockSpec((tb, S5_W), lambda i: (rev(i), 0)),
            pl.BlockSpec((tb, CB), lambda i: (rev(i), CB_XA)),
            pl.BlockSpec((tb, CB), lambda i: (rev(i), CB_GA)),
            pl.BlockSpec((N_LT, tb, 128), lambda i: (0, rev(i), 0)),
            pl.BlockSpec((N_LT, HALO, 128), lambda i: (0, jnp.maximum(rev(i) * per_halo - 1, 0), 0)),
            const((128, 2 * N_STATE)),
            const((2 * N_STATE, 128)),
            const((N_LT, 128)),
            const((tb, tb)),
            const((tb, tb)),
            const((1, S5_W)),
            const((S5_W, S5_W)),
            const((1, S5_W)),
        ],
        out_specs=[
            pl.BlockSpec((tb, 2 * CB), lambda i: (rev(i), 0)),
            const((128, 2 * N_STATE)),
            const((128, 2 * N_STATE)),
            const((N_LT, 128)),
            const((1, S5_W)),
            const((S5_W, S5_W)),
            const((1, S5_W)),
        ],
        out_shape=[
            jax.ShapeDtypeStruct((t, 2 * CB), BF16),
            jax.ShapeDtypeStruct((128, 2 * N_STATE), F32),
            jax.ShapeDtypeStruct((128, 2 * N_STATE), F32),
            jax.ShapeDtypeStruct((N_LT, 128), F32),
            jax.ShapeDtypeStruct((1, S5_W), F32),
            jax.ShapeDtypeStruct((S5_W, S5_W), F32),
            jax.ShapeDtypeStruct((1, S5_W), F32),
        ],
        scratch_shapes=[pltpu.VMEM((N_LT, tb, 128), F32), pltpu.VMEM((N_LT, 128), F32), pltpu.VMEM((N_LT, tb, 128), F32)],
        compiler_params=_params("arbitrary"),
    )(dya, ys, z, z, s, s, bc, cc, a, pm, pt, dvec, wglu, bglu)


def _sgu_norm(v0, v1, lng_ref, lnb_ref):
    g0, g1 = _gelu(v0), _gelu(v1)
    mu = (jnp.sum(g0, axis=-1, keepdims=True) + jnp.sum(g1, axis=-1, keepdims=True)) * (1.0 / SGU_W)
    c0, c1 = g0 - mu, g1 - mu
    var = (jnp.sum(c0 * c0, axis=-1, keepdims=True) + jnp.sum(c1 * c1, axis=-1, keepdims=True)) * (1.0 / SGU_W)
    rstd = lax.rsqrt(var + LN_EPS)
    vh0, vh1 = c0 * rstd, c1 * rstd
    vn0 = vh0 * lng_ref[:, 0:CB] + lnb_ref[:, 0:CB]
    vn1 = vh1 * lng_ref[:, CB:] + lnb_ref[:, CB:]
    return (vh0, vh1), (vn0, vn1), rstd


def _sgu_fwd(z, lng, lnb, ws, bsx, tb=512):
    t = z.shape[0]
    tb = min(tb, t)

    def body(u0_ref, u1_ref, v0_ref, v1_ref, gb0_ref, gb1_ref, lng_ref, lnb_ref, ws_ref, bsx_ref, yb_ref):
        _, (vn0, vn1), _ = _sgu_norm(v0_ref[...], v1_ref[...], lng_ref, lnb_ref)
        for half, (vn, u_ref, gb_ref) in enumerate(((vn0, u0_ref, gb0_ref), (vn1, u1_ref, gb1_ref))):
            vnb = vn.astype(BF16)
            silu_gb, _ = _silu_and_grad(gb_ref[...])
            gate = _gelu(u_ref[...]) * silu_gb
            for hh in range(4):
                h = 4 * half + hh
                for c in range(tb // CHUNK):
                    rows, cols = slice(CHUNK * c, CHUNK * (c + 1)), slice(128 * hh, 128 * (hh + 1))
                    sp = _dot(ws_ref[h], vnb[rows, cols]) + bsx_ref[h]
                    yb_ref[rows, CB * half + 128 * hh:CB * half + 128 * (hh + 1)] = (gate[rows, cols] * sp).astype(BF16)

    zb = lambda j: pl.BlockSpec((tb, CB), lambda i, j=j: (i, j))
    const = lambda shape: pl.BlockSpec(shape, lambda i: (0,) * len(shape))
    return pl.pallas_call(
        body,
        name="sgu_fwd",
        grid=(t // tb,),
        in_specs=[zb(CB_U), zb(CB_U + 1), zb(CB_V), zb(CB_V + 1), zb(CB_GB), zb(CB_GB + 1),
                  const((1, SGU_W)), const((1, SGU_W)), const((SGU_HEADS, CHUNK, CHUNK)), const((SGU_HEADS, CHUNK, 128))],
        out_specs=pl.BlockSpec((tb, SGU_W), lambda i: (i, 0)),
        out_shape=jax.ShapeDtypeStruct((t, SGU_W), BF16),
        compiler_params=_params("arbitrary"),
    )(z, z, z, z, z, z, lng, lnb, ws, bsx)


def _sgu_bwd(dyb, z, lng, lnb, ws, wst, bsx, tb=512):
    t = z.shape[0]
    tb = min(tb, t)

    def body(dyb_ref, u0_ref, u1_ref, v0_ref, v1_ref, gb0_ref, gb1_ref, lng_ref, lnb_ref, ws_ref, wst_ref, bsx_ref,
             dz_ref, dlng_ref, dlnb_ref, dws_ref, dbs_ref, dvn):
        @pl.when(pl.program_id(0) == 0)
        def _():
            for r in (dlng_ref, dlnb_ref, dws_ref, dbs_ref):
                r[...] = jnp.zeros_like(r)

        v0, v1 = v0_ref[...], v1_ref[...]
        (vh0, vh1), (vn0, vn1), rstd = _sgu_norm(v0, v1, lng_ref, lnb_ref)
        causal = (lax.broadcasted_iota(jnp.int32, (CHUNK, CHUNK), 0) >= lax.broadcasted_iota(jnp.int32, (CHUNK, CHUNK), 1))
        for half, (vn, u_ref, gb_ref) in enumerate(((vn0, u0_ref, gb0_ref), (vn1, u1_ref, gb1_ref))):
            vnb = vn.astype(BF16)
            u = u_ref[...]
            ug, ug_grad = _gelu_and_grad(u)
            silu_gb, silu_gb_grad = _silu_and_grad(gb_ref[...])
            dyb = dyb_ref[:, CB * half:CB * (half + 1)]
            dyb0 = dyb * silu_gb
            ds = dyb0 * ug
            sp_cols = []
            for hh in range(4):
                h = 4 * half + hh
                cols = slice(128 * hh, 128 * (hh + 1))
                sp_rows = []
                for c in range(tb // CHUNK):
                    rows = slice(CHUNK * c, CHUNK * (c + 1))
                    vt = vnb[rows, cols]
                    sp_rows.append(_dot(ws_ref[h], vt) + bsx_ref[h])
                    dst = ds[rows, cols]
                    dstb = dst.astype(BF16)
                    dbs_ref[h] += dst
                    dws_ref[h] += jnp.where(causal, _dot(dstb, vt, NT), 0.0)
                    dvn[rows, CB * half + 128 * hh:CB * half + 128 * (hh + 1)] = _dot(wst_ref[h], dstb)
                sp_cols.append(jnp.concatenate(sp_rows, axis=0))
            sp = jnp.concatenate(sp_cols, axis=1)
            dz_ref[:, CB * half:CB * (half + 1)] = (dyb0 * sp * ug_grad).astype(BF16)
            dz_ref[:, 2 * SGU_W + CB * half:2 * SGU_W + CB * (half + 1)] = (dyb * (ug * sp) * silu_gb_grad).astype(BF16)

        dvn0, dvn1 = dvn[:, 0:CB], dvn[:, CB:]
        dlng_ref[:, 0:CB] += _rowsum(dvn0 * vh0)
        dlng_ref[:, CB:] += _rowsum(dvn1 * vh1)
        dlnb_ref[:, 0:CB] += _rowsum(dvn0)
        dlnb_ref[:, CB:] += _rowsum(dvn1)
        dh0, dh1 = dvn0 * lng_ref[:, 0:CB], dvn1 * lng_ref[:, CB:]
        m1 = (jnp.sum(dh0, axis=-1, keepdims=True) + jnp.sum(dh1, axis=-1, keepdims=True)) * (1.0 / SGU_W)
        m2 = (jnp.sum(dh0 * vh0, axis=-1, keepdims=True) + jnp.sum(dh1 * vh1, axis=-1, keepdims=True)) * (1.0 / SGU_W)
        dz_ref[:, SGU_W:SGU_W + CB] = (rstd * (dh0 - m1 - vh0 * m2) * _gelu_grad(v0)).astype(BF16)
        dz_ref[:, SGU_W + CB:2 * SGU_W] = (rstd * (dh1 - m1 - vh1 * m2) * _gelu_grad(v1)).astype(BF16)

    zb = lambda j: pl.BlockSpec((tb, CB), lambda i, j=j: (i, j))
    const = lambda shape: pl.BlockSpec(shape, lambda i: (0,) * len(shape))
    hmat = (SGU_HEADS, CHUNK, CHUNK)
    return pl.pallas_call(
        body,
        name="sgu_bwd",
        grid=(t // tb,),
        in_specs=[pl.BlockSpec((tb, SGU_W), lambda i: (i, 0)),
                  zb(CB_U), zb(CB_U + 1), zb(CB_V), zb(CB_V + 1), zb(CB_GB), zb(CB_GB + 1),
                  const((1, SGU_W)), const((1, SGU_W)), const(hmat), const(hmat), const(hmat)],
        out_specs=[pl.BlockSpec((tb, 3 * SGU_W), lambda i: (i, 0)),
                   const((1, SGU_W)), const((1, SGU_W)), const(hmat), const(hmat)],
        out_shape=[jax.ShapeDtypeStruct((t, 3 * SGU_W), BF16),
                   jax.ShapeDtypeStruct((1, SGU_W), F32), jax.ShapeDtypeStruct((1, SGU_W), F32),
                   jax.ShapeDtypeStruct(hmat, F32), jax.ShapeDtypeStruct(hmat, F32)],
        scratch_shapes=[pltpu.VMEM((tb, SGU_W), F32)],
        compiler_params=_params("arbitrary"),
    )(dyb, z, z, z, z, z, z, lng, lnb, ws, wst, bsx)


def _window_sums(ext, lookahead):
    n = ext.shape[0]
    out = []
    for gi, w in enumerate(POOL_WINDOWS):
        acc = ext[:, 128 * gi:128 * (gi + 1)]
        k = 1
        while k < w:
            acc = acc + pltpu.roll(acc, (n - k) if lookahead else k, 0)
            k *= 2
        out.append(acc)
    return jnp.concatenate(out, axis=1)


def _pool_counts(row0, tb):
    pos = (row0 + 1 + lax.broadcasted_iota(jnp.int32, (tb, POOL_W), 0)).astype(F32)
    lane = lax.broadcasted_iota(jnp.int32, (tb, POOL_W), 1)
    win = jnp.where(lane < 128, 2.0, jnp.where(lane < 256, 4.0, jnp.where(lane < 384, 8.0, 16.0)))
    return jnp.minimum(pos, win)


def _pool_fwd(z, wpool, scale, tb=1024):
    t = z.shape[0]
    tb = min(tb, t)

    def body(xc_ref, gc_ref, wp_ref, sc_ref, yc_ref, halo):
        i = pl.program_id(0)

        @pl.when(i == 0)
        def _():
            halo[...] = jnp.zeros_like(halo)

        xc = xc_ref[...]
        sums = _window_sums(jnp.concatenate([halo[...], xc], axis=0), lookahead=False)[HALO:, :]
        halo[...] = xc[tb - HALO:, :]
        pb = (sums / _pool_counts(i * tb, tb) - xc).astype(BF16)
        q = jnp.concatenate([_dot(pb[:, 128 * gi:128 * (gi + 1)], wp_ref[gi]) for gi in range(4)], axis=1)
        silu_gc, _ = _silu_and_grad(gc_ref[...])
        yc_ref[...] = (q * sc_ref[...] * silu_gc).astype(BF16)

    const = lambda shape: pl.BlockSpec(shape, lambda i: (0,) * len(shape))
    return pl.pallas_call(
        body,
        name="pool_fwd",
        grid=(t // tb,),
        in_specs=[pl.BlockSpec((tb, CB), lambda i: (i, CB_XC)), pl.BlockSpec((tb, CB), lambda i: (i, CB_GC)),
                  const((4, 128, 128)), const((1, POOL_W))],
        out_specs=pl.BlockSpec((tb, POOL_W), lambda i: (i, 0)),
        out_shape=jax.ShapeDtypeStruct((t, POOL_W), BF16),
        scratch_shapes=[pltpu.VMEM((HALO, POOL_W), F32)],
        compiler_params=_params("arbitrary"),
    )(z, z, wpool, scale)


def _pool_bwd(dyc, z, wpool, scale, tb=1024):
    t = z.shape[0]
    tb = min(tb, t)
    nb = t // tb
    rev = lambda i: nb - 1 - i
    per_halo = tb // HALO

    def body(dyc_ref, xc_ref, xp_ref, gc_ref, wp_ref, sc_ref, dz_ref, dwp_ref, dsc_ref, ehalo):
        i = pl.program_id(0)

        @pl.when(i == 0)
        def _():
            ehalo[...] = jnp.zeros_like(ehalo)
            dwp_ref[...] = jnp.zeros_like(dwp_ref)
            dsc_ref[...] = jnp.zeros_like(dsc_ref)

        xc = xc_ref[...]
        prev = jnp.where(i < nb - 1, xp_ref[...], 0.0)
        sums = _window_sums(jnp.concatenate([prev, xc], axis=0), lookahead=False)[HALO:, :]
        cnt = _pool_counts(rev(i) * tb, tb)
        pb = (sums / cnt - xc).astype(BF16)
        q = jnp.concatenate([_dot(pb[:, 128 * gi:128 * (gi + 1)], wp_ref[gi]) for gi in range(4)], axis=1)
        silu_gc, silu_gc_grad = _silu_and_grad(gc_ref[...])
        dyc = dyc_ref[...]
        dz_ref[:, POOL_W:] = (dyc * (q * sc_ref[...]) * silu_gc_grad).astype(BF16)
        dyc0 = dyc * silu_gc
        dsc_ref[...] += _rowsum(dyc0 * q)
        dqb = (dyc0 * sc_ref[...]).astype(BF16)
        dp_cols = []
        for gi in range(4):
            cols = slice(128 * gi, 128 * (gi + 1))
            dwp_ref[gi] += _dot(pb[:, cols], dqb[:, cols], TN)
            dp_cols.append(_dot(dqb[:, cols], wp_ref[gi], NT))
        dp = jnp.concatenate(dp_cols, axis=1)
        e = dp / cnt
        fut = _window_sums(jnp.concatenate([e, ehalo[...]], axis=0), lookahead=True)[:tb, :]
        ehalo[...] = e[:HALO, :]
        dz_ref[:, 0:POOL_W] = (fut - dp).astype(BF16)

    const = lambda shape: pl.BlockSpec(shape, lambda i: (0,) * len(shape))
    return pl.pallas_call(
        body,
        name="pool_bwd",
        grid=(nb,),
        in_specs=[pl.BlockSpec((tb, POOL_W), lambda i: (rev(i), 0)),
                  pl.BlockSpec((tb, CB), lambda i: (rev(i), CB_XC)),
                  pl.BlockSpec((HALO, CB), lambda i: (jnp.maximum(rev(i) * per_halo - 1, 0), CB_XC)),
                  pl.BlockSpec((tb, CB), lambda i: (rev(i), CB_GC)),
                  const((4, 128, 128)), const((1, POOL_W))],
        out_specs=[pl.BlockSpec((tb, 2 * POOL_W), lambda i: (rev(i), 0)), const((4, 128, 128)), const((1, POOL_W))],
        out_shape=[jax.ShapeDtypeStruct((t, 2 * POOL_W), BF16),
                   jax.ShapeDtypeStruct((4, 128, 128), F32), jax.ShapeDtypeStruct((1, POOL_W), F32)],
        scratch_shapes=[pltpu.VMEM((HALO, POOL_W), F32)],
        compiler_params=_params("arbitrary"),
    )(dyc, z, z, z, wpool, scale)


def _adam_math(w, m, v, g):
    nm = ADAM_B1 * m + (1.0 - ADAM_B1) * g
    nv = ADAM_B2 * v + (1.0 - ADAM_B2) * (g * g)
    m_hat = nm / (1.0 - ADAM_B1 ** ADAM_STEP)
    v_hat = nv / (1.0 - ADAM_B2 ** ADAM_STEP)
    return -ADAM_LR * (m_hat / (jnp.sqrt(v_hat) + ADAM_EPS) + ADAM_WD * w), nm, nv


def _sum_small(r_a, r_b, steps=7):
    rows_a, rows_b = r_a.shape[1], r_b.shape[1]
    tr = rows_a // steps

    def body(a_ref, b_ref, ga_ref, gb_ref):
        for src, dst in ((a_ref, ga_ref), (b_ref, gb_ref)):
            g = src[0].astype(F32)
            for k in range(1, N_DEV):
                g = g + src[k].astype(F32)
            dst[...] = g

    return pl.pallas_call(
        body,
        name="sum_small",
        grid=(steps,),
        in_specs=[pl.BlockSpec((N_DEV, tr, 128), lambda i: (0, i, 0)), pl.BlockSpec((N_DEV, rows_b, 128), lambda i: (0, 0, 0))],
        out_specs=[pl.BlockSpec((tr, 128), lambda i: (i, 0)), pl.BlockSpec((rows_b, 128), lambda i: (0, 0))],
        out_shape=[jax.ShapeDtypeStruct((rows_a, 128), F32), jax.ShapeDtypeStruct((rows_b, 128), F32)],
        compiler_params=_params("arbitrary"),
    )(r_a, r_b)


def _adamw_small(ws, ms, vs, gs):
    n = len(ws)
    whole = pl.BlockSpec(memory_space=pltpu.VMEM)

    def body(*refs):
        ins, outs = refs[:4 * n], refs[4 * n:]
        for i in range(n):
            w, m, v, g = (ins[4 * i + j][...] for j in range(4))
            outs[3 * i][...], outs[3 * i + 1][...], outs[3 * i + 2][...] = _adam_math(w, m, v, g)

    return pl.pallas_call(
        body,
        name="adamw_small",
        in_specs=[whole] * (4 * n),
        out_specs=[whole] * (3 * n),
        out_shape=[jax.ShapeDtypeStruct(w.shape, F32) for w in ws for _ in range(3)],
        compiler_params=pltpu.CompilerParams(vmem_limit_bytes=VMEM_LIMIT_WHOLE),
    )(*[a for group in zip(ws, ms, vs, gs) for a in group])


def _adamw(w, m, v, parts, tr, name):
    r, c = w.shape
    n_slab = len(parts)
    per_slab = r // n_slab // tr

    def body(w_ref, m_ref, v_ref, *refs):
        p_refs, (g_ref, d_ref, nm_ref, nv_ref) = refs[:n_slab], refs[n_slab:]
        for s, p_ref in enumerate(p_refs):
            @pl.when(pl.program_id(0) // per_slab == s)
            def _(p_ref=p_ref):
                g = p_ref[0].astype(F32)
                for k in range(1, N_DEV):
                    g = g + p_ref[k].astype(F32)
                g_ref[...] = g

        d_ref[...], nm_ref[...], nv_ref[...] = _adam_math(w_ref[...], m_ref[...], v_ref[...], g_ref[...])

    blk = pl.BlockSpec((tr, c), lambda i: (i, 0))
    slab = lambda s: pl.BlockSpec((N_DEV, tr, c), lambda i, s=s: (0, jnp.clip(i - s * per_slab, 0, per_slab - 1), 0))
    return pl.pallas_call(
        body,
        name=name,
        grid=(r // tr,),
        in_specs=[blk, blk, blk] + [slab(s) for s in range(n_slab)],
        out_specs=[blk, blk, blk, blk],
        out_shape=[jax.ShapeDtypeStruct((r, c), F32)] * 4,
        compiler_params=_params("arbitrary"),
    )(w, m, v, *parts)


MESH = pl.DeviceIdType.MESH
ANY = pl.BlockSpec(memory_space=pl.ANY)


def _dev_index(dev):
    return 4 * dev[0] + 2 * dev[1] + dev[2]


WHOLE_SHAPES = ((D_MODEL, IN_COLS), (D_MODEL, D_MODEL), (S5_W, S5_W))
SHARD_SHAPES =((D_MODEL, SH_IN), (SH_OUT, D_MODEL), (SH_GLU, S5_W))


def _shard_of(ref, ti, idx):
    if ti == 0:
        return ref.at[:, pl.ds(pl.multiple_of(idx * SH_IN, 128), SH_IN)]
    rows = SHARD_SHAPES[ti][0]
    return ref.at[pl.ds(pl.multiple_of(idx * rows, rows), rows), :]


def _peer(mask, x, y, c):
    return (1 - x if mask & 4 else x, 1 - y if mask & 2 else y, 1 - c if mask & 1 else c)


def _allgather_weights(shard_list, kinds):
    n_t = len(kinds)

    def body(*refs):
        shards, wholes = refs[0:n_t], refs[n_t:2 * n_t]
        send_sems, recv_sems, local_sems = refs[2 * n_t:]
        x, y, c = lax.axis_index("x"), lax.axis_index("y"), lax.axis_index("c")
        me, sibling = (x, y, c), (x, y, 1 - c)
        chips = [(1 - x, y), (x, 1 - y), (1 - x, 1 - y)]

        def slot(ti, dev):
            return _shard_of(wholes[ti], kinds[ti], _dev_index(dev))

        def copy(k, ti, block, to, own=False):
            return pltpu.make_async_remote_copy(
                src_ref=shards[ti] if own else slot(ti, block), dst_ref=slot(ti, block),
                send_sem=send_sems.at[n_t * k + ti], recv_sem=recv_sems.at[n_t * k + ti],
                device_id=to, device_id_type=MESH)

        mine = [pltpu.make_async_copy(shards[ti], slot(ti, me), local_sems.at[ti]) for ti in range(n_t)]
        for cp in mine:
            cp.start()
        first = [copy(0, ti, me, sibling, own=True) for ti in range(n_t)]
        first += [copy(1 + j, ti, me, (*chip, c), own=True) for j, chip in enumerate(chips) for ti in range(n_t)]
        for cp in first:
            cp.start()
        passed = []
        for j, chip in enumerate(chips):
            for ti in range(n_t):
                copy(1 + j, ti, (*chip, c), me).wait_recv()
            onward = [copy(4 + j, ti, (*chip, c), sibling) for ti in range(n_t)]
            for cp in onward:
                cp.start()
            passed += onward
        for ti in range(n_t):
            copy(0, ti, sibling, me).wait_recv()
        for j, chip in enumerate(chips):
            for ti in range(n_t):
                copy(4 + j, ti, (*chip, 1 - c), me).wait_recv()
        for cp in first + passed:
            cp.wait_send()
        for cp in mine:
            cp.wait()

    return pl.pallas_call(
        body,
        name="allgather_weights",
        in_specs=[ANY] * n_t,
        out_specs=[ANY] * n_t,
        out_shape=[jax.ShapeDtypeStruct(WHOLE_SHAPES[k], BF16) for k in kinds],
        scratch_shapes=[pltpu.SemaphoreType.DMA((7 * n_t,)), pltpu.SemaphoreType.DMA((7 * n_t,)),
                        pltpu.SemaphoreType.DMA((n_t,))],
    )(*shard_list)


HBM = pl.BlockSpec(memory_space=pltpu.HBM)
SEM = pl.BlockSpec(memory_space=pltpu.SEMAPHORE)
GATHER, SCATTER, SHARE = "gather", "scatter", "share"


def _split_route(kind, ti, sending, me_idx, p_idx, src_ref, land_ref):
    owner = me_idx if sending else p_idx
    if kind == GATHER:
        return src_ref, _shard_of(land_ref, ti, owner)
    if kind == SCATTER:
        return _shard_of(src_ref, ti, p_idx), land_ref.at[owner]
    return src_ref, land_ref.at[owner]


def _split_start(name, srcs, lands, kinds, after=None):
    n = len(srcs)
    arrays = list(srcs) + list(lands) + ([] if after is None else [after])

    def body(*refs):
        src_refs, land_refs = refs[0:n], refs[n:2 * n]
        send_sems, recv_sems, token = refs[len(arrays)], refs[len(arrays) + 1], refs[-1]
        x, y, c = lax.axis_index("x"), lax.axis_index("y"), lax.axis_index("c")
        me_idx = _dev_index((x, y, c))
        for mask in range(N_DEV):
            p = _peer(mask, x, y, c)
            for i, (kind, ti) in enumerate(kinds):
                k = mask * n + i
                src, dst = _split_route(kind, ti, True, me_idx, _dev_index(p), src_refs[i], land_refs[i])
                pltpu.make_async_remote_copy(src_ref=src, dst_ref=dst, send_sem=send_sems.at[k], recv_sem=recv_sems.at[k],
                                             device_id=p, device_id_type=MESH).start()
        token[...] = jnp.zeros_like(token)

    n_copies = N_DEV * n
    return pl.pallas_call(
        body,
        name=name,
        in_specs=[HBM] * len(arrays),
        out_specs=(SEM, SEM) + (HBM,) * (2 * n) + (pl.BlockSpec(memory_space=pltpu.VMEM),),
        out_shape=(pltpu.SemaphoreType.DMA((n_copies,)), pltpu.SemaphoreType.DMA((n_copies,)))
        + tuple(pltpu.HBM(a.shape, a.dtype) for a in arrays[:2 * n]) + (jax.ShapeDtypeStruct((8, 128), F32),),
        input_output_aliases={i: 2 + i for i in range(2 * n)},
        compiler_params=pltpu.CompilerParams(has_side_effects=pltpu.SideEffectType.DATAFLOW_SIDE_EFFECTING),
    )(*[pltpu.with_memory_space_constraint(a, pltpu.HBM) for a in arrays])


def _split_wait(name, started, kinds, after):
    n = len(kinds)
    send_sems, recv_sems, thru = started[0], started[1], started[2:2 + 2 * n]

    def body(*refs):
        src_refs, land_refs = refs[0:n], refs[n:2 * n]
        send_sems, recv_sems = refs[2 * n], refs[2 * n + 1]
        x, y, c = lax.axis_index("x"), lax.axis_index("y"), lax.axis_index("c")
        me_idx = _dev_index((x, y, c))
        for mask in range(N_DEV):
            p = _peer(mask, x, y, c)
            for i, (kind, ti) in enumerate(kinds):
                k = mask * n + i
                src, dst = _split_route(kind, ti, False, me_idx, _dev_index(p), src_refs[i], land_refs[i])
                cp = pltpu.make_async_remote_copy(src_ref=src, dst_ref=dst, send_sem=send_sems.at[k],
                                                  recv_sem=recv_sems.at[k], device_id=p, device_id_type=MESH)
                cp.wait_send()
                cp.wait_recv()

    res = pl.pallas_call(
        body,
        name=name,
        in_specs=[HBM] * (2 * n) + [SEM, SEM, pl.BlockSpec(memory_space=pl.ANY)],
        out_specs=(HBM,) * (2 * n),
        out_shape=tuple(pltpu.HBM(a.shape, a.dtype) for a in thru),
        input_output_aliases={i: i for i in range(2 * n)},
        compiler_params=pltpu.CompilerParams(has_side_effects=pltpu.SideEffectType.DATAFLOW_SIDE_EFFECTING),
    )(*thru, send_sems, recv_sems, after)
    return res[n:2 * n]


def _empty_zones(srcs, kinds):
    def shape(src, kind, ti):
        if kind == GATHER:
            return WHOLE_SHAPES[ti]
        return (N_DEV,) + (SHARD_SHAPES[ti] if kind == SCATTER else src.shape)

    return [lax.empty(shape(src, *k), src.dtype) for src, k in zip(srcs, kinds)]


def _share_small(buf):
    def body(b_ref, o_ref, send_sems, recv_sems, local_sem):
        x, y, c = lax.axis_index("x"), lax.axis_index("y"), lax.axis_index("c")
        me_idx = _dev_index((x, y, c))
        local = pltpu.make_async_copy(b_ref, o_ref.at[me_idx], local_sem)
        local.start()

        def copy(mask, owner):
            return pltpu.make_async_remote_copy(
                src_ref=b_ref, dst_ref=o_ref.at[owner], send_sem=send_sems.at[mask - 1], recv_sem=recv_sems.at[mask - 1],
                device_id=_peer(mask, x, y, c), device_id_type=MESH)

        sends = [copy(mask, me_idx) for mask in range(1, N_DEV)]
        for cp in sends:
            cp.start()
        for mask in range(1, N_DEV):
            copy(mask, _dev_index(_peer(mask, x, y, c))).wait_recv()
        for cp in sends:
            cp.wait_send()
        local.wait()

    return pl.pallas_call(
        body,
        name="share_small",
        in_specs=[ANY],
        out_specs=ANY,
        out_shape=jax.ShapeDtypeStruct((N_DEV,) + buf.shape, buf.dtype),
        scratch_shapes=[pltpu.SemaphoreType.DMA((N_DEV - 1,)), pltpu.SemaphoreType.DMA((N_DEV - 1,)),
                        pltpu.SemaphoreType.DMA],
    )(buf)


def _s5_prep(lam_re, lam_im, b_re, b_im, c_re, c_im, d_skip, log_dt):
    dt = jnp.exp(log_dt)[:, None]
    mag = jnp.exp(lam_re * dt)
    a_re, a_im = mag * jnp.cos(lam_im * dt), mag * jnp.sin(lam_im * dt)
    den = lam_re * lam_re + lam_im * lam_im
    f_re = ((a_re - 1.0) * lam_re + a_im * lam_im) / den
    f_im = (a_im * lam_re - (a_re - 1.0) * lam_im) / den
    bb_re = f_re[..., None] * b_re - f_im[..., None] * b_im
    bb_im = f_re[..., None] * b_im + f_im[..., None] * b_re
    eye = jnp.eye(8, dtype=F32)

    def in_map(bb):
        return jnp.einsum("jgph,gk->ghjkp", bb.reshape(4, 8, S5_STATE, S5_CH), eye).reshape(128, N_STATE)

    def out_map(cm):
        return jnp.einsum("jghp,gk->ghjkp", cm.reshape(4, 8, S5_CH, S5_STATE), eye).reshape(128, N_STATE)

    a = jnp.concatenate([a_re.reshape(N_PAIR, 128), a_im.reshape(N_PAIR, 128)])
    bc = jnp.concatenate([in_map(bb_re), in_map(bb_im)], axis=1)
    cct = jnp.concatenate([out_map(c_re), out_map(-c_im)], axis=1)
    return a, bc, cct, d_skip.reshape(1, S5_W)


WEIGHTS = ["norm_g", "w_in", "lam_re", "lam_im", "b_re", "b_im", "c_re", "c_im", "d_skip", "log_dt", "w_glu", "b_glu",
           "ln_g", "ln_b", "w_s", "b_s", "w_pool", "pool_scale", "w_out", "final_g"]
SHARDED = ("w_in", "w_glu", "w_out")
SMALL = [n for n in WEIGHTS if n not in SHARDED]
INPUTS = ["x"] + WEIGHTS + ["loss_target"] + ["m_" + n for n in WEIGHTS] + ["v_" + n for n in WEIGHTS]
SMALL_B = ["norm_g", "final_g"]
SMALL_A = [n for n in SMALL if n not in SMALL_B]
SMALL_TILE = 16 * 128
LOSS_AT = (DEPTH + 1) * D_MODEL


def _pack_small(arrays, dtype):
    flat = jnp.concatenate([a.reshape(-1) for a in arrays])
    pad = (-flat.shape[0]) % SMALL_TILE
    return jnp.pad(flat, (0, pad)).astype(dtype).reshape(-1, 128)


def _unpack_small(packed, like):
    flat = packed.reshape(-1)
    out, off = [], 0
    for a in like:
        out.append(flat[off:off + a.size].reshape(a.shape))
        off += a.size
    return out


def _layer_fwd(p, l, x, wi, wo, wg, token=None, late=None):
    row = lambda v: v.reshape(1, -1)
    causal = jnp.tril(jnp.ones((CHUNK, CHUNK), dtype=bool))
    (a, bc, cct, dvec), prep_vjp = jax.vjp(
        _s5_prep, p["lam_re"][l], p["lam_im"][l], p["b_re"][l], p["b_im"][l], p["c_re"][l], p["c_im"][l],
        p["d_skip"][l], p["log_dt"][l])
    ws_f32 = jnp.where(causal[None], p["w_s"][l], 0.0)
    pm, pt = _step_major(S5_TB)
    c = dict(
        x=x, wi=wi, wo=wo, wg=wg, a=a, bc=bc.astype(BF16), cc=cct.T.astype(BF16), dvec=dvec, prep_vjp=prep_vjp,
        pm=pm, pt=pt,
        ws=ws_f32.astype(BF16), wst=jnp.swapaxes(ws_f32, 1, 2).astype(BF16),
        bsx=jnp.broadcast_to(p["b_s"][l][:, :, None], (SGU_HEADS, CHUNK, 128)),
        wpool=p["w_pool"][l].astype(BF16), scale=row(p["pool_scale"][l]),
        lng=row(p["ln_g"][l]), lnb=row(p["ln_b"][l]), bglu=row(p["b_glu"][l]), norm_g=row(p["norm_g"][l]))
    c["z"], c["h"] = _rms_inproj(x, c["norm_g"], wi, token)
    if late is not None:
        wo, wg = late(c["z"])
        c["wo"], c["wg"] = wo, wg
    c["ya"], c["s"], c["ys"] = _s5_fwd(c["z"], c["bc"], c["cc"], a, pm, pt, dvec, wg, c["bglu"], S5_TB)
    c["yb"] = _sgu_fwd(c["z"], c["lng"], c["lnb"], c["ws"], c["bsx"])
    c["yc"] = _pool_fwd(c["z"], c["wpool"], c["scale"])
    return _outproj(x, c["ya"], c["yb"], c["yc"], wo), c


def _mixers_bwd(c, dx, token=None):
    dya, dyb, dyc = _outproj_bwd(dx, c["wo"], token)
    dwo = _wgrad_out(c["ya"], c["yb"], c["yc"], dx, token)
    dz_s5, dbc, dcct, da, dd, dwg, dbglu = _s5_bwd(
        dya, c["ys"], c["z"], c["s"], c["bc"], c["cc"], c["a"], c["pm"], c["pt"], c["dvec"], c["wg"], c["bglu"], S5_TB)
    dz_sgu, dlng, dlnb, dws, dbsx = _sgu_bwd(dyb, c["z"], c["lng"], c["lnb"], c["ws"], c["wst"], c["bsx"])
    dz_pool, dwp, dsc = _pool_bwd(dyc, c["z"], c["wpool"], c["scale"])
    g_lam_re, g_lam_im, g_b_re, g_b_im, g_c_re, g_c_im, g_d, g_dt = c["prep_vjp"]((da, dbc, dcct, dd))
    small = dict(lam_re=g_lam_re, lam_im=g_lam_im, b_re=g_b_re, b_im=g_b_im, c_re=g_c_re,
                 c_im=g_c_im, d_skip=g_d, log_dt=g_dt, b_glu=dbglu.reshape(-1), ln_g=dlng.reshape(-1),
                 ln_b=dlnb.reshape(-1), w_s=dws, b_s=jnp.sum(dbsx, axis=-1), w_pool=dwp, pool_scale=dsc.reshape(-1))
    return (dz_s5, dz_sgu, dz_pool), dwo, dwg, small


def _inproj_grads(c, dz, dx, token_w=None, token_x=None):
    dwi = _wgrad_in(c["h"], *dz, token_w)
    dx, dnorm = _inproj_bwd(*dz, c["wi"], c["x"], c["norm_g"], dx, token_x)
    return dwi, dx, dnorm.reshape(-1)


def kernel(x, norm_g, w_in, lam_re, lam_im, b_re, b_im, c_re, c_im, d_skip, log_dt, w_glu, b_glu, ln_g, ln_b, w_s, b_s, w_pool, pool_scale, w_out, final_g, loss_target, m_norm_g, m_w_in, m_lam_re, m_lam_im, m_b_re, m_b_im, m_c_re, m_c_im, m_d_skip, m_log_dt, m_w_glu, m_b_glu, m_ln_g, m_ln_b, m_w_s, m_b_s, m_w_pool, m_pool_scale, m_w_out, m_final_g, v_norm_g, v_w_in, v_lam_re, v_lam_im, v_b_re, v_b_im, v_c_re, v_c_im, v_d_skip, v_log_dt, v_w_glu, v_b_glu, v_ln_g, v_ln_b, v_w_s, v_b_s, v_w_pool, v_pool_scale, v_w_out, v_final_g):
    p = dict(zip(INPUTS, (x, norm_g, w_in, lam_re, lam_im, b_re, b_im, c_re, c_im, d_skip, log_dt, w_glu, b_glu, ln_g, ln_b, w_s, b_s, w_pool, pool_scale, w_out, final_g, loss_target, m_norm_g, m_w_in, m_lam_re, m_lam_im, m_b_re, m_b_im, m_c_re, m_c_im, m_d_skip, m_log_dt, m_w_glu, m_b_glu, m_ln_g, m_ln_b, m_w_s, m_b_s, m_w_pool, m_pool_scale, m_w_out, m_final_g, v_norm_g, v_w_in, v_lam_re, v_lam_im, v_b_re, v_b_im, v_c_re, v_c_im, v_d_skip, v_log_dt, v_w_glu, v_b_glu, v_ln_g, v_ln_b, v_w_s, v_b_s, v_w_pool, v_pool_scale, v_w_out, v_final_g)))

    shards = [[w[l].astype(BF16) for w in (w_in, w_out, w_glu)] for l in range(DEPTH)]
    gather3 = [(GATHER, ti) for ti in range(3)]
    scatter3 = [(SCATTER, ti) for ti in range(3)]

    (wi0,) = _allgather_weights(shards[0][:1], [0])
    late_kinds = gather3[1:]
    gather0b = _split_start("gather0b_start", shards[0][1:], _empty_zones(shards[0][1:], late_kinds), late_kinds, after=wi0)
    gather1 = _split_start("gather1_start", shards[1], _empty_zones(shards[1], gather3), gather3, after=gather0b[-1])
    x1, c0 = _layer_fwd(p, 0, x[0], wi0, None, None, token=gather1[-1],
                        late=lambda z: _split_wait("gather0b_wait", gather0b, late_kinds, z))
    whole1 = _split_wait("gather1_wait", gather1, gather3, x1)
    x2, c1 = _layer_fwd(p, 1, x1, *whole1)

    dx, loss_tile, dfinal = _final_loss(x2, final_g.reshape(1, -1), loss_target[0])

    dz1, dwo1, dwg1, small1 = _mixers_bwd(c1, dx)
    dwi1, dx, dnorm1 = _inproj_grads(c1, dz1, dx)
    partials1 = [dwi1, dwo1, dwg1.astype(BF16)]
    grads1 = _split_start("grads1_start", partials1, _empty_zones(partials1, scatter3), scatter3)
    dz0, dwo0, dwg0, small0 = _mixers_bwd(c0, dx, token=grads1[-1])
    small_a = _pack_small([jnp.stack([small0[n], small1[n]]) for n in SMALL_A], BF16)
    srcs_a, kinds_a = [dwo0, dwg0.astype(BF16), small_a], [(SCATTER, 1), (SCATTER, 2), (SHARE, None)]
    grads0a = _split_start("grads0a_start", srcs_a, _empty_zones(srcs_a, kinds_a), kinds_a)
    dwi0 = _wgrad_in(c0["h"], *dz0, grads0a[-1])
    kinds_b = [(SCATTER, 0)]
    grads0b = _split_start("grads0b_start", [dwi0], _empty_zones([dwi0], kinds_b), kinds_b)
    dx, dnorm0 = _inproj_bwd(*dz0, c0["wi"], c0["x"], c0["norm_g"], dx, grads0b[-1])
    parts1 = _split_wait("grads1_wait", grads1, scatter3, dx)
    r_out0, r_glu0, r_a = _split_wait("grads0a_wait", grads0a, kinds_a, dx)
    (r_in0,) = _split_wait("grads0b_wait", grads0b, kinds_b, dx)
    parts0 = [r_in0, r_out0, r_glu0]
    r_b = _share_small(_pack_small(
        [jnp.stack([dnorm0.reshape(-1), dnorm1]), dfinal.reshape(-1), loss_tile[0, 0:1]], F32))
    loss = jnp.sum(r_b.reshape(N_DEV, -1)[:, LOSS_AT])

    out = {}

    def adam(name, ti, tr):
        shape2d = (DEPTH * SHARD_SHAPES[ti][0], SHARD_SHAPES[ti][1])
        res = _adamw(p[name].reshape(shape2d), p["m_" + name].reshape(shape2d), p["v_" + name].reshape(shape2d),
                     [parts0[ti], parts1[ti]], tr, "adamw_" + name)
        out[name] = [r.reshape(p[name].shape) for r in res]

    adam("w_in", 0, 512)
    adam("w_out", 1, 128)
    adam("w_glu", 2, 64)
    g_a, g_b = _sum_small(r_a, r_b)
    grads = dict(zip(SMALL_A, _unpack_small(g_a, [p[n] for n in SMALL_A])))
    grads.update(zip(SMALL_B, _unpack_small(g_b, [p[n] for n in SMALL_B])))
    rank2 = lambda a: a.reshape(1, -1) if a.ndim == 1 else a
    res = _adamw_small(*[[rank2(src[pre + n]) for n in SMALL] for src, pre in ((p, ""), (p, "m_"), (p, "v_"), (grads, ""))])
    for i, n in enumerate(SMALL):
        out[n] = [grads[n]] + [r.reshape(p[n].shape) for r in res[3 * i:3 * i + 3]]

    return (loss, dx[None], *[out[n][0] for n in WEIGHTS], *[out[n][1] for n in WEIGHTS],
            *[out[n][2] for n in WEIGHTS], *[out[n][3] for n in WEIGHTS])
```
